```python
import math
import jax, jax.numpy as jnp
from jax import lax
import numpy as np

D_MODEL = 1024
BATCH = 2
SEQ = 8192
DEPTH = 2
DEC_BATCH = 32
DEC_SEQ = 8
PAST_LEN = 8192
PAGE_SIZE = 128

HEAD_DIM = 64
D_MIX = D_MODEL
RET_W = D_MIX // 4
S5_W = D_MIX // 4
FOX_W = D_MIX - RET_W - S5_W
H_RET = RET_W // HEAD_DIM
H_FOX = FOX_W // HEAD_DIM
S5_GROUP_CH = 16
S5_GROUPS = S5_W // S5_GROUP_CH
S5_STATE = 64
IN_COLS = 4 * RET_W + S5_W + 3 * FOX_W + H_FOX
RET_CHUNK = 128
FOX_BLOCK = 128
ROPE_THETA = 10000.0
N_EXPERT_GROUPS = 4
EXPERTS_PER_GROUP = 8
N_EXPERTS = N_EXPERT_GROUPS * EXPERTS_PER_GROUP
D_EXPERT = D_MODEL // 2
MOE_TOP_K = 2
MOE_BLOCK = 128
RMS_EPS = 1e-6

kernel_name = 'hymba_retnet_s5_fox_hmoe_step'


def rmsnorm(x, g):
    xf = x.astype(jnp.float32)
    y = xf * lax.rsqrt(jnp.mean(xf * xf, axis=-1, keepdims=True) + RMS_EPS)
    return (y * g.astype(jnp.float32)).astype(x.dtype)


def rope(x, pos):
    half = x.shape[-1] // 2
    inv = ROPE_THETA ** (-jnp.arange(half, dtype=jnp.float32) / half)
    ang = pos[:, None] * inv[None, :]
    cos = jnp.cos(ang)[None, :, None, :]
    sin = jnp.sin(ang)[None, :, None, :]
    x1, x2 = x[..., :half], x[..., half:]
    return jnp.concatenate([x1 * cos - x2 * sin, x2 * cos + x1 * sin], axis=-1)


def retention(q, k, v, s0, pos0):
    f32 = jnp.float32
    Bn, T, H, dh = q.shape
    C = min(RET_CHUNK, T)
    n_chunks = T // C
    pos = pos0 + jnp.arange(T, dtype=f32)
    q = rope(q.astype(f32), pos)
    k = rope(k.astype(f32), pos) * (dh ** -0.5)
    v = v.astype(f32)
    log_gamma = jnp.log1p(-jnp.power(2.0, -5.0 - jnp.arange(H, dtype=f32)))
    idx = jnp.arange(C, dtype=f32)
    diff = idx[:, None] - idx[None, :]
    inner_decay = jnp.where(diff >= 0, jnp.exp(log_gamma[:, None, None] * jnp.maximum(diff, 0.0)), 0.0)
    q_decay = jnp.exp((idx[:, None] + 1.0) * log_gamma[None, :])
    k_decay = jnp.exp((C - 1.0 - idx[:, None]) * log_gamma[None, :])
    chunk_decay = jnp.exp(C * log_gamma)

    def to_chunks(t):
        return t.reshape(Bn, n_chunks, C, H, dh).swapaxes(0, 1)

    def step(S, qkv):
        qc, kc, vc = qkv
        scores = jnp.einsum('bihd,bjhd->bhij', qc, kc) * inner_decay[None]
        inner = jnp.einsum('bhij,bjhe->bihe', scores, vc)
        cross = jnp.einsum('bihd,bhde->bihe', qc, S) * q_decay[None, :, :, None]
        S_new = S * chunk_decay[None, :, None, None] + jnp.einsum(
            'bjhd,bjhe->bhde', kc * k_decay[None, :, :, None], vc)
        return S_new, inner + cross

    s_final, out = lax.scan(step, s0.astype(f32), (to_chunks(q), to_chunks(k), to_chunks(v)))
    return out.swapaxes(0, 1).reshape(Bn, T, H, dh), s_final


def s5_scan(u, x0, lam_re, lam_im, b_re, b_im, c_re, c_im, d_skip, log_dt):
    f32 = jnp.float32
    Bn, T, _ = u.shape
    ug = u.astype(f32).reshape(Bn, T, S5_GROUPS, S5_GROUP_CH)
    lam = lax.complex(lam_re.astype(f32), lam_im.astype(f32))
    dt = jnp.exp(log_dt.astype(f32))[:, None]
    lam_bar = jnp.exp(lam * dt)
    b_bar = ((lam_bar - 1.0) / lam)[:, :, None] * lax.complex(b_re.astype(f32), b_im.astype(f32))
    bu = jnp.einsum('gpc,btgc->btgp', b_bar, ug.astype(jnp.complex64))
    bu = bu.at[:, 0].add(lam_bar[None] * x0)
    a = jnp.broadcast_to(lam_bar, bu.shape)

    def combine(e1, e2):
        a1, b1 = e1
        a2, b2 = e2
        return a1 * a2, a2 * b1 + b2

    _, xs = lax.associative_scan(combine, (a, bu), axis=1)
    c = lax.complex(c_re.astype(f32), c_im.astype(f32))
    y = jnp.real(jnp.einsum('gcp,btgp->btgc', c, xs)) + d_skip.astype(f32).reshape(S5_GROUPS, S5_GROUP_CH) * ug
    return y.reshape(Bn, T, S5_W), xs[:, -1]


def fox_attend(q, k, v, cq, ck, qpos, kpos):
    s = jnp.einsum('bqhd,bkhd->bhqk', q, k).astype(jnp.float32) * (HEAD_DIM ** -0.5)
    s = s + jnp.transpose(cq, (0, 2, 1))[:, :, :, None] - jnp.transpose(ck, (0, 2, 1))[:, :, None, :]
    s = jnp.where(qpos[:, None] >= kpos[None, :], s, -jnp.inf)
    p = jax.nn.softmax(s, axis=-1)
    return jnp.einsum('bhqk,bkhd->bqhd', p.astype(v.dtype), v)


def fox_prompt(q, k, v, logf):
    Bn, T, H, dh = q.shape
    c = jnp.cumsum(logf.astype(jnp.float32), axis=1)
    Bq = min(FOX_BLOCK, T)
    n_blocks = T // Bq
    kpos = jnp.arange(T)

    def block(i):
        s0 = i * Bq
        qb = lax.dynamic_slice_in_dim(q, s0, Bq, axis=1)
        cb = lax.dynamic_slice_in_dim(c, s0, Bq, axis=1)
        return fox_attend(qb, k, v, cb, c, s0 + jnp.arange(Bq), kpos)

    out = lax.map(block, jnp.arange(n_blocks))
    return out.swapaxes(0, 1).reshape(Bn, T, H, dh)


def fox_sample(q, k_new, v_new, logf_new, k_past, v_past, logf_past):
    n_past = k_past.shape[1]
    Tq = q.shape[1]
    k = jnp.concatenate([k_past, k_new], axis=1)
    v = jnp.concatenate([v_past, v_new], axis=1)
    c = jnp.cumsum(jnp.concatenate([logf_past.astype(jnp.float32), logf_new], axis=1), axis=1)
    return fox_attend(q, k, v, c[:, n_past:], c, n_past + jnp.arange(Tq), jnp.arange(n_past + Tq))


def hier_moe(h, w_rg, b_rg, w_re, b_re, w1, w3, w2):
    f32 = jnp.float32
    Bn, T, D = h.shape
    N = Bn * T
    xf = h.reshape(N, D)
    g_prob = jax.nn.softmax((xf @ w_rg + b_rg).astype(f32), axis=-1)
    g_w, g_idx = lax.top_k(g_prob, 1)
    e_logits = (xf @ w_re + b_re).astype(f32).reshape(N, N_EXPERT_GROUPS, EXPERTS_PER_GROUP)
    e_logits = jnp.take_along_axis(e_logits, g_idx[:, :, None], axis=1)[:, 0]
    e_w, e_idx = lax.top_k(jax.nn.softmax(e_logits, axis=-1), MOE_TOP_K)
    gates = g_w * e_w / jnp.sum(e_w, axis=-1, keepdims=True)
    eid = (g_idx * EXPERTS_PER_GROUP + e_idx).reshape(-1)
    tok = jnp.repeat(jnp.arange(N, dtype=jnp.int32), MOE_TOP_K)
    gate = gates.reshape(-1)
    order = jnp.argsort(eid)
    se, stok, sgate = eid[order], tok[order], gate[order]
    counts = jax.ops.segment_sum(jnp.ones_like(eid), eid, num_segments=N_EXPERTS)
    padded = (counts + MOE_BLOCK - 1) // MOE_BLOCK * MOE_BLOCK
    start = jnp.cumsum(counts) - counts
    pend = jnp.cumsum(padded)
    pstart = pend - padded
    dest = pstart[se] + (jnp.arange(N * MOE_TOP_K, dtype=jnp.int32) - start[se])
    n_blocks = -(-N * MOE_TOP_K // MOE_BLOCK) + N_EXPERTS
    n_rows = n_blocks * MOE_BLOCK
    row_tok = jnp.full((n_rows,), N, jnp.int32).at[dest].set(stok)
    row_gate = jnp.zeros((n_rows,), f32).at[dest].set(sgate)
    blk_e = jnp.minimum(jnp.searchsorted(pend, jnp.arange(n_blocks, dtype=jnp.int32) * MOE_BLOCK, side='right'),
                        N_EXPERTS - 1)
    x_rows = jnp.concatenate([xf, jnp.zeros((1, D), xf.dtype)], axis=0)[row_tok].reshape(n_blocks, MOE_BLOCK, D)

    def expert_block(args):
        xb, e = args
        return (jax.nn.silu(xb @ w1[e]) * (xb @ w3[e])) @ w2[e]

    y_rows = lax.map(expert_block, (x_rows, blk_e)).reshape(n_rows, D)
    y = jax.ops.segment_sum(y_rows.astype(f32) * row_gate[:, None], row_tok, num_segments=N + 1)[:N]
    return y.reshape(Bn, T, D).astype(h.dtype)


def layer(x, l, P, ret_s0, s5_x0, pos0, fox_past):
    f32 = jnp.float32
    Bn, T, _ = x.shape
    h = rmsnorm(x, P['norm1'][l])
    proj = h @ P['w_in'][l]
    cuts = tuple(int(c) for c in np.cumsum([RET_W, RET_W, RET_W, RET_W, S5_W, FOX_W, FOX_W, FOX_W]))
    rq, rk, rv, rg, su, fq, fk, fv, ff = jnp.split(proj, cuts, axis=-1)

    def heads(t):
        return t.reshape(Bn, T, -1, HEAD_DIM)

    ret_o, ret_s = retention(heads(rq), heads(rk), heads(rv), ret_s0, pos0)
    ret_o = ret_o * lax.rsqrt(jnp.mean(ret_o * ret_o, axis=-1, keepdims=True) + RMS_EPS)
    ret_o = (jax.nn.silu(rg.astype(f32)) * ret_o.reshape(Bn, T, RET_W) * P['ret_gn'][l].astype(f32)).astype(x.dtype)
    s5_y, s5_s = s5_scan(su, s5_x0, P['s5_lam_re'][l], P['s5_lam_im'][l], P['s5_b_re'][l], P['s5_b_im'][l],
                         P['s5_c_re'][l], P['s5_c_im'][l], P['s5_d'][l], P['s5_log_dt'][l])
    s5_y = jax.nn.gelu(s5_y)
    s5_o = (s5_y * jax.nn.sigmoid(s5_y @ P['s5_glu_w'][l].astype(f32) + P['s5_glu_b'][l].astype(f32))).astype(x.dtype)
    logf = jax.nn.log_sigmoid(ff.astype(f32) + P['fox_fb'][l].astype(f32))
    fq, fk, fv = heads(fq), heads(fk), heads(fv)
    if fox_past is None:
        fox_o = fox_prompt(fq, fk, fv, logf)
    else:
        fox_o = fox_sample(fq, fk, fv, logf, fox_past[0], fox_past[1], fox_past[2])
    mixed = jnp.concatenate([ret_o, s5_o, fox_o.reshape(Bn, T, FOX_W).astype(x.dtype)], axis=-1)
    x = x + (mixed @ P['w_out'][l]).astype(x.dtype)
    x = x + hier_moe(rmsnorm(x, P['norm2'][l]), P['w_rg'][l], P['b_rg'][l], P['w_re'][l], P['b_re'][l],
                     P['w1'][l], P['w3'][l], P['w2'][l])
    return x, (ret_s, s5_s, fk, fv, logf)


def setup_inputs(seed: int = 0) -> dict:
    key = jax.random.key(seed)
    ks = jax.random.split(key, 40)
    f32 = jnp.float32
    n_pages = PAST_LEN // PAGE_SIZE
    n_used = DEC_BATCH * n_pages
    n_phys = (5 * n_used + 3) // 4

    def nrm(k, shape, scale):
        return jax.random.normal(k, shape, f32) * scale

    page_table = jax.random.permutation(ks[0], n_phys)[:n_used].reshape(DEC_BATCH, n_pages).astype(jnp.int32)
    return {
        'x_prompt': nrm(ks[1], (BATCH, SEQ, D_MODEL), 1.0),
        'x_sample': nrm(ks[2], (DEC_BATCH, DEC_SEQ, D_MODEL), 1.0),
        'cache_k': nrm(ks[3], (DEPTH, n_phys, PAGE_SIZE, H_FOX, HEAD_DIM), 1.0),
        'cache_v': nrm(ks[4], (DEPTH, n_phys, PAGE_SIZE, H_FOX, HEAD_DIM), 1.0),
        'cache_logf': jax.nn.log_sigmoid(nrm(ks[5], (DEPTH, n_phys, PAGE_SIZE, H_FOX), 1.0) + 3.0),
        'state_ret': nrm(ks[6], (DEPTH, DEC_BATCH, H_RET, HEAD_DIM, HEAD_DIM), 0.5),
        'state_s5': nrm(ks[7], (DEPTH, DEC_BATCH, S5_GROUPS, S5_STATE, 2), 0.5),
        'page_table': page_table,
        'norm1': 1.0 + nrm(ks[8], (DEPTH, D_MODEL), 0.02),
        'w_in': nrm(ks[9], (DEPTH, D_MODEL, IN_COLS), D_MODEL ** -0.5),
        'ret_gn': 1.0 + nrm(ks[10], (DEPTH, RET_W), 0.02),
        's5_lam_re': -0.5 + nrm(ks[11], (DEPTH, S5_GROUPS, S5_STATE), 0.01),
        's5_lam_im': math.pi * jnp.arange(S5_STATE, dtype=f32)[None, None, :] + nrm(ks[12], (DEPTH, S5_GROUPS, S5_STATE), 0.01),
        's5_b_re': nrm(ks[13], (DEPTH, S5_GROUPS, S5_STATE, S5_GROUP_CH), (2 * S5_GROUP_CH) ** -0.5),
        's5_b_im': nrm(ks[14], (DEPTH, S5_GROUPS, S5_STATE, S5_GROUP_CH), (2 * S5_GROUP_CH) ** -0.5),
        's5_c_re': nrm(ks[15], (DEPTH, S5_GROUPS, S5_GROUP_CH, S5_STATE), (2 * S5_STATE) ** -0.5),
        's5_c_im': nrm(ks[16], (DEPTH, S5_GROUPS, S5_GROUP_CH, S5_STATE), (2 * S5_STATE) ** -0.5),
        's5_d': nrm(ks[17], (DEPTH, S5_W), 0.5),
        's5_log_dt': jax.random.uniform(ks[18], (DEPTH, S5_GROUPS), f32, math.log(1e-3), math.log(1e-1)),
        's5_glu_w': nrm(ks[19], (DEPTH, S5_W, S5_W), S5_W ** -0.5),
        's5_glu_b': nrm(ks[20], (DEPTH, S5_W), 0.01),
        'fox_fb': jax.random.uniform(ks[21], (DEPTH, H_FOX), f32, 1.0, 4.0),
        'w_out': nrm(ks[22], (DEPTH, D_MIX, D_MODEL), D_MIX ** -0.5),
        'norm2': 1.0 + nrm(ks[23], (DEPTH, D_MODEL), 0.02),
        'w_rg': nrm(ks[24], (DEPTH, D_MODEL, N_EXPERT_GROUPS), D_MODEL ** -0.5),
        'b_rg': nrm(ks[25], (DEPTH, N_EXPERT_GROUPS), 0.01),
        'w_re': nrm(ks[26], (DEPTH, D_MODEL, N_EXPERTS), D_MODEL ** -0.5),
        'b_re': nrm(ks[27], (DEPTH, N_EXPERTS), 0.01),
        'w1': nrm(ks[28], (DEPTH, N_EXPERTS, D_MODEL, D_EXPERT), D_MODEL ** -0.5),
        'w3': nrm(ks[29], (DEPTH, N_EXPERTS, D_MODEL, D_EXPERT), D_MODEL ** -0.5),
        'w2': nrm(ks[30], (DEPTH, N_EXPERTS, D_EXPERT, D_MODEL), D_EXPERT ** -0.5),
        'norm_f': 1.0 + nrm(ks[31], (D_MODEL,), 0.02),
    }


def reference(x_prompt, x_sample, cache_k, cache_v, cache_logf, state_ret, state_s5, page_table,
              norm1, w_in, ret_gn, s5_lam_re, s5_lam_im, s5_b_re, s5_b_im, s5_c_re, s5_c_im, s5_d,
              s5_log_dt, s5_glu_w, s5_glu_b, fox_fb, w_out, norm2, w_rg, b_rg, w_re, b_re, w1, w3, w2, norm_f):
    f32 = jnp.float32
    P = {'norm1': norm1, 'w_in': w_in, 'ret_gn': ret_gn, 's5_lam_re': s5_lam_re, 's5_lam_im': s5_lam_im,
         's5_b_re': s5_b_re, 's5_b_im': s5_b_im, 's5_c_re': s5_c_re, 's5_c_im': s5_c_im, 's5_d': s5_d,
         's5_log_dt': s5_log_dt, 's5_glu_w': s5_glu_w, 's5_glu_b': s5_glu_b, 'fox_fb': fox_fb, 'w_out': w_out,
         'norm2': norm2, 'w_rg': w_rg, 'b_rg': b_rg, 'w_re': w_re, 'b_re': b_re, 'w1': w1, 'w3': w3, 'w2': w2}
    Bp = x_prompt.shape[0]
    Bd = x_sample.shape[0]
    n_past = page_table.shape[1] * cache_k.shape[2]
    xp, xs = x_prompt, x_sample
    st_p, st_s = [], []
    for l in range(DEPTH):
        ret0 = jnp.zeros((Bp, H_RET, HEAD_DIM, HEAD_DIM), f32)
        s50 = jnp.zeros((Bp, S5_GROUPS, S5_STATE), jnp.complex64)
        xp, sp = layer(xp, l, P, ret0, s50, 0, None)
        k_past = cache_k[l][page_table].reshape(Bd, n_past, H_FOX, HEAD_DIM)
        v_past = cache_v[l][page_table].reshape(Bd, n_past, H_FOX, HEAD_DIM)
        lf_past = cache_logf[l][page_table].reshape(Bd, n_past, H_FOX)
        s5x0 = lax.complex(state_s5[l, ..., 0].astype(f32), state_s5[l, ..., 1].astype(f32))
        xs, ss = layer(xs, l, P, state_ret[l], s5x0, n_past, (k_past, v_past, lf_past))
        st_p.append(sp)
        st_s.append(ss)
    y_prompt = rmsnorm(xp, norm_f)
    y_sample = rmsnorm(xs, norm_f)
    k_prompt = jnp.stack([s[2] for s in st_p])
    v_prompt = jnp.stack([s[3] for s in st_p])
    logf_prompt = jnp.stack([s[4] for s in st_p])
    k_sample = jnp.stack([s[2] for s in st_s])
    v_sample = jnp.stack([s[3] for s in st_s])
    logf_sample = jnp.stack([s[4] for s in st_s])
    ret_prompt = jnp.stack([s[0] for s in st_p])
    ret_sample = jnp.stack([s[0] for s in st_s])
    s5_prompt = jnp.stack([jnp.stack([jnp.real(s[1]), jnp.imag(s[1])], axis=-1) for s in st_p])
    s5_sample = jnp.stack([jnp.stack([jnp.real(s[1]), jnp.imag(s[1])], axis=-1) for s in st_s])
    return (y_prompt, y_sample, k_prompt, v_prompt, logf_prompt, k_sample, v_sample, logf_sample,
            ret_prompt, ret_sample, s5_prompt, s5_sample)
```

```python
import functools
import math

import jax
import jax.numpy as jnp
from jax import lax
from jax.experimental import pallas as pl
from jax.experimental.pallas import tpu as pltpu

F32 = jnp.float32
BF16 = jnp.bfloat16
HIGHEST = lax.Precision.HIGHEST

HEAD_DIM = 64
H_RET = 4
H_FOX = 8
RET_W = H_RET * HEAD_DIM
S5_W = 256
FOX_W = H_FOX * HEAD_DIM
S5_GROUPS = 16
S5_GROUP_CH = 16
S5_STATE = 64
S5_LANES = S5_GROUPS * S5_STATE
N_EXPERTS = 32
N_EXPERT_GROUPS = 4
EXPERTS_PER_GROUP = 8
RET_CHUNK = 128
ROPE_THETA = 10000.0
RMS_EPS = 1e-6
NEG_BIG = -1e30

LANES = 128
SUBLANES = 8
TOKEN_TILE = 256
MOE_ROWS = 256
FOX_TQ = 512
PAGES_PER_STEP = 8
VMEM_LIMIT = 48 * 1024 * 1024


def _params(*sem):
    return pltpu.CompilerParams(dimension_semantics=sem, vmem_limit_bytes=VMEM_LIMIT)


def _sigmoid(x):
    return 1.0 / (1.0 + jnp.exp(-x))


def _dot(a, b):
    return jnp.dot(a, b, preferred_element_type=F32)


def _dot_nt(a, b):
    return lax.dot_general(a, b, (((1,), (1,)), ((), ())), preferred_element_type=F32)


def _dot_tn(a, b):
    return lax.dot_general(a, b, (((0,), (0,)), ((), ())), preferred_element_type=F32)


def _in_proj_kernel(x_ref, g_ref, w_ref, wff_ref, fb_ref, cos_ref, sin_ref,
                    ret_ref, su_ref, fq_ref, fk_ref, fv_ref, fkb_ref, fvb_ref, logf_ref):
    x = x_ref[...]
    ms = jnp.mean(x * x, axis=-1, keepdims=True)
    h = (x * lax.rsqrt(ms + RMS_EPS) * g_ref[...]).astype(BF16)
    tm = x.shape[0]
    cos = cos_ref[...]
    sin = sin_ref[...]
    lane = lax.broadcasted_iota(jnp.int32, (tm, RET_W), 1)
    first_half = (lane % HEAD_DIM) < (HEAD_DIM // 2)

    def rope(t):
        swapped = jnp.where(first_half,
                            pltpu.roll(t, RET_W - HEAD_DIM // 2, axis=1),
                            pltpu.roll(t, HEAD_DIM // 2, axis=1))
        return t * cos + swapped * sin

    ret_ref[:, 0:RET_W] = rope(_dot(h, w_ref[:, 0:RET_W]))
    ret_ref[:, RET_W:2 * RET_W] = rope(_dot(h, w_ref[:, RET_W:2 * RET_W])) * (HEAD_DIM ** -0.5)
    ret_ref[:, 2 * RET_W:4 * RET_W] = _dot(h, w_ref[:, 2 * RET_W:4 * RET_W])
    c0 = 4 * RET_W
    su_ref[...] = _dot(h, w_ref[:, c0:c0 + S5_W])
    c0 += S5_W
    fq_ref[...] = (_dot(h, w_ref[:, c0:c0 + FOX_W]) * (HEAD_DIM ** -0.5)).astype(BF16)
    c0 += FOX_W
    fk = _dot(h, w_ref[:, c0:c0 + FOX_W])
    fk_ref[...] = fk
    fkb_ref[...] = fk.astype(BF16)
    c0 += FOX_W
    fv = _dot(h, w_ref[:, c0:c0 + FOX_W])
    fv_ref[...] = fv
    fvb_ref[...] = fv.astype(BF16)
    z = _dot(h, wff_ref[...]) + fb_ref[...]
    logf = jnp.minimum(z, 0.0) - jnp.log1p(jnp.exp(-jnp.abs(z)))
    lane128 = lax.broadcasted_iota(jnp.int32, (tm, LANES), 1)
    logf_ref[...] = jnp.where(lane128 < H_FOX, logf, 0.0)


def _in_proj(x, g, w_main, w_ff, fb, cos_t, sin_t, n_prompt_tiles, pos_tiles):
    n = x.shape[0]
    tm = TOKEN_TILE
    wcols = w_main.shape[1]

    def row(i):
        return (i, 0)

    def const(i):
        return (0, 0)

    def pos_map(i):
        return (jnp.where(i < n_prompt_tiles, i % pos_tiles, pos_tiles), 0)

    outs = (
        jax.ShapeDtypeStruct((n, 4 * RET_W), F32),
        jax.ShapeDtypeStruct((n, S5_W), F32),
        jax.ShapeDtypeStruct((n, FOX_W), BF16),
        jax.ShapeDtypeStruct((n, FOX_W), F32),
        jax.ShapeDtypeStruct((n, FOX_W), F32),
        jax.ShapeDtypeStruct((n, FOX_W), BF16),
        jax.ShapeDtypeStruct((n, FOX_W), BF16),
        jax.ShapeDtypeStruct((n, LANES), F32),
    )
    return pl.pallas_call(
        _in_proj_kernel,
        grid=(n // tm,),
        in_specs=[
            pl.BlockSpec((tm, x.shape[1]), row),
            pl.BlockSpec((1, x.shape[1]), const),
            pl.BlockSpec((x.shape[1], wcols), const),
            pl.BlockSpec((x.shape[1], LANES), const),
            pl.BlockSpec((1, LANES), const),
            pl.BlockSpec((tm, RET_W), pos_map),
            pl.BlockSpec((tm, RET_W), pos_map),
        ],
        out_specs=(
            pl.BlockSpec((tm, 4 * RET_W), row),
            pl.BlockSpec((tm, S5_W), row),
            pl.BlockSpec((tm, FOX_W), row),
            pl.BlockSpec((tm, FOX_W), row),
            pl.BlockSpec((tm, FOX_W), row),
            pl.BlockSpec((tm, FOX_W), row),
            pl.BlockSpec((tm, FOX_W), row),
            pl.BlockSpec((tm, LANES), row),
        ),
        out_shape=outs,
        compiler_params=_params("arbitrary"),
        name="in_proj",
    )(x, g, w_main, w_ff, fb, cos_t, sin_t)


def _retention_kernel(nseq, seq_len, nchunk, x_ref, s0_ref, dm_ref, qd_ref, kd_ref, cd_ref, bd_ref,
                      gn_ref, gw_ref, o_ref, sout_ref, s_scr):
    step = pl.program_id(1)
    rows = nseq * seq_len

    @pl.when(step == 0)
    def _():
        s_scr[...] = s0_ref[...]

    lane_head = lax.broadcasted_iota(jnp.int32, (rows, RET_W), 1) // HEAD_DIM
    row_seq = lax.broadcasted_iota(jnp.int32, (rows, 1), 0) // seq_len

    def chunk(c, _):
        r0 = pl.multiple_of(c * rows, rows)
        q = x_ref[pl.ds(r0, rows), 0:RET_W]
        k = x_ref[pl.ds(r0, rows), RET_W:2 * RET_W]
        v = x_ref[pl.ds(r0, rows), 2 * RET_W:3 * RET_W]
        g = x_ref[pl.ds(r0, rows), 3 * RET_W:4 * RET_W]
        qb = q.astype(BF16)
        kb = k.astype(BF16)
        vb = v.astype(BF16)
        ps = []
        vs = []
        for h in range(H_RET):
            hm = lane_head == h
            sc = _dot_nt(jnp.where(hm, q, 0.0).astype(BF16), kb)
            ps.append((sc * dm_ref[h]).astype(BF16))
            vs.append(jnp.where(hm, v, 0.0).astype(BF16))
        inner = _dot(jnp.concatenate(ps, axis=1), jnp.concatenate(vs, axis=0))
        kdec = k * kd_ref[...]
        cd = cd_ref[...]
        bd = bd_ref[...]
        if nseq == 1:
            s_old = s_scr[0]
            cross = _dot(qb, s_old.astype(BF16))
            s_scr[0] = s_old * cd + _dot_tn(kdec.astype(BF16), vb) * bd
        else:
            def seq_body(s, cross):
                rm = row_seq == s
                s_old = s_scr[s]
                cross = jnp.where(rm, _dot(qb, s_old.astype(BF16)), cross)
                kv = _dot_tn(jnp.where(rm, kdec, 0.0).astype(BF16), vb)
                s_scr[s] = s_old * cd + kv * bd
                return cross

            cross = lax.fori_loop(0, nseq, seq_body, jnp.zeros((rows, RET_W), F32))
        o = inner + cross * qd_ref[...]
        o2 = o * o
        hi = o2.astype(BF16)
        lo = (o2 - hi.astype(F32)).astype(BF16)
        ms = _dot(hi, gn_ref[...]) + _dot(lo, gn_ref[...])
        on = o * lax.rsqrt(ms + RMS_EPS)
        o_ref[pl.ds(r0, rows), :] = (g * _sigmoid(g) * on * gw_ref[...]).astype(BF16)
        return 0

    lax.fori_loop(0, nchunk, chunk, 0)

    @pl.when(step == pl.num_programs(1) - 1)
    def _():
        sout_ref[...] = s_scr[...]


def _retention_tables(nseq, seq_len):
    rows = nseq * seq_len
    log_gamma = jnp.log1p(-jnp.power(2.0, -5.0 - jnp.arange(H_RET, dtype=F32)))
    idx = jnp.arange(rows, dtype=jnp.int32)
    pos = (idx % seq_len).astype(F32)
    seq = idx // seq_len
    diff = pos[:, None] - pos[None, :]
    ok = (seq[:, None] == seq[None, :]) & (diff >= 0)
    dm = jnp.where(ok[None], jnp.exp(log_gamma[:, None, None] * jnp.maximum(diff, 0.0)[None]), 0.0)
    lg_lane = jnp.repeat(log_gamma, HEAD_DIM)
    qd = jnp.exp((pos[:, None] + 1.0) * lg_lane[None, :])
    kd = jnp.exp((seq_len - 1.0 - pos[:, None]) * lg_lane[None, :])
    cd = jnp.exp(seq_len * lg_lane)[None, :]
    head = jnp.arange(RET_W) // HEAD_DIM
    bd = (head[:, None] == head[None, :]).astype(F32)
    gn = (bd / HEAD_DIM).astype(BF16)
    return dm, qd, kd, cd, bd, gn


def _retention(ret_in, s0_bd, gw, nbatch, row0, seq_total, nseq, seq_len, rows_per_step):
    rows = nseq * seq_len
    nchunk = rows_per_step // rows
    steps = seq_total // rows_per_step
    blk0 = row0 // rows_per_step
    dm, qd, kd, cd, bd, gn = _retention_tables(nseq, seq_len)

    def rmap(b, i):
        return (blk0 + b * steps + i, 0)

    def omap(b, i):
        return (b * steps + i, 0)

    def c2(b, i):
        return (0, 0)

    def c3(b, i):
        return (0, 0, 0)

    def smap(b, i):
        return (b, 0, 0)

    in_specs = [
        pl.BlockSpec((rows_per_step, 4 * RET_W), rmap),
        pl.BlockSpec((nseq, RET_W, RET_W), smap),
        pl.BlockSpec((H_RET, rows, rows), c3),
        pl.BlockSpec((rows, RET_W), c2),
        pl.BlockSpec((rows, RET_W), c2),
        pl.BlockSpec((1, RET_W), c2),
        pl.BlockSpec((RET_W, RET_W), c2),
        pl.BlockSpec((RET_W, RET_W), c2),
        pl.BlockSpec((1, RET_W), c2),
    ]
    return pl.pallas_call(
        functools.partial(_retention_kernel, nseq, seq_len, nchunk),
        grid=(nbatch, steps),
        in_specs=in_specs,
        out_specs=(
            pl.BlockSpec((rows_per_step, RET_W), omap),
            pl.BlockSpec((nseq, RET_W, RET_W), smap),
        ),
        out_shape=(
            jax.ShapeDtypeStruct((nbatch * seq_total, RET_W), BF16),
            jax.ShapeDtypeStruct((nbatch * nseq, RET_W, RET_W), F32),
        ),
        scratch_shapes=[pltpu.VMEM((nseq, RET_W, RET_W), F32)],
        compiler_params=_params("arbitrary", "arbitrary"),
        name="retention",
    )(ret_in, s0_bd, dm, qd, kd, cd, bd, gn, gw)


def _s5_kernel(short, u_ref, x0r_ref, x0i_ref, bre_ref, bim_ref, pre_ref, pim_ref, cm_ref, d_ref,
               gw_ref, gb_ref, o_ref, str_ref, sti_ref, bur, bui, car, cai):
    step = pl.program_id(1)
    rows = u_ref.shape[0]
    u = u_ref[...]
    ub = u.astype(BF16)
    bur[...] = _dot(ub, bre_ref[...])
    bui[...] = _dot(ub, bim_ref[...])

    if not short:
        @pl.when(step == 0)
        def _():
            car[...] = x0r_ref[0]
            cai[...] = x0i_ref[0]

    pr = pre_ref[...]
    pi = pim_ref[...]
    rowi = lax.broadcasted_iota(jnp.int32, (SUBLANES, S5_LANES), 0)

    def group(j, carry):
        cr, ci = carry
        r0 = pl.multiple_of(j * SUBLANES, SUBLANES)
        xr = bur[pl.ds(r0, SUBLANES), :]
        xi = bui[pl.ds(r0, SUBLANES), :]
        for s in (1, 2, 4):
            ar = pr[s - 1:s]
            ai = pi[s - 1:s]
            sr = jnp.where(rowi >= s, pltpu.roll(xr, s, axis=0), 0.0)
            si = jnp.where(rowi >= s, pltpu.roll(xi, s, axis=0), 0.0)
            xr, xi = xr + ar * sr - ai * si, xi + ar * si + ai * sr
        if short:
            cr = x0r_ref[j]
            ci = x0i_ref[j]
        xr, xi = xr + pr * cr - pi * ci, xi + pr * ci + pi * cr
        bur[pl.ds(r0, SUBLANES), :] = xr
        bui[pl.ds(r0, SUBLANES), :] = xi
        cr = xr[SUBLANES - 1:SUBLANES]
        ci = xi[SUBLANES - 1:SUBLANES]
        if short:
            str_ref[j] = cr
            sti_ref[j] = ci
        return cr, ci

    if short:
        init = (jnp.zeros((1, S5_LANES), F32), jnp.zeros((1, S5_LANES), F32))
    else:
        init = (car[...], cai[...])
    cr, ci = lax.fori_loop(0, rows // SUBLANES, group, init)
    if not short:
        car[...] = cr
        cai[...] = ci

        @pl.when(step == pl.num_programs(1) - 1)
        def _():
            str_ref[0] = cr
            sti_ref[0] = ci

    xs = jnp.concatenate([bur[...].astype(BF16), bui[...].astype(BF16)], axis=1)
    y = _dot(xs, cm_ref[...]) + d_ref[...] * u
    yg = 0.5 * y * (1.0 + jnp.tanh(math.sqrt(2.0 / math.pi) * (y + 0.044715 * (y * y * y))))
    z = _dot(yg.astype(BF16), gw_ref[...]) + gb_ref[...]
    o_ref[...] = (yg * _sigmoid(z)).astype(BF16)


def _s5_tables(lam_re, lam_im, b_re, b_im, c_re, c_im, log_dt):
    lam = lax.complex(lam_re.astype(F32), lam_im.astype(F32))
    dt = jnp.exp(log_dt.astype(F32))[:, None]
    lam_bar = jnp.exp(lam * dt)
    b_bar = ((lam_bar - 1.0) / lam)[:, :, None] * lax.complex(b_re.astype(F32), b_im.astype(F32))
    k = jnp.arange(1, SUBLANES + 1, dtype=F32)[:, None, None]
    powers = jnp.exp((lam * dt)[None] * k)
    pre = jnp.real(powers).reshape(SUBLANES, S5_LANES)
    pim = jnp.imag(powers).reshape(SUBLANES, S5_LANES)
    eye = jnp.eye(S5_GROUPS, dtype=F32)
    bre = jnp.einsum('gpc,gh->gchp', jnp.real(b_bar), eye).reshape(S5_W, S5_LANES).astype(BF16)
    bim = jnp.einsum('gpc,gh->gchp', jnp.imag(b_bar), eye).reshape(S5_W, S5_LANES).astype(BF16)
    cre = jnp.einsum('gcp,gh->hpgc', c_re.astype(F32), eye).reshape(S5_LANES, S5_W)
    cim = jnp.einsum('gcp,gh->hpgc', c_im.astype(F32), eye).reshape(S5_LANES, S5_W)
    cm = jnp.concatenate([cre, -cim], axis=0).astype(BF16)
    return bre, bim, pre, pim, cm


def _s5(su, x0r, x0i, tables, d, glu_w, glu_b, nbatch, row0, seq_total, rows_per_step, short):
    bre, bim, pre, pim, cm = tables
    steps = seq_total // rows_per_step
    blk0 = row0 // rows_per_step
    nstate = x0r.shape[0]
    sblk = nstate if short else 1

    def rmap(b, i):
        return (blk0 + b * steps + i, 0)

    def omap(b, i):
        return (b * steps + i, 0)

    def c2(b, i):
        return (0, 0)

    def smap(b, i):
        return (0 if short else b, 0, 0)

    in_specs = [
        pl.BlockSpec((rows_per_step, S5_W), rmap),
        pl.BlockSpec((sblk, 1, S5_LANES), smap),
        pl.BlockSpec((sblk, 1, S5_LANES), smap),
        pl.BlockSpec((S5_W, S5_LANES), c2),
        pl.BlockSpec((S5_W, S5_LANES), c2),
        pl.BlockSpec((SUBLANES, S5_LANES), c2),
        pl.BlockSpec((SUBLANES, S5_LANES), c2),
        pl.BlockSpec((2 * S5_LANES, S5_W), c2),
        pl.BlockSpec((1, S5_W), c2),
        pl.BlockSpec((S5_W, S5_W), c2),
        pl.BlockSpec((1, S5_W), c2),
    ]
    return pl.pallas_call(
        functools.partial(_s5_kernel, short),
        grid=(nbatch, steps),
        in_specs=in_specs,
        out_specs=(
            pl.BlockSpec((rows_per_step, S5_W), omap),
            pl.BlockSpec((sblk, 1, S5_LANES), smap),
            pl.BlockSpec((sblk, 1, S5_LANES), smap),
        ),
        out_shape=(
            jax.ShapeDtypeStruct((nbatch * seq_total, S5_W), BF16),
            jax.ShapeDtypeStruct((nstate, 1, S5_LANES), F32),
            jax.ShapeDtypeStruct((nstate, 1, S5_LANES), F32),
        ),
        scratch_shapes=[
            pltpu.VMEM((rows_per_step, S5_LANES), F32),
            pltpu.VMEM((rows_per_step, S5_LANES), F32),
            pltpu.VMEM((1, S5_LANES), F32),
            pltpu.VMEM((1, S5_LANES), F32),
        ],
        compiler_params=_params("arbitrary", "arbitrary"),
        name="s5",
    )(su, x0r, x0i, bre, bim, pre, pim, cm, d, glu_w, glu_b)


def _cumsum_kernel(lf_ref, tri_ref, c_ref, carry):
    @pl.when(pl.program_id(1) == 0)
    def _():
        carry[...] = jnp.zeros_like(carry)

    c = jnp.dot(tri_ref[...], lf_ref[...], precision=HIGHEST, preferred_element_type=F32) + carry[...]
    c_ref[...] = c
    rows = c.shape[0]
    carry[...] = c[rows - 1:rows, :]


def _cumsum_logf(logf, nbatch, seq_total, rows_per_step):
    steps = seq_total // rows_per_step
    idx = jnp.arange(rows_per_step)
    tri = (idx[:, None] >= idx[None, :]).astype(F32)
    return pl.pallas_call(
        _cumsum_kernel,
        grid=(nbatch, steps),
        in_specs=[
            pl.BlockSpec((rows_per_step, LANES), lambda b, i: (b * steps + i, 0)),
            pl.BlockSpec((rows_per_step, rows_per_step), lambda b, i: (0, 0)),
        ],
        out_specs=pl.BlockSpec((rows_per_step, LANES), lambda b, i: (b * steps + i, 0)),
        out_shape=jax.ShapeDtypeStruct((nbatch * seq_total, LANES), F32),
        scratch_shapes=[pltpu.VMEM((1, LANES), F32)],
        compiler_params=_params("arbitrary", "arbitrary"),
        name="cumsum_logf",
    )(logf, tri)


_FOX_HEADS_PER_STEP = 4


def _fox_prompt_kernel(tq, q_ref, k_ref, v_ref, c_ref, ct_ref, o_ref, acc, m_scr, l_scr):
    hg = pl.program_id(1)
    qi = pl.program_id(2)
    nh = _FOX_HEADS_PER_STEP
    width = nh * HEAD_DIM
    q = q_ref[...]
    lane_head = lax.broadcasted_iota(jnp.int32, (tq, width), 1) // HEAD_DIM
    lane128 = lax.broadcasted_iota(jnp.int32, (tq, LANES), 1)
    c_all = c_ref[...]
    qs = []
    cqs = []
    for h in range(nh):
        qs.append(jnp.where(lane_head == h, q, jnp.zeros_like(q)))
        cqs.append(jnp.sum(jnp.where(lane128 == hg * nh + h, c_all, 0.0), axis=1, keepdims=True))
        m_scr[h] = jnp.full((tq, 1), NEG_BIG, F32)
        l_scr[h] = jnp.zeros((tq, 1), F32)
        acc[h] = jnp.zeros((tq, width), F32)
    causal = (lax.broadcasted_iota(jnp.int32, (tq, tq), 0) >= lax.broadcasted_iota(jnp.int32, (tq, tq), 1))

    def kv_step(j, masked):
        ks = pl.multiple_of(j * tq, tq)
        kb = k_ref[pl.ds(ks, tq), :]
        vb = v_ref[pl.ds(ks, tq), :]
        for h in range(nh):
            ck = ct_ref[0, 0, pl.ds(h, 1), pl.ds(ks, tq)]
            s = _dot_nt(qs[h], kb) + (cqs[h] - ck)
            if masked:
                s = jnp.where(causal, s, NEG_BIG)
            m_old = m_scr[h]
            m_new = jnp.maximum(m_old, jnp.max(s, axis=1, keepdims=True))
            p = jnp.exp(s - m_new)
            alpha = jnp.exp(m_old - m_new)
            l_scr[h] = alpha * l_scr[h] + jnp.sum(p, axis=1, keepdims=True)
            acc[h] = alpha * acc[h] + _dot(p.astype(BF16), vb)
            m_scr[h] = m_new

    def body(j, _):
        kv_step(j, False)
        return 0

    lax.fori_loop(0, qi, body, 0)
    kv_step(qi, True)
    out = jnp.zeros((tq, width), F32)
    for h in range(nh):
        out = jnp.where(lane_head == h, acc[h] / l_scr[h], out)
    o_ref[...] = out.astype(BF16)


def _fox_prompt(fq, fkb, fvb, c, ct, nbatch, seq_total, n_total):
    tq = min(FOX_TQ, seq_total)
    nq = seq_total // tq
    nh = _FOX_HEADS_PER_STEP
    width = nh * HEAD_DIM
    ngroups = H_FOX // nh
    return pl.pallas_call(
        functools.partial(_fox_prompt_kernel, tq),
        grid=(nbatch, ngroups, nq),
        in_specs=[
            pl.BlockSpec((tq, width), lambda b, g, i: (b * nq + i, g)),
            pl.BlockSpec((seq_total, width), lambda b, g, i: (b, g)),
            pl.BlockSpec((seq_total, width), lambda b, g, i: (b, g)),
            pl.BlockSpec((tq, LANES), lambda b, g, i: (b * nq + i, 0)),
            pl.BlockSpec((1, 1, nh, seq_total), lambda b, g, i: (b, g, 0, 0)),
        ],
        out_specs=pl.BlockSpec((tq, width), lambda b, g, i: (b * nq + i, g)),
        out_shape=jax.ShapeDtypeStruct((n_total, FOX_W), BF16),
        scratch_shapes=[
            pltpu.VMEM((nh, tq, width), F32),
            pltpu.VMEM((nh, tq, 1), F32),
            pltpu.VMEM((nh, tq, 1), F32),
        ],
        compiler_params=_params("arbitrary", "arbitrary", "arbitrary"),
        name="fox_prompt",
    )(fq, fkb, fvb, c, ct)


def _fox_sample_kernel(npg, tnew, pt_ref, qm_ref, *refs):
    del pt_ref
    k_refs = refs[0:npg]
    v_refs = refs[npg:2 * npg]
    lf_refs = refs[2 * npg:3 * npg]
    knew_ref, vnew_ref, lfnew_ref, u_ref, sel_ref, o_ref, m_scr, l_scr, acc, run = refs[3 * npg:]
    g = pl.program_id(1)
    nrow = tnew * H_FOX

    @pl.when(g == 0)
    def _():
        m_scr[...] = jnp.full(m_scr.shape, NEG_BIG, F32)
        l_scr[...] = jnp.zeros(l_scr.shape, F32)
        acc[...] = jnp.zeros(acc.shape, F32)
        run[...] = jnp.zeros(run.shape, F32)

    qm = qm_ref[0]

    def page(kf, vf, lft, mask):
        kb = kf.astype(BF16)
        vb = vf.astype(BF16)
        cpage = run[...] + jnp.dot(lft, u_ref[...], precision=HIGHEST, preferred_element_type=F32)
        s = _dot_nt(qm, kb) - jnp.concatenate([cpage] * tnew, axis=0)
        if mask is not None:
            s = jnp.where(mask, s, NEG_BIG)
        m_old = m_scr[...]
        m_new = jnp.maximum(m_old, jnp.max(s, axis=1, keepdims=True))
        p = jnp.exp(s - m_new)
        alpha = jnp.exp(m_old - m_new)
        l_scr[...] = alpha * l_scr[...] + jnp.sum(p, axis=1, keepdims=True)
        acc[...] = alpha * acc[...] + _dot(p.astype(BF16), vb)
        m_scr[...] = m_new
        npage = cpage.shape[1]
        run[...] = jnp.broadcast_to(cpage[:, npage - 1:npage], run.shape)

    for j in range(npg):
        page(k_refs[j][0, 0], v_refs[j][0, 0], lf_refs[j][0, 0], None)

    @pl.when(g == pl.num_programs(1) - 1)
    def _():
        npage = knew_ref.shape[1]
        rt = lax.broadcasted_iota(jnp.int32, (nrow, npage), 0) // H_FOX
        col = lax.broadcasted_iota(jnp.int32, (nrow, npage), 1)
        page(knew_ref[0], vnew_ref[0], lfnew_ref[0], col <= rt)
        o = acc[...] / l_scr[...]
        rh = lax.broadcasted_iota(jnp.int32, (nrow, FOX_W), 0) % H_FOX
        lh = lax.broadcasted_iota(jnp.int32, (nrow, FOX_W), 1) // HEAD_DIM
        om = jnp.where(rh == lh, o, 0.0)
        o_ref[0] = jnp.dot(sel_ref[...], om, precision=HIGHEST, preferred_element_type=F32)


def _fox_sample(layer, page_table, qm, cache_k4, cache_v4, cache_lft, knew, vnew, lfnew):
    nb, npages = page_table.shape
    page = cache_k4.shape[2]
    tnew = qm.shape[1] // H_FOX
    npg = PAGES_PER_STEP
    ngrp = npages // npg
    idx = jnp.arange(page)
    u = (idx[:, None] <= idx[None, :]).astype(F32)
    sel = (jnp.arange(tnew)[:, None] == (jnp.arange(tnew * H_FOX)[None, :] // H_FOX)).astype(F32)

    def pmap(j):
        def f(b, g, pt):
            return (layer, pt[b * npages + g * npg + j], 0, 0)
        return f

    def bmap(b, g, pt):
        return (b, 0, 0)

    def c2(b, g, pt):
        return (0, 0)

    in_specs = [pl.BlockSpec((1, tnew * H_FOX, FOX_W), bmap)]
    in_specs += [pl.BlockSpec((1, 1, page, FOX_W), pmap(j)) for j in range(npg)]
    in_specs += [pl.BlockSpec((1, 1, page, FOX_W), pmap(j)) for j in range(npg)]
    in_specs += [pl.BlockSpec((1, 1, H_FOX, page), pmap(j)) for j in range(npg)]
    in_specs += [
        pl.BlockSpec((1, page, FOX_W), bmap),
        pl.BlockSpec((1, page, FOX_W), bmap),
        pl.BlockSpec((1, H_FOX, page), bmap),
        pl.BlockSpec((page, page), c2),
        pl.BlockSpec((tnew, tnew * H_FOX), c2),
    ]
    grid_spec = pltpu.PrefetchScalarGridSpec(
        num_scalar_prefetch=1,
        grid=(nb, ngrp),
        in_specs=in_specs,
        out_specs=pl.BlockSpec((1, tnew, FOX_W), bmap),
        scratch_shapes=[
            pltpu.VMEM((tnew * H_FOX, 1), F32),
            pltpu.VMEM((tnew * H_FOX, 1), F32),
            pltpu.VMEM((tnew * H_FOX, FOX_W), F32),
            pltpu.VMEM((H_FOX, page), F32),
        ],
    )
    args = [page_table.reshape(-1), qm] + [cache_k4] * npg + [cache_v4] * npg + [cache_lft] * npg
    args += [knew, vnew, lfnew, u, sel]
    return pl.pallas_call(
        functools.partial(_fox_sample_kernel, npg, tnew),
        grid_spec=grid_spec,
        out_shape=jax.ShapeDtypeStruct((nb, tnew, FOX_W), F32),
        compiler_params=_params("arbitrary", "arbitrary"),
        name="fox_sample",
    )(*args)


_LANE_GROUP0 = N_EXPERTS


def _out_proj_kernel(n_prompt_tiles, x_ref, retp_ref, rets_ref, s5p_ref, s5s_ref, foxp_ref, foxs_ref, wo_ref, g_ref,
                     wr_ref, br_ref, tri_ref, x1_ref, h2_ref, info_ref, cnt_ref, carry):
    i = pl.program_id(0)

    @pl.when(i == 0)
    def _():
        carry[...] = jnp.zeros_like(carry)

    is_prompt = i < n_prompt_tiles
    ret = jnp.where(is_prompt, retp_ref[...], rets_ref[...])
    s5 = jnp.where(is_prompt, s5p_ref[...], s5s_ref[...])
    fox = jnp.where(is_prompt, foxp_ref[...], foxs_ref[...].astype(BF16))
    x1 = x_ref[...] + _dot(ret, wo_ref[0:RET_W, :])
    x1 = x1 + _dot(s5, wo_ref[RET_W:RET_W + S5_W, :])
    x1 = x1 + _dot(fox, wo_ref[RET_W + S5_W:, :])
    x1_ref[...] = x1
    ms = jnp.mean(x1 * x1, axis=-1, keepdims=True)
    h2 = x1 * lax.rsqrt(ms + RMS_EPS) * g_ref[...]
    h2_ref[...] = h2
    logits = jnp.dot(h2, wr_ref[...], precision=HIGHEST, preferred_element_type=F32) + br_ref[...]
    tm = logits.shape[0]
    lane = lax.broadcasted_iota(jnp.int32, (tm, LANES), 1)
    big = jnp.int32(LANES)
    gmask = (lane >= _LANE_GROUP0) & (lane < _LANE_GROUP0 + N_EXPERT_GROUPS)
    gl = jnp.where(gmask, logits, NEG_BIG)
    gmax = jnp.max(gl, axis=1, keepdims=True)
    gsum = jnp.sum(jnp.where(gmask, jnp.exp(gl - gmax), 0.0), axis=1, keepdims=True)
    g_w = 1.0 / gsum
    g_idx = jnp.min(jnp.where(gl == gmax, lane, big), axis=1, keepdims=True) - _LANE_GROUP0
    lo = g_idx * EXPERTS_PER_GROUP
    emask = (lane >= lo) & (lane < lo + EXPERTS_PER_GROUP)
    el = jnp.where(emask, logits, NEG_BIG)
    e1 = jnp.max(el, axis=1, keepdims=True)
    idx1 = jnp.min(jnp.where(el == e1, lane, big), axis=1, keepdims=True)
    esum = jnp.sum(jnp.where(emask, jnp.exp(el - e1), 0.0), axis=1, keepdims=True)
    el2 = jnp.where(lane == idx1, NEG_BIG, el)
    e2 = jnp.max(el2, axis=1, keepdims=True)
    idx2 = jnp.min(jnp.where(el2 == e2, lane, big), axis=1, keepdims=True)
    p1 = 1.0 / esum
    p2 = jnp.exp(e2 - e1) / esum
    gate1 = g_w * p1 / (p1 + p2)
    gate2 = g_w * p2 / (p1 + p2)
    oh1 = lane == idx1
    oh2 = lane == idx2
    a = jnp.where(oh1 | oh2, 1.0, 0.0)
    cum = _dot(tri_ref[...], a.astype(BF16))
    before = cum - a + carry[...]
    rank1 = jnp.sum(jnp.where(oh1, before, 0.0), axis=1, keepdims=True)
    rank2 = jnp.sum(jnp.where(oh2, before, 0.0), axis=1, keepdims=True)
    carry[...] = carry[...] + cum[tm - 1:tm, :]
    cnt_ref[...] = carry[...]
    info = jnp.where(lane == 0, idx1.astype(F32), 0.0)
    info = jnp.where(lane == 1, idx2.astype(F32), info)
    info = jnp.where(lane == 2, rank1, info)
    info = jnp.where(lane == 3, rank2, info)
    info = jnp.where(lane == 4, gate1, info)
    info = jnp.where(lane == 5, gate2, info)
    info_ref[...] = info


def _out_proj(x, ret_p, ret_s, s5_p, s5_s, fox_p, fox_s, w_out, g2, w_r, b_r, n_prompt_tiles):
    n, d = x.shape
    tm = TOKEN_TILE
    idx = jnp.arange(tm)
    tri = (idx[:, None] >= idx[None, :]).astype(BF16)

    def row(i):
        return (i, 0)

    def const(i):
        return (0, 0)

    def pmap(i):
        return (jnp.minimum(i, n_prompt_tiles - 1), 0)

    def smap(i):
        return (jnp.maximum(i - n_prompt_tiles, 0), 0)

    return pl.pallas_call(
        functools.partial(_out_proj_kernel, n_prompt_tiles),
        grid=(n // tm,),
        in_specs=[
            pl.BlockSpec((tm, d), row),
            pl.BlockSpec((tm, RET_W), pmap),
            pl.BlockSpec((tm, RET_W), smap),
            pl.BlockSpec((tm, S5_W), pmap),
            pl.BlockSpec((tm, S5_W), smap),
            pl.BlockSpec((tm, FOX_W), pmap),
            pl.BlockSpec((tm, FOX_W), smap),
            pl.BlockSpec((d, d), const),
            pl.BlockSpec((1, d), const),
            pl.BlockSpec((d, LANES), const),
            pl.BlockSpec((1, LANES), const),
            pl.BlockSpec((tm, tm), const),
        ],
        out_specs=(
            pl.BlockSpec((tm, d), row),
            pl.BlockSpec((tm, d), row),
            pl.BlockSpec((tm, LANES), row),
            pl.BlockSpec((1, LANES), const),
        ),
        out_shape=(
            jax.ShapeDtypeStruct((n, d), F32),
            jax.ShapeDtypeStruct((n, d), F32),
            jax.ShapeDtypeStruct((n, LANES), F32),
            jax.ShapeDtypeStruct((1, LANES), F32),
        ),
        scratch_shapes=[pltpu.VMEM((1, LANES), F32)],
        compiler_params=_params("arbitrary"),
        name="out_proj_router",
    )(x, ret_p, ret_s, s5_p, s5_s, fox_p, fox_s, w_out, g2, w_r, b_r, tri)


def _dispatch_kernel(dest_ref, h_ref, rows_in_ref, rows_ref, sem):
    del rows_in_ref
    i = pl.program_id(0)
    tm = h_ref.shape[0]

    def row_copy(r, d):
        return pltpu.make_async_copy(h_ref.at[pl.ds(r, 1)], rows_ref.at[pl.ds(d, 1)], sem)

    def issue(r, _):
        t = i * tm + r
        row_copy(r, dest_ref[2 * t]).start()
        row_copy(r, dest_ref[2 * t + 1]).start()
        return 0

    lax.fori_loop(0, tm, issue, 0)

    def drain(r, _):
        row_copy(0, 0).wait()
        row_copy(0, 0).wait()
        return 0

    lax.fori_loop(0, tm, drain, 0)


def _dispatch(dest, h2, n_rows):
    n, d = h2.shape
    tm = TOKEN_TILE
    grid_spec = pltpu.PrefetchScalarGridSpec(
        num_scalar_prefetch=1,
        grid=(n // tm,),
        in_specs=[
            pl.BlockSpec((tm, d), lambda i, dest: (i, 0)),
            pl.BlockSpec(memory_space=pl.ANY),
        ],
        out_specs=pl.BlockSpec(memory_space=pl.ANY),
        scratch_shapes=[pltpu.SemaphoreType.DMA(())],
    )
    return pl.pallas_call(
        _dispatch_kernel,
        grid_spec=grid_spec,
        out_shape=jax.ShapeDtypeStruct((n_rows, d), F32),
        input_output_aliases={2: 0},
        compiler_params=_params("arbitrary"),
        name="moe_dispatch",
    )(dest, h2, jnp.zeros((n_rows, d), F32))


def _experts_kernel(be_ref, nu_ref, x_ref, w1_ref, w3_ref, w2_ref, y_ref, w1b, w3b, w2b):
    i = pl.program_id(0)
    prev = be_ref[jnp.maximum(i - 1, 0)]
    fresh = (i == 0) | (be_ref[i] != prev)
    active = i < nu_ref[0]

    @pl.when(active & fresh)
    def _():
        w1b[...] = w1_ref[0, 0].astype(BF16)
        w3b[...] = w3_ref[0, 0].astype(BF16)
        w2b[...] = w2_ref[0, 0].astype(BF16)

    @pl.when(active)
    def _():
        xb = x_ref[...].astype(BF16)
        h1 = _dot(xb, w1b[...])
        h3 = _dot(xb, w3b[...])
        a = (h1 * _sigmoid(h1) * h3).astype(BF16)
        y_ref[...] = _dot(a, w2b[...])

    @pl.when(jnp.logical_not(active))
    def _():
        y_ref[...] = jnp.zeros_like(y_ref)


def _experts(layer, blk_e, n_used, x_rows, w1, w3, w2):
    n_rows, d = x_rows.shape
    de = w1.shape[3]
    nblk = n_rows // MOE_ROWS

    def xmap(i, be, nu):
        return (jnp.minimum(i, nu[0] - 1), 0)

    def wmap(i, be, nu):
        return (layer, be[jnp.minimum(i, nu[0] - 1)], 0, 0)

    grid_spec = pltpu.PrefetchScalarGridSpec(
        num_scalar_prefetch=2,
        grid=(nblk,),
        in_specs=[
            pl.BlockSpec((MOE_ROWS, d), xmap),
            pl.BlockSpec((1, 1, d, de), wmap),
            pl.BlockSpec((1, 1, d, de), wmap),
            pl.BlockSpec((1, 1, de, d), wmap),
        ],
        out_specs=pl.BlockSpec((MOE_ROWS, d), lambda i, be, nu: (i, 0)),
        scratch_shapes=[
            pltpu.VMEM((d, de), BF16),
            pltpu.VMEM((d, de), BF16),
            pltpu.VMEM((de, d), BF16),
        ],
    )
    return pl.pallas_call(
        _experts_kernel,
        grid_spec=grid_spec,
        out_shape=jax.ShapeDtypeStruct((n_rows, d), F32),
        compiler_params=_params("arbitrary"),
        name="moe_experts",
    )(blk_e, n_used, x_rows, w1, w3, w2)


def _combine_kernel(final, dest_ref, x1_ref, info_ref, gf_ref, y_ref, o_ref, buf, sem):
    i = pl.program_id(0)
    tm = x1_ref.shape[0]

    def row_copy(k, r, d):
        return pltpu.make_async_copy(y_ref.at[pl.ds(d, 1)], buf.at[k, pl.ds(r, 1)], sem)

    def issue(r, _):
        t = i * tm + r
        row_copy(0, r, dest_ref[2 * t]).start()
        row_copy(1, r, dest_ref[2 * t + 1]).start()
        return 0

    lax.fori_loop(0, tm, issue, 0)

    def drain(r, _):
        row_copy(0, 0, 0).wait()
        row_copy(1, 0, 0).wait()
        return 0

    lax.fori_loop(0, tm, drain, 0)
    info = info_ref[...]
    x2 = x1_ref[...] + info[:, 4:5] * buf[0] + info[:, 5:6] * buf[1]
    if final:
        ms = jnp.mean(x2 * x2, axis=-1, keepdims=True)
        x2 = x2 * lax.rsqrt(ms + RMS_EPS) * gf_ref[...]
    o_ref[...] = x2


def _combine(dest, x1, info, gf, y_rows, final):
    n, d = x1.shape
    tm = TOKEN_TILE
    grid_spec = pltpu.PrefetchScalarGridSpec(
        num_scalar_prefetch=1,
        grid=(n // tm,),
        in_specs=[
            pl.BlockSpec((tm, d), lambda i, dest: (i, 0)),
            pl.BlockSpec((tm, LANES), lambda i, dest: (i, 0)),
            pl.BlockSpec((1, d), lambda i, dest: (0, 0)),
            pl.BlockSpec(memory_space=pl.ANY),
        ],
        out_specs=pl.BlockSpec((tm, d), lambda i, dest: (i, 0)),
        scratch_shapes=[
            pltpu.VMEM((2, tm, d), F32),
            pltpu.SemaphoreType.DMA(()),
        ],
    )
    return pl.pallas_call(
        functools.partial(_combine_kernel, final),
        grid_spec=grid_spec,
        out_shape=jax.ShapeDtypeStruct((n, d), F32),
        compiler_params=_params("arbitrary"),
        name="moe_combine",
    )(dest, x1, info, gf, y_rows)


def _moe_plan(info, counts, n_tokens):
    eid = info[:, 0:2].astype(jnp.int32)
    rank = info[:, 2:4].astype(jnp.int32)
    cnt = counts[0, :N_EXPERTS].astype(jnp.int32)
    padded = (cnt + MOE_ROWS - 1) // MOE_ROWS * MOE_ROWS
    pend = jnp.cumsum(padded)
    pstart = pend - padded
    dest = (pstart[eid] + rank).reshape(-1)
    nblk = -(-n_tokens * 2 // MOE_ROWS) + N_EXPERTS
    blk_e = jnp.minimum(
        jnp.searchsorted(pend, jnp.arange(nblk, dtype=jnp.int32) * MOE_ROWS, side='right'), N_EXPERTS - 1)
    n_used = (pend[-1:] // MOE_ROWS).astype(jnp.int32)
    return dest.astype(jnp.int32), blk_e.astype(jnp.int32), n_used, nblk * MOE_ROWS


def _rope_tables(seq, past, tnew, tile):
    half = HEAD_DIM // 2
    inv = ROPE_THETA ** (-jnp.arange(half, dtype=F32) / half)
    pos = jnp.concatenate([jnp.arange(seq, dtype=F32), past + (jnp.arange(tile) % tnew).astype(F32)])
    ang = pos[:, None] * inv[None, :]
    cos = jnp.cos(ang)
    sin = jnp.sin(ang)
    cos_h = jnp.concatenate([cos, cos], axis=1)
    sin_h = jnp.concatenate([-sin, sin], axis=1)
    return jnp.tile(cos_h, (1, H_RET)), jnp.tile(sin_h, (1, H_RET))


def _block_diag_states(s):
    b = s.shape[0]
    eye = jnp.eye(H_RET, dtype=s.dtype)
    return jnp.einsum('bhde,hg->bhdge', s, eye).reshape(b, RET_W, RET_W)


def _diag_blocks(s_bd):
    b = s_bd.shape[0]
    s5 = s_bd.reshape(b, H_RET, HEAD_DIM, H_RET, HEAD_DIM)
    return jnp.stack([s5[:, h, :, h, :] for h in range(H_RET)], axis=1)


def kernel(x_prompt, x_sample, cache_k, cache_v, cache_logf, state_ret, state_s5, page_table, norm1, w_in, ret_gn,
           s5_lam_re, s5_lam_im, s5_b_re, s5_b_im, s5_c_re, s5_c_im, s5_d, s5_log_dt, s5_glu_w, s5_glu_b, fox_fb,
           w_out, norm2, w_rg, b_rg, w_re, b_re, w1, w3, w2, norm_f):
    bp, seq, d = x_prompt.shape
    bd_, tnew, _ = x_sample.shape
    depth = w_in.shape[0]
    n_phys, page = cache_k.shape[1], cache_k.shape[2]
    npages = page_table.shape[1]
    past = npages * page
    n_p = bp * seq
    n_s = bd_ * tnew
    n = n_p + n_s
    tm = TOKEN_TILE
    assert n_p % tm == 0 and n_s == tm and seq % tm == 0
    n_prompt_tiles = n_p // tm

    x = jnp.concatenate([x_prompt.reshape(n_p, d), x_sample.reshape(n_s, d)], axis=0)
    cos_t, sin_t = _rope_tables(seq, past, tnew, tm)
    cache_k4 = cache_k.reshape(depth, n_phys, page, FOX_W)
    cache_v4 = cache_v.reshape(depth, n_phys, page, FOX_W)
    cache_lft = jnp.swapaxes(cache_logf, 2, 3)
    ff0 = w_in.shape[2] - H_FOX

    outs = {k: [] for k in ('kp', 'vp', 'lp', 'ks', 'vs', 'ls', 'rp', 'rs', 'sp', 'ss')}
    y = None
    for l in range(depth):
        w_main = w_in[l, :, :ff0].astype(BF16)
        w_ff = jnp.pad(w_in[l, :, ff0:], ((0, 0), (0, LANES - H_FOX))).astype(BF16)
        fb = jnp.pad(fox_fb[l], (0, LANES - H_FOX))[None, :]
        ret_in, su, fq, fk, fv, fkb, fvb, logf = _in_proj(
            x, norm1[l][None, :], w_main, w_ff, fb, cos_t, sin_t, n_prompt_tiles, seq // tm)

        gw = ret_gn[l][None, :]
        zero_state = jnp.zeros((bp, RET_W, RET_W), F32)
        ret_p, rs_p = _retention(ret_in, zero_state, gw, bp, 0, seq, 1, min(RET_CHUNK, seq), min(1024, seq))
        seq_per_chunk = 8
        chunk_rows = seq_per_chunk * tnew
        ret_s, rs_s = _retention(ret_in, _block_diag_states(state_ret[l]), gw, bd_ // seq_per_chunk, n_p,
                                 chunk_rows, seq_per_chunk, tnew, chunk_rows)

        tables = _s5_tables(s5_lam_re[l], s5_lam_im[l], s5_b_re[l], s5_b_im[l], s5_c_re[l], s5_c_im[l],
                            s5_log_dt[l])
        dsk = s5_d[l][None, :]
        gluw = s5_glu_w[l].astype(BF16)
        glub = s5_glu_b[l][None, :]
        zs = jnp.zeros((bp, 1, S5_LANES), F32)
        s5_p, sp_r, sp_i = _s5(su, zs, zs, tables, dsk, gluw, glub, bp, 0, seq, min(512, seq), False)
        x0 = state_s5[l].reshape(bd_, 1, S5_LANES, 2)
        s5_s, ss_r, ss_i = _s5(su, x0[..., 0], x0[..., 1], tables, dsk, gluw, glub, 1, n_p, n_s, n_s, True)

        c = _cumsum_logf(logf, bp, seq, min(512, seq))
        nh = _FOX_HEADS_PER_STEP
        ct = jnp.transpose(c[:, :H_FOX].reshape(bp, seq, H_FOX // nh, nh), (0, 2, 3, 1))
        fox_p = _fox_prompt(fq, fkb, fvb, c, ct, bp, seq, n_p)
        q_s = fq[n_p:].reshape(bd_, tnew, 1, FOX_W)
        head_of_lane = jnp.arange(FOX_W) // HEAD_DIM
        hmask = (jnp.arange(H_FOX)[:, None] == head_of_lane[None, :])[None, None]
        qm = jnp.where(hmask, q_s, jnp.zeros((), BF16)).reshape(bd_, tnew * H_FOX, FOX_W)
        pad_rows = ((0, 0), (0, page - tnew), (0, 0))
        knew = jnp.pad(fk[n_p:].reshape(bd_, tnew, FOX_W), pad_rows)
        vnew = jnp.pad(fv[n_p:].reshape(bd_, tnew, FOX_W), pad_rows)
        lfnew = jnp.pad(jnp.swapaxes(logf[n_p:, :H_FOX].reshape(bd_, tnew, H_FOX), 1, 2),
                        ((0, 0), (0, 0), (0, page - tnew)))
        fox_s = _fox_sample(l, page_table, qm, cache_k4, cache_v4, cache_lft, knew, vnew, lfnew)
        fox_s = fox_s.reshape(n_s, FOX_W)

        w_r = jnp.pad(jnp.concatenate([w_re[l], w_rg[l]], axis=1),
                      ((0, 0), (0, LANES - N_EXPERTS - N_EXPERT_GROUPS)))
        b_r = jnp.pad(jnp.concatenate([b_re[l], b_rg[l]]), (0, LANES - N_EXPERTS - N_EXPERT_GROUPS))[None, :]
        x1, h2, info, counts = _out_proj(x, ret_p, ret_s, s5_p, s5_s, fox_p, fox_s, w_out[l].astype(BF16),
                                         norm2[l][None, :], w_r, b_r, n_prompt_tiles)
        dest, blk_e, n_used, n_rows = _moe_plan(info, counts, n)
        x_rows = _dispatch(dest, h2, n_rows)
        y_rows = _experts(l, blk_e, n_used, x_rows, w1, w3, w2)
        final = l == depth - 1
        x = _combine(dest, x1, info, norm_f[None, :], y_rows, final)

        outs['kp'].append(fk[:n_p].reshape(bp, seq, H_FOX, HEAD_DIM))
        outs['vp'].append(fv[:n_p].reshape(bp, seq, H_FOX, HEAD_DIM))
        outs['lp'].append(logf[:n_p, :H_FOX].reshape(bp, seq, H_FOX))
        outs['ks'].append(fk[n_p:].reshape(bd_, tnew, H_FOX, HEAD_DIM))
        outs['vs'].append(fv[n_p:].reshape(bd_, tnew, H_FOX, HEAD_DIM))
        outs['ls'].append(logf[n_p:, :H_FOX].reshape(bd_, tnew, H_FOX))
        outs['rp'].append(_diag_blocks(rs_p))
        outs['rs'].append(_diag_blocks(rs_s))
        outs['sp'].append(jnp.stack([sp_r.reshape(bp, S5_GROUPS, S5_STATE),
                                     sp_i.reshape(bp, S5_GROUPS, S5_STATE)], axis=-1))
        outs['ss'].append(jnp.stack([ss_r.reshape(bd_, S5_GROUPS, S5_STATE),
                                     ss_i.reshape(bd_, S5_GROUPS, S5_STATE)], axis=-1))

    y_prompt = x[:n_p].reshape(bp, seq, d)
    y_sample = x[n_p:].reshape(bd_, tnew, d)
    st = {k: jnp.stack(v) for k, v in outs.items()}
    return (y_prompt, y_sample, st['kp'], st['vp'], st['lp'], st['ks'], st['vs'], st['ls'],
            st['rp'], st['rs'], st['sp'], st['ss'])
```

```python
import functools
import math

import numpy as np

import jax
import jax.numpy as jnp
from jax import lax
from jax.experimental import pallas as pl
from jax.experimental.pallas import tpu as pltpu

F32 = jnp.float32
BF16 = jnp.bfloat16
HIGHEST = lax.Precision.HIGHEST

HEAD_DIM = 64
H_RET = 4
H_FOX = 8
RET_W = H_RET * HEAD_DIM
S5_W = 256
FOX_W = H_FOX * HEAD_DIM
S5_GROUPS = 16
S5_GROUP_CH = 16
S5_STATE = 64
S5_LANES = S5_GROUPS * S5_STATE
N_EXPERTS = 32
N_EXPERT_GROUPS = 4
EXPERTS_PER_GROUP = 8
RET_CHUNK = 128
ROPE_THETA = 10000.0
RMS_EPS = 1e-6
NEG_BIG = -1e30
LOG2E = math.log2(math.e)

LANES = 128
SUBLANES = 8
TOKEN_TILE = 256
MOE_ROWS = 256
FOX_TQ = 512
PAGES_PER_STEP = 8
VMEM_LIMIT = 48 * 1024 * 1024


def _params(*sem):
    return pltpu.CompilerParams(dimension_semantics=sem, vmem_limit_bytes=VMEM_LIMIT)


def _sigmoid(x):
    return 1.0 / (1.0 + jnp.exp(-x))


def _dot(a, b):
    return jnp.dot(a, b, preferred_element_type=F32)


def _dot_nt(a, b):
    return lax.dot_general(a, b, (((1,), (1,)), ((), ())), preferred_element_type=F32)


def _dot_tn(a, b):
    return lax.dot_general(a, b, (((0,), (0,)), ((), ())), preferred_element_type=F32)


class _MatmulMode:
    def __init__(self, exact):
        self.dtype = F32 if exact else BF16
        self.precision = HIGHEST if exact else None

    def cast(self, x):
        return x.astype(self.dtype)

    def dot(self, a, b):
        return jnp.dot(a, b, precision=self.precision, preferred_element_type=F32)

    def dot_nt(self, a, b):
        return lax.dot_general(a, b, (((1,), (1,)), ((), ())), precision=self.precision,
                               preferred_element_type=F32)

    def dot_tn(self, a, b):
        return lax.dot_general(a, b, (((0,), (0,)), ((), ())), precision=self.precision,
                               preferred_element_type=F32)


_FAST = _MatmulMode(False)
_EXACT = _MatmulMode(True)


def _in_proj_kernel(mm, x_ref, g_ref, w_ref, wff_ref, fb_ref, cos_ref, sin_ref,
                    ret_ref, su_ref, fq_ref, fk_ref, fv_ref, fkb_ref, fvb_ref, logf_ref):
    _dot = mm.dot
    x = x_ref[...]
    ms = jnp.mean(x * x, axis=-1, keepdims=True)
    h = mm.cast(x * lax.rsqrt(ms + RMS_EPS) * g_ref[...])
    tm = x.shape[0]
    cos = cos_ref[...]
    sin = sin_ref[...]
    lane = lax.broadcasted_iota(jnp.int32, (tm, RET_W), 1)
    first_half = (lane % HEAD_DIM) < (HEAD_DIM // 2)

    def rope(t):
        swapped = jnp.where(first_half,
                            pltpu.roll(t, RET_W - HEAD_DIM // 2, axis=1),
                            pltpu.roll(t, HEAD_DIM // 2, axis=1))
        return t * cos + swapped * sin

    ret_ref[:, 0:RET_W] = rope(_dot(h, w_ref[:, 0:RET_W]))
    ret_ref[:, RET_W:2 * RET_W] = rope(_dot(h, w_ref[:, RET_W:2 * RET_W])) * (HEAD_DIM ** -0.5)
    ret_ref[:, 2 * RET_W:4 * RET_W] = _dot(h, w_ref[:, 2 * RET_W:4 * RET_W])
    c0 = 4 * RET_W
    su_ref[...] = _dot(h, w_ref[:, c0:c0 + S5_W])
    c0 += S5_W
    fq_ref[...] = (_dot(h, w_ref[:, c0:c0 + FOX_W]) * (HEAD_DIM ** -0.5 * LOG2E)).astype(BF16)
    c0 += FOX_W
    fk = _dot(h, w_ref[:, c0:c0 + FOX_W])
    fk_ref[...] = fk
    fkb_ref[...] = fk.astype(BF16)
    c0 += FOX_W
    fv = _dot(h, w_ref[:, c0:c0 + FOX_W])
    fv_ref[...] = fv
    fvb_ref[...] = fv.astype(BF16)
    z = _dot(h, wff_ref[...]) + fb_ref[...]
    logf = jnp.minimum(z, 0.0) - jnp.log1p(jnp.exp(-jnp.abs(z)))
    lane128 = lax.broadcasted_iota(jnp.int32, (tm, LANES), 1)
    logf_ref[...] = jnp.where(lane128 < H_FOX, logf, 0.0)


def _in_proj(mm, x, g, w_main, w_ff, fb, cos_t, sin_t, tile0, ntiles, pos_tile0, pos_tiles):
    tm = TOKEN_TILE
    n = ntiles * tm
    wcols = w_main.shape[1]

    def xrow(i):
        return (tile0 + i, 0)

    def row(i):
        return (i, 0)

    def const(i):
        return (0, 0)

    def pos_map(i):
        return (pos_tile0 + i % pos_tiles, 0)

    outs = (
        jax.ShapeDtypeStruct((n, 4 * RET_W), F32),
        jax.ShapeDtypeStruct((n, S5_W), F32),
        jax.ShapeDtypeStruct((n, FOX_W), BF16),
        jax.ShapeDtypeStruct((n, FOX_W), F32),
        jax.ShapeDtypeStruct((n, FOX_W), F32),
        jax.ShapeDtypeStruct((n, FOX_W), BF16),
        jax.ShapeDtypeStruct((n, FOX_W), BF16),
        jax.ShapeDtypeStruct((n, LANES), F32),
    )
    return pl.pallas_call(
        functools.partial(_in_proj_kernel, mm),
        grid=(ntiles,),
        in_specs=[
            pl.BlockSpec((tm, x.shape[1]), xrow),
            pl.BlockSpec((1, x.shape[1]), const),
            pl.BlockSpec((x.shape[1], wcols), const),
            pl.BlockSpec((x.shape[1], LANES), const),
            pl.BlockSpec((1, LANES), const),
            pl.BlockSpec((tm, RET_W), pos_map),
            pl.BlockSpec((tm, RET_W), pos_map),
        ],
        out_specs=(
            pl.BlockSpec((tm, 4 * RET_W), row),
            pl.BlockSpec((tm, S5_W), row),
            pl.BlockSpec((tm, FOX_W), row),
            pl.BlockSpec((tm, FOX_W), row),
            pl.BlockSpec((tm, FOX_W), row),
            pl.BlockSpec((tm, FOX_W), row),
            pl.BlockSpec((tm, FOX_W), row),
            pl.BlockSpec((tm, LANES), row),
        ),
        out_shape=outs,
        compiler_params=_params("arbitrary"),
        name="in_proj",
    )(x, g, w_main, w_ff, fb, cos_t, sin_t)


def _retention_kernel(mm, nseq, seq_len, nchunk, x_ref, s0_ref, dm_ref, qd_ref, kd_ref, cd_ref, bd_ref,
                      gn_ref, gw_ref, o_ref, sout_ref, s_scr):
    cast, _dot, _dot_nt, _dot_tn = mm.cast, mm.dot, mm.dot_nt, mm.dot_tn
    step = pl.program_id(1)
    rows = nseq * seq_len

    @pl.when(step == 0)
    def _():
        s_scr[...] = s0_ref[...]

    lane_head = lax.broadcasted_iota(jnp.int32, (rows, RET_W), 1) // HEAD_DIM
    row_seq = lax.broadcasted_iota(jnp.int32, (rows, 1), 0) // seq_len

    def chunk(c, _):
        r0 = pl.multiple_of(c * rows, rows)
        q = x_ref[pl.ds(r0, rows), 0:RET_W]
        k = x_ref[pl.ds(r0, rows), RET_W:2 * RET_W]
        v = x_ref[pl.ds(r0, rows), 2 * RET_W:3 * RET_W]
        g = x_ref[pl.ds(r0, rows), 3 * RET_W:4 * RET_W]
        qb = cast(q)
        kb = cast(k)
        vb = cast(v)
        ps = []
        vs = []
        for h in range(H_RET):
            hm = lane_head == h
            sc = _dot_nt(cast(jnp.where(hm, q, 0.0)), kb)
            ps.append(cast(sc * dm_ref[h]))
            vs.append(cast(jnp.where(hm, v, 0.0)))
        inner = _dot(jnp.concatenate(ps, axis=1), jnp.concatenate(vs, axis=0))
        kdec = k * kd_ref[...]
        cd = cd_ref[...]
        bd = bd_ref[...]
        if nseq == 1:
            s_old = s_scr[0]
            cross = _dot(qb, cast(s_old))
            s_scr[0] = s_old * cd + _dot_tn(cast(kdec), vb) * bd
        else:
            def seq_body(s, cross):
                rm = row_seq == s
                s_old = s_scr[s]
                cross = jnp.where(rm, _dot(qb, cast(s_old)), cross)
                kv = _dot_tn(cast(jnp.where(rm, kdec, 0.0)), vb)
                s_scr[s] = s_old * cd + kv * bd
                return cross

            cross = lax.fori_loop(0, nseq, seq_body, jnp.zeros((rows, RET_W), F32))
        o = inner + cross * qd_ref[...]
        o2 = o * o
        hi = o2.astype(BF16)
        lo = (o2 - hi.astype(F32)).astype(BF16)
        ms = jnp.dot(hi, gn_ref[...], preferred_element_type=F32) + jnp.dot(
            lo, gn_ref[...], preferred_element_type=F32)
        on = o * lax.rsqrt(ms + RMS_EPS)
        o_ref[pl.ds(r0, rows), :] = (g * _sigmoid(g) * on * gw_ref[...]).astype(o_ref.dtype)
        return 0

    lax.fori_loop(0, nchunk, chunk, 0)

    @pl.when(step == pl.num_programs(1) - 1)
    def _():
        sout_ref[...] = s_scr[...]


def _retention_tables(nseq, seq_len):
    rows = nseq * seq_len
    log_gamma = jnp.log1p(-jnp.power(2.0, -5.0 - jnp.arange(H_RET, dtype=F32)))
    idx = jnp.arange(rows, dtype=jnp.int32)
    pos = (idx % seq_len).astype(F32)
    seq = idx // seq_len
    diff = pos[:, None] - pos[None, :]
    ok = (seq[:, None] == seq[None, :]) & (diff >= 0)
    dm = jnp.where(ok[None], jnp.exp(log_gamma[:, None, None] * jnp.maximum(diff, 0.0)[None]), 0.0)
    lg_lane = jnp.repeat(log_gamma, HEAD_DIM)
    qd = jnp.exp((pos[:, None] + 1.0) * lg_lane[None, :])
    kd = jnp.exp((seq_len - 1.0 - pos[:, None]) * lg_lane[None, :])
    cd = jnp.exp(seq_len * lg_lane)[None, :]
    head = jnp.arange(RET_W) // HEAD_DIM
    bd = (head[:, None] == head[None, :]).astype(F32)
    gn = (bd / HEAD_DIM).astype(BF16)
    return dm, qd, kd, cd, bd, gn


def _retention(mm, ret_in, s0_bd, gw, nbatch, row0, seq_total, nseq, seq_len, rows_per_step):
    rows = nseq * seq_len
    nchunk = rows_per_step // rows
    steps = seq_total // rows_per_step
    blk0 = row0 // rows_per_step
    dm, qd, kd, cd, bd, gn = _retention_tables(nseq, seq_len)

    def rmap(b, i):
        return (blk0 + b * steps + i, 0)

    def omap(b, i):
        return (b * steps + i, 0)

    def c2(b, i):
        return (0, 0)

    def c3(b, i):
        return (0, 0, 0)

    def smap(b, i):
        return (b, 0, 0)

    in_specs = [
        pl.BlockSpec((rows_per_step, 4 * RET_W), rmap),
        pl.BlockSpec((nseq, RET_W, RET_W), smap),
        pl.BlockSpec((H_RET, rows, rows), c3),
        pl.BlockSpec((rows, RET_W), c2),
        pl.BlockSpec((rows, RET_W), c2),
        pl.BlockSpec((1, RET_W), c2),
        pl.BlockSpec((RET_W, RET_W), c2),
        pl.BlockSpec((RET_W, RET_W), c2),
        pl.BlockSpec((1, RET_W), c2),
    ]
    return pl.pallas_call(
        functools.partial(_retention_kernel, mm, nseq, seq_len, nchunk),
        grid=(nbatch, steps),
        in_specs=in_specs,
        out_specs=(
            pl.BlockSpec((rows_per_step, RET_W), omap),
            pl.BlockSpec((nseq, RET_W, RET_W), smap),
        ),
        out_shape=(
            jax.ShapeDtypeStruct((nbatch * seq_total, RET_W), mm.dtype),
            jax.ShapeDtypeStruct((nbatch * nseq, RET_W, RET_W), F32),
        ),
        scratch_shapes=[pltpu.VMEM((nseq, RET_W, RET_W), F32)],
        compiler_params=_params("arbitrary", "arbitrary"),
        name="retention",
    )(ret_in, s0_bd, dm, qd, kd, cd, bd, gn, gw)


def _s5_kernel(mm, short, u_ref, x0r_ref, x0i_ref, bre_ref, bim_ref, pre_ref, pim_ref, cm_ref, d_ref,
               gw_ref, gb_ref, o_ref, str_ref, sti_ref, bur, bui, car, cai):
    cast, _dot = mm.cast, mm.dot
    step = pl.program_id(1)
    rows = u_ref.shape[0]
    u = u_ref[...]
    ub = cast(u)
    bur[...] = _dot(ub, bre_ref[...])
    bui[...] = _dot(ub, bim_ref[...])

    if not short:
        @pl.when(step == 0)
        def _():
            car[...] = x0r_ref[0]
            cai[...] = x0i_ref[0]

    pr = pre_ref[...]
    pi = pim_ref[...]
    rowi = lax.broadcasted_iota(jnp.int32, (SUBLANES, S5_LANES), 0)

    def group(j, carry):
        cr, ci = carry
        r0 = pl.multiple_of(j * SUBLANES, SUBLANES)
        xr = bur[pl.ds(r0, SUBLANES), :]
        xi = bui[pl.ds(r0, SUBLANES), :]
        for s in (1, 2, 4):
            ar = pr[s - 1:s]
            ai = pi[s - 1:s]
            sr = jnp.where(rowi >= s, pltpu.roll(xr, s, axis=0), 0.0)
            si = jnp.where(rowi >= s, pltpu.roll(xi, s, axis=0), 0.0)
            xr, xi = xr + ar * sr - ai * si, xi + ar * si + ai * sr
        if short:
            cr = x0r_ref[j]
            ci = x0i_ref[j]
        xr, xi = xr + pr * cr - pi * ci, xi + pr * ci + pi * cr
        bur[pl.ds(r0, SUBLANES), :] = xr
        bui[pl.ds(r0, SUBLANES), :] = xi
        cr = xr[SUBLANES - 1:SUBLANES]
        ci = xi[SUBLANES - 1:SUBLANES]
        if short:
            str_ref[j] = cr
            sti_ref[j] = ci
        return cr, ci

    if short:
        init = (jnp.zeros((1, S5_LANES), F32), jnp.zeros((1, S5_LANES), F32))
    else:
        init = (car[...], cai[...])
    cr, ci = lax.fori_loop(0, rows // SUBLANES, group, init)
    if not short:
        car[...] = cr
        cai[...] = ci

        @pl.when(step == pl.num_programs(1) - 1)
        def _():
            str_ref[0] = cr
            sti_ref[0] = ci

    xs = jnp.concatenate([cast(bur[...]), cast(bui[...])], axis=1)
    y = _dot(xs, cm_ref[...]) + d_ref[...] * u
    yg = 0.5 * y * (1.0 + jnp.tanh(math.sqrt(2.0 / math.pi) * (y + 0.044715 * (y * y * y))))
    z = _dot(cast(yg), gw_ref[...]) + gb_ref[...]
    o_ref[...] = (yg * _sigmoid(z)).astype(o_ref.dtype)


def _s5_tables(lam_re, lam_im, b_re, b_im, c_re, c_im, log_dt):
    lam = lax.complex(lam_re.astype(F32), lam_im.astype(F32))
    dt = jnp.exp(log_dt.astype(F32))[:, None]
    lam_bar = jnp.exp(lam * dt)
    b_bar = ((lam_bar - 1.0) / lam)[:, :, None] * lax.complex(b_re.astype(F32), b_im.astype(F32))
    k = jnp.arange(1, SUBLANES + 1, dtype=F32)[:, None, None]
    powers = jnp.exp((lam * dt)[None] * k)
    pre = jnp.real(powers).reshape(SUBLANES, S5_LANES)
    pim = jnp.imag(powers).reshape(SUBLANES, S5_LANES)
    eye = jnp.eye(S5_GROUPS, dtype=F32)
    bre = jnp.einsum('gpc,gh->gchp', jnp.real(b_bar), eye).reshape(S5_W, S5_LANES)
    bim = jnp.einsum('gpc,gh->gchp', jnp.imag(b_bar), eye).reshape(S5_W, S5_LANES)
    cre = jnp.einsum('gcp,gh->hpgc', c_re.astype(F32), eye).reshape(S5_LANES, S5_W)
    cim = jnp.einsum('gcp,gh->hpgc', c_im.astype(F32), eye).reshape(S5_LANES, S5_W)
    cm = jnp.concatenate([cre, -cim], axis=0)
    return bre, bim, pre, pim, cm


def _s5(mm, su, x0r, x0i, tables, d, glu_w, glu_b, nbatch, row0, seq_total, rows_per_step, short):
    bre, bim, pre, pim, cm = tables
    bre, bim, cm, glu_w = mm.cast(bre), mm.cast(bim), mm.cast(cm), mm.cast(glu_w)
    steps = seq_total // rows_per_step
    blk0 = row0 // rows_per_step
    nstate = x0r.shape[0]
    sblk = nstate if short else 1

    def rmap(b, i):
        return (blk0 + b * steps + i, 0)

    def omap(b, i):
        return (b * steps + i, 0)

    def c2(b, i):
        return (0, 0)

    def smap(b, i):
        return (0 if short else b, 0, 0)

    in_specs = [
        pl.BlockSpec((rows_per_step, S5_W), rmap),
        pl.BlockSpec((sblk, 1, S5_LANES), smap),
        pl.BlockSpec((sblk, 1, S5_LANES), smap),
        pl.BlockSpec((S5_W, S5_LANES), c2),
        pl.BlockSpec((S5_W, S5_LANES), c2),
        pl.BlockSpec((SUBLANES, S5_LANES), c2),
        pl.BlockSpec((SUBLANES, S5_LANES), c2),
        pl.BlockSpec((2 * S5_LANES, S5_W), c2),
        pl.BlockSpec((1, S5_W), c2),
        pl.BlockSpec((S5_W, S5_W), c2),
        pl.BlockSpec((1, S5_W), c2),
    ]
    return pl.pallas_call(
        functools.partial(_s5_kernel, mm, short),
        grid=(nbatch, steps),
        in_specs=in_specs,
        out_specs=(
            pl.BlockSpec((rows_per_step, S5_W), omap),
            pl.BlockSpec((sblk, 1, S5_LANES), smap),
            pl.BlockSpec((sblk, 1, S5_LANES), smap),
        ),
        out_shape=(
            jax.ShapeDtypeStruct((nbatch * seq_total, S5_W), mm.dtype),
            jax.ShapeDtypeStruct((nstate, 1, S5_LANES), F32),
            jax.ShapeDtypeStruct((nstate, 1, S5_LANES), F32),
        ),
        scratch_shapes=[
            pltpu.VMEM((rows_per_step, S5_LANES), F32),
            pltpu.VMEM((rows_per_step, S5_LANES), F32),
            pltpu.VMEM((1, S5_LANES), F32),
            pltpu.VMEM((1, S5_LANES), F32),
        ],
        compiler_params=_params("arbitrary", "arbitrary"),
        name="s5",
    )(su, x0r, x0i, bre, bim, pre, pim, cm, d, glu_w, glu_b)


_PAIR_W = 2 * LANES
_AUG_PER_HEAD = 6
_N_PAIRS = H_FOX // 2


def _aug_placement():
    pq = np.zeros((3 * LANES, _N_PAIRS * LANES), np.float32)
    pk = np.zeros((3 * LANES, _N_PAIRS * LANES), np.float32)
    oq = np.zeros((1, _N_PAIRS * LANES), np.float32)
    ok = np.zeros((1, _N_PAIRS * LANES), np.float32)
    for h in range(H_FOX):
        base = (h // 2) * LANES + (h % 2) * _AUG_PER_HEAD
        for part in range(3):
            pq[part * LANES + h, base + part] = 1.0
            oq[0, base + 3 + part] = 1.0
            ok[0, base + part] = 1.0
            pk[part * LANES + h, base + 3 + part] = -1.0
    return (jnp.asarray(pq, BF16), jnp.asarray(pk, BF16), jnp.asarray(oq), jnp.asarray(ok))


def _split3(v):
    hi = v.astype(BF16)
    r1 = v - hi.astype(F32)
    mid = r1.astype(BF16)
    lo = (r1 - mid.astype(F32)).astype(BF16)
    return jnp.concatenate([hi, mid, lo], axis=1)


def _fox_prep_kernel(lf_ref, q_ref, k_ref, v_ref, tri_ref, pq_ref, pk_ref, oq_ref, ok_ref,
                     qa_ref, ka_ref, va_ref, carry):
    @pl.when(pl.program_id(1) == 0)
    def _():
        carry[...] = jnp.zeros_like(carry)

    c = jnp.dot(tri_ref[...], lf_ref[...], precision=HIGHEST, preferred_element_type=F32) + carry[...]
    rows = c.shape[0]
    carry[...] = c[rows - 1:rows, :]
    parts = _split3(c * LOG2E)
    augq = (_dot(parts, pq_ref[...]) + oq_ref[...]).astype(BF16)
    augk = (_dot(parts, pk_ref[...]) + ok_ref[...]).astype(BF16)
    lane = lax.broadcasted_iota(jnp.int32, (rows, LANES), 1)
    one = jnp.ones((rows, LANES), BF16)
    for p in range(_N_PAIRS):
        src = slice(p * LANES, (p + 1) * LANES)
        qa_ref[:, p * _PAIR_W:p * _PAIR_W + LANES] = q_ref[:, src]
        qa_ref[:, p * _PAIR_W + LANES:(p + 1) * _PAIR_W] = augq[:, src]
        ka_ref[:, p * _PAIR_W:p * _PAIR_W + LANES] = k_ref[:, src]
        ka_ref[:, p * _PAIR_W + LANES:(p + 1) * _PAIR_W] = augk[:, src]
        vp = v_ref[:, src]
        va_ref[:, p * _PAIR_W:p * _PAIR_W + LANES] = jnp.where(lane < HEAD_DIM, vp, one)
        va_ref[:, p * _PAIR_W + LANES:(p + 1) * _PAIR_W] = jnp.where(lane < HEAD_DIM, one, vp)


def _fox_prep(logf, fq, fkb, fvb, nbatch, seq_total, rows_per_step):
    steps = seq_total // rows_per_step
    idx = jnp.arange(rows_per_step)
    tri = (idx[:, None] >= idx[None, :]).astype(F32)
    pq, pk, oq, ok = _aug_placement()
    wide = _N_PAIRS * _PAIR_W

    def rmap(b, i):
        return (b * steps + i, 0)

    def const(b, i):
        return (0, 0)

    out = jax.ShapeDtypeStruct((nbatch * seq_total, wide), BF16)
    return pl.pallas_call(
        _fox_prep_kernel,
        grid=(nbatch, steps),
        in_specs=[
            pl.BlockSpec((rows_per_step, LANES), rmap),
            pl.BlockSpec((rows_per_step, FOX_W), rmap),
            pl.BlockSpec((rows_per_step, FOX_W), rmap),
            pl.BlockSpec((rows_per_step, FOX_W), rmap),
            pl.BlockSpec((rows_per_step, rows_per_step), const),
            pl.BlockSpec(pq.shape, const),
            pl.BlockSpec(pk.shape, const),
            pl.BlockSpec(oq.shape, const),
            pl.BlockSpec(ok.shape, const),
        ],
        out_specs=(pl.BlockSpec((rows_per_step, wide), rmap),) * 3,
        out_shape=(out, out, out),
        scratch_shapes=[pltpu.VMEM((1, LANES), F32)],
        compiler_params=_params("arbitrary", "arbitrary"),
        name="fox_prep",
    )(logf, fq, fkb, fvb, tri, pq, pk, oq, ok)


def _fox_prompt_kernel(tq, q_ref, k_ref, v_ref, o_ref, acc, m_scr):
    qi = pl.program_id(2)
    q = q_ref[...]
    lane = lax.broadcasted_iota(jnp.int32, (tq, _PAIR_W), 1)
    qs = []
    for par in range(2):
        own = ((lane >= par * HEAD_DIM) & (lane < (par + 1) * HEAD_DIM)) | (
            (lane >= LANES + par * _AUG_PER_HEAD) & (lane < LANES + (par + 1) * _AUG_PER_HEAD))
        qs.append(jnp.where(own, q, jnp.zeros_like(q)))
        m_scr[par] = jnp.full((tq, LANES), NEG_BIG, F32)
        acc[par] = jnp.zeros((tq, LANES), F32)
    causal = (lax.broadcasted_iota(jnp.int32, (tq, tq), 0) >= lax.broadcasted_iota(jnp.int32, (tq, tq), 1))

    def kv_step(j, masked):
        ks = pl.multiple_of(j * tq, tq)
        kb = k_ref[pl.ds(ks, tq), :]
        for par in range(2):
            s = _dot_nt(qs[par], kb)
            if masked:
                s = jnp.where(causal, s, NEG_BIG)
            m_old = m_scr[par]
            m_new = jnp.maximum(m_old, jnp.max(s, axis=1, keepdims=True))
            p = jnp.exp2(s - jnp.concatenate([m_new] * (tq // LANES), axis=1))
            vb = v_ref[pl.ds(ks, tq), par * LANES:(par + 1) * LANES]
            acc[par] = jnp.exp2(m_old - m_new) * acc[par] + _dot(p.astype(BF16), vb)
            m_scr[par] = m_new

    def body(j, _):
        kv_step(j, False)
        return 0

    lax.fori_loop(0, qi, body, 0)
    kv_step(qi, True)
    lane128 = lax.broadcasted_iota(jnp.int32, (tq, LANES), 1)
    outs = []
    for par in range(2):
        a = acc[par]
        outs.append(a / pltpu.roll(a, HEAD_DIM, axis=1))
    o_ref[...] = jnp.where(lane128 < HEAD_DIM, outs[0], outs[1]).astype(BF16)


def _fox_prompt(qa, ka, va, nbatch, seq_total):
    tq = min(FOX_TQ, seq_total)
    nq = seq_total // tq
    return pl.pallas_call(
        functools.partial(_fox_prompt_kernel, tq),
        grid=(nbatch, _N_PAIRS, nq),
        in_specs=[
            pl.BlockSpec((tq, _PAIR_W), lambda b, g, i: (b * nq + i, g)),
            pl.BlockSpec((seq_total, _PAIR_W), lambda b, g, i: (b, g)),
            pl.BlockSpec((seq_total, _PAIR_W), lambda b, g, i: (b, g)),
        ],
        out_specs=pl.BlockSpec((tq, LANES), lambda b, g, i: (b * nq + i, g)),
        out_shape=jax.ShapeDtypeStruct((nbatch * seq_total, FOX_W), BF16),
        scratch_shapes=[
            pltpu.VMEM((2, tq, LANES), F32),
            pltpu.VMEM((2, tq, LANES), F32),
        ],
        compiler_params=_params("arbitrary", "arbitrary", "arbitrary"),
        name="fox_prompt",
    )(qa, ka, va)


def _fox_sample_kernel(npg, tnew, pt_ref, qm_ref, *refs):
    del pt_ref
    k_refs = refs[0:npg]
    v_refs = refs[npg:2 * npg]
    lf_refs = refs[2 * npg:3 * npg]
    knew_ref, vnew_ref, lfnew_ref, u_ref, sel_ref, o_ref, m_scr, l_scr, acc, run, kb_scr, vb_scr = refs[3 * npg:]
    g = pl.program_id(1)
    nrow = tnew * H_FOX
    npage = knew_ref.shape[1]

    @pl.when(g == 0)
    def _():
        m_scr[...] = jnp.full(m_scr.shape, NEG_BIG, F32)
        l_scr[...] = jnp.zeros(l_scr.shape, F32)
        acc[...] = jnp.zeros(acc.shape, F32)
        run[...] = jnp.zeros(run.shape, F32)

    qm = qm_ref[0]

    def cum_within(lft):
        return jnp.dot(lft, u_ref[...], precision=HIGHEST, preferred_element_type=F32)

    def attend(kb, vb, ckeys, mask):
        s = _dot_nt(qm, kb) - jnp.concatenate([ckeys * LOG2E] * tnew, axis=0)
        if mask is not None:
            s = jnp.where(mask, s, NEG_BIG)
        m_old = m_scr[...]
        m_new = jnp.maximum(m_old, jnp.max(s, axis=1, keepdims=True))
        p = jnp.exp2(s - m_new)
        alpha = jnp.exp2(m_old - m_new)
        l_scr[...] = alpha * l_scr[...] + jnp.sum(p, axis=1, keepdims=True)
        acc[...] = alpha * acc[...] + _dot(p.astype(BF16), vb)
        m_scr[...] = m_new

    base = run[...]
    cs = []
    for j in range(npg):
        kb_scr[j * npage:(j + 1) * npage, :] = k_refs[j][0, 0].astype(BF16)
        vb_scr[j * npage:(j + 1) * npage, :] = v_refs[j][0, 0].astype(BF16)
        cw = cum_within(lf_refs[j][0, 0])
        cs.append(base + cw)
        base = base + jnp.broadcast_to(cw[:, npage - 1:npage], base.shape)
    run[...] = base
    attend(kb_scr[...], vb_scr[...], jnp.concatenate(cs, axis=1), None)

    @pl.when(g == pl.num_programs(1) - 1)
    def _():
        rt = lax.broadcasted_iota(jnp.int32, (nrow, npage), 0) // H_FOX
        col = lax.broadcasted_iota(jnp.int32, (nrow, npage), 1)
        attend(knew_ref[0].astype(BF16), vnew_ref[0].astype(BF16), run[...] + cum_within(lfnew_ref[0]), col <= rt)
        o = acc[...] / l_scr[...]
        rh = lax.broadcasted_iota(jnp.int32, (nrow, FOX_W), 0) % H_FOX
        lh = lax.broadcasted_iota(jnp.int32, (nrow, FOX_W), 1) // HEAD_DIM
        om = jnp.where(rh == lh, o, 0.0)
        o_ref[0] = jnp.dot(sel_ref[...], om, precision=HIGHEST, preferred_element_type=F32)


def _fox_sample(layer, page_table, qm, cache_k4, cache_v4, cache_lft, knew, vnew, lfnew):
    nb, npages = page_table.shape
    page = cache_k4.shape[2]
    tnew = qm.shape[1] // H_FOX
    npg = PAGES_PER_STEP
    ngrp = npages // npg
    idx = jnp.arange(page)
    u = (idx[:, None] <= idx[None, :]).astype(F32)
    sel = (jnp.arange(tnew)[:, None] == (jnp.arange(tnew * H_FOX)[None, :] // H_FOX)).astype(F32)

    def pmap(j):
        def f(b, g, pt):
            return (layer, pt[b * npages + g * npg + j], 0, 0)
        return f

    def bmap(b, g, pt):
        return (b, 0, 0)

    def c2(b, g, pt):
        return (0, 0)

    in_specs = [pl.BlockSpec((1, tnew * H_FOX, FOX_W), bmap)]
    in_specs += [pl.BlockSpec((1, 1, page, FOX_W), pmap(j)) for j in range(npg)]
    in_specs += [pl.BlockSpec((1, 1, page, FOX_W), pmap(j)) for j in range(npg)]
    in_specs += [pl.BlockSpec((1, 1, H_FOX, page), pmap(j)) for j in range(npg)]
    in_specs += [
        pl.BlockSpec((1, page, FOX_W), bmap),
        pl.BlockSpec((1, page, FOX_W), bmap),
        pl.BlockSpec((1, H_FOX, page), bmap),
        pl.BlockSpec((page, page), c2),
        pl.BlockSpec((tnew, tnew * H_FOX), c2),
    ]
    grid_spec = pltpu.PrefetchScalarGridSpec(
        num_scalar_prefetch=1,
        grid=(nb, ngrp),
        in_specs=in_specs,
        out_specs=pl.BlockSpec((1, tnew, FOX_W), bmap),
        scratch_shapes=[
            pltpu.VMEM((tnew * H_FOX, 1), F32),
            pltpu.VMEM((tnew * H_FOX, 1), F32),
            pltpu.VMEM((tnew * H_FOX, FOX_W), F32),
            pltpu.VMEM((H_FOX, page), F32),
            pltpu.VMEM((npg * page, FOX_W), BF16),
            pltpu.VMEM((npg * page, FOX_W), BF16),
        ],
    )
    args = [page_table.reshape(-1), qm] + [cache_k4] * npg + [cache_v4] * npg + [cache_lft] * npg
    args += [knew, vnew, lfnew, u, sel]
    return pl.pallas_call(
        functools.partial(_fox_sample_kernel, npg, tnew),
        grid_spec=grid_spec,
        out_shape=jax.ShapeDtypeStruct((nb, tnew, FOX_W), F32),
        compiler_params=_params("arbitrary", "arbitrary"),
        name="fox_sample",
    )(*args)


_LANE_GROUP0 = N_EXPERTS


def _out_proj_kernel(n_prompt_tiles, x_ref, retp_ref, rets_ref, s5p_ref, s5s_ref, foxp_ref, foxs_ref, wo_ref,
                     wof_ref, g_ref, wr_ref, br_ref, tri_ref, x1_ref, h2_ref, info_ref, cnt_ref, carry):
    i = pl.program_id(0)

    @pl.when(i == 0)
    def _():
        carry[...] = jnp.zeros_like(carry)

    def project(mm, ret_ref, s5_ref, fox_ref, w_ref):
        x1 = x_ref[...] + mm.dot(ret_ref[...], w_ref[0:RET_W, :])
        x1 = x1 + mm.dot(s5_ref[...], w_ref[RET_W:RET_W + S5_W, :])
        x1_ref[...] = x1 + mm.dot(fox_ref[...], w_ref[RET_W + S5_W:, :])

    @pl.when(i < n_prompt_tiles)
    def _():
        project(_FAST, retp_ref, s5p_ref, foxp_ref, wo_ref)

    @pl.when(i >= n_prompt_tiles)
    def _():
        project(_EXACT, rets_ref, s5s_ref, foxs_ref, wof_ref)

    x1 = x1_ref[...]
    ms = jnp.mean(x1 * x1, axis=-1, keepdims=True)
    h2 = x1 * lax.rsqrt(ms + RMS_EPS) * g_ref[...]
    h2_ref[...] = h2
    logits = jnp.dot(h2, wr_ref[...], precision=HIGHEST, preferred_element_type=F32) + br_ref[...]
    tm = logits.shape[0]
    lane = lax.broadcasted_iota(jnp.int32, (tm, LANES), 1)
    big = jnp.int32(LANES)
    gmask = (lane >= _LANE_GROUP0) & (lane < _LANE_GROUP0 + N_EXPERT_GROUPS)
    gl = jnp.where(gmask, logits, NEG_BIG)
    gmax = jnp.max(gl, axis=1, keepdims=True)
    gsum = jnp.sum(jnp.where(gmask, jnp.exp(gl - gmax), 0.0), axis=1, keepdims=True)
    g_w = 1.0 / gsum
    g_idx = jnp.min(jnp.where(gl == gmax, lane, big), axis=1, keepdims=True) - _LANE_GROUP0
    lo = g_idx * EXPERTS_PER_GROUP
    emask = (lane >= lo) & (lane < lo + EXPERTS_PER_GROUP)
    el = jnp.where(emask, logits, NEG_BIG)
    e1 = jnp.max(el, axis=1, keepdims=True)
    idx1 = jnp.min(jnp.where(el == e1, lane, big), axis=1, keepdims=True)
    esum = jnp.sum(jnp.where(emask, jnp.exp(el - e1), 0.0), axis=1, keepdims=True)
    el2 = jnp.where(lane == idx1, NEG_BIG, el)
    e2 = jnp.max(el2, axis=1, keepdims=True)
    idx2 = jnp.min(jnp.where(el2 == e2, lane, big), axis=1, keepdims=True)
    p1 = 1.0 / esum
    p2 = jnp.exp(e2 - e1) / esum
    gate1 = g_w * p1 / (p1 + p2)
    gate2 = g_w * p2 / (p1 + p2)
    oh1 = lane == idx1
    oh2 = lane == idx2
    a = jnp.where(oh1 | oh2, 1.0, 0.0)
    cum = _dot(tri_ref[...], a.astype(BF16))
    before = cum - a + carry[...]
    rank1 = jnp.sum(jnp.where(oh1, before, 0.0), axis=1, keepdims=True)
    rank2 = jnp.sum(jnp.where(oh2, before, 0.0), axis=1, keepdims=True)
    carry[...] = carry[...] + cum[tm - 1:tm, :]
    cnt_ref[...] = carry[...]
    info = jnp.where(lane == 0, idx1.astype(F32), 0.0)
    info = jnp.where(lane == 1, idx2.astype(F32), info)
    info = jnp.where(lane == 2, rank1, info)
    info = jnp.where(lane == 3, rank2, info)
    info = jnp.where(lane == 4, gate1, info)
    info = jnp.where(lane == 5, gate2, info)
    info_ref[...] = info


def _out_proj(x, ret_p, ret_s, s5_p, s5_s, fox_p, fox_s, w_out, g2, w_r, b_r, n_prompt_tiles):
    n, d = x.shape
    w_out_fast = w_out.astype(BF16)
    tm = TOKEN_TILE
    idx = jnp.arange(tm)
    tri = (idx[:, None] >= idx[None, :]).astype(BF16)

    def row(i):
        return (i, 0)

    def const(i):
        return (0, 0)

    def pmap(i):
        return (jnp.minimum(i, n_prompt_tiles - 1), 0)

    def smap(i):
        return (jnp.maximum(i - n_prompt_tiles, 0), 0)

    return pl.pallas_call(
        functools.partial(_out_proj_kernel, n_prompt_tiles),
        grid=(n // tm,),
        in_specs=[
            pl.BlockSpec((tm, d), row),
            pl.BlockSpec((tm, RET_W), pmap),
            pl.BlockSpec((tm, RET_W), smap),
            pl.BlockSpec((tm, S5_W), pmap),
            pl.BlockSpec((tm, S5_W), smap),
            pl.BlockSpec((tm, FOX_W), pmap),
            pl.BlockSpec((tm, FOX_W), smap),
            pl.BlockSpec((d, d), const),
            pl.BlockSpec((d, d), const),
            pl.BlockSpec((1, d), const),
            pl.BlockSpec((d, LANES), const),
            pl.BlockSpec((1, LANES), const),
            pl.BlockSpec((tm, tm), const),
        ],
        out_specs=(
            pl.BlockSpec((tm, d), row),
            pl.BlockSpec((tm, d), row),
            pl.BlockSpec((tm, LANES), row),
            pl.BlockSpec((1, LANES), const),
        ),
        out_shape=(
            jax.ShapeDtypeStruct((n, d), F32),
            jax.ShapeDtypeStruct((n, d), F32),
            jax.ShapeDtypeStruct((n, LANES), F32),
            jax.ShapeDtypeStruct((1, LANES), F32),
        ),
        scratch_shapes=[pltpu.VMEM((1, LANES), F32)],
        compiler_params=_params("arbitrary"),
        name="out_proj_router",
    )(x, ret_p, ret_s, s5_p, s5_s, fox_p, fox_s, w_out_fast, w_out, g2, w_r, b_r, tri)


def _dispatch_kernel(dest_ref, h_ref, rows_in_ref, rows_ref, sem):
    del rows_in_ref
    i = pl.program_id(0)
    tm = h_ref.shape[0]

    def row_copy(r, d):
        return pltpu.make_async_copy(h_ref.at[pl.ds(r, 1)], rows_ref.at[pl.ds(d, 1)], sem)

    def issue(r, _):
        t = i * tm + r
        row_copy(r, dest_ref[2 * t]).start()
        row_copy(r, dest_ref[2 * t + 1]).start()
        return 0

    lax.fori_loop(0, tm, issue, 0)

    def drain(r, _):
        row_copy(0, 0).wait()
        row_copy(0, 0).wait()
        return 0

    lax.fori_loop(0, tm, drain, 0)


def _dispatch(dest, h2, n_rows):
    n, d = h2.shape
    tm = TOKEN_TILE
    grid_spec = pltpu.PrefetchScalarGridSpec(
        num_scalar_prefetch=1,
        grid=(n // tm,),
        in_specs=[
            pl.BlockSpec((tm, d), lambda i, dest: (i, 0)),
            pl.BlockSpec(memory_space=pl.ANY),
        ],
        out_specs=pl.BlockSpec(memory_space=pl.ANY),
        scratch_shapes=[pltpu.SemaphoreType.DMA(())],
    )
    return pl.pallas_call(
        _dispatch_kernel,
        grid_spec=grid_spec,
        out_shape=jax.ShapeDtypeStruct((n_rows, d), F32),
        input_output_aliases={2: 0},
        compiler_params=_params("arbitrary"),
        name="moe_dispatch",
    )(dest, h2, jnp.zeros((n_rows, d), F32))


def _experts_kernel(be_ref, nu_ref, x_ref, w1_ref, w3_ref, w2_ref, y_ref, w1b, w3b, w2b):
    i = pl.program_id(0)
    prev = be_ref[jnp.maximum(i - 1, 0)]
    fresh = (i == 0) | (be_ref[i] != prev)
    active = i < nu_ref[0]

    @pl.when(active & fresh)
    def _():
        w1b[...] = w1_ref[0, 0].astype(BF16)
        w3b[...] = w3_ref[0, 0].astype(BF16)
        w2b[...] = w2_ref[0, 0].astype(BF16)

    @pl.when(active)
    def _():
        xb = x_ref[...].astype(BF16)
        h1 = _dot(xb, w1b[...])
        h3 = _dot(xb, w3b[...])
        a = (h1 * _sigmoid(h1) * h3).astype(BF16)
        y_ref[...] = _dot(a, w2b[...])

    @pl.when(jnp.logical_not(active))
    def _():
        y_ref[...] = jnp.zeros_like(y_ref)


def _experts(layer, blk_e, n_used, x_rows, w1, w3, w2):
    n_rows, d = x_rows.shape
    de = w1.shape[3]
    nblk = n_rows // MOE_ROWS

    def xmap(i, be, nu):
        return (jnp.minimum(i, nu[0] - 1), 0)

    def wmap(i, be, nu):
        return (layer, be[jnp.minimum(i, nu[0] - 1)], 0, 0)

    grid_spec = pltpu.PrefetchScalarGridSpec(
        num_scalar_prefetch=2,
        grid=(nblk,),
        in_specs=[
            pl.BlockSpec((MOE_ROWS, d), xmap),
            pl.BlockSpec((1, 1, d, de), wmap),
            pl.BlockSpec((1, 1, d, de), wmap),
            pl.BlockSpec((1, 1, de, d), wmap),
        ],
        out_specs=pl.BlockSpec((MOE_ROWS, d), lambda i, be, nu: (i, 0)),
        scratch_shapes=[
            pltpu.VMEM((d, de), BF16),
            pltpu.VMEM((d, de), BF16),
            pltpu.VMEM((de, d), BF16),
        ],
    )
    return pl.pallas_call(
        _experts_kernel,
        grid_spec=grid_spec,
        out_shape=jax.ShapeDtypeStruct((n_rows, d), F32),
        compiler_params=_params("arbitrary"),
        name="moe_experts",
    )(blk_e, n_used, x_rows, w1, w3, w2)


def _combine_kernel(final, dest_ref, x1_ref, info_ref, gf_ref, y_ref, o_ref, buf, sem):
    i = pl.program_id(0)
    tm = x1_ref.shape[0]

    def row_copy(k, r, d):
        return pltpu.make_async_copy(y_ref.at[pl.ds(d, 1)], buf.at[k, pl.ds(r, 1)], sem)

    def issue(r, _):
        t = i * tm + r
        row_copy(0, r, dest_ref[2 * t]).start()
        row_copy(1, r, dest_ref[2 * t + 1]).start()
        return 0

    lax.fori_loop(0, tm, issue, 0)

    def drain(r, _):
        row_copy(0, 0, 0).wait()
        row_copy(1, 0, 0).wait()
        return 0

    lax.fori_loop(0, tm, drain, 0)
    info = info_ref[...]
    x2 = x1_ref[...] + info[:, 4:5] * buf[0] + info[:, 5:6] * buf[1]
    if final:
        ms = jnp.mean(x2 * x2, axis=-1, keepdims=True)
        x2 = x2 * lax.rsqrt(ms + RMS_EPS) * gf_ref[...]
    o_ref[...] = x2


def _combine(dest, x1, info, gf, y_rows, final):
    n, d = x1.shape
    tm = TOKEN_TILE
    grid_spec = pltpu.PrefetchScalarGridSpec(
        num_scalar_prefetch=1,
        grid=(n // tm,),
        in_specs=[
            pl.BlockSpec((tm, d), lambda i, dest: (i, 0)),
            pl.BlockSpec((tm, LANES), lambda i, dest: (i, 0)),
            pl.BlockSpec((1, d), lambda i, dest: (0, 0)),
            pl.BlockSpec(memory_space=pl.ANY),
        ],
        out_specs=pl.BlockSpec((tm, d), lambda i, dest: (i, 0)),
        scratch_shapes=[
            pltpu.VMEM((2, tm, d), F32),
            pltpu.SemaphoreType.DMA(()),
        ],
    )
    return pl.pallas_call(
        functools.partial(_combine_kernel, final),
        grid_spec=grid_spec,
        out_shape=jax.ShapeDtypeStruct((n, d), F32),
        compiler_params=_params("arbitrary"),
        name="moe_combine",
    )(dest, x1, info, gf, y_rows)


def _moe_plan(info, counts, n_tokens):
    eid = info[:, 0:2].astype(jnp.int32)
    rank = info[:, 2:4].astype(jnp.int32)
    cnt = counts[0, :N_EXPERTS].astype(jnp.int32)
    padded = (cnt + MOE_ROWS - 1) // MOE_ROWS * MOE_ROWS
    pend = jnp.cumsum(padded)
    pstart = pend - padded
    dest = (pstart[eid] + rank).reshape(-1)
    nblk = -(-n_tokens * 2 // MOE_ROWS) + N_EXPERTS
    blk_start = jnp.arange(nblk, dtype=jnp.int32) * MOE_ROWS
    blk_e = jnp.minimum(jnp.sum((pend[None, :] <= blk_start[:, None]).astype(jnp.int32), axis=1), N_EXPERTS - 1)
    n_used = (pend[-1:] // MOE_ROWS).astype(jnp.int32)
    return dest.astype(jnp.int32), blk_e.astype(jnp.int32), n_used, nblk * MOE_ROWS


def _rope_tables(seq, past, tnew, tile):
    half = HEAD_DIM // 2
    inv = ROPE_THETA ** (-jnp.arange(half, dtype=F32) / half)
    pos = jnp.concatenate([jnp.arange(seq, dtype=F32), past + (jnp.arange(tile) % tnew).astype(F32)])
    ang = pos[:, None] * inv[None, :]
    cos = jnp.cos(ang)
    sin = jnp.sin(ang)
    cos_h = jnp.concatenate([cos, cos], axis=1)
    sin_h = jnp.concatenate([-sin, sin], axis=1)
    return jnp.tile(cos_h, (1, H_RET)), jnp.tile(sin_h, (1, H_RET))


def _block_diag_states(s):
    b = s.shape[0]
    eye = jnp.eye(H_RET, dtype=s.dtype)
    return jnp.einsum('bhde,hg->bhdge', s, eye).reshape(b, RET_W, RET_W)


def _diag_blocks(s_bd):
    b = s_bd.shape[0]
    s5 = s_bd.reshape(b, H_RET, HEAD_DIM, H_RET, HEAD_DIM)
    return jnp.stack([s5[:, h, :, h, :] for h in range(H_RET)], axis=1)


def kernel(x_prompt, x_sample, cache_k, cache_v, cache_logf, state_ret, state_s5, page_table, norm1, w_in, ret_gn,
           s5_lam_re, s5_lam_im, s5_b_re, s5_b_im, s5_c_re, s5_c_im, s5_d, s5_log_dt, s5_glu_w, s5_glu_b, fox_fb,
           w_out, norm2, w_rg, b_rg, w_re, b_re, w1, w3, w2, norm_f):
    bp, seq, d = x_prompt.shape
    bd_, tnew, _ = x_sample.shape
    depth = w_in.shape[0]
    n_phys, page = cache_k.shape[1], cache_k.shape[2]
    npages = page_table.shape[1]
    past = npages * page
    n_p = bp * seq
    n_s = bd_ * tnew
    n = n_p + n_s
    tm = TOKEN_TILE
    assert n_p % tm == 0 and n_s == tm and seq % tm == 0
    n_prompt_tiles = n_p // tm

    x = jnp.concatenate([x_prompt.reshape(n_p, d), x_sample.reshape(n_s, d)], axis=0)
    cos_t, sin_t = _rope_tables(seq, past, tnew, tm)
    cache_k4 = cache_k.reshape(depth, n_phys, page, FOX_W)
    cache_v4 = cache_v.reshape(depth, n_phys, page, FOX_W)
    cache_lft = jnp.swapaxes(cache_logf, 2, 3)
    ff0 = w_in.shape[2] - H_FOX

    outs = {k: [] for k in ('kp', 'vp', 'lp', 'ks', 'vs', 'ls', 'rp', 'rs', 'sp', 'ss')}
    y = None
    for l in range(depth):
        w_main = w_in[l, :, :ff0]
        w_ff = jnp.pad(w_in[l, :, ff0:], ((0, 0), (0, LANES - H_FOX)))
        fb = jnp.pad(fox_fb[l], (0, LANES - H_FOX))[None, :]
        g1 = norm1[l][None, :]
        ret_in, su, fq, fk, fv, fkb, fvb, logf = _in_proj(
            _FAST, x, g1, w_main.astype(BF16), w_ff.astype(BF16), fb, cos_t, sin_t,
            0, n_prompt_tiles, 0, seq // tm)
        ret_in_s, su_s, fq_s, fk_s, fv_s, _, _, logf_s = _in_proj(
            _EXACT, x, g1, w_main, w_ff, fb, cos_t, sin_t, n_prompt_tiles, 1, seq // tm, 1)

        gw = ret_gn[l][None, :]
        zero_state = jnp.zeros((bp, RET_W, RET_W), F32)
        ret_p, rs_p = _retention(_FAST, ret_in, zero_state, gw, bp, 0, seq, 1, min(RET_CHUNK, seq),
                                 min(1024, seq))
        seq_per_chunk = 8
        chunk_rows = seq_per_chunk * tnew
        ret_s, rs_s = _retention(_EXACT, ret_in_s, _block_diag_states(state_ret[l]), gw, bd_ // seq_per_chunk, 0,
                                 chunk_rows, seq_per_chunk, tnew, chunk_rows)

        tables = _s5_tables(s5_lam_re[l], s5_lam_im[l], s5_b_re[l], s5_b_im[l], s5_c_re[l], s5_c_im[l],
                            s5_log_dt[l])
        dsk = s5_d[l][None, :]
        glub = s5_glu_b[l][None, :]
        zs = jnp.zeros((bp, 1, S5_LANES), F32)
        s5_p, sp_r, sp_i = _s5(_FAST, su, zs, zs, tables, dsk, s5_glu_w[l], glub, bp, 0, seq, min(512, seq),
                               False)
        x0 = state_s5[l].reshape(bd_, 1, S5_LANES, 2)
        s5_s, ss_r, ss_i = _s5(_EXACT, su_s, x0[..., 0], x0[..., 1], tables, dsk, s5_glu_w[l], glub, 1, 0, n_s,
                               n_s, True)

        qa, ka, va = _fox_prep(logf, fq, fkb, fvb, bp, seq, min(512, seq))
        fox_p = _fox_prompt(qa, ka, va, bp, seq)
        q_s = fq_s.reshape(bd_, tnew, 1, FOX_W)
        head_of_lane = jnp.arange(FOX_W) // HEAD_DIM
        hmask = (jnp.arange(H_FOX)[:, None] == head_of_lane[None, :])[None, None]
        qm = jnp.where(hmask, q_s, jnp.zeros((), BF16)).reshape(bd_, tnew * H_FOX, FOX_W)
        pad_rows = ((0, 0), (0, page - tnew), (0, 0))
        knew = jnp.pad(fk_s.reshape(bd_, tnew, FOX_W), pad_rows)
        vnew = jnp.pad(fv_s.reshape(bd_, tnew, FOX_W), pad_rows)
        lfnew = jnp.pad(jnp.swapaxes(logf_s[:, :H_FOX].reshape(bd_, tnew, H_FOX), 1, 2),
                        ((0, 0), (0, 0), (0, page - tnew)))
        fox_s = _fox_sample(l, page_table, qm, cache_k4, cache_v4, cache_lft, knew, vnew, lfnew)
        fox_s = fox_s.reshape(n_s, FOX_W)

        w_r = jnp.pad(jnp.concatenate([w_re[l], w_rg[l]], axis=1),
                      ((0, 0), (0, LANES - N_EXPERTS - N_EXPERT_GROUPS)))
        b_r = jnp.pad(jnp.concatenate([b_re[l], b_rg[l]]), (0, LANES - N_EXPERTS - N_EXPERT_GROUPS))[None, :]
        x1, h2, info, counts = _out_proj(x, ret_p, ret_s, s5_p, s5_s, fox_p, fox_s, w_out[l],
                                         norm2[l][None, :], w_r, b_r, n_prompt_tiles)
        dest, blk_e, n_used, n_rows = _moe_plan(info, counts, n)
        x_rows = _dispatch(dest, h2, n_rows)
        y_rows = _experts(l, blk_e, n_used, x_rows, w1, w3, w2)
        final = l == depth - 1
        x = _combine(dest, x1, info, norm_f[None, :], y_rows, final)

        outs['kp'].append(fk.reshape(bp, seq, H_FOX, HEAD_DIM))
        outs['vp'].append(fv.reshape(bp, seq, H_FOX, HEAD_DIM))
        outs['lp'].append(logf[:, :H_FOX].reshape(bp, seq, H_FOX))
        outs['ks'].append(fk_s.reshape(bd_, tnew, H_FOX, HEAD_DIM))
        outs['vs'].append(fv_s.reshape(bd_, tnew, H_FOX, HEAD_DIM))
        outs['ls'].append(logf_s[:, :H_FOX].reshape(bd_, tnew, H_FOX))
        outs['rp'].append(_diag_blocks(rs_p))
        outs['rs'].append(_diag_blocks(rs_s))
        outs['sp'].append(jnp.stack([sp_r.reshape(bp, S5_GROUPS, S5_STATE),
                                     sp_i.reshape(bp, S5_GROUPS, S5_STATE)], axis=-1))
        outs['ss'].append(jnp.stack([ss_r.reshape(bd_, S5_GROUPS, S5_STATE),
                                     ss_i.reshape(bd_, S5_GROUPS, S5_STATE)], axis=-1))

    y_prompt = x[:n_p].reshape(bp, seq, d)
    y_sample = x[n_p:].reshape(bd_, tnew, d)
    st = {k: jnp.stack(v) for k, v in outs.items()}
    return (y_prompt, y_sample, st['kp'], st['vp'], st['lp'], st['ks'], st['vs'], st['ls'],
            st['rp'], st['rs'], st['sp'], st['ss'])
```

```python
import functools
import math

import numpy as np

import jax
import jax.numpy as jnp
from jax import lax
from jax.experimental import pallas as pl
from jax.experimental.pallas import tpu as pltpu

F32 = jnp.float32
BF16 = jnp.bfloat16
HIGHEST = lax.Precision.HIGHEST

HEAD_DIM = 64
H_RET = 4
H_FOX = 8
RET_W = H_RET * HEAD_DIM
S5_W = 256
FOX_W = H_FOX * HEAD_DIM
S5_GROUPS = 16
S5_GROUP_CH = 16
S5_STATE = 64
S5_LANES = S5_GROUPS * S5_STATE
N_EXPERTS = 32
N_EXPERT_GROUPS = 4
EXPERTS_PER_GROUP = 8
RET_CHUNK = 128
ROPE_THETA = 10000.0
RMS_EPS = 1e-6
NEG_BIG = -1e30
LOG2E = math.log2(math.e)

LANES = 128
SUBLANES = 8
TOKEN_TILE = 256
MOE_ROWS = 256
FOX_TQ = 512
PAGES_PER_STEP = 8
VMEM_LIMIT = 48 * 1024 * 1024


def _params(*sem):
    return pltpu.CompilerParams(dimension_semantics=sem, vmem_limit_bytes=VMEM_LIMIT)


def _sigmoid(x):
    return 1.0 / (1.0 + jnp.exp(-x))


def _dot(a, b):
    return jnp.dot(a, b, preferred_element_type=F32)


def _dot_nt(a, b):
    return lax.dot_general(a, b, (((1,), (1,)), ((), ())), preferred_element_type=F32)


def _dot_tn(a, b):
    return lax.dot_general(a, b, (((0,), (0,)), ((), ())), preferred_element_type=F32)


class _MatmulMode:
    def __init__(self, exact):
        self.dtype = F32 if exact else BF16
        self.precision = HIGHEST if exact else None

    def cast(self, x):
        return x.astype(self.dtype)

    def dot(self, a, b):
        return jnp.dot(a, b, precision=self.precision, preferred_element_type=F32)

    def dot_nt(self, a, b):
        return lax.dot_general(a, b, (((1,), (1,)), ((), ())), precision=self.precision,
                               preferred_element_type=F32)

    def dot_tn(self, a, b):
        return lax.dot_general(a, b, (((0,), (0,)), ((), ())), precision=self.precision,
                               preferred_element_type=F32)


_FAST = _MatmulMode(False)
_EXACT = _MatmulMode(True)


def _in_proj_kernel(mm, x_ref, g_ref, w_ref, wff_ref, fb_ref, cos_ref, sin_ref,
                    ret_ref, su_ref, fq_ref, fk_ref, fv_ref, fkb_ref, fvb_ref, logf_ref):
    _dot = mm.dot
    x = x_ref[...]
    ms = jnp.mean(x * x, axis=-1, keepdims=True)
    h = mm.cast(x * lax.rsqrt(ms + RMS_EPS) * g_ref[...])
    tm = x.shape[0]
    cos = cos_ref[...]
    sin = sin_ref[...]
    lane = lax.broadcasted_iota(jnp.int32, (tm, RET_W), 1)
    first_half = (lane % HEAD_DIM) < (HEAD_DIM // 2)

    def rope(t):
        swapped = jnp.where(first_half,
                            pltpu.roll(t, RET_W - HEAD_DIM // 2, axis=1),
                            pltpu.roll(t, HEAD_DIM // 2, axis=1))
        return t * cos + swapped * sin

    ret_ref[:, 0:RET_W] = rope(_dot(h, w_ref[:, 0:RET_W]))
    ret_ref[:, RET_W:2 * RET_W] = rope(_dot(h, w_ref[:, RET_W:2 * RET_W])) * (HEAD_DIM ** -0.5)
    ret_ref[:, 2 * RET_W:4 * RET_W] = _dot(h, w_ref[:, 2 * RET_W:4 * RET_W])
    c0 = 4 * RET_W
    su_ref[...] = _dot(h, w_ref[:, c0:c0 + S5_W])
    c0 += S5_W
    fq_ref[...] = (_dot(h, w_ref[:, c0:c0 + FOX_W]) * (HEAD_DIM ** -0.5 * LOG2E)).astype(BF16)
    c0 += FOX_W
    fk = _dot(h, w_ref[:, c0:c0 + FOX_W])
    fk_ref[...] = fk
    fkb_ref[...] = fk.astype(BF16)
    c0 += FOX_W
    fv = _dot(h, w_ref[:, c0:c0 + FOX_W])
    fv_ref[...] = fv
    fvb_ref[...] = fv.astype(BF16)
    z = _dot(h, wff_ref[...]) + fb_ref[...]
    logf = jnp.minimum(z, 0.0) - jnp.log1p(jnp.exp(-jnp.abs(z)))
    lane128 = lax.broadcasted_iota(jnp.int32, (tm, LANES), 1)
    logf_ref[...] = jnp.where(lane128 < H_FOX, logf, 0.0)


def _in_proj(mm, x, g, w_main, w_ff, fb, cos_t, sin_t, tile0, ntiles, pos_tile0, pos_tiles):
    tm = TOKEN_TILE
    n = ntiles * tm
    wcols = w_main.shape[1]

    def xrow(i):
        return (tile0 + i, 0)

    def row(i):
        return (i, 0)

    def const(i):
        return (0, 0)

    def pos_map(i):
        return (pos_tile0 + i % pos_tiles, 0)

    outs = (
        jax.ShapeDtypeStruct((n, 4 * RET_W), F32),
        jax.ShapeDtypeStruct((n, S5_W), F32),
        jax.ShapeDtypeStruct((n, FOX_W), BF16),
        jax.ShapeDtypeStruct((n, FOX_W), F32),
        jax.ShapeDtypeStruct((n, FOX_W), F32),
        jax.ShapeDtypeStruct((n, FOX_W), BF16),
        jax.ShapeDtypeStruct((n, FOX_W), BF16),
        jax.ShapeDtypeStruct((n, LANES), F32),
    )
    return pl.pallas_call(
        functools.partial(_in_proj_kernel, mm),
        grid=(ntiles,),
        in_specs=[
            pl.BlockSpec((tm, x.shape[1]), xrow),
            pl.BlockSpec((1, x.shape[1]), const),
            pl.BlockSpec((x.shape[1], wcols), const),
            pl.BlockSpec((x.shape[1], LANES), const),
            pl.BlockSpec((1, LANES), const),
            pl.BlockSpec((tm, RET_W), pos_map),
            pl.BlockSpec((tm, RET_W), pos_map),
        ],
        out_specs=(
            pl.BlockSpec((tm, 4 * RET_W), row),
            pl.BlockSpec((tm, S5_W), row),
            pl.BlockSpec((tm, FOX_W), row),
            pl.BlockSpec((tm, FOX_W), row),
            pl.BlockSpec((tm, FOX_W), row),
            pl.BlockSpec((tm, FOX_W), row),
            pl.BlockSpec((tm, FOX_W), row),
            pl.BlockSpec((tm, LANES), row),
        ),
        out_shape=outs,
        compiler_params=_params("arbitrary"),
        name="in_proj",
    )(x, g, w_main, w_ff, fb, cos_t, sin_t)


def _retention_kernel(mm, nseq, seq_len, nchunk, x_ref, s0_ref, dm_ref, qd_ref, kd_ref, cd_ref, bd_ref,
                      gn_ref, gw_ref, o_ref, sout_ref, s_scr):
    cast, _dot, _dot_nt, _dot_tn = mm.cast, mm.dot, mm.dot_nt, mm.dot_tn
    step = pl.program_id(1)
    rows = nseq * seq_len

    @pl.when(step == 0)
    def _():
        s_scr[...] = s0_ref[...]

    lane_head = lax.broadcasted_iota(jnp.int32, (rows, RET_W), 1) // HEAD_DIM
    row_seq = lax.broadcasted_iota(jnp.int32, (rows, 1), 0) // seq_len

    def chunk(c, _):
        r0 = pl.multiple_of(c * rows, rows)
        q = x_ref[pl.ds(r0, rows), 0:RET_W]
        k = x_ref[pl.ds(r0, rows), RET_W:2 * RET_W]
        v = x_ref[pl.ds(r0, rows), 2 * RET_W:3 * RET_W]
        g = x_ref[pl.ds(r0, rows), 3 * RET_W:4 * RET_W]
        qb = cast(q)
        kb = cast(k)
        vb = cast(v)
        ps = []
        vs = []
        for h in range(H_RET):
            hm = lane_head == h
            sc = _dot_nt(cast(jnp.where(hm, q, 0.0)), kb)
            ps.append(cast(sc * dm_ref[h]))
            vs.append(cast(jnp.where(hm, v, 0.0)))
        inner = _dot(jnp.concatenate(ps, axis=1), jnp.concatenate(vs, axis=0))
        kdec = k * kd_ref[...]
        cd = cd_ref[...]
        bd = bd_ref[...]
        if nseq == 1:
            s_old = s_scr[0]
            cross = _dot(qb, cast(s_old))
            s_scr[0] = s_old * cd + _dot_tn(cast(kdec), vb) * bd
        else:
            def seq_body(s, cross):
                rm = row_seq == s
                s_old = s_scr[s]
                cross = jnp.where(rm, _dot(qb, cast(s_old)), cross)
                kv = _dot_tn(cast(jnp.where(rm, kdec, 0.0)), vb)
                s_scr[s] = s_old * cd + kv * bd
                return cross

            cross = lax.fori_loop(0, nseq, seq_body, jnp.zeros((rows, RET_W), F32))
        o = inner + cross * qd_ref[...]
        o2 = o * o
        hi = o2.astype(BF16)
        lo = (o2 - hi.astype(F32)).astype(BF16)
        ms = jnp.dot(hi, gn_ref[...], preferred_element_type=F32) + jnp.dot(
            lo, gn_ref[...], preferred_element_type=F32)
        on = o * lax.rsqrt(ms + RMS_EPS)
        o_ref[pl.ds(r0, rows), :] = (g * _sigmoid(g) * on * gw_ref[...]).astype(o_ref.dtype)
        return 0

    lax.fori_loop(0, nchunk, chunk, 0)

    @pl.when(step == pl.num_programs(1) - 1)
    def _():
        sout_ref[...] = s_scr[...]


def _retention_tables(nseq, seq_len):
    rows = nseq * seq_len
    log_gamma = jnp.log1p(-jnp.power(2.0, -5.0 - jnp.arange(H_RET, dtype=F32)))
    idx = jnp.arange(rows, dtype=jnp.int32)
    pos = (idx % seq_len).astype(F32)
    seq = idx // seq_len
    diff = pos[:, None] - pos[None, :]
    ok = (seq[:, None] == seq[None, :]) & (diff >= 0)
    dm = jnp.where(ok[None], jnp.exp(log_gamma[:, None, None] * jnp.maximum(diff, 0.0)[None]), 0.0)
    lg_lane = jnp.repeat(log_gamma, HEAD_DIM)
    qd = jnp.exp((pos[:, None] + 1.0) * lg_lane[None, :])
    kd = jnp.exp((seq_len - 1.0 - pos[:, None]) * lg_lane[None, :])
    cd = jnp.exp(seq_len * lg_lane)[None, :]
    head = jnp.arange(RET_W) // HEAD_DIM
    bd = (head[:, None] == head[None, :]).astype(F32)
    gn = (bd / HEAD_DIM).astype(BF16)
    return dm, qd, kd, cd, bd, gn


def _retention(mm, ret_in, s0_bd, gw, nbatch, row0, seq_total, nseq, seq_len, rows_per_step):
    rows = nseq * seq_len
    nchunk = rows_per_step // rows
    steps = seq_total // rows_per_step
    blk0 = row0 // rows_per_step
    dm, qd, kd, cd, bd, gn = _retention_tables(nseq, seq_len)

    def rmap(b, i):
        return (blk0 + b * steps + i, 0)

    def omap(b, i):
        return (b * steps + i, 0)

    def c2(b, i):
        return (0, 0)

    def c3(b, i):
        return (0, 0, 0)

    def smap(b, i):
        return (b, 0, 0)

    in_specs = [
        pl.BlockSpec((rows_per_step, 4 * RET_W), rmap),
        pl.BlockSpec((nseq, RET_W, RET_W), smap),
        pl.BlockSpec((H_RET, rows, rows), c3),
        pl.BlockSpec((rows, RET_W), c2),
        pl.BlockSpec((rows, RET_W), c2),
        pl.BlockSpec((1, RET_W), c2),
        pl.BlockSpec((RET_W, RET_W), c2),
        pl.BlockSpec((RET_W, RET_W), c2),
        pl.BlockSpec((1, RET_W), c2),
    ]
    return pl.pallas_call(
        functools.partial(_retention_kernel, mm, nseq, seq_len, nchunk),
        grid=(nbatch, steps),
        in_specs=in_specs,
        out_specs=(
            pl.BlockSpec((rows_per_step, RET_W), omap),
            pl.BlockSpec((nseq, RET_W, RET_W), smap),
        ),
        out_shape=(
            jax.ShapeDtypeStruct((nbatch * seq_total, RET_W), mm.dtype),
            jax.ShapeDtypeStruct((nbatch * nseq, RET_W, RET_W), F32),
        ),
        scratch_shapes=[pltpu.VMEM((nseq, RET_W, RET_W), F32)],
        compiler_params=_params("arbitrary", "arbitrary"),
        name="retention",
    )(ret_in, s0_bd, dm, qd, kd, cd, bd, gn, gw)


def _s5_kernel(mm, short, u_ref, x0r_ref, x0i_ref, bre_ref, bim_ref, pre_ref, pim_ref, cm_ref, d_ref,
               gw_ref, gb_ref, o_ref, str_ref, sti_ref, bur, bui, car, cai):
    cast, _dot = mm.cast, mm.dot
    step = pl.program_id(1)
    rows = u_ref.shape[0]
    u = u_ref[...]
    ub = cast(u)
    bur[...] = _dot(ub, bre_ref[...])
    bui[...] = _dot(ub, bim_ref[...])

    if not short:
        @pl.when(step == 0)
        def _():
            car[...] = x0r_ref[0]
            cai[...] = x0i_ref[0]

    pr = pre_ref[...]
    pi = pim_ref[...]
    rowi = lax.broadcasted_iota(jnp.int32, (SUBLANES, S5_LANES), 0)

    def group(j, carry):
        cr, ci = carry
        r0 = pl.multiple_of(j * SUBLANES, SUBLANES)
        xr = bur[pl.ds(r0, SUBLANES), :]
        xi = bui[pl.ds(r0, SUBLANES), :]
        for s in (1, 2, 4):
            ar = pr[s - 1:s]
            ai = pi[s - 1:s]
            sr = jnp.where(rowi >= s, pltpu.roll(xr, s, axis=0), 0.0)
            si = jnp.where(rowi >= s, pltpu.roll(xi, s, axis=0), 0.0)
            xr, xi = xr + ar * sr - ai * si, xi + ar * si + ai * sr
        if short:
            cr = x0r_ref[j]
            ci = x0i_ref[j]
        xr, xi = xr + pr * cr - pi * ci, xi + pr * ci + pi * cr
        bur[pl.ds(r0, SUBLANES), :] = xr
        bui[pl.ds(r0, SUBLANES), :] = xi
        cr = xr[SUBLANES - 1:SUBLANES]
        ci = xi[SUBLANES - 1:SUBLANES]
        if short:
            str_ref[j] = cr
            sti_ref[j] = ci
        return cr, ci

    if short:
        init = (jnp.zeros((1, S5_LANES), F32), jnp.zeros((1, S5_LANES), F32))
    else:
        init = (car[...], cai[...])
    cr, ci = lax.fori_loop(0, rows // SUBLANES, group, init)
    if not short:
        car[...] = cr
        cai[...] = ci

        @pl.when(step == pl.num_programs(1) - 1)
        def _():
            str_ref[0] = cr
            sti_ref[0] = ci

    xs = jnp.concatenate([cast(bur[...]), cast(bui[...])], axis=1)
    y = _dot(xs, cm_ref[...]) + d_ref[...] * u
    yg = 0.5 * y * (1.0 + jnp.tanh(math.sqrt(2.0 / math.pi) * (y + 0.044715 * (y * y * y))))
    z = _dot(cast(yg), gw_ref[...]) + gb_ref[...]
    o_ref[...] = (yg * _sigmoid(z)).astype(o_ref.dtype)


def _s5_tables(lam_re, lam_im, b_re, b_im, c_re, c_im, log_dt):
    lam = lax.complex(lam_re.astype(F32), lam_im.astype(F32))
    dt = jnp.exp(log_dt.astype(F32))[:, None]
    lam_bar = jnp.exp(lam * dt)
    b_bar = ((lam_bar - 1.0) / lam)[:, :, None] * lax.complex(b_re.astype(F32), b_im.astype(F32))
    k = jnp.arange(1, SUBLANES + 1, dtype=F32)[:, None, None]
    powers = jnp.exp((lam * dt)[None] * k)
    pre = jnp.real(powers).reshape(SUBLANES, S5_LANES)
    pim = jnp.imag(powers).reshape(SUBLANES, S5_LANES)
    eye = jnp.eye(S5_GROUPS, dtype=F32)
    bre = jnp.einsum('gpc,gh->gchp', jnp.real(b_bar), eye).reshape(S5_W, S5_LANES)
    bim = jnp.einsum('gpc,gh->gchp', jnp.imag(b_bar), eye).reshape(S5_W, S5_LANES)
    cre = jnp.einsum('gcp,gh->hpgc', c_re.astype(F32), eye).reshape(S5_LANES, S5_W)
    cim = jnp.einsum('gcp,gh->hpgc', c_im.astype(F32), eye).reshape(S5_LANES, S5_W)
    cm = jnp.concatenate([cre, -cim], axis=0)
    return bre, bim, pre, pim, cm


def _s5(mm, su, x0r, x0i, tables, d, glu_w, glu_b, nbatch, row0, seq_total, rows_per_step, short):
    bre, bim, pre, pim, cm = tables
    bre, bim, cm, glu_w = mm.cast(bre), mm.cast(bim), mm.cast(cm), mm.cast(glu_w)
    steps = seq_total // rows_per_step
    blk0 = row0 // rows_per_step
    nstate = x0r.shape[0]
    sblk = nstate if short else 1

    def rmap(b, i):
        return (blk0 + b * steps + i, 0)

    def omap(b, i):
        return (b * steps + i, 0)

    def c2(b, i):
        return (0, 0)

    def smap(b, i):
        return (0 if short else b, 0, 0)

    in_specs = [
        pl.BlockSpec((rows_per_step, S5_W), rmap),
        pl.BlockSpec((sblk, 1, S5_LANES), smap),
        pl.BlockSpec((sblk, 1, S5_LANES), smap),
        pl.BlockSpec((S5_W, S5_LANES), c2),
        pl.BlockSpec((S5_W, S5_LANES), c2),
        pl.BlockSpec((SUBLANES, S5_LANES), c2),
        pl.BlockSpec((SUBLANES, S5_LANES), c2),
        pl.BlockSpec((2 * S5_LANES, S5_W), c2),
        pl.BlockSpec((1, S5_W), c2),
        pl.BlockSpec((S5_W, S5_W), c2),
        pl.BlockSpec((1, S5_W), c2),
    ]
    return pl.pallas_call(
        functools.partial(_s5_kernel, mm, short),
        grid=(nbatch, steps),
        in_specs=in_specs,
        out_specs=(
            pl.BlockSpec((rows_per_step, S5_W), omap),
            pl.BlockSpec((sblk, 1, S5_LANES), smap),
            pl.BlockSpec((sblk, 1, S5_LANES), smap),
        ),
        out_shape=(
            jax.ShapeDtypeStruct((nbatch * seq_total, S5_W), mm.dtype),
            jax.ShapeDtypeStruct((nstate, 1, S5_LANES), F32),
            jax.ShapeDtypeStruct((nstate, 1, S5_LANES), F32),
        ),
        scratch_shapes=[
            pltpu.VMEM((rows_per_step, S5_LANES), F32),
            pltpu.VMEM((rows_per_step, S5_LANES), F32),
            pltpu.VMEM((1, S5_LANES), F32),
            pltpu.VMEM((1, S5_LANES), F32),
        ],
        compiler_params=_params("arbitrary", "arbitrary"),
        name="s5",
    )(su, x0r, x0i, bre, bim, pre, pim, cm, d, glu_w, glu_b)


_PAIR_W = 2 * LANES
_AUG_PER_HEAD = 6
_N_PAIRS = H_FOX // 2


def _aug_placement():
    pq = np.zeros((3 * LANES, _N_PAIRS * LANES), np.float32)
    pk = np.zeros((3 * LANES, _N_PAIRS * LANES), np.float32)
    oq = np.zeros((1, _N_PAIRS * LANES), np.float32)
    ok = np.zeros((1, _N_PAIRS * LANES), np.float32)
    for h in range(H_FOX):
        base = (h // 2) * LANES + (h % 2) * _AUG_PER_HEAD
        for part in range(3):
            pq[part * LANES + h, base + part] = 1.0
            oq[0, base + 3 + part] = 1.0
            ok[0, base + part] = 1.0
            pk[part * LANES + h, base + 3 + part] = -1.0
    return (jnp.asarray(pq, BF16), jnp.asarray(pk, BF16), jnp.asarray(oq), jnp.asarray(ok))


def _split3(v):
    hi = v.astype(BF16)
    r1 = v - hi.astype(F32)
    mid = r1.astype(BF16)
    lo = (r1 - mid.astype(F32)).astype(BF16)
    return jnp.concatenate([hi, mid, lo], axis=1)


def _fox_prep_kernel(lf_ref, q_ref, k_ref, v_ref, tri_ref, pq_ref, pk_ref, oq_ref, ok_ref,
                     qa_ref, ka_ref, va_ref, carry):
    @pl.when(pl.program_id(1) == 0)
    def _():
        carry[...] = jnp.zeros_like(carry)

    c = jnp.dot(tri_ref[...], lf_ref[...], precision=HIGHEST, preferred_element_type=F32) + carry[...]
    rows = c.shape[0]
    carry[...] = c[rows - 1:rows, :]
    parts = _split3(c * LOG2E)
    augq = (_dot(parts, pq_ref[...]) + oq_ref[...]).astype(BF16)
    augk = (_dot(parts, pk_ref[...]) + ok_ref[...]).astype(BF16)
    lane = lax.broadcasted_iota(jnp.int32, (rows, LANES), 1)
    one = jnp.ones((rows, LANES), BF16)
    for p in range(_N_PAIRS):
        src = slice(p * LANES, (p + 1) * LANES)
        qa_ref[:, p * _PAIR_W:p * _PAIR_W + LANES] = q_ref[:, src]
        qa_ref[:, p * _PAIR_W + LANES:(p + 1) * _PAIR_W] = augq[:, src]
        ka_ref[:, p * _PAIR_W:p * _PAIR_W + LANES] = k_ref[:, src]
        ka_ref[:, p * _PAIR_W + LANES:(p + 1) * _PAIR_W] = augk[:, src]
        vp = v_ref[:, src]
        va_ref[:, p * _PAIR_W:p * _PAIR_W + LANES] = jnp.where(lane < HEAD_DIM, vp, one)
        va_ref[:, p * _PAIR_W + LANES:(p + 1) * _PAIR_W] = jnp.where(lane < HEAD_DIM, one, vp)


def _fox_prep(logf, fq, fkb, fvb, nbatch, seq_total, rows_per_step):
    steps = seq_total // rows_per_step
    idx = jnp.arange(rows_per_step)
    tri = (idx[:, None] >= idx[None, :]).astype(F32)
    pq, pk, oq, ok = _aug_placement()
    wide = _N_PAIRS * _PAIR_W

    def rmap(b, i):
        return (b * steps + i, 0)

    def const(b, i):
        return (0, 0)

    out = jax.ShapeDtypeStruct((nbatch * seq_total, wide), BF16)
    return pl.pallas_call(
        _fox_prep_kernel,
        grid=(nbatch, steps),
        in_specs=[
            pl.BlockSpec((rows_per_step, LANES), rmap),
            pl.BlockSpec((rows_per_step, FOX_W), rmap),
            pl.BlockSpec((rows_per_step, FOX_W), rmap),
            pl.BlockSpec((rows_per_step, FOX_W), rmap),
            pl.BlockSpec((rows_per_step, rows_per_step), const),
            pl.BlockSpec(pq.shape, const),
            pl.BlockSpec(pk.shape, const),
            pl.BlockSpec(oq.shape, const),
            pl.BlockSpec(ok.shape, const),
        ],
        out_specs=(pl.BlockSpec((rows_per_step, wide), rmap),) * 3,
        out_shape=(out, out, out),
        scratch_shapes=[pltpu.VMEM((1, LANES), F32)],
        compiler_params=_params("arbitrary", "arbitrary"),
        name="fox_prep",
    )(logf, fq, fkb, fvb, tri, pq, pk, oq, ok)


def _fox_prompt_kernel(tq, q_ref, k_ref, v_ref, o_ref, acc, m_scr):
    qi = pl.program_id(2)
    q = q_ref[...]
    lane = lax.broadcasted_iota(jnp.int32, (tq, _PAIR_W), 1)
    qs = []
    for par in range(2):
        own = ((lane >= par * HEAD_DIM) & (lane < (par + 1) * HEAD_DIM)) | (
            (lane >= LANES + par * _AUG_PER_HEAD) & (lane < LANES + (par + 1) * _AUG_PER_HEAD))
        qs.append(jnp.where(own, q, jnp.zeros_like(q)))
        m_scr[par] = jnp.full((tq, LANES), NEG_BIG, F32)
        acc[par] = jnp.zeros((tq, LANES), F32)
    causal = (lax.broadcasted_iota(jnp.int32, (tq, tq), 0) >= lax.broadcasted_iota(jnp.int32, (tq, tq), 1))

    def kv_step(j, masked):
        ks = pl.multiple_of(j * tq, tq)
        kb = k_ref[pl.ds(ks, tq), :]
        for par in range(2):
            s = _dot_nt(qs[par], kb)
            if masked:
                s = jnp.where(causal, s, NEG_BIG)
            m_old = m_scr[par]
            m_new = jnp.maximum(m_old, jnp.max(s, axis=1, keepdims=True))
            p = jnp.exp2(s - jnp.concatenate([m_new] * (tq // LANES), axis=1))
            vb = v_ref[pl.ds(ks, tq), par * LANES:(par + 1) * LANES]
            acc[par] = jnp.exp2(m_old - m_new) * acc[par] + _dot(p.astype(BF16), vb)
            m_scr[par] = m_new

    def body(j, _):
        kv_step(j, False)
        return 0

    lax.fori_loop(0, qi, body, 0)
    kv_step(qi, True)
    lane128 = lax.broadcasted_iota(jnp.int32, (tq, LANES), 1)
    outs = []
    for par in range(2):
        a = acc[par]
        outs.append(a / pltpu.roll(a, HEAD_DIM, axis=1))
    o_ref[...] = jnp.where(lane128 < HEAD_DIM, outs[0], outs[1]).astype(BF16)


def _fox_prompt(qa, ka, va, nbatch, seq_total):
    tq = min(FOX_TQ, seq_total)
    nq = seq_total // tq
    return pl.pallas_call(
        functools.partial(_fox_prompt_kernel, tq),
        grid=(nbatch, _N_PAIRS, nq),
        in_specs=[
            pl.BlockSpec((tq, _PAIR_W), lambda b, g, i: (b * nq + i, g)),
            pl.BlockSpec((seq_total, _PAIR_W), lambda b, g, i: (b, g)),
            pl.BlockSpec((seq_total, _PAIR_W), lambda b, g, i: (b, g)),
        ],
        out_specs=pl.BlockSpec((tq, LANES), lambda b, g, i: (b * nq + i, g)),
        out_shape=jax.ShapeDtypeStruct((nbatch * seq_total, FOX_W), BF16),
        scratch_shapes=[
            pltpu.VMEM((2, tq, LANES), F32),
            pltpu.VMEM((2, tq, LANES), F32),
        ],
        compiler_params=_params("arbitrary", "arbitrary", "arbitrary"),
        name="fox_prompt",
    )(qa, ka, va)


def _fox_sample_kernel(npg, tnew, pt_ref, q_ref, *refs):
    del pt_ref
    k_refs = refs[0:npg]
    v_refs = refs[npg:2 * npg]
    lf_refs = refs[2 * npg:3 * npg]
    (knew_ref, vnew_ref, lfnew_ref, u_ref, hm_ref, gidx_ref, o_ref,
     m_scr, l_scr, acc, run, kb_scr, vb_scr) = refs[3 * npg:]
    g = pl.program_id(1)
    nrow = tnew * H_FOX
    prow = k_refs[0].shape[2]
    page = prow // H_FOX

    @pl.when(g == 0)
    def _():
        m_scr[...] = jnp.full(m_scr.shape, NEG_BIG, F32)
        l_scr[...] = jnp.zeros(l_scr.shape, F32)
        acc[...] = jnp.zeros(acc.shape, F32)
        run[...] = jnp.zeros(run.shape, F32)

    q2 = q_ref[0]

    def cum_within(lft):
        return jnp.dot(lft, u_ref[...], precision=HIGHEST, preferred_element_type=F32)

    def spread(c, nblk):
        return jnp.concatenate(
            [jnp.take_along_axis(c, gidx_ref[:, cb * LANES:(cb + 1) * LANES], axis=1) for cb in range(nblk)], axis=1)

    def attend(kb, vb, bias):
        s = _dot_nt(q2, kb) + bias
        m_old = m_scr[...]
        m_new = jnp.maximum(m_old, jnp.max(s, axis=1, keepdims=True))
        p = jnp.exp2(s - m_new)
        alpha = jnp.exp2(m_old - m_new)
        l_scr[...] = alpha * l_scr[...] + jnp.sum(p, axis=1, keepdims=True)
        acc[...] = alpha * acc[...] + _dot(p.astype(BF16), vb)
        m_scr[...] = m_new

    base = run[...]
    cs = []
    for j in range(npg):
        kb_scr[j * prow:(j + 1) * prow, :] = k_refs[j][0, 0].astype(BF16)
        vb_scr[j * prow:(j + 1) * prow, :] = v_refs[j][0, 0].astype(BF16)
        cw = cum_within(lf_refs[j][0, 0])
        cs.append(spread(base + cw, prow // LANES))
        base = base + jnp.broadcast_to(cw[:, page - 1:page], base.shape)
    run[...] = base
    ckeys = jnp.concatenate(cs, axis=1) * LOG2E
    attend(kb_scr[...], vb_scr[...], hm_ref[...] - jnp.concatenate([ckeys] * tnew, axis=0))

    @pl.when(g == pl.num_programs(1) - 1)
    def _():
        row = lax.broadcasted_iota(jnp.int32, (nrow, nrow), 0)
        col = lax.broadcasted_iota(jnp.int32, (nrow, nrow), 1)
        ok = (col % H_FOX == row % H_FOX) & (col // H_FOX <= row // H_FOX)
        cn = spread(run[...] + cum_within(lfnew_ref[0]), 1)[:, 0:nrow] * LOG2E
        bias = jnp.where(ok, -jnp.concatenate([cn] * tnew, axis=0), NEG_BIG)
        attend(knew_ref[0].astype(BF16), vnew_ref[0].astype(BF16), bias)
        o_ref[0] = acc[...] / l_scr[...]


def _fox_sample(layer, page_table, q2, cache_k2, cache_v2, cache_lft, knew, vnew, lfnew):
    nb, npages = page_table.shape
    prow = cache_k2.shape[2]
    page = prow // H_FOX
    nrow = q2.shape[1]
    tnew = nrow // H_FOX
    npg = PAGES_PER_STEP
    ngrp = npages // npg
    idx = jnp.arange(page)
    u = (idx[:, None] <= idx[None, :]).astype(F32)
    colh = jnp.arange(npg * prow) % H_FOX
    hm = jnp.where(colh[None, :] == (jnp.arange(nrow) % H_FOX)[:, None], 0.0, NEG_BIG).astype(F32)
    gidx = jnp.broadcast_to((jnp.arange(prow, dtype=jnp.int32) // H_FOX)[None, :], (H_FOX, prow))

    def pmap(j):
        def f(b, g, pt):
            return (layer, pt[b * npages + g * npg + j], 0, 0)
        return f

    def bmap(b, g, pt):
        return (b, 0, 0)

    def c2(b, g, pt):
        return (0, 0)

    in_specs = [pl.BlockSpec((1, nrow, HEAD_DIM), bmap)]
    in_specs += [pl.BlockSpec((1, 1, prow, HEAD_DIM), pmap(j)) for j in range(npg)]
    in_specs += [pl.BlockSpec((1, 1, prow, HEAD_DIM), pmap(j)) for j in range(npg)]
    in_specs += [pl.BlockSpec((1, 1, H_FOX, page), pmap(j)) for j in range(npg)]
    in_specs += [
        pl.BlockSpec((1, nrow, HEAD_DIM), bmap),
        pl.BlockSpec((1, nrow, HEAD_DIM), bmap),
        pl.BlockSpec((1, H_FOX, page), bmap),
        pl.BlockSpec((page, page), c2),
        pl.BlockSpec(hm.shape, c2),
        pl.BlockSpec(gidx.shape, c2),
    ]
    grid_spec = pltpu.PrefetchScalarGridSpec(
        num_scalar_prefetch=1,
        grid=(nb, ngrp),
        in_specs=in_specs,
        out_specs=pl.BlockSpec((1, nrow, HEAD_DIM), bmap),
        scratch_shapes=[
            pltpu.VMEM((nrow, 1), F32),
            pltpu.VMEM((nrow, 1), F32),
            pltpu.VMEM((nrow, HEAD_DIM), F32),
            pltpu.VMEM((H_FOX, page), F32),
            pltpu.VMEM((npg * prow, HEAD_DIM), BF16),
            pltpu.VMEM((npg * prow, HEAD_DIM), BF16),
        ],
    )
    args = [page_table.reshape(-1), q2] + [cache_k2] * npg + [cache_v2] * npg + [cache_lft] * npg
    args += [knew, vnew, lfnew, u, hm, gidx]
    return pl.pallas_call(
        functools.partial(_fox_sample_kernel, npg, tnew),
        grid_spec=grid_spec,
        out_shape=jax.ShapeDtypeStruct((nb, nrow, HEAD_DIM), F32),
        compiler_params=_params("arbitrary", "arbitrary"),
        name="fox_sample",
    )(*args)


_LANE_GROUP0 = N_EXPERTS


def _out_proj_kernel(n_prompt_tiles, x_ref, retp_ref, rets_ref, s5p_ref, s5s_ref, foxp_ref, foxs_ref, wo_ref,
                     wof_ref, g_ref, wr_ref, br_ref, tri_ref, x1_ref, h2_ref, info_ref, cnt_ref, carry):
    i = pl.program_id(0)

    @pl.when(i == 0)
    def _():
        carry[...] = jnp.zeros_like(carry)

    def project(mm, ret_ref, s5_ref, fox_ref, w_ref):
        x1 = x_ref[...] + mm.dot(ret_ref[...], w_ref[0:RET_W, :])
        x1 = x1 + mm.dot(s5_ref[...], w_ref[RET_W:RET_W + S5_W, :])
        x1_ref[...] = x1 + mm.dot(fox_ref[...], w_ref[RET_W + S5_W:, :])

    @pl.when(i < n_prompt_tiles)
    def _():
        project(_FAST, retp_ref, s5p_ref, foxp_ref, wo_ref)

    @pl.when(i >= n_prompt_tiles)
    def _():
        project(_EXACT, rets_ref, s5s_ref, foxs_ref, wof_ref)

    x1 = x1_ref[...]
    ms = jnp.mean(x1 * x1, axis=-1, keepdims=True)
    h2 = x1 * lax.rsqrt(ms + RMS_EPS) * g_ref[...]
    h2_ref[...] = h2
    logits = jnp.dot(h2, wr_ref[...], precision=HIGHEST, preferred_element_type=F32) + br_ref[...]
    tm = logits.shape[0]
    lane = lax.broadcasted_iota(jnp.int32, (tm, LANES), 1)
    big = jnp.int32(LANES)
    gmask = (lane >= _LANE_GROUP0) & (lane < _LANE_GROUP0 + N_EXPERT_GROUPS)
    gl = jnp.where(gmask, logits, NEG_BIG)
    gmax = jnp.max(gl, axis=1, keepdims=True)
    gsum = jnp.sum(jnp.where(gmask, jnp.exp(gl - gmax), 0.0), axis=1, keepdims=True)
    g_w = 1.0 / gsum
    g_idx = jnp.min(jnp.where(gl == gmax, lane, big), axis=1, keepdims=True) - _LANE_GROUP0
    lo = g_idx * EXPERTS_PER_GROUP
    emask = (lane >= lo) & (lane < lo + EXPERTS_PER_GROUP)
    el = jnp.where(emask, logits, NEG_BIG)
    e1 = jnp.max(el, axis=1, keepdims=True)
    idx1 = jnp.min(jnp.where(el == e1, lane, big), axis=1, keepdims=True)
    esum = jnp.sum(jnp.where(emask, jnp.exp(el - e1), 0.0), axis=1, keepdims=True)
    el2 = jnp.where(lane == idx1, NEG_BIG, el)
    e2 = jnp.max(el2, axis=1, keepdims=True)
    idx2 = jnp.min(jnp.where(el2 == e2, lane, big), axis=1, keepdims=True)
    p1 = 1.0 / esum
    p2 = jnp.exp(e2 - e1) / esum
    gate1 = g_w * p1 / (p1 + p2)
    gate2 = g_w * p2 / (p1 + p2)
    oh1 = lane == idx1
    oh2 = lane == idx2
    a = jnp.where(oh1 | oh2, 1.0, 0.0)
    cum = _dot(tri_ref[...], a.astype(BF16))
    before = cum - a + carry[...]
    rank1 = jnp.sum(jnp.where(oh1, before, 0.0), axis=1, keepdims=True)
    rank2 = jnp.sum(jnp.where(oh2, before, 0.0), axis=1, keepdims=True)
    carry[...] = carry[...] + cum[tm - 1:tm, :]
    cnt_ref[...] = carry[...]
    info = jnp.where(lane == 0, idx1.astype(F32), 0.0)
    info = jnp.where(lane == 1, idx2.astype(F32), info)
    info = jnp.where(lane == 2, rank1, info)
    info = jnp.where(lane == 3, rank2, info)
    info = jnp.where(lane == 4, gate1, info)
    info = jnp.where(lane == 5, gate2, info)
    info_ref[...] = info


def _out_proj(x, ret_p, ret_s, s5_p, s5_s, fox_p, fox_s, w_out, g2, w_r, b_r, n_prompt_tiles):
    n, d = x.shape
    w_out_fast = w_out.astype(BF16)
    tm = TOKEN_TILE
    idx = jnp.arange(tm)
    tri = (idx[:, None] >= idx[None, :]).astype(BF16)

    def row(i):
        return (i, 0)

    def const(i):
        return (0, 0)

    def pmap(i):
        return (jnp.minimum(i, n_prompt_tiles - 1), 0)

    def smap(i):
        return (jnp.maximum(i - n_prompt_tiles, 0), 0)

    return pl.pallas_call(
        functools.partial(_out_proj_kernel, n_prompt_tiles),
        grid=(n // tm,),
        in_specs=[
            pl.BlockSpec((tm, d), row),
            pl.BlockSpec((tm, RET_W), pmap),
            pl.BlockSpec((tm, RET_W), smap),
            pl.BlockSpec((tm, S5_W), pmap),
            pl.BlockSpec((tm, S5_W), smap),
            pl.BlockSpec((tm, FOX_W), pmap),
            pl.BlockSpec((tm, FOX_W), smap),
            pl.BlockSpec((d, d), const),
            pl.BlockSpec((d, d), const),
            pl.BlockSpec((1, d), const),
            pl.BlockSpec((d, LANES), const),
            pl.BlockSpec((1, LANES), const),
            pl.BlockSpec((tm, tm), const),
        ],
        out_specs=(
            pl.BlockSpec((tm, d), row),
            pl.BlockSpec((tm, d), row),
            pl.BlockSpec((tm, LANES), row),
            pl.BlockSpec((1, LANES), const),
        ),
        out_shape=(
            jax.ShapeDtypeStruct((n, d), F32),
            jax.ShapeDtypeStruct((n, d), F32),
            jax.ShapeDtypeStruct((n, LANES), F32),
            jax.ShapeDtypeStruct((1, LANES), F32),
        ),
        scratch_shapes=[pltpu.VMEM((1, LANES), F32)],
        compiler_params=_params("arbitrary"),
        name="out_proj_router",
    )(x, ret_p, ret_s, s5_p, s5_s, fox_p, fox_s, w_out_fast, w_out, g2, w_r, b_r, tri)


def _dispatch_kernel(dest_ref, h_ref, rows_in_ref, rows_ref, sem):
    del rows_in_ref
    i = pl.program_id(0)
    tm = h_ref.shape[0]

    def row_copy(r, d):
        return pltpu.make_async_copy(h_ref.at[pl.ds(r, 1)], rows_ref.at[pl.ds(d, 1)], sem)

    def issue(r, _):
        t = i * tm + r
        row_copy(r, dest_ref[2 * t]).start()
        row_copy(r, dest_ref[2 * t + 1]).start()
        return 0

    lax.fori_loop(0, tm, issue, 0)

    def drain(r, _):
        row_copy(0, 0).wait()
        row_copy(0, 0).wait()
        return 0

    lax.fori_loop(0, tm, drain, 0)


def _dispatch(dest, h2, n_rows):
    n, d = h2.shape
    tm = TOKEN_TILE
    grid_spec = pltpu.PrefetchScalarGridSpec(
        num_scalar_prefetch=1,
        grid=(n // tm,),
        in_specs=[
            pl.BlockSpec((tm, d), lambda i, dest: (i, 0)),
            pl.BlockSpec(memory_space=pl.ANY),
        ],
        out_specs=pl.BlockSpec(memory_space=pl.ANY),
        scratch_shapes=[pltpu.SemaphoreType.DMA(())],
    )
    return pl.pallas_call(
        _dispatch_kernel,
        grid_spec=grid_spec,
        out_shape=jax.ShapeDtypeStruct((n_rows, d), F32),
        input_output_aliases={2: 0},
        compiler_params=_params("arbitrary"),
        name="moe_dispatch",
    )(dest, h2, jnp.zeros((n_rows, d), F32))


def _experts_kernel(be_ref, nu_ref, x_ref, w1_ref, w3_ref, w2_ref, y_ref, w1b, w3b, w2b):
    i = pl.program_id(0)
    prev = be_ref[jnp.maximum(i - 1, 0)]
    fresh = (i == 0) | (be_ref[i] != prev)
    active = i < nu_ref[0]

    @pl.when(active & fresh)
    def _():
        w1b[...] = w1_ref[0, 0].astype(BF16)
        w3b[...] = w3_ref[0, 0].astype(BF16)
        w2b[...] = w2_ref[0, 0].astype(BF16)

    @pl.when(active)
    def _():
        xb = x_ref[...].astype(BF16)
        h1 = _dot(xb, w1b[...])
        h3 = _dot(xb, w3b[...])
        a = (h1 * _sigmoid(h1) * h3).astype(BF16)
        y_ref[...] = _dot(a, w2b[...])

    @pl.when(jnp.logical_not(active))
    def _():
        y_ref[...] = jnp.zeros_like(y_ref)


def _experts(layer, blk_e, n_used, x_rows, w1, w3, w2):
    n_rows, d = x_rows.shape
    de = w1.shape[3]
    nblk = n_rows // MOE_ROWS

    def last_used(i, nu):
        return jnp.minimum(i, jnp.maximum(nu[0] - 1, 0))

    def xmap(i, be, nu):
        return (last_used(i, nu), 0)

    def wmap(i, be, nu):
        return (layer, be[last_used(i, nu)], 0, 0)

    grid_spec = pltpu.PrefetchScalarGridSpec(
        num_scalar_prefetch=2,
        grid=(nblk,),
        in_specs=[
            pl.BlockSpec((MOE_ROWS, d), xmap),
            pl.BlockSpec((1, 1, d, de), wmap),
            pl.BlockSpec((1, 1, d, de), wmap),
            pl.BlockSpec((1, 1, de, d), wmap),
        ],
        out_specs=pl.BlockSpec((MOE_ROWS, d), lambda i, be, nu: (i, 0)),
        scratch_shapes=[
            pltpu.VMEM((d, de), BF16),
            pltpu.VMEM((d, de), BF16),
            pltpu.VMEM((de, d), BF16),
        ],
    )
    return pl.pallas_call(
        _experts_kernel,
        grid_spec=grid_spec,
        out_shape=jax.ShapeDtypeStruct((n_rows, d), F32),
        compiler_params=_params("arbitrary"),
        name="moe_experts",
    )(blk_e, n_used, x_rows, w1, w3, w2)


def _combine_kernel(final, dest_ref, x1_ref, info_ref, gf_ref, y_ref, o_ref, buf, sem):
    i = pl.program_id(0)
    tm = x1_ref.shape[0]

    def row_copy(k, r, d):
        return pltpu.make_async_copy(y_ref.at[pl.ds(d, 1)], buf.at[k, pl.ds(r, 1)], sem)

    def issue(r, _):
        t = i * tm + r
        row_copy(0, r, dest_ref[2 * t]).start()
        row_copy(1, r, dest_ref[2 * t + 1]).start()
        return 0

    lax.fori_loop(0, tm, issue, 0)

    def drain(r, _):
        row_copy(0, 0, 0).wait()
        row_copy(1, 0, 0).wait()
        return 0

    lax.fori_loop(0, tm, drain, 0)
    info = info_ref[...]
    x2 = x1_ref[...] + info[:, 4:5] * buf[0] + info[:, 5:6] * buf[1]
    if final:
        ms = jnp.mean(x2 * x2, axis=-1, keepdims=True)
        x2 = x2 * lax.rsqrt(ms + RMS_EPS) * gf_ref[...]
    o_ref[...] = x2


def _combine(dest, x1, info, gf, y_rows, final):
    n, d = x1.shape
    tm = TOKEN_TILE
    grid_spec = pltpu.PrefetchScalarGridSpec(
        num_scalar_prefetch=1,
        grid=(n // tm,),
        in_specs=[
            pl.BlockSpec((tm, d), lambda i, dest: (i, 0)),
            pl.BlockSpec((tm, LANES), lambda i, dest: (i, 0)),
            pl.BlockSpec((1, d), lambda i, dest: (0, 0)),
            pl.BlockSpec(memory_space=pl.ANY),
        ],
        out_specs=pl.BlockSpec((tm, d), lambda i, dest: (i, 0)),
        scratch_shapes=[
            pltpu.VMEM((2, tm, d), F32),
            pltpu.SemaphoreType.DMA(()),
        ],
    )
    return pl.pallas_call(
        functools.partial(_combine_kernel, final),
        grid_spec=grid_spec,
        out_shape=jax.ShapeDtypeStruct((n, d), F32),
        compiler_params=_params("arbitrary"),
        name="moe_combine",
    )(dest, x1, info, gf, y_rows)


def _moe_plan(info, counts, n_tokens):
    eid = info[:, 0:2].astype(jnp.int32)
    rank = info[:, 2:4].astype(jnp.int32)
    cnt = counts[0, :N_EXPERTS].astype(jnp.int32)
    padded = (cnt + MOE_ROWS - 1) // MOE_ROWS * MOE_ROWS
    pend = jnp.cumsum(padded)
    pstart = pend - padded
    dest = (pstart[eid] + rank).reshape(-1)
    nblk = -(-n_tokens * 2 // MOE_ROWS) + N_EXPERTS
    blk_start = jnp.arange(nblk, dtype=jnp.int32) * MOE_ROWS
    blk_e = jnp.minimum(jnp.sum((pend[None, :] <= blk_start[:, None]).astype(jnp.int32), axis=1), N_EXPERTS - 1)
    n_used = (pend[-1:] // MOE_ROWS).astype(jnp.int32)
    return dest.astype(jnp.int32), blk_e.astype(jnp.int32), n_used, nblk * MOE_ROWS


def _rope_tables(seq, past, tnew, tile):
    half = HEAD_DIM // 2
    inv = ROPE_THETA ** (-jnp.arange(half, dtype=F32) / half)
    pos = jnp.concatenate([jnp.arange(seq, dtype=F32), past + (jnp.arange(tile) % tnew).astype(F32)])
    ang = pos[:, None] * inv[None, :]
    cos = jnp.cos(ang)
    sin = jnp.sin(ang)
    cos_h = jnp.concatenate([cos, cos], axis=1)
    sin_h = jnp.concatenate([-sin, sin], axis=1)
    return jnp.tile(cos_h, (1, H_RET)), jnp.tile(sin_h, (1, H_RET))


def _block_diag_states(s):
    b = s.shape[0]
    eye = jnp.eye(H_RET, dtype=s.dtype)
    return jnp.einsum('bhde,hg->bhdge', s, eye).reshape(b, RET_W, RET_W)


def _diag_blocks(s_bd):
    b = s_bd.shape[0]
    s5 = s_bd.reshape(b, H_RET, HEAD_DIM, H_RET, HEAD_DIM)
    return jnp.stack([s5[:, h, :, h, :] for h in range(H_RET)], axis=1)


def kernel(x_prompt, x_sample, cache_k, cache_v, cache_logf, state_ret, state_s5, page_table, norm1, w_in, ret_gn,
           s5_lam_re, s5_lam_im, s5_b_re, s5_b_im, s5_c_re, s5_c_im, s5_d, s5_log_dt, s5_glu_w, s5_glu_b, fox_fb,
           w_out, norm2, w_rg, b_rg, w_re, b_re, w1, w3, w2, norm_f):
    bp, seq, d = x_prompt.shape
    bd_, tnew, _ = x_sample.shape
    depth = w_in.shape[0]
    n_phys, page = cache_k.shape[1], cache_k.shape[2]
    npages = page_table.shape[1]
    past = npages * page
    n_p = bp * seq
    n_s = bd_ * tnew
    n = n_p + n_s
    tm = TOKEN_TILE
    assert n_p % tm == 0 and n_s == tm and seq % tm == 0
    n_prompt_tiles = n_p // tm

    x = jnp.concatenate([x_prompt.reshape(n_p, d), x_sample.reshape(n_s, d)], axis=0)
    cos_t, sin_t = _rope_tables(seq, past, tnew, tm)
    cache_k2 = cache_k.reshape(depth, n_phys, page * H_FOX, HEAD_DIM)
    cache_v2 = cache_v.reshape(depth, n_phys, page * H_FOX, HEAD_DIM)
    cache_lft = jnp.swapaxes(cache_logf, 2, 3)
    ff0 = w_in.shape[2] - H_FOX

    outs = {k: [] for k in ('kp', 'vp', 'lp', 'ks', 'vs', 'ls', 'rp', 'rs', 'sp', 'ss')}
    y = None
    for l in range(depth):
        w_main = w_in[l, :, :ff0]
        w_ff = jnp.pad(w_in[l, :, ff0:], ((0, 0), (0, LANES - H_FOX)))
        fb = jnp.pad(fox_fb[l], (0, LANES - H_FOX))[None, :]
        g1 = norm1[l][None, :]
        ret_in, su, fq, fk, fv, fkb, fvb, logf = _in_proj(
            _FAST, x, g1, w_main.astype(BF16), w_ff.astype(BF16), fb, cos_t, sin_t,
            0, n_prompt_tiles, 0, seq // tm)
        ret_in_s, su_s, fq_s, fk_s, fv_s, _, _, logf_s = _in_proj(
            _EXACT, x, g1, w_main, w_ff, fb, cos_t, sin_t, n_prompt_tiles, 1, seq // tm, 1)

        gw = ret_gn[l][None, :]
        zero_state = jnp.zeros((bp, RET_W, RET_W), F32)
        ret_p, rs_p = _retention(_FAST, ret_in, zero_state, gw, bp, 0, seq, 1, min(RET_CHUNK, seq),
                                 min(1024, seq))
        seq_per_chunk = 8
        chunk_rows = seq_per_chunk * tnew
        ret_s, rs_s = _retention(_EXACT, ret_in_s, _block_diag_states(state_ret[l]), gw, bd_ // seq_per_chunk, 0,
                                 chunk_rows, seq_per_chunk, tnew, chunk_rows)

        tables = _s5_tables(s5_lam_re[l], s5_lam_im[l], s5_b_re[l], s5_b_im[l], s5_c_re[l], s5_c_im[l],
                            s5_log_dt[l])
        dsk = s5_d[l][None, :]
        glub = s5_glu_b[l][None, :]
        zs = jnp.zeros((bp, 1, S5_LANES), F32)
        s5_p, sp_r, sp_i = _s5(_FAST, su, zs, zs, tables, dsk, s5_glu_w[l], glub, bp, 0, seq, min(512, seq),
                               False)
        x0 = state_s5[l].reshape(bd_, 1, S5_LANES, 2)
        s5_s, ss_r, ss_i = _s5(_EXACT, su_s, x0[..., 0], x0[..., 1], tables, dsk, s5_glu_w[l], glub, 1, 0, n_s,
                               n_s, True)

        qa, ka, va = _fox_prep(logf, fq, fkb, fvb, bp, seq, min(512, seq))
        fox_p = _fox_prompt(qa, ka, va, bp, seq)
        rows_th = (bd_, tnew * H_FOX, HEAD_DIM)
        lfnew = jnp.pad(jnp.swapaxes(logf_s[:, :H_FOX].reshape(bd_, tnew, H_FOX), 1, 2),
                        ((0, 0), (0, 0), (0, page - tnew)))
        fox_s = _fox_sample(l, page_table, fq_s.reshape(rows_th), cache_k2, cache_v2, cache_lft,
                            fk_s.reshape(rows_th), fv_s.reshape(rows_th), lfnew)
        fox_s = fox_s.reshape(n_s, FOX_W)

        w_r = jnp.pad(jnp.concatenate([w_re[l], w_rg[l]], axis=1),
                      ((0, 0), (0, LANES - N_EXPERTS - N_EXPERT_GROUPS)))
        b_r = jnp.pad(jnp.concatenate([b_re[l], b_rg[l]]), (0, LANES - N_EXPERTS - N_EXPERT_GROUPS))[None, :]
        x1, h2, info, counts = _out_proj(x, ret_p, ret_s, s5_p, s5_s, fox_p, fox_s, w_out[l],
                                         norm2[l][None, :], w_r, b_r, n_prompt_tiles)
        dest, blk_e, n_used, n_rows = _moe_plan(info, counts, n)
        x_rows = _dispatch(dest, h2, n_rows)
        y_rows = _experts(l, blk_e, n_used, x_rows, w1, w3, w2)
        final = l == depth - 1
        x = _combine(dest, x1, info, norm_f[None, :], y_rows, final)

        outs['kp'].append(fk.reshape(bp, seq, H_FOX, HEAD_DIM))
        outs['vp'].append(fv.reshape(bp, seq, H_FOX, HEAD_DIM))
        outs['lp'].append(logf[:, :H_FOX].reshape(bp, seq, H_FOX))
        outs['ks'].append(fk_s.reshape(bd_, tnew, H_FOX, HEAD_DIM))
        outs['vs'].append(fv_s.reshape(bd_, tnew, H_FOX, HEAD_DIM))
        outs['ls'].append(logf_s[:, :H_FOX].reshape(bd_, tnew, H_FOX))
        outs['rp'].append(_diag_blocks(rs_p))
        outs['rs'].append(_diag_blocks(rs_s))
        outs['sp'].append(jnp.stack([sp_r.reshape(bp, S5_GROUPS, S5_STATE),
                                     sp_i.reshape(bp, S5_GROUPS, S5_STATE)], axis=-1))
        outs['ss'].append(jnp.stack([ss_r.reshape(bd_, S5_GROUPS, S5_STATE),
                                     ss_i.reshape(bd_, S5_GROUPS, S5_STATE)], axis=-1))

    y_prompt = x[:n_p].reshape(bp, seq, d)
    y_sample = x[n_p:].reshape(bd_, tnew, d)
    st = {k: jnp.stack(v) for k, v in outs.items()}
    return (y_prompt, y_sample, st['kp'], st['vp'], st['lp'], st['ks'], st['vs'], st['ls'],
            st['rp'], st['rs'], st['sp'], st['ss'])
```

```python
import functools
import math

import numpy as np

import jax
import jax.numpy as jnp
from jax import lax
from jax.experimental import pallas as pl
from jax.experimental.pallas import tpu as pltpu

F32 = jnp.float32
BF16 = jnp.bfloat16
HIGHEST = lax.Precision.HIGHEST

HEAD_DIM = 64
H_RET = 4
H_FOX = 8
RET_W = H_RET * HEAD_DIM
S5_W = 256
FOX_W = H_FOX * HEAD_DIM
S5_GROUPS = 16
S5_GROUP_CH = 16
S5_STATE = 64
S5_LANES = S5_GROUPS * S5_STATE
N_EXPERTS = 32
N_EXPERT_GROUPS = 4
EXPERTS_PER_GROUP = 8
RET_CHUNK = 128
ROPE_THETA = 10000.0
RMS_EPS = 1e-6
NEG_BIG = -1e30
LOG2E = math.log2(math.e)

LANES = 128
SUBLANES = 8
TOKEN_TILE = 256
MOE_ROWS = 256
FOX_TQ = 512
PAGES_PER_STEP = 8
VMEM_LIMIT = 48 * 1024 * 1024


def _params(*sem):
    return pltpu.CompilerParams(dimension_semantics=sem, vmem_limit_bytes=VMEM_LIMIT)


def _sigmoid(x):
    return 1.0 / (1.0 + jnp.exp(-x))


def _dot(a, b):
    return jnp.dot(a, b, preferred_element_type=F32)


def _dot_nt(a, b):
    return lax.dot_general(a, b, (((1,), (1,)), ((), ())), preferred_element_type=F32)


def _dot_tn(a, b):
    return lax.dot_general(a, b, (((0,), (0,)), ((), ())), preferred_element_type=F32)


class _MatmulMode:
    def __init__(self, exact):
        self.dtype = F32 if exact else BF16
        self.precision = HIGHEST if exact else None

    def cast(self, x):
        return x.astype(self.dtype)

    def dot(self, a, b):
        return jnp.dot(a, b, precision=self.precision, preferred_element_type=F32)

    def dot_nt(self, a, b):
        return lax.dot_general(a, b, (((1,), (1,)), ((), ())), precision=self.precision,
                               preferred_element_type=F32)

    def dot_tn(self, a, b):
        return lax.dot_general(a, b, (((0,), (0,)), ((), ())), precision=self.precision,
                               preferred_element_type=F32)


_FAST = _MatmulMode(False)
_EXACT = _MatmulMode(True)


def _in_proj_kernel(mm, x_ref, g_ref, w_ref, wff_ref, fb_ref, cos_ref, sin_ref,
                    ret_ref, su_ref, fq_ref, fk_ref, fv_ref, fkb_ref, fvb_ref, logf_ref):
    _dot = mm.dot
    x = x_ref[...]
    ms = jnp.mean(x * x, axis=-1, keepdims=True)
    h = mm.cast(x * lax.rsqrt(ms + RMS_EPS) * g_ref[...])
    tm = x.shape[0]
    cos = cos_ref[...]
    sin = sin_ref[...]
    lane = lax.broadcasted_iota(jnp.int32, (tm, RET_W), 1)
    first_half = (lane % HEAD_DIM) < (HEAD_DIM // 2)

    def rope(t):
        swapped = jnp.where(first_half,
                            pltpu.roll(t, RET_W - HEAD_DIM // 2, axis=1),
                            pltpu.roll(t, HEAD_DIM // 2, axis=1))
        return t * cos + swapped * sin

    ret_ref[:, 0:RET_W] = rope(_dot(h, w_ref[:, 0:RET_W]))
    ret_ref[:, RET_W:2 * RET_W] = rope(_dot(h, w_ref[:, RET_W:2 * RET_W])) * (HEAD_DIM ** -0.5)
    ret_ref[:, 2 * RET_W:4 * RET_W] = _dot(h, w_ref[:, 2 * RET_W:4 * RET_W])
    c0 = 4 * RET_W
    su_ref[...] = _dot(h, w_ref[:, c0:c0 + S5_W])
    c0 += S5_W
    fq_ref[...] = (_dot(h, w_ref[:, c0:c0 + FOX_W]) * (HEAD_DIM ** -0.5 * LOG2E)).astype(BF16)
    c0 += FOX_W
    fk = _dot(h, w_ref[:, c0:c0 + FOX_W])
    fk_ref[...] = fk
    fkb_ref[...] = fk.astype(BF16)
    c0 += FOX_W
    fv = _dot(h, w_ref[:, c0:c0 + FOX_W])
    fv_ref[...] = fv
    fvb_ref[...] = fv.astype(BF16)
    z = _dot(h, wff_ref[...]) + fb_ref[...]
    logf = jnp.minimum(z, 0.0) - jnp.log1p(jnp.exp(-jnp.abs(z)))
    lane128 = lax.broadcasted_iota(jnp.int32, (tm, LANES), 1)
    logf_ref[...] = jnp.where(lane128 < H_FOX, logf, 0.0)


def _in_proj(mm, x, g, w_main, w_ff, fb, cos_t, sin_t, tile0, ntiles, pos_tile0, pos_tiles):
    tm = TOKEN_TILE
    n = ntiles * tm
    wcols = w_main.shape[1]

    def xrow(i):
        return (tile0 + i, 0)

    def row(i):
        return (i, 0)

    def const(i):
        return (0, 0)

    def pos_map(i):
        return (pos_tile0 + i % pos_tiles, 0)

    outs = (
        jax.ShapeDtypeStruct((n, 4 * RET_W), F32),
        jax.ShapeDtypeStruct((n, S5_W), F32),
        jax.ShapeDtypeStruct((n, FOX_W), BF16),
        jax.ShapeDtypeStruct((n, FOX_W), F32),
        jax.ShapeDtypeStruct((n, FOX_W), F32),
        jax.ShapeDtypeStruct((n, FOX_W), BF16),
        jax.ShapeDtypeStruct((n, FOX_W), BF16),
        jax.ShapeDtypeStruct((n, LANES), F32),
    )
    return pl.pallas_call(
        functools.partial(_in_proj_kernel, mm),
        grid=(ntiles,),
        in_specs=[
            pl.BlockSpec((tm, x.shape[1]), xrow),
            pl.BlockSpec((1, x.shape[1]), const),
            pl.BlockSpec((x.shape[1], wcols), const),
            pl.BlockSpec((x.shape[1], LANES), const),
            pl.BlockSpec((1, LANES), const),
            pl.BlockSpec((tm, RET_W), pos_map),
            pl.BlockSpec((tm, RET_W), pos_map),
        ],
        out_specs=(
            pl.BlockSpec((tm, 4 * RET_W), row),
            pl.BlockSpec((tm, S5_W), row),
            pl.BlockSpec((tm, FOX_W), row),
            pl.BlockSpec((tm, FOX_W), row),
            pl.BlockSpec((tm, FOX_W), row),
            pl.BlockSpec((tm, FOX_W), row),
            pl.BlockSpec((tm, FOX_W), row),
            pl.BlockSpec((tm, LANES), row),
        ),
        out_shape=outs,
        compiler_params=_params("arbitrary"),
        name="in_proj",
    )(x, g, w_main, w_ff, fb, cos_t, sin_t)


def _retention_kernel(mm, nseq, seq_len, nchunk, x_ref, s0_ref, dm_ref, qd_ref, kd_ref, cd_ref, bd_ref,
                      gn_ref, gw_ref, o_ref, sout_ref, s_scr):
    cast, _dot, _dot_nt, _dot_tn = mm.cast, mm.dot, mm.dot_nt, mm.dot_tn
    step = pl.program_id(1)
    rows = nseq * seq_len

    @pl.when(step == 0)
    def _():
        s_scr[...] = s0_ref[...]

    lane_head = lax.broadcasted_iota(jnp.int32, (rows, RET_W), 1) // HEAD_DIM
    row_seq = lax.broadcasted_iota(jnp.int32, (rows, 1), 0) // seq_len

    def chunk(c, _):
        r0 = pl.multiple_of(c * rows, rows)
        q = x_ref[pl.ds(r0, rows), 0:RET_W]
        k = x_ref[pl.ds(r0, rows), RET_W:2 * RET_W]
        v = x_ref[pl.ds(r0, rows), 2 * RET_W:3 * RET_W]
        g = x_ref[pl.ds(r0, rows), 3 * RET_W:4 * RET_W]
        qb = cast(q)
        kb = cast(k)
        vb = cast(v)
        ps = []
        vs = []
        for h in range(H_RET):
            hm = lane_head == h
            sc = _dot_nt(cast(jnp.where(hm, q, 0.0)), kb)
            ps.append(cast(sc * dm_ref[h]))
            vs.append(cast(jnp.where(hm, v, 0.0)))
        inner = _dot(jnp.concatenate(ps, axis=1), jnp.concatenate(vs, axis=0))
        kdec = k * kd_ref[...]
        cd = cd_ref[...]
        bd = bd_ref[...]
        if nseq == 1:
            s_old = s_scr[0]
            cross = _dot(qb, cast(s_old))
            s_scr[0] = s_old * cd + _dot_tn(cast(kdec), vb) * bd
        else:
            def seq_body(s, cross):
                rm = row_seq == s
                s_old = s_scr[s]
                cross = jnp.where(rm, _dot(qb, cast(s_old)), cross)
                kv = _dot_tn(cast(jnp.where(rm, kdec, 0.0)), vb)
                s_scr[s] = s_old * cd + kv * bd
                return cross

            cross = lax.fori_loop(0, nseq, seq_body, jnp.zeros((rows, RET_W), F32))
        o = inner + cross * qd_ref[...]
        o2 = o * o
        hi = o2.astype(BF16)
        lo = (o2 - hi.astype(F32)).astype(BF16)
        ms = jnp.dot(hi, gn_ref[...], preferred_element_type=F32) + jnp.dot(
            lo, gn_ref[...], preferred_element_type=F32)
        on = o * lax.rsqrt(ms + RMS_EPS)
        o_ref[pl.ds(r0, rows), :] = (g * _sigmoid(g) * on * gw_ref[...]).astype(o_ref.dtype)
        return 0

    lax.fori_loop(0, nchunk, chunk, 0)

    @pl.when(step == pl.num_programs(1) - 1)
    def _():
        sout_ref[...] = s_scr[...]


def _retention_tables(nseq, seq_len):
    rows = nseq * seq_len
    log_gamma = jnp.log1p(-jnp.power(2.0, -5.0 - jnp.arange(H_RET, dtype=F32)))
    idx = jnp.arange(rows, dtype=jnp.int32)
    pos = (idx % seq_len).astype(F32)
    seq = idx // seq_len
    diff = pos[:, None] - pos[None, :]
    ok = (seq[:, None] == seq[None, :]) & (diff >= 0)
    dm = jnp.where(ok[None], jnp.exp(log_gamma[:, None, None] * jnp.maximum(diff, 0.0)[None]), 0.0)
    lg_lane = jnp.repeat(log_gamma, HEAD_DIM)
    qd = jnp.exp((pos[:, None] + 1.0) * lg_lane[None, :])
    kd = jnp.exp((seq_len - 1.0 - pos[:, None]) * lg_lane[None, :])
    cd = jnp.exp(seq_len * lg_lane)[None, :]
    head = jnp.arange(RET_W) // HEAD_DIM
    bd = (head[:, None] == head[None, :]).astype(F32)
    gn = (bd / HEAD_DIM).astype(BF16)
    return dm, qd, kd, cd, bd, gn


def _retention(mm, ret_in, s0_bd, gw, nbatch, row0, seq_total, nseq, seq_len, rows_per_step):
    rows = nseq * seq_len
    nchunk = rows_per_step // rows
    steps = seq_total // rows_per_step
    blk0 = row0 // rows_per_step
    dm, qd, kd, cd, bd, gn = _retention_tables(nseq, seq_len)

    def rmap(b, i):
        return (blk0 + b * steps + i, 0)

    def omap(b, i):
        return (b * steps + i, 0)

    def c2(b, i):
        return (0, 0)

    def c3(b, i):
        return (0, 0, 0)

    def smap(b, i):
        return (b, 0, 0)

    in_specs = [
        pl.BlockSpec((rows_per_step, 4 * RET_W), rmap),
        pl.BlockSpec((nseq, RET_W, RET_W), smap),
        pl.BlockSpec((H_RET, rows, rows), c3),
        pl.BlockSpec((rows, RET_W), c2),
        pl.BlockSpec((rows, RET_W), c2),
        pl.BlockSpec((1, RET_W), c2),
        pl.BlockSpec((RET_W, RET_W), c2),
        pl.BlockSpec((RET_W, RET_W), c2),
        pl.BlockSpec((1, RET_W), c2),
    ]
    return pl.pallas_call(
        functools.partial(_retention_kernel, mm, nseq, seq_len, nchunk),
        grid=(nbatch, steps),
        in_specs=in_specs,
        out_specs=(
            pl.BlockSpec((rows_per_step, RET_W), omap),
            pl.BlockSpec((nseq, RET_W, RET_W), smap),
        ),
        out_shape=(
            jax.ShapeDtypeStruct((nbatch * seq_total, RET_W), mm.dtype),
            jax.ShapeDtypeStruct((nbatch * nseq, RET_W, RET_W), F32),
        ),
        scratch_shapes=[pltpu.VMEM((nseq, RET_W, RET_W), F32)],
        compiler_params=_params("arbitrary", "arbitrary"),
        name="retention",
    )(ret_in, s0_bd, dm, qd, kd, cd, bd, gn, gw)


def _s5_kernel(mm, short, u_ref, x0r_ref, x0i_ref, bre_ref, bim_ref, pre_ref, pim_ref, cm_ref, d_ref,
               gw_ref, gb_ref, o_ref, str_ref, sti_ref, bur, bui, car, cai):
    cast, _dot = mm.cast, mm.dot
    step = pl.program_id(1)
    rows = u_ref.shape[0]
    u = u_ref[...]
    ub = cast(u)
    bur[...] = _dot(ub, bre_ref[...])
    bui[...] = _dot(ub, bim_ref[...])

    if not short:
        @pl.when(step == 0)
        def _():
            car[...] = x0r_ref[0]
            cai[...] = x0i_ref[0]

    pr = pre_ref[...]
    pi = pim_ref[...]
    rowi = lax.broadcasted_iota(jnp.int32, (SUBLANES, S5_LANES), 0)

    def group(j, carry):
        cr, ci = carry
        r0 = pl.multiple_of(j * SUBLANES, SUBLANES)
        xr = bur[pl.ds(r0, SUBLANES), :]
        xi = bui[pl.ds(r0, SUBLANES), :]
        for s in (1, 2, 4):
            ar = pr[s - 1:s]
            ai = pi[s - 1:s]
            sr = jnp.where(rowi >= s, pltpu.roll(xr, s, axis=0), 0.0)
            si = jnp.where(rowi >= s, pltpu.roll(xi, s, axis=0), 0.0)
            xr, xi = xr + ar * sr - ai * si, xi + ar * si + ai * sr
        if short:
            cr = x0r_ref[j]
            ci = x0i_ref[j]
        xr, xi = xr + pr * cr - pi * ci, xi + pr * ci + pi * cr
        bur[pl.ds(r0, SUBLANES), :] = xr
        bui[pl.ds(r0, SUBLANES), :] = xi
        cr = xr[SUBLANES - 1:SUBLANES]
        ci = xi[SUBLANES - 1:SUBLANES]
        if short:
            str_ref[j] = cr
            sti_ref[j] = ci
        return cr, ci

    if short:
        init = (jnp.zeros((1, S5_LANES), F32), jnp.zeros((1, S5_LANES), F32))
    else:
        init = (car[...], cai[...])
    cr, ci = lax.fori_loop(0, rows // SUBLANES, group, init)
    if not short:
        car[...] = cr
        cai[...] = ci

        @pl.when(step == pl.num_programs(1) - 1)
        def _():
            str_ref[0] = cr
            sti_ref[0] = ci

    xs = jnp.concatenate([cast(bur[...]), cast(bui[...])], axis=1)
    y = _dot(xs, cm_ref[...]) + d_ref[...] * u
    yg = 0.5 * y * (1.0 + jnp.tanh(math.sqrt(2.0 / math.pi) * (y + 0.044715 * (y * y * y))))
    z = _dot(cast(yg), gw_ref[...]) + gb_ref[...]
    o_ref[...] = (yg * _sigmoid(z)).astype(o_ref.dtype)


def _s5_tables(lam_re, lam_im, b_re, b_im, c_re, c_im, log_dt):
    lam = lax.complex(lam_re.astype(F32), lam_im.astype(F32))
    dt = jnp.exp(log_dt.astype(F32))[:, None]
    lam_bar = jnp.exp(lam * dt)
    b_bar = ((lam_bar - 1.0) / lam)[:, :, None] * lax.complex(b_re.astype(F32), b_im.astype(F32))
    k = jnp.arange(1, SUBLANES + 1, dtype=F32)[:, None, None]
    powers = jnp.exp((lam * dt)[None] * k)
    pre = jnp.real(powers).reshape(SUBLANES, S5_LANES)
    pim = jnp.imag(powers).reshape(SUBLANES, S5_LANES)
    eye = jnp.eye(S5_GROUPS, dtype=F32)
    bre = jnp.einsum('gpc,gh->gchp', jnp.real(b_bar), eye).reshape(S5_W, S5_LANES)
    bim = jnp.einsum('gpc,gh->gchp', jnp.imag(b_bar), eye).reshape(S5_W, S5_LANES)
    cre = jnp.einsum('gcp,gh->hpgc', c_re.astype(F32), eye).reshape(S5_LANES, S5_W)
    cim = jnp.einsum('gcp,gh->hpgc', c_im.astype(F32), eye).reshape(S5_LANES, S5_W)
    cm = jnp.concatenate([cre, -cim], axis=0)
    return bre, bim, pre, pim, cm


def _s5(mm, su, x0r, x0i, tables, d, glu_w, glu_b, nbatch, row0, seq_total, rows_per_step, short):
    bre, bim, pre, pim, cm = tables
    bre, bim, cm, glu_w = mm.cast(bre), mm.cast(bim), mm.cast(cm), mm.cast(glu_w)
    steps = seq_total // rows_per_step
    blk0 = row0 // rows_per_step
    nstate = x0r.shape[0]
    sblk = nstate if short else 1

    def rmap(b, i):
        return (blk0 + b * steps + i, 0)

    def omap(b, i):
        return (b * steps + i, 0)

    def c2(b, i):
        return (0, 0)

    def smap(b, i):
        return (0 if short else b, 0, 0)

    in_specs = [
        pl.BlockSpec((rows_per_step, S5_W), rmap),
        pl.BlockSpec((sblk, 1, S5_LANES), smap),
        pl.BlockSpec((sblk, 1, S5_LANES), smap),
        pl.BlockSpec((S5_W, S5_LANES), c2),
        pl.BlockSpec((S5_W, S5_LANES), c2),
        pl.BlockSpec((SUBLANES, S5_LANES), c2),
        pl.BlockSpec((SUBLANES, S5_LANES), c2),
        pl.BlockSpec((2 * S5_LANES, S5_W), c2),
        pl.BlockSpec((1, S5_W), c2),
        pl.BlockSpec((S5_W, S5_W), c2),
        pl.BlockSpec((1, S5_W), c2),
    ]
    return pl.pallas_call(
        functools.partial(_s5_kernel, mm, short),
        grid=(nbatch, steps),
        in_specs=in_specs,
        out_specs=(
            pl.BlockSpec((rows_per_step, S5_W), omap),
            pl.BlockSpec((sblk, 1, S5_LANES), smap),
            pl.BlockSpec((sblk, 1, S5_LANES), smap),
        ),
        out_shape=(
            jax.ShapeDtypeStruct((nbatch * seq_total, S5_W), mm.dtype),
            jax.ShapeDtypeStruct((nstate, 1, S5_LANES), F32),
            jax.ShapeDtypeStruct((nstate, 1, S5_LANES), F32),
        ),
        scratch_shapes=[
            pltpu.VMEM((rows_per_step, S5_LANES), F32),
            pltpu.VMEM((rows_per_step, S5_LANES), F32),
            pltpu.VMEM((1, S5_LANES), F32),
            pltpu.VMEM((1, S5_LANES), F32),
        ],
        compiler_params=_params("arbitrary", "arbitrary"),
        name="s5",
    )(su, x0r, x0i, bre, bim, pre, pim, cm, d, glu_w, glu_b)


_PAIR_W = 2 * LANES
_AUG_PER_HEAD = 6
_N_PAIRS = H_FOX // 2


def _aug_placement():
    pq = np.zeros((3 * LANES, _N_PAIRS * LANES), np.float32)
    pk = np.zeros((3 * LANES, _N_PAIRS * LANES), np.float32)
    oq = np.zeros((1, _N_PAIRS * LANES), np.float32)
    ok = np.zeros((1, _N_PAIRS * LANES), np.float32)
    for h in range(H_FOX):
        base = (h // 2) * LANES + (h % 2) * _AUG_PER_HEAD
        for part in range(3):
            pq[part * LANES + h, base + part] = 1.0
            oq[0, base + 3 + part] = 1.0
            ok[0, base + part] = 1.0
            pk[part * LANES + h, base + 3 + part] = -1.0
    return (jnp.asarray(pq, BF16), jnp.asarray(pk, BF16), jnp.asarray(oq), jnp.asarray(ok))


def _split3(v):
    hi = v.astype(BF16)
    r1 = v - hi.astype(F32)
    mid = r1.astype(BF16)
    lo = (r1 - mid.astype(F32)).astype(BF16)
    return jnp.concatenate([hi, mid, lo], axis=1)


def _fox_prep_kernel(lf_ref, q_ref, k_ref, v_ref, tri_ref, pq_ref, pk_ref, oq_ref, ok_ref,
                     qa_ref, ka_ref, va_ref, carry):
    @pl.when(pl.program_id(1) == 0)
    def _():
        carry[...] = jnp.zeros_like(carry)

    c = jnp.dot(tri_ref[...], lf_ref[...], precision=HIGHEST, preferred_element_type=F32) + carry[...]
    rows = c.shape[0]
    carry[...] = c[rows - 1:rows, :]
    parts = _split3(c * LOG2E)
    augq = (_dot(parts, pq_ref[...]) + oq_ref[...]).astype(BF16)
    augk = (_dot(parts, pk_ref[...]) + ok_ref[...]).astype(BF16)
    lane = lax.broadcasted_iota(jnp.int32, (rows, LANES), 1)
    one = jnp.ones((rows, LANES), BF16)
    for p in range(_N_PAIRS):
        src = slice(p * LANES, (p + 1) * LANES)
        qa_ref[:, p * _PAIR_W:p * _PAIR_W + LANES] = q_ref[:, src]
        qa_ref[:, p * _PAIR_W + LANES:(p + 1) * _PAIR_W] = augq[:, src]
        ka_ref[:, p * _PAIR_W:p * _PAIR_W + LANES] = k_ref[:, src]
        ka_ref[:, p * _PAIR_W + LANES:(p + 1) * _PAIR_W] = augk[:, src]
        vp = v_ref[:, src]
        va_ref[:, p * _PAIR_W:p * _PAIR_W + LANES] = jnp.where(lane < HEAD_DIM, vp, one)
        va_ref[:, p * _PAIR_W + LANES:(p + 1) * _PAIR_W] = jnp.where(lane < HEAD_DIM, one, vp)


def _fox_prep(logf, fq, fkb, fvb, nbatch, seq_total, rows_per_step):
    steps = seq_total // rows_per_step
    idx = jnp.arange(rows_per_step)
    tri = (idx[:, None] >= idx[None, :]).astype(F32)
    pq, pk, oq, ok = _aug_placement()
    wide = _N_PAIRS * _PAIR_W

    def rmap(b, i):
        return (b * steps + i, 0)

    def const(b, i):
        return (0, 0)

    out = jax.ShapeDtypeStruct((nbatch * seq_total, wide), BF16)
    return pl.pallas_call(
        _fox_prep_kernel,
        grid=(nbatch, steps),
        in_specs=[
            pl.BlockSpec((rows_per_step, LANES), rmap),
            pl.BlockSpec((rows_per_step, FOX_W), rmap),
            pl.BlockSpec((rows_per_step, FOX_W), rmap),
            pl.BlockSpec((rows_per_step, FOX_W), rmap),
            pl.BlockSpec((rows_per_step, rows_per_step), const),
            pl.BlockSpec(pq.shape, const),
            pl.BlockSpec(pk.shape, const),
            pl.BlockSpec(oq.shape, const),
            pl.BlockSpec(ok.shape, const),
        ],
        out_specs=(pl.BlockSpec((rows_per_step, wide), rmap),) * 3,
        out_shape=(out, out, out),
        scratch_shapes=[pltpu.VMEM((1, LANES), F32)],
        compiler_params=_params("arbitrary", "arbitrary"),
        name="fox_prep",
    )(logf, fq, fkb, fvb, tri, pq, pk, oq, ok)


_PAIRS_PER_STEP = 2


def _fox_prompt_kernel(tq, q_ref, k_ref, v_ref, o_ref, acc, m_scr):
    qi = pl.program_id(2)
    nhead = 2 * _PAIRS_PER_STEP
    lane = lax.broadcasted_iota(jnp.int32, (tq, _PAIR_W), 1)
    qs = []
    for h in range(nhead):
        par = h % 2
        q = q_ref[:, (h // 2) * _PAIR_W:(h // 2 + 1) * _PAIR_W]
        own = ((lane >= par * HEAD_DIM) & (lane < (par + 1) * HEAD_DIM)) | (
            (lane >= LANES + par * _AUG_PER_HEAD) & (lane < LANES + (par + 1) * _AUG_PER_HEAD))
        qs.append(jnp.where(own, q, jnp.zeros_like(q)))
        m_scr[h] = jnp.full((tq, LANES), NEG_BIG, F32)
        acc[h] = jnp.zeros((tq, LANES), F32)
    causal = (lax.broadcasted_iota(jnp.int32, (tq, tq), 0) >= lax.broadcasted_iota(jnp.int32, (tq, tq), 1))

    def kv_step(j, masked):
        ks = pl.multiple_of(j * tq, tq)
        for h in range(nhead):
            kb = k_ref[pl.ds(ks, tq), (h // 2) * _PAIR_W:(h // 2 + 1) * _PAIR_W]
            s = _dot_nt(qs[h], kb)
            if masked:
                s = jnp.where(causal, s, NEG_BIG)
            m_old = m_scr[h]
            m_new = jnp.maximum(m_old, jnp.max(s, axis=1, keepdims=True))
            p = jnp.exp2(s - jnp.concatenate([m_new] * (tq // LANES), axis=1))
            vb = v_ref[pl.ds(ks, tq), h * LANES:(h + 1) * LANES]
            acc[h] = jnp.exp2(m_old - m_new) * acc[h] + _dot(p.astype(BF16), vb)
            m_scr[h] = m_new

    def body(j, _):
        kv_step(j, False)
        return 0

    lax.fori_loop(0, qi, body, 0)
    kv_step(qi, True)
    lane128 = lax.broadcasted_iota(jnp.int32, (tq, LANES), 1)
    for pr in range(_PAIRS_PER_STEP):
        outs = []
        for par in range(2):
            a = acc[2 * pr + par]
            outs.append(a / pltpu.roll(a, HEAD_DIM, axis=1))
        o_ref[:, pr * LANES:(pr + 1) * LANES] = jnp.where(lane128 < HEAD_DIM, outs[0], outs[1]).astype(BF16)


def _fox_prompt(qa, ka, va, nbatch, seq_total):
    tq = min(FOX_TQ, seq_total)
    nq = seq_total // tq
    w = _PAIRS_PER_STEP * _PAIR_W
    return pl.pallas_call(
        functools.partial(_fox_prompt_kernel, tq),
        grid=(nbatch, _N_PAIRS // _PAIRS_PER_STEP, nq),
        in_specs=[
            pl.BlockSpec((tq, w), lambda b, g, i: (b * nq + i, g)),
            pl.BlockSpec((seq_total, w), lambda b, g, i: (b, g)),
            pl.BlockSpec((seq_total, w), lambda b, g, i: (b, g)),
        ],
        out_specs=pl.BlockSpec((tq, _PAIRS_PER_STEP * LANES), lambda b, g, i: (b * nq + i, g)),
        out_shape=jax.ShapeDtypeStruct((nbatch * seq_total, FOX_W), BF16),
        scratch_shapes=[
            pltpu.VMEM((2 * _PAIRS_PER_STEP, tq, LANES), F32),
            pltpu.VMEM((2 * _PAIRS_PER_STEP, tq, LANES), F32),
        ],
        compiler_params=_params("arbitrary", "arbitrary", "arbitrary"),
        name="fox_prompt",
    )(qa, ka, va)


def _fox_sample_kernel(npg, tnew, pt_ref, qm_ref, *refs):
    del pt_ref
    k_refs = refs[0:npg]
    v_refs = refs[npg:2 * npg]
    lf_refs = refs[2 * npg:3 * npg]
    knew_ref, vnew_ref, lfnew_ref, u_ref, sel_ref, o_ref, m_scr, l_scr, acc, run, kb_scr, vb_scr = refs[3 * npg:]
    g = pl.program_id(1)
    nrow = tnew * H_FOX
    npage = knew_ref.shape[1]

    @pl.when(g == 0)
    def _():
        m_scr[...] = jnp.full(m_scr.shape, NEG_BIG, F32)
        l_scr[...] = jnp.zeros(l_scr.shape, F32)
        acc[...] = jnp.zeros(acc.shape, F32)
        run[...] = jnp.zeros(run.shape, F32)

    qm = qm_ref[0]

    def cum_within(lft):
        return jnp.dot(lft, u_ref[...], precision=HIGHEST, preferred_element_type=F32)

    def attend(k, v, ckeys, mask, transposed):
        qk = _dot(qm, k) if transposed else _dot_nt(qm, k)
        s = qk - jnp.concatenate([ckeys * LOG2E] * tnew, axis=0)
        if mask is not None:
            s = jnp.where(mask, s, NEG_BIG)
        m_old = m_scr[...]
        m_new = jnp.maximum(m_old, jnp.max(s, axis=1, keepdims=True))
        p = jnp.exp2(s - m_new).astype(BF16)
        alpha = jnp.exp2(m_old - m_new)
        l_scr[...] = alpha * l_scr[...] + jnp.sum(p.astype(F32), axis=1, keepdims=True)
        acc[...] = alpha * acc[...] + (_dot_nt(p, v) if transposed else _dot(p, v))
        m_scr[...] = m_new

    base = run[...]
    cs = []
    for j in range(npg):
        kb_scr[:, j * npage:(j + 1) * npage] = k_refs[j][0, 0].astype(BF16)
        vb_scr[:, j * npage:(j + 1) * npage] = v_refs[j][0, 0].astype(BF16)
        cw = cum_within(lf_refs[j][0, 0])
        cs.append(base + cw)
        base = base + jnp.broadcast_to(cw[:, npage - 1:npage], base.shape)
    run[...] = base
    attend(kb_scr[...], vb_scr[...], jnp.concatenate(cs, axis=1), None, True)

    @pl.when(g == pl.num_programs(1) - 1)
    def _():
        rt = lax.broadcasted_iota(jnp.int32, (nrow, npage), 0) // H_FOX
        col = lax.broadcasted_iota(jnp.int32, (nrow, npage), 1)
        attend(knew_ref[0].astype(BF16), vnew_ref[0].astype(BF16), run[...] + cum_within(lfnew_ref[0]), col <= rt,
               False)
        o = acc[...] / l_scr[...]
        rh = lax.broadcasted_iota(jnp.int32, (nrow, FOX_W), 0) % H_FOX
        lh = lax.broadcasted_iota(jnp.int32, (nrow, FOX_W), 1) // HEAD_DIM
        om = jnp.where(rh == lh, o, 0.0)
        o_ref[0] = jnp.dot(sel_ref[...], om, precision=HIGHEST, preferred_element_type=F32)


def _fox_sample(layer, page_table, qm, cache_kt, cache_vt, cache_lft, knew, vnew, lfnew):
    nb, npages = page_table.shape
    page = cache_kt.shape[3]
    tnew = qm.shape[1] // H_FOX
    npg = PAGES_PER_STEP
    ngrp = npages // npg
    idx = jnp.arange(page)
    u = (idx[:, None] <= idx[None, :]).astype(F32)
    sel = (jnp.arange(tnew)[:, None] == (jnp.arange(tnew * H_FOX)[None, :] // H_FOX)).astype(F32)

    def pmap(j):
        def f(b, g, pt):
            return (layer, pt[b * npages + g * npg + j], 0, 0)
        return f

    def bmap(b, g, pt):
        return (b, 0, 0)

    def c2(b, g, pt):
        return (0, 0)

    in_specs = [pl.BlockSpec((1, tnew * H_FOX, FOX_W), bmap)]
    in_specs += [pl.BlockSpec((1, 1, FOX_W, page), pmap(j)) for j in range(npg)]
    in_specs += [pl.BlockSpec((1, 1, FOX_W, page), pmap(j)) for j in range(npg)]
    in_specs += [pl.BlockSpec((1, 1, H_FOX, page), pmap(j)) for j in range(npg)]
    in_specs += [
        pl.BlockSpec((1, page, FOX_W), bmap),
        pl.BlockSpec((1, page, FOX_W), bmap),
        pl.BlockSpec((1, H_FOX, page), bmap),
        pl.BlockSpec((page, page), c2),
        pl.BlockSpec((tnew, tnew * H_FOX), c2),
    ]
    grid_spec = pltpu.PrefetchScalarGridSpec(
        num_scalar_prefetch=1,
        grid=(nb, ngrp),
        in_specs=in_specs,
        out_specs=pl.BlockSpec((1, tnew, FOX_W), bmap),
        scratch_shapes=[
            pltpu.VMEM((tnew * H_FOX, 1), F32),
            pltpu.VMEM((tnew * H_FOX, 1), F32),
            pltpu.VMEM((tnew * H_FOX, FOX_W), F32),
            pltpu.VMEM((H_FOX, page), F32),
            pltpu.VMEM((FOX_W, npg * page), BF16),
            pltpu.VMEM((FOX_W, npg * page), BF16),
        ],
    )
    args = [page_table.reshape(-1), qm] + [cache_kt] * npg + [cache_vt] * npg + [cache_lft] * npg
    args += [knew, vnew, lfnew, u, sel]
    return pl.pallas_call(
        functools.partial(_fox_sample_kernel, npg, tnew),
        grid_spec=grid_spec,
        out_shape=jax.ShapeDtypeStruct((nb, tnew, FOX_W), F32),
        compiler_params=_params("arbitrary", "arbitrary"),
        name="fox_sample",
    )(*args)


_LANE_GROUP0 = N_EXPERTS


def _out_proj_kernel(n_prompt_tiles, x_ref, retp_ref, rets_ref, s5p_ref, s5s_ref, foxp_ref, foxs_ref, wo_ref,
                     wof_ref, g_ref, wr_ref, br_ref, tri_ref, x1_ref, h2_ref, info_ref, cnt_ref, carry):
    i = pl.program_id(0)

    @pl.when(i == 0)
    def _():
        carry[...] = jnp.zeros_like(carry)

    def project(mm, ret_ref, s5_ref, fox_ref, w_ref):
        x1 = x_ref[...] + mm.dot(ret_ref[...], w_ref[0:RET_W, :])
        x1 = x1 + mm.dot(s5_ref[...], w_ref[RET_W:RET_W + S5_W, :])
        x1_ref[...] = x1 + mm.dot(fox_ref[...], w_ref[RET_W + S5_W:, :])

    @pl.when(i < n_prompt_tiles)
    def _():
        project(_FAST, retp_ref, s5p_ref, foxp_ref, wo_ref)

    @pl.when(i >= n_prompt_tiles)
    def _():
        project(_EXACT, rets_ref, s5s_ref, foxs_ref, wof_ref)

    x1 = x1_ref[...]
    ms = jnp.mean(x1 * x1, axis=-1, keepdims=True)
    h2 = x1 * lax.rsqrt(ms + RMS_EPS) * g_ref[...]
    h2_ref[...] = h2
    logits = jnp.dot(h2, wr_ref[...], precision=HIGHEST, preferred_element_type=F32) + br_ref[...]
    tm = logits.shape[0]
    lane = lax.broadcasted_iota(jnp.int32, (tm, LANES), 1)
    big = jnp.int32(LANES)
    gmask = (lane >= _LANE_GROUP0) & (lane < _LANE_GROUP0 + N_EXPERT_GROUPS)
    gl = jnp.where(gmask, logits, NEG_BIG)
    gmax = jnp.max(gl, axis=1, keepdims=True)
    gsum = jnp.sum(jnp.where(gmask, jnp.exp(gl - gmax), 0.0), axis=1, keepdims=True)
    g_w = 1.0 / gsum
    g_idx = jnp.min(jnp.where(gl == gmax, lane, big), axis=1, keepdims=True) - _LANE_GROUP0
    lo = g_idx * EXPERTS_PER_GROUP
    emask = (lane >= lo) & (lane < lo + EXPERTS_PER_GROUP)
    el = jnp.where(emask, logits, NEG_BIG)
    e1 = jnp.max(el, axis=1, keepdims=True)
    idx1 = jnp.min(jnp.where(el == e1, lane, big), axis=1, keepdims=True)
    esum = jnp.sum(jnp.where(emask, jnp.exp(el - e1), 0.0), axis=1, keepdims=True)
    el2 = jnp.where(lane == idx1, NEG_BIG, el)
    e2 = jnp.max(el2, axis=1, keepdims=True)
    idx2 = jnp.min(jnp.where(el2 == e2, lane, big), axis=1, keepdims=True)
    p1 = 1.0 / esum
    p2 = jnp.exp(e2 - e1) / esum
    gate1 = g_w * p1 / (p1 + p2)
    gate2 = g_w * p2 / (p1 + p2)
    oh1 = lane == idx1
    oh2 = lane == idx2
    a = jnp.where(oh1 | oh2, 1.0, 0.0)
    cum = _dot(tri_ref[...], a.astype(BF16))
    before = cum - a + carry[...]
    rank1 = jnp.sum(jnp.where(oh1, before, 0.0), axis=1, keepdims=True)
    rank2 = jnp.sum(jnp.where(oh2, before, 0.0), axis=1, keepdims=True)
    carry[...] = carry[...] + cum[tm - 1:tm, :]
    cnt_ref[...] = carry[...]
    info = jnp.where(lane == 0, idx1.astype(F32), 0.0)
    info = jnp.where(lane == 1, idx2.astype(F32), info)
    info = jnp.where(lane == 2, rank1, info)
    info = jnp.where(lane == 3, rank2, info)
    info = jnp.where(lane == 4, gate1, info)
    info = jnp.where(lane == 5, gate2, info)
    info_ref[...] = info


def _out_proj(x, ret_p, ret_s, s5_p, s5_s, fox_p, fox_s, w_out, g2, w_r, b_r, n_prompt_tiles):
    n, d = x.shape
    w_out_fast = w_out.astype(BF16)
    tm = TOKEN_TILE
    idx = jnp.arange(tm)
    tri = (idx[:, None] >= idx[None, :]).astype(BF16)

    def row(i):
        return (i, 0)

    def const(i):
        return (0, 0)

    def pmap(i):
        return (jnp.minimum(i, n_prompt_tiles - 1), 0)

    def smap(i):
        return (jnp.maximum(i - n_prompt_tiles, 0), 0)

    return pl.pallas_call(
        functools.partial(_out_proj_kernel, n_prompt_tiles),
        grid=(n // tm,),
        in_specs=[
            pl.BlockSpec((tm, d), row),
            pl.BlockSpec((tm, RET_W), pmap),
            pl.BlockSpec((tm, RET_W), smap),
            pl.BlockSpec((tm, S5_W), pmap),
            pl.BlockSpec((tm, S5_W), smap),
            pl.BlockSpec((tm, FOX_W), pmap),
            pl.BlockSpec((tm, FOX_W), smap),
            pl.BlockSpec((d, d), const),
            pl.BlockSpec((d, d), const),
            pl.BlockSpec((1, d), const),
            pl.BlockSpec((d, LANES), const),
            pl.BlockSpec((1, LANES), const),
            pl.BlockSpec((tm, tm), const),
        ],
        out_specs=(
            pl.BlockSpec((tm, d), row),
            pl.BlockSpec((tm, d), row),
            pl.BlockSpec((tm, LANES), row),
            pl.BlockSpec((1, LANES), const),
        ),
        out_shape=(
            jax.ShapeDtypeStruct((n, d), F32),
            jax.ShapeDtypeStruct((n, d), F32),
            jax.ShapeDtypeStruct((n, LANES), F32),
            jax.ShapeDtypeStruct((1, LANES), F32),
        ),
        scratch_shapes=[pltpu.VMEM((1, LANES), F32)],
        compiler_params=_params("arbitrary"),
        name="out_proj_router",
    )(x, ret_p, ret_s, s5_p, s5_s, fox_p, fox_s, w_out_fast, w_out, g2, w_r, b_r, tri)


def _dispatch_kernel(dest_ref, h_ref, rows_in_ref, rows_ref, sem):
    del rows_in_ref
    i = pl.program_id(0)
    tm = h_ref.shape[0]

    def row_copy(r, d):
        return pltpu.make_async_copy(h_ref.at[pl.ds(r, 1)], rows_ref.at[pl.ds(d, 1)], sem)

    def issue(r, _):
        t = i * tm + r
        row_copy(r, dest_ref[2 * t]).start()
        row_copy(r, dest_ref[2 * t + 1]).start()
        return 0

    lax.fori_loop(0, tm, issue, 0)

    def drain(r, _):
        row_copy(0, 0).wait()
        row_copy(0, 0).wait()
        return 0

    lax.fori_loop(0, tm, drain, 0)


def _dispatch(dest, h2, n_rows):
    n, d = h2.shape
    tm = TOKEN_TILE
    grid_spec = pltpu.PrefetchScalarGridSpec(
        num_scalar_prefetch=1,
        grid=(n // tm,),
        in_specs=[
            pl.BlockSpec((tm, d), lambda i, dest: (i, 0)),
            pl.BlockSpec(memory_space=pl.ANY),
        ],
        out_specs=pl.BlockSpec(memory_space=pl.ANY),
        scratch_shapes=[pltpu.SemaphoreType.DMA(())],
    )
    return pl.pallas_call(
        _dispatch_kernel,
        grid_spec=grid_spec,
        out_shape=jax.ShapeDtypeStruct((n_rows, d), F32),
        input_output_aliases={2: 0},
        compiler_params=_params("arbitrary"),
        name="moe_dispatch",
    )(dest, h2, jnp.zeros((n_rows, d), F32))


def _experts_kernel(be_ref, nu_ref, x_ref, w1_ref, w3_ref, w2_ref, y_ref, w1b, w3b, w2b):
    i = pl.program_id(0)
    prev = be_ref[jnp.maximum(i - 1, 0)]
    fresh = (i == 0) | (be_ref[i] != prev)
    active = i < nu_ref[0]

    @pl.when(active & fresh)
    def _():
        w1b[...] = w1_ref[0, 0].astype(BF16)
        w3b[...] = w3_ref[0, 0].astype(BF16)
        w2b[...] = w2_ref[0, 0].astype(BF16)

    @pl.when(active)
    def _():
        xb = x_ref[...].astype(BF16)
        h1 = _dot(xb, w1b[...])
        h3 = _dot(xb, w3b[...])
        a = (h1 * _sigmoid(h1) * h3).astype(BF16)
        y_ref[...] = _dot(a, w2b[...])

    @pl.when(jnp.logical_not(active))
    def _():
        y_ref[...] = jnp.zeros_like(y_ref)


def _experts(layer, blk_e, n_used, x_rows, w1, w3, w2):
    n_rows, d = x_rows.shape
    de = w1.shape[3]
    nblk = n_rows // MOE_ROWS

    def last_used(i, nu):
        return jnp.minimum(i, jnp.maximum(nu[0] - 1, 0))

    def xmap(i, be, nu):
        return (last_used(i, nu), 0)

    def wmap(i, be, nu):
        return (layer, be[last_used(i, nu)], 0, 0)

    grid_spec = pltpu.PrefetchScalarGridSpec(
        num_scalar_prefetch=2,
        grid=(nblk,),
        in_specs=[
            pl.BlockSpec((MOE_ROWS, d), xmap),
            pl.BlockSpec((1, 1, d, de), wmap),
            pl.BlockSpec((1, 1, d, de), wmap),
            pl.BlockSpec((1, 1, de, d), wmap),
        ],
        out_specs=pl.BlockSpec((MOE_ROWS, d), lambda i, be, nu: (i, 0)),
        scratch_shapes=[
            pltpu.VMEM((d, de), BF16),
            pltpu.VMEM((d, de), BF16),
            pltpu.VMEM((de, d), BF16),
        ],
    )
    return pl.pallas_call(
        _experts_kernel,
        grid_spec=grid_spec,
        out_shape=jax.ShapeDtypeStruct((n_rows, d), F32),
        compiler_params=_params("arbitrary"),
        name="moe_experts",
    )(blk_e, n_used, x_rows, w1, w3, w2)


def _combine_kernel(final, dest_ref, x1_ref, info_ref, gf_ref, y_ref, o_ref, buf, sem):
    i = pl.program_id(0)
    tm = x1_ref.shape[0]

    def row_copy(k, r, d):
        return pltpu.make_async_copy(y_ref.at[pl.ds(d, 1)], buf.at[k, pl.ds(r, 1)], sem)

    def issue(r, _):
        t = i * tm + r
        row_copy(0, r, dest_ref[2 * t]).start()
        row_copy(1, r, dest_ref[2 * t + 1]).start()
        return 0

    lax.fori_loop(0, tm, issue, 0)

    def drain(r, _):
        row_copy(0, 0, 0).wait()
        row_copy(1, 0, 0).wait()
        return 0

    lax.fori_loop(0, tm, drain, 0)
    info = info_ref[...]
    x2 = x1_ref[...] + info[:, 4:5] * buf[0] + info[:, 5:6] * buf[1]
    if final:
        ms = jnp.mean(x2 * x2, axis=-1, keepdims=True)
        x2 = x2 * lax.rsqrt(ms + RMS_EPS) * gf_ref[...]
    o_ref[...] = x2


def _combine(dest, x1, info, gf, y_rows, final):
    n, d = x1.shape
    tm = TOKEN_TILE
    grid_spec = pltpu.PrefetchScalarGridSpec(
        num_scalar_prefetch=1,
        grid=(n // tm,),
        in_specs=[
            pl.BlockSpec((tm, d), lambda i, dest: (i, 0)),
            pl.BlockSpec((tm, LANES), lambda i, dest: (i, 0)),
            pl.BlockSpec((1, d), lambda i, dest: (0, 0)),
            pl.BlockSpec(memory_space=pl.ANY),
        ],
        out_specs=pl.BlockSpec((tm, d), lambda i, dest: (i, 0)),
        scratch_shapes=[
            pltpu.VMEM((2, tm, d), F32),
            pltpu.SemaphoreType.DMA(()),
        ],
    )
    return pl.pallas_call(
        functools.partial(_combine_kernel, final),
        grid_spec=grid_spec,
        out_shape=jax.ShapeDtypeStruct((n, d), F32),
        compiler_params=_params("arbitrary"),
        name="moe_combine",
    )(dest, x1, info, gf, y_rows)


def _moe_plan(info, counts, n_tokens):
    eid = info[:, 0:2].astype(jnp.int32)
    rank = info[:, 2:4].astype(jnp.int32)
    cnt = counts[0, :N_EXPERTS].astype(jnp.int32)
    padded = (cnt + MOE_ROWS - 1) // MOE_ROWS * MOE_ROWS
    pend = jnp.cumsum(padded)
    pstart = pend - padded
    dest = (pstart[eid] + rank).reshape(-1)
    nblk = -(-n_tokens * 2 // MOE_ROWS) + N_EXPERTS
    blk_start = jnp.arange(nblk, dtype=jnp.int32) * MOE_ROWS
    blk_e = jnp.minimum(jnp.sum((pend[None, :] <= blk_start[:, None]).astype(jnp.int32), axis=1), N_EXPERTS - 1)
    n_used = (pend[-1:] // MOE_ROWS).astype(jnp.int32)
    return dest.astype(jnp.int32), blk_e.astype(jnp.int32), n_used, nblk * MOE_ROWS


def _rope_tables(seq, past, tnew, tile):
    half = HEAD_DIM // 2
    inv = ROPE_THETA ** (-jnp.arange(half, dtype=F32) / half)
    pos = jnp.concatenate([jnp.arange(seq, dtype=F32), past + (jnp.arange(tile) % tnew).astype(F32)])
    ang = pos[:, None] * inv[None, :]
    cos = jnp.cos(ang)
    sin = jnp.sin(ang)
    cos_h = jnp.concatenate([cos, cos], axis=1)
    sin_h = jnp.concatenate([-sin, sin], axis=1)
    return jnp.tile(cos_h, (1, H_RET)), jnp.tile(sin_h, (1, H_RET))


def _block_diag_states(s):
    b = s.shape[0]
    eye = jnp.eye(H_RET, dtype=s.dtype)
    return jnp.einsum('bhde,hg->bhdge', s, eye).reshape(b, RET_W, RET_W)


def _diag_blocks(s_bd):
    b = s_bd.shape[0]
    s5 = s_bd.reshape(b, H_RET, HEAD_DIM, H_RET, HEAD_DIM)
    return jnp.stack([s5[:, h, :, h, :] for h in range(H_RET)], axis=1)


def kernel(x_prompt, x_sample, cache_k, cache_v, cache_logf, state_ret, state_s5, page_table, norm1, w_in, ret_gn,
           s5_lam_re, s5_lam_im, s5_b_re, s5_b_im, s5_c_re, s5_c_im, s5_d, s5_log_dt, s5_glu_w, s5_glu_b, fox_fb,
           w_out, norm2, w_rg, b_rg, w_re, b_re, w1, w3, w2, norm_f):
    bp, seq, d = x_prompt.shape
    bd_, tnew, _ = x_sample.shape
    depth = w_in.shape[0]
    n_phys, page = cache_k.shape[1], cache_k.shape[2]
    npages = page_table.shape[1]
    past = npages * page
    n_p = bp * seq
    n_s = bd_ * tnew
    n = n_p + n_s
    tm = TOKEN_TILE
    assert n_p % tm == 0 and n_s == tm and seq % tm == 0
    n_prompt_tiles = n_p // tm

    x = jnp.concatenate([x_prompt.reshape(n_p, d), x_sample.reshape(n_s, d)], axis=0)
    cos_t, sin_t = _rope_tables(seq, past, tnew, tm)
    cache_kt = jnp.transpose(cache_k, (0, 1, 3, 4, 2)).reshape(depth, n_phys, FOX_W, page)
    cache_vt = jnp.transpose(cache_v, (0, 1, 3, 4, 2)).reshape(depth, n_phys, FOX_W, page)
    cache_lft = jnp.swapaxes(cache_logf, 2, 3)
    ff0 = w_in.shape[2] - H_FOX

    outs = {k: [] for k in ('kp', 'vp', 'lp', 'ks', 'vs', 'ls', 'rp', 'rs', 'sp', 'ss')}
    y = None
    for l in range(depth):
        w_main = w_in[l, :, :ff0]
        w_ff = jnp.pad(w_in[l, :, ff0:], ((0, 0), (0, LANES - H_FOX)))
        fb = jnp.pad(fox_fb[l], (0, LANES - H_FOX))[None, :]
        g1 = norm1[l][None, :]
        ret_in, su, fq, fk, fv, fkb, fvb, logf = _in_proj(
            _FAST, x, g1, w_main.astype(BF16), w_ff.astype(BF16), fb, cos_t, sin_t,
            0, n_prompt_tiles, 0, seq // tm)
        ret_in_s, su_s, fq_s, fk_s, fv_s, _, _, logf_s = _in_proj(
            _EXACT, x, g1, w_main, w_ff, fb, cos_t, sin_t, n_prompt_tiles, 1, seq // tm, 1)

        gw = ret_gn[l][None, :]
        zero_state = jnp.zeros((bp, RET_W, RET_W), F32)
        ret_p, rs_p = _retention(_FAST, ret_in, zero_state, gw, bp, 0, seq, 1, min(RET_CHUNK, seq),
                                 min(1024, seq))
        seq_per_chunk = 8
        chunk_rows = seq_per_chunk * tnew
        ret_s, rs_s = _retention(_EXACT, ret_in_s, _block_diag_states(state_ret[l]), gw, bd_ // seq_per_chunk, 0,
                                 chunk_rows, seq_per_chunk, tnew, chunk_rows)

        tables = _s5_tables(s5_lam_re[l], s5_lam_im[l], s5_b_re[l], s5_b_im[l], s5_c_re[l], s5_c_im[l],
                            s5_log_dt[l])
        dsk = s5_d[l][None, :]
        glub = s5_glu_b[l][None, :]
        zs = jnp.zeros((bp, 1, S5_LANES), F32)
        s5_p, sp_r, sp_i = _s5(_FAST, su, zs, zs, tables, dsk, s5_glu_w[l], glub, bp, 0, seq, min(512, seq),
                               False)
        x0 = state_s5[l].reshape(bd_, 1, S5_LANES, 2)
        s5_s, ss_r, ss_i = _s5(_EXACT, su_s, x0[..., 0], x0[..., 1], tables, dsk, s5_glu_w[l], glub, 1, 0, n_s,
                               n_s, True)

        qa, ka, va = _fox_prep(logf, fq, fkb, fvb, bp, seq, min(512, seq))
        fox_p = _fox_prompt(qa, ka, va, bp, seq)
        q_s = fq_s.reshape(bd_, tnew, 1, FOX_W)
        head_of_lane = jnp.arange(FOX_W) // HEAD_DIM
        hmask = (jnp.arange(H_FOX)[:, None] == head_of_lane[None, :])[None, None]
        qm = jnp.where(hmask, q_s, jnp.zeros((), BF16)).reshape(bd_, tnew * H_FOX, FOX_W)
        pad_rows = ((0, 0), (0, page - tnew), (0, 0))
        knew = jnp.pad(fk_s.reshape(bd_, tnew, FOX_W), pad_rows)
        vnew = jnp.pad(fv_s.reshape(bd_, tnew, FOX_W), pad_rows)
        lfnew = jnp.pad(jnp.swapaxes(logf_s[:, :H_FOX].reshape(bd_, tnew, H_FOX), 1, 2),
                        ((0, 0), (0, 0), (0, page - tnew)))
        fox_s = _fox_sample(l, page_table, qm, cache_kt, cache_vt, cache_lft, knew, vnew, lfnew)
        fox_s = fox_s.reshape(n_s, FOX_W)

        w_r = jnp.pad(jnp.concatenate([w_re[l], w_rg[l]], axis=1),
                      ((0, 0), (0, LANES - N_EXPERTS - N_EXPERT_GROUPS)))
        b_r = jnp.pad(jnp.concatenate([b_re[l], b_rg[l]]), (0, LANES - N_EXPERTS - N_EXPERT_GROUPS))[None, :]
        x1, h2, info, counts = _out_proj(x, ret_p, ret_s, s5_p, s5_s, fox_p, fox_s, w_out[l],
                                         norm2[l][None, :], w_r, b_r, n_prompt_tiles)
        dest, blk_e, n_used, n_rows = _moe_plan(info, counts, n)
        x_rows = _dispatch(dest, h2, n_rows)
        y_rows = _experts(l, blk_e, n_used, x_rows, w1, w3, w2)
        final = l == depth - 1
        x = _combine(dest, x1, info, norm_f[None, :], y_rows, final)

        outs['kp'].append(fk.reshape(bp, seq, H_FOX, HEAD_DIM))
        outs['vp'].append(fv.reshape(bp, seq, H_FOX, HEAD_DIM))
        outs['lp'].append(logf[:, :H_FOX].reshape(bp, seq, H_FOX))
        outs['ks'].append(fk_s.reshape(bd_, tnew, H_FOX, HEAD_DIM))
        outs['vs'].append(fv_s.reshape(bd_, tnew, H_FOX, HEAD_DIM))
        outs['ls'].append(logf_s[:, :H_FOX].reshape(bd_, tnew, H_FOX))
        outs['rp'].append(_diag_blocks(rs_p))
        outs['rs'].append(_diag_blocks(rs_s))
        outs['sp'].append(jnp.stack([sp_r.reshape(bp, S5_GROUPS, S5_STATE),
                                     sp_i.reshape(bp, S5_GROUPS, S5_STATE)], axis=-1))
        outs['ss'].append(jnp.stack([ss_r.reshape(bd_, S5_GROUPS, S5_STATE),
                                     ss_i.reshape(bd_, S5_GROUPS, S5_STATE)], axis=-1))

    y_prompt = x[:n_p].reshape(bp, seq, d)
    y_sample = x[n_p:].reshape(bd_, tnew, d)
    st = {k: jnp.stack(v) for k, v in outs.items()}
    return (y_prompt, y_sample, st['kp'], st['vp'], st['lp'], st['ks'], st['vs'], st['ls'],
            st['rp'], st['rs'], st['sp'], st['ss'])
```

```python
import functools
import math

import numpy as np

import jax
import jax.numpy as jnp
from jax import lax
from jax.experimental import pallas as pl
from jax.experimental.pallas import tpu as pltpu

F32 = jnp.float32
BF16 = jnp.bfloat16
HIGHEST = lax.Precision.HIGHEST

HEAD_DIM = 64
H_RET = 4
H_FOX = 8
RET_W = H_RET * HEAD_DIM
S5_W = 256
FOX_W = H_FOX * HEAD_DIM
S5_GROUPS = 16
S5_GROUP_CH = 16
S5_STATE = 64
S5_LANES = S5_GROUPS * S5_STATE
N_EXPERTS = 32
N_EXPERT_GROUPS = 4
EXPERTS_PER_GROUP = 8
RET_CHUNK = 128
ROPE_THETA = 10000.0
RMS_EPS = 1e-6
NEG_BIG = -1e30
LOG2E = math.log2(math.e)

LANES = 128
SUBLANES = 8
TOKEN_TILE = 256
MOE_ROWS = 256
FOX_TQ = 1024
PAGES_PER_STEP = 8
VMEM_LIMIT = 48 * 1024 * 1024


def _params(*sem):
    return pltpu.CompilerParams(dimension_semantics=sem, vmem_limit_bytes=VMEM_LIMIT)


def _sigmoid(x):
    return 1.0 / (1.0 + jnp.exp(-x))


def _dot(a, b):
    return jnp.dot(a, b, preferred_element_type=F32)


def _dot_nt(a, b):
    return lax.dot_general(a, b, (((1,), (1,)), ((), ())), preferred_element_type=F32)


def _dot_tn(a, b):
    return lax.dot_general(a, b, (((0,), (0,)), ((), ())), preferred_element_type=F32)


class _MatmulMode:
    def __init__(self, exact):
        self.dtype = F32 if exact else BF16
        self.precision = HIGHEST if exact else None

    def cast(self, x):
        return x.astype(self.dtype)

    def dot(self, a, b):
        return jnp.dot(a, b, precision=self.precision, preferred_element_type=F32)

    def dot_nt(self, a, b):
        return lax.dot_general(a, b, (((1,), (1,)), ((), ())), precision=self.precision,
                               preferred_element_type=F32)

    def dot_tn(self, a, b):
        return lax.dot_general(a, b, (((0,), (0,)), ((), ())), precision=self.precision,
                               preferred_element_type=F32)


_FAST = _MatmulMode(False)
_EXACT = _MatmulMode(True)


def _in_proj_kernel(mm, x_ref, g_ref, w_ref, wff_ref, fb_ref, cos_ref, sin_ref,
                    ret_ref, su_ref, fq_ref, fk_ref, fv_ref, fkb_ref, fvb_ref, logf_ref):
    _dot = mm.dot
    x = x_ref[...]
    ms = jnp.mean(x * x, axis=-1, keepdims=True)
    h = mm.cast(x * lax.rsqrt(ms + RMS_EPS) * g_ref[...])
    tm = x.shape[0]
    cos = cos_ref[...]
    sin = sin_ref[...]
    lane = lax.broadcasted_iota(jnp.int32, (tm, RET_W), 1)
    first_half = (lane % HEAD_DIM) < (HEAD_DIM // 2)

    def rope(t):
        swapped = jnp.where(first_half,
                            pltpu.roll(t, RET_W - HEAD_DIM // 2, axis=1),
                            pltpu.roll(t, HEAD_DIM // 2, axis=1))
        return t * cos + swapped * sin

    ret_ref[:, 0:RET_W] = rope(_dot(h, w_ref[:, 0:RET_W]))
    ret_ref[:, RET_W:2 * RET_W] = rope(_dot(h, w_ref[:, RET_W:2 * RET_W])) * (HEAD_DIM ** -0.5)
    ret_ref[:, 2 * RET_W:4 * RET_W] = _dot(h, w_ref[:, 2 * RET_W:4 * RET_W])
    c0 = 4 * RET_W
    su_ref[...] = _dot(h, w_ref[:, c0:c0 + S5_W])
    c0 += S5_W
    fq_ref[...] = (_dot(h, w_ref[:, c0:c0 + FOX_W]) * (HEAD_DIM ** -0.5 * LOG2E)).astype(BF16)
    c0 += FOX_W
    fk = _dot(h, w_ref[:, c0:c0 + FOX_W])
    fk_ref[...] = fk
    fkb_ref[...] = fk.astype(BF16)
    c0 += FOX_W
    fv = _dot(h, w_ref[:, c0:c0 + FOX_W])
    fv_ref[...] = fv
    fvb_ref[...] = fv.astype(BF16)
    z = _dot(h, wff_ref[...]) + fb_ref[...]
    logf = jnp.minimum(z, 0.0) - jnp.log1p(jnp.exp(-jnp.abs(z)))
    lane128 = lax.broadcasted_iota(jnp.int32, (tm, LANES), 1)
    logf_ref[...] = jnp.where(lane128 < H_FOX, logf, 0.0)


def _in_proj(mm, x, g, w_main, w_ff, fb, cos_t, sin_t, tile0, ntiles, pos_tile0, pos_tiles):
    tm = TOKEN_TILE
    n = ntiles * tm
    wcols = w_main.shape[1]

    def xrow(i):
        return (tile0 + i, 0)

    def row(i):
        return (i, 0)

    def const(i):
        return (0, 0)

    def pos_map(i):
        return (pos_tile0 + i % pos_tiles, 0)

    outs = (
        jax.ShapeDtypeStruct((n, 4 * RET_W), F32),
        jax.ShapeDtypeStruct((n, S5_W), F32),
        jax.ShapeDtypeStruct((n, FOX_W), BF16),
        jax.ShapeDtypeStruct((n, FOX_W), F32),
        jax.ShapeDtypeStruct((n, FOX_W), F32),
        jax.ShapeDtypeStruct((n, FOX_W), BF16),
        jax.ShapeDtypeStruct((n, FOX_W), BF16),
        jax.ShapeDtypeStruct((n, LANES), F32),
    )
    return pl.pallas_call(
        functools.partial(_in_proj_kernel, mm),
        grid=(ntiles,),
        in_specs=[
            pl.BlockSpec((tm, x.shape[1]), xrow),
            pl.BlockSpec((1, x.shape[1]), const),
            pl.BlockSpec((x.shape[1], wcols), const),
            pl.BlockSpec((x.shape[1], LANES), const),
            pl.BlockSpec((1, LANES), const),
            pl.BlockSpec((tm, RET_W), pos_map),
            pl.BlockSpec((tm, RET_W), pos_map),
        ],
        out_specs=(
            pl.BlockSpec((tm, 4 * RET_W), row),
            pl.BlockSpec((tm, S5_W), row),
            pl.BlockSpec((tm, FOX_W), row),
            pl.BlockSpec((tm, FOX_W), row),
            pl.BlockSpec((tm, FOX_W), row),
            pl.BlockSpec((tm, FOX_W), row),
            pl.BlockSpec((tm, FOX_W), row),
            pl.BlockSpec((tm, LANES), row),
        ),
        out_shape=outs,
        compiler_params=_params("arbitrary"),
        name="in_proj",
    )(x, g, w_main, w_ff, fb, cos_t, sin_t)


def _retention_kernel(mm, nseq, seq_len, nchunk, x_ref, s0_ref, dm_ref, qd_ref, kd_ref, cd_ref, bd_ref,
                      gn_ref, gw_ref, o_ref, sout_ref, s_scr):
    cast, _dot, _dot_nt, _dot_tn = mm.cast, mm.dot, mm.dot_nt, mm.dot_tn
    step = pl.program_id(1)
    rows = nseq * seq_len

    @pl.when(step == 0)
    def _():
        s_scr[...] = s0_ref[...]

    lane_head = lax.broadcasted_iota(jnp.int32, (rows, RET_W), 1) // HEAD_DIM
    row_seq = lax.broadcasted_iota(jnp.int32, (rows, 1), 0) // seq_len

    def chunk(c, _):
        r0 = pl.multiple_of(c * rows, rows)
        q = x_ref[pl.ds(r0, rows), 0:RET_W]
        k = x_ref[pl.ds(r0, rows), RET_W:2 * RET_W]
        v = x_ref[pl.ds(r0, rows), 2 * RET_W:3 * RET_W]
        g = x_ref[pl.ds(r0, rows), 3 * RET_W:4 * RET_W]
        qb = cast(q)
        kb = cast(k)
        vb = cast(v)
        ps = []
        vs = []
        for h in range(H_RET):
            hm = lane_head == h
            sc = _dot_nt(cast(jnp.where(hm, q, 0.0)), kb)
            ps.append(cast(sc * dm_ref[h]))
            vs.append(cast(jnp.where(hm, v, 0.0)))
        inner = _dot(jnp.concatenate(ps, axis=1), jnp.concatenate(vs, axis=0))
        kdec = k * kd_ref[...]
        cd = cd_ref[...]
        bd = bd_ref[...]
        if nseq == 1:
            s_old = s_scr[0]
            cross = _dot(qb, cast(s_old))
            s_scr[0] = s_old * cd + _dot_tn(cast(kdec), vb) * bd
        else:
            def seq_body(s, cross):
                rm = row_seq == s
                s_old = s_scr[s]
                cross = jnp.where(rm, _dot(qb, cast(s_old)), cross)
                kv = _dot_tn(cast(jnp.where(rm, kdec, 0.0)), vb)
                s_scr[s] = s_old * cd + kv * bd
                return cross

            cross = lax.fori_loop(0, nseq, seq_body, jnp.zeros((rows, RET_W), F32))
        o = inner + cross * qd_ref[...]
        o2 = o * o
        hi = o2.astype(BF16)
        lo = (o2 - hi.astype(F32)).astype(BF16)
        ms = jnp.dot(hi, gn_ref[...], preferred_element_type=F32) + jnp.dot(
            lo, gn_ref[...], preferred_element_type=F32)
        on = o * lax.rsqrt(ms + RMS_EPS)
        o_ref[pl.ds(r0, rows), :] = (g * _sigmoid(g) * on * gw_ref[...]).astype(o_ref.dtype)
        return 0

    lax.fori_loop(0, nchunk, chunk, 0)

    @pl.when(step == pl.num_programs(1) - 1)
    def _():
        sout_ref[...] = s_scr[...]


def _retention_tables(nseq, seq_len):
    rows = nseq * seq_len
    log_gamma = jnp.log1p(-jnp.power(2.0, -5.0 - jnp.arange(H_RET, dtype=F32)))
    idx = jnp.arange(rows, dtype=jnp.int32)
    pos = (idx % seq_len).astype(F32)
    seq = idx // seq_len
    diff = pos[:, None] - pos[None, :]
    ok = (seq[:, None] == seq[None, :]) & (diff >= 0)
    dm = jnp.where(ok[None], jnp.exp(log_gamma[:, None, None] * jnp.maximum(diff, 0.0)[None]), 0.0)
    lg_lane = jnp.repeat(log_gamma, HEAD_DIM)
    qd = jnp.exp((pos[:, None] + 1.0) * lg_lane[None, :])
    kd = jnp.exp((seq_len - 1.0 - pos[:, None]) * lg_lane[None, :])
    cd = jnp.exp(seq_len * lg_lane)[None, :]
    head = jnp.arange(RET_W) // HEAD_DIM
    bd = (head[:, None] == head[None, :]).astype(F32)
    gn = (bd / HEAD_DIM).astype(BF16)
    return dm, qd, kd, cd, bd, gn


def _retention(mm, ret_in, s0_bd, gw, nbatch, row0, seq_total, nseq, seq_len, rows_per_step):
    rows = nseq * seq_len
    nchunk = rows_per_step // rows
    steps = seq_total // rows_per_step
    blk0 = row0 // rows_per_step
    dm, qd, kd, cd, bd, gn = _retention_tables(nseq, seq_len)

    def rmap(b, i):
        return (blk0 + b * steps + i, 0)

    def omap(b, i):
        return (b * steps + i, 0)

    def c2(b, i):
        return (0, 0)

    def c3(b, i):
        return (0, 0, 0)

    def smap(b, i):
        return (b, 0, 0)

    in_specs = [
        pl.BlockSpec((rows_per_step, 4 * RET_W), rmap),
        pl.BlockSpec((nseq, RET_W, RET_W), smap),
        pl.BlockSpec((H_RET, rows, rows), c3),
        pl.BlockSpec((rows, RET_W), c2),
        pl.BlockSpec((rows, RET_W), c2),
        pl.BlockSpec((1, RET_W), c2),
        pl.BlockSpec((RET_W, RET_W), c2),
        pl.BlockSpec((RET_W, RET_W), c2),
        pl.BlockSpec((1, RET_W), c2),
    ]
    return pl.pallas_call(
        functools.partial(_retention_kernel, mm, nseq, seq_len, nchunk),
        grid=(nbatch, steps),
        in_specs=in_specs,
        out_specs=(
            pl.BlockSpec((rows_per_step, RET_W), omap),
            pl.BlockSpec((nseq, RET_W, RET_W), smap),
        ),
        out_shape=(
            jax.ShapeDtypeStruct((nbatch * seq_total, RET_W), mm.dtype),
            jax.ShapeDtypeStruct((nbatch * nseq, RET_W, RET_W), F32),
        ),
        scratch_shapes=[pltpu.VMEM((nseq, RET_W, RET_W), F32)],
        compiler_params=_params("arbitrary", "arbitrary"),
        name="retention",
    )(ret_in, s0_bd, dm, qd, kd, cd, bd, gn, gw)


def _s5_kernel(mm, short, u_ref, x0r_ref, x0i_ref, bre_ref, bim_ref, pre_ref, pim_ref, cm_ref, d_ref,
               gw_ref, gb_ref, o_ref, str_ref, sti_ref, bur, bui, car, cai):
    cast, _dot = mm.cast, mm.dot
    step = pl.program_id(1)
    rows = u_ref.shape[0]
    u = u_ref[...]
    ub = cast(u)
    bur[...] = _dot(ub, bre_ref[...])
    bui[...] = _dot(ub, bim_ref[...])

    if not short:
        @pl.when(step == 0)
        def _():
            car[...] = x0r_ref[0]
            cai[...] = x0i_ref[0]

    pr = pre_ref[...]
    pi = pim_ref[...]
    rowi = lax.broadcasted_iota(jnp.int32, (SUBLANES, S5_LANES), 0)

    def group(j, carry):
        cr, ci = carry
        r0 = pl.multiple_of(j * SUBLANES, SUBLANES)
        xr = bur[pl.ds(r0, SUBLANES), :]
        xi = bui[pl.ds(r0, SUBLANES), :]
        for s in (1, 2, 4):
            ar = pr[s - 1:s]
            ai = pi[s - 1:s]
            sr = jnp.where(rowi >= s, pltpu.roll(xr, s, axis=0), 0.0)
            si = jnp.where(rowi >= s, pltpu.roll(xi, s, axis=0), 0.0)
            xr, xi = xr + ar * sr - ai * si, xi + ar * si + ai * sr
        if short:
            cr = x0r_ref[j]
            ci = x0i_ref[j]
        xr, xi = xr + pr * cr - pi * ci, xi + pr * ci + pi * cr
        bur[pl.ds(r0, SUBLANES), :] = xr
        bui[pl.ds(r0, SUBLANES), :] = xi
        cr = xr[SUBLANES - 1:SUBLANES]
        ci = xi[SUBLANES - 1:SUBLANES]
        if short:
            str_ref[j] = cr
            sti_ref[j] = ci
        return cr, ci

    if short:
        init = (jnp.zeros((1, S5_LANES), F32), jnp.zeros((1, S5_LANES), F32))
    else:
        init = (car[...], cai[...])
    cr, ci = lax.fori_loop(0, rows // SUBLANES, group, init)
    if not short:
        car[...] = cr
        cai[...] = ci

        @pl.when(step == pl.num_programs(1) - 1)
        def _():
            str_ref[0] = cr
            sti_ref[0] = ci

    xs = jnp.concatenate([cast(bur[...]), cast(bui[...])], axis=1)
    y = _dot(xs, cm_ref[...]) + d_ref[...] * u
    yg = 0.5 * y * (1.0 + jnp.tanh(math.sqrt(2.0 / math.pi) * (y + 0.044715 * (y * y * y))))
    z = _dot(cast(yg), gw_ref[...]) + gb_ref[...]
    o_ref[...] = (yg * _sigmoid(z)).astype(o_ref.dtype)


def _s5_tables(lam_re, lam_im, b_re, b_im, c_re, c_im, log_dt):
    lam = lax.complex(lam_re.astype(F32), lam_im.astype(F32))
    dt = jnp.exp(log_dt.astype(F32))[:, None]
    lam_bar = jnp.exp(lam * dt)
    b_bar = ((lam_bar - 1.0) / lam)[:, :, None] * lax.complex(b_re.astype(F32), b_im.astype(F32))
    k = jnp.arange(1, SUBLANES + 1, dtype=F32)[:, None, None]
    powers = jnp.exp((lam * dt)[None] * k)
    pre = jnp.real(powers).reshape(SUBLANES, S5_LANES)
    pim = jnp.imag(powers).reshape(SUBLANES, S5_LANES)
    eye = jnp.eye(S5_GROUPS, dtype=F32)
    bre = jnp.einsum('gpc,gh->gchp', jnp.real(b_bar), eye).reshape(S5_W, S5_LANES)
    bim = jnp.einsum('gpc,gh->gchp', jnp.imag(b_bar), eye).reshape(S5_W, S5_LANES)
    cre = jnp.einsum('gcp,gh->hpgc', c_re.astype(F32), eye).reshape(S5_LANES, S5_W)
    cim = jnp.einsum('gcp,gh->hpgc', c_im.astype(F32), eye).reshape(S5_LANES, S5_W)
    cm = jnp.concatenate([cre, -cim], axis=0)
    return bre, bim, pre, pim, cm


def _s5(mm, su, x0r, x0i, tables, d, glu_w, glu_b, nbatch, row0, seq_total, rows_per_step, short):
    bre, bim, pre, pim, cm = tables
    bre, bim, cm, glu_w = mm.cast(bre), mm.cast(bim), mm.cast(cm), mm.cast(glu_w)
    steps = seq_total // rows_per_step
    blk0 = row0 // rows_per_step
    nstate = x0r.shape[0]
    sblk = nstate if short else 1

    def rmap(b, i):
        return (blk0 + b * steps + i, 0)

    def omap(b, i):
        return (b * steps + i, 0)

    def c2(b, i):
        return (0, 0)

    def smap(b, i):
        return (0 if short else b, 0, 0)

    in_specs = [
        pl.BlockSpec((rows_per_step, S5_W), rmap),
        pl.BlockSpec((sblk, 1, S5_LANES), smap),
        pl.BlockSpec((sblk, 1, S5_LANES), smap),
        pl.BlockSpec((S5_W, S5_LANES), c2),
        pl.BlockSpec((S5_W, S5_LANES), c2),
        pl.BlockSpec((SUBLANES, S5_LANES), c2),
        pl.BlockSpec((SUBLANES, S5_LANES), c2),
        pl.BlockSpec((2 * S5_LANES, S5_W), c2),
        pl.BlockSpec((1, S5_W), c2),
        pl.BlockSpec((S5_W, S5_W), c2),
        pl.BlockSpec((1, S5_W), c2),
    ]
    return pl.pallas_call(
        functools.partial(_s5_kernel, mm, short),
        grid=(nbatch, steps),
        in_specs=in_specs,
        out_specs=(
            pl.BlockSpec((rows_per_step, S5_W), omap),
            pl.BlockSpec((sblk, 1, S5_LANES), smap),
            pl.BlockSpec((sblk, 1, S5_LANES), smap),
        ),
        out_shape=(
            jax.ShapeDtypeStruct((nbatch * seq_total, S5_W), mm.dtype),
            jax.ShapeDtypeStruct((nstate, 1, S5_LANES), F32),
            jax.ShapeDtypeStruct((nstate, 1, S5_LANES), F32),
        ),
        scratch_shapes=[
            pltpu.VMEM((rows_per_step, S5_LANES), F32),
            pltpu.VMEM((rows_per_step, S5_LANES), F32),
            pltpu.VMEM((1, S5_LANES), F32),
            pltpu.VMEM((1, S5_LANES), F32),
        ],
        compiler_params=_params("arbitrary", "arbitrary"),
        name="s5",
    )(su, x0r, x0i, bre, bim, pre, pim, cm, d, glu_w, glu_b)


_PAIR_W = 2 * LANES
_AUG_PER_HEAD = 6
_N_PAIRS = H_FOX // 2


def _aug_placement():
    pq = np.zeros((3 * LANES, _N_PAIRS * LANES), np.float32)
    pk = np.zeros((3 * LANES, _N_PAIRS * LANES), np.float32)
    oq = np.zeros((1, _N_PAIRS * LANES), np.float32)
    ok = np.zeros((1, _N_PAIRS * LANES), np.float32)
    for h in range(H_FOX):
        base = (h // 2) * LANES + (h % 2) * _AUG_PER_HEAD
        for part in range(3):
            pq[part * LANES + h, base + part] = 1.0
            oq[0, base + 3 + part] = 1.0
            ok[0, base + part] = 1.0
            pk[part * LANES + h, base + 3 + part] = -1.0
    return (jnp.asarray(pq, BF16), jnp.asarray(pk, BF16), jnp.asarray(oq), jnp.asarray(ok))


def _split3(v):
    hi = v.astype(BF16)
    r1 = v - hi.astype(F32)
    mid = r1.astype(BF16)
    lo = (r1 - mid.astype(F32)).astype(BF16)
    return jnp.concatenate([hi, mid, lo], axis=1)


def _fox_prep_kernel(lf_ref, q_ref, k_ref, v_ref, tri_ref, pq_ref, pk_ref, oq_ref, ok_ref,
                     qa_ref, ka_ref, va_ref, carry):
    @pl.when(pl.program_id(1) == 0)
    def _():
        carry[...] = jnp.zeros_like(carry)

    c = jnp.dot(tri_ref[...], lf_ref[...], precision=HIGHEST, preferred_element_type=F32) + carry[...]
    rows = c.shape[0]
    carry[...] = c[rows - 1:rows, :]
    parts = _split3(c * LOG2E)
    augq = (_dot(parts, pq_ref[...]) + oq_ref[...]).astype(BF16)
    augk = (_dot(parts, pk_ref[...]) + ok_ref[...]).astype(BF16)
    lane = lax.broadcasted_iota(jnp.int32, (rows, LANES), 1)
    one = jnp.ones((rows, LANES), BF16)
    for p in range(_N_PAIRS):
        src = slice(p * LANES, (p + 1) * LANES)
        qa_ref[:, p * _PAIR_W:p * _PAIR_W + LANES] = q_ref[:, src]
        qa_ref[:, p * _PAIR_W + LANES:(p + 1) * _PAIR_W] = augq[:, src]
        ka_ref[:, p * _PAIR_W:p * _PAIR_W + LANES] = k_ref[:, src]
        ka_ref[:, p * _PAIR_W + LANES:(p + 1) * _PAIR_W] = augk[:, src]
        vp = v_ref[:, src]
        va_ref[:, p * _PAIR_W:p * _PAIR_W + LANES] = jnp.where(lane < HEAD_DIM, vp, one)
        va_ref[:, p * _PAIR_W + LANES:(p + 1) * _PAIR_W] = jnp.where(lane < HEAD_DIM, one, vp)


def _fox_prep(logf, fq, fkb, fvb, nbatch, seq_total, rows_per_step):
    steps = seq_total // rows_per_step
    idx = jnp.arange(rows_per_step)
    tri = (idx[:, None] >= idx[None, :]).astype(F32)
    pq, pk, oq, ok = _aug_placement()
    wide = _N_PAIRS * _PAIR_W

    def rmap(b, i):
        return (b * steps + i, 0)

    def const(b, i):
        return (0, 0)

    out = jax.ShapeDtypeStruct((nbatch * seq_total, wide), BF16)
    return pl.pallas_call(
        _fox_prep_kernel,
        grid=(nbatch, steps),
        in_specs=[
            pl.BlockSpec((rows_per_step, LANES), rmap),
            pl.BlockSpec((rows_per_step, FOX_W), rmap),
            pl.BlockSpec((rows_per_step, FOX_W), rmap),
            pl.BlockSpec((rows_per_step, FOX_W), rmap),
            pl.BlockSpec((rows_per_step, rows_per_step), const),
            pl.BlockSpec(pq.shape, const),
            pl.BlockSpec(pk.shape, const),
            pl.BlockSpec(oq.shape, const),
            pl.BlockSpec(ok.shape, const),
        ],
        out_specs=(pl.BlockSpec((rows_per_step, wide), rmap),) * 3,
        out_shape=(out, out, out),
        scratch_shapes=[pltpu.VMEM((1, LANES), F32)],
        compiler_params=_params("arbitrary", "arbitrary"),
        name="fox_prep",
    )(logf, fq, fkb, fvb, tri, pq, pk, oq, ok)


_PAIRS_PER_STEP = 2


def _fox_prompt_kernel(tq, q_ref, k_ref, v_ref, o_ref, acc, m_scr):
    qi = pl.program_id(2)
    nhead = 2 * _PAIRS_PER_STEP
    lane = lax.broadcasted_iota(jnp.int32, (tq, _PAIR_W), 1)
    qs = []
    for h in range(nhead):
        par = h % 2
        q = q_ref[:, (h // 2) * _PAIR_W:(h // 2 + 1) * _PAIR_W]
        own = ((lane >= par * HEAD_DIM) & (lane < (par + 1) * HEAD_DIM)) | (
            (lane >= LANES + par * _AUG_PER_HEAD) & (lane < LANES + (par + 1) * _AUG_PER_HEAD))
        qs.append(jnp.where(own, q, jnp.zeros_like(q)))
        m_scr[h] = jnp.full((tq, LANES), NEG_BIG, F32)
        acc[h] = jnp.zeros((tq, LANES), F32)
    causal = (lax.broadcasted_iota(jnp.int32, (tq, tq), 0) >= lax.broadcasted_iota(jnp.int32, (tq, tq), 1))

    def kv_step(j, masked):
        ks = pl.multiple_of(j * tq, tq)
        for h in range(nhead):
            kb = k_ref[pl.ds(ks, tq), (h // 2) * _PAIR_W:(h // 2 + 1) * _PAIR_W]
            s = _dot_nt(qs[h], kb)
            if masked:
                s = jnp.where(causal, s, NEG_BIG)
            m_old = m_scr[h]
            m_new = jnp.maximum(m_old, jnp.max(s, axis=1, keepdims=True))
            p = jnp.exp2(s - jnp.concatenate([m_new] * (tq // LANES), axis=1))
            vb = v_ref[pl.ds(ks, tq), h * LANES:(h + 1) * LANES]
            acc[h] = jnp.exp2(m_old - m_new) * acc[h] + _dot(p.astype(BF16), vb)
            m_scr[h] = m_new

    def body(j, _):
        kv_step(j, False)
        return 0

    lax.fori_loop(0, qi, body, 0)
    kv_step(qi, True)
    lane128 = lax.broadcasted_iota(jnp.int32, (tq, LANES), 1)
    for pr in range(_PAIRS_PER_STEP):
        outs = []
        for par in range(2):
            a = acc[2 * pr + par]
            outs.append(a / pltpu.roll(a, HEAD_DIM, axis=1))
        o_ref[:, pr * LANES:(pr + 1) * LANES] = jnp.where(lane128 < HEAD_DIM, outs[0], outs[1]).astype(BF16)


def _fox_prompt(qa, ka, va, nbatch, seq_total):
    tq = min(FOX_TQ, seq_total)
    nq = seq_total // tq
    w = _PAIRS_PER_STEP * _PAIR_W
    return pl.pallas_call(
        functools.partial(_fox_prompt_kernel, tq),
        grid=(nbatch, _N_PAIRS // _PAIRS_PER_STEP, nq),
        in_specs=[
            pl.BlockSpec((tq, w), lambda b, g, i: (b * nq + i, g)),
            pl.BlockSpec((seq_total, w), lambda b, g, i: (b, g)),
            pl.BlockSpec((seq_total, w), lambda b, g, i: (b, g)),
        ],
        out_specs=pl.BlockSpec((tq, _PAIRS_PER_STEP * LANES), lambda b, g, i: (b * nq + i, g)),
        out_shape=jax.ShapeDtypeStruct((nbatch * seq_total, FOX_W), BF16),
        scratch_shapes=[
            pltpu.VMEM((2 * _PAIRS_PER_STEP, tq, LANES), F32),
            pltpu.VMEM((2 * _PAIRS_PER_STEP, tq, LANES), F32),
        ],
        compiler_params=_params("arbitrary", "arbitrary", "arbitrary"),
        name="fox_prompt",
    )(qa, ka, va)


def _fox_sample_kernel(npg, tnew, pt_ref, qm_ref, *refs):
    del pt_ref
    k_refs = refs[0:npg]
    v_refs = refs[npg:2 * npg]
    lf_refs = refs[2 * npg:3 * npg]
    knew_ref, vnew_ref, lfnew_ref, u_ref, sel_ref, o_ref, m_scr, l_scr, acc, run, kb_scr, vb_scr = refs[3 * npg:]
    g = pl.program_id(1)
    nrow = tnew * H_FOX
    npage = knew_ref.shape[1]

    @pl.when(g == 0)
    def _():
        m_scr[...] = jnp.full(m_scr.shape, NEG_BIG, F32)
        l_scr[...] = jnp.zeros(l_scr.shape, F32)
        acc[...] = jnp.zeros(acc.shape, F32)
        run[...] = jnp.zeros(run.shape, F32)

    qm = qm_ref[0]

    def cum_within(lft):
        return jnp.dot(lft, u_ref[...], precision=HIGHEST, preferred_element_type=F32)

    def attend(k, v, ckeys, mask, transposed):
        qk = _dot(qm, k) if transposed else _dot_nt(qm, k)
        s = qk - jnp.concatenate([ckeys * LOG2E] * tnew, axis=0)
        if mask is not None:
            s = jnp.where(mask, s, NEG_BIG)
        m_old = m_scr[...]
        m_new = jnp.maximum(m_old, jnp.max(s, axis=1, keepdims=True))
        p = jnp.exp2(s - m_new).astype(BF16)
        alpha = jnp.exp2(m_old - m_new)
        l_scr[...] = alpha * l_scr[...] + jnp.sum(p.astype(F32), axis=1, keepdims=True)
        acc[...] = alpha * acc[...] + (_dot_nt(p, v) if transposed else _dot(p, v))
        m_scr[...] = m_new

    base = run[...]
    cs = []
    for j in range(npg):
        kb_scr[:, j * npage:(j + 1) * npage] = k_refs[j][0, 0].astype(BF16)
        vb_scr[:, j * npage:(j + 1) * npage] = v_refs[j][0, 0].astype(BF16)
        cw = cum_within(lf_refs[j][0, 0])
        cs.append(base + cw)
        base = base + jnp.broadcast_to(cw[:, npage - 1:npage], base.shape)
    run[...] = base
    attend(kb_scr[...], vb_scr[...], jnp.concatenate(cs, axis=1), None, True)

    @pl.when(g == pl.num_programs(1) - 1)
    def _():
        rt = lax.broadcasted_iota(jnp.int32, (nrow, npage), 0) // H_FOX
        col = lax.broadcasted_iota(jnp.int32, (nrow, npage), 1)
        attend(knew_ref[0].astype(BF16), vnew_ref[0].astype(BF16), run[...] + cum_within(lfnew_ref[0]), col <= rt,
               False)
        o = acc[...] / l_scr[...]
        rh = lax.broadcasted_iota(jnp.int32, (nrow, FOX_W), 0) % H_FOX
        lh = lax.broadcasted_iota(jnp.int32, (nrow, FOX_W), 1) // HEAD_DIM
        om = jnp.where(rh == lh, o, 0.0)
        o_ref[0] = jnp.dot(sel_ref[...], om, precision=HIGHEST, preferred_element_type=F32)


def _fox_sample(layer, page_table, qm, cache_kt, cache_vt, cache_lft, knew, vnew, lfnew):
    nb, npages = page_table.shape
    page = cache_kt.shape[3]
    tnew = qm.shape[1] // H_FOX
    npg = PAGES_PER_STEP
    ngrp = npages // npg
    idx = jnp.arange(page)
    u = (idx[:, None] <= idx[None, :]).astype(F32)
    sel = (jnp.arange(tnew)[:, None] == (jnp.arange(tnew * H_FOX)[None, :] // H_FOX)).astype(F32)

    def pmap(j):
        def f(b, g, pt):
            return (layer, pt[b * npages + g * npg + j], 0, 0)
        return f

    def bmap(b, g, pt):
        return (b, 0, 0)

    def c2(b, g, pt):
        return (0, 0)

    in_specs = [pl.BlockSpec((1, tnew * H_FOX, FOX_W), bmap)]
    in_specs += [pl.BlockSpec((1, 1, FOX_W, page), pmap(j)) for j in range(npg)]
    in_specs += [pl.BlockSpec((1, 1, FOX_W, page), pmap(j)) for j in range(npg)]
    in_specs += [pl.BlockSpec((1, 1, H_FOX, page), pmap(j)) for j in range(npg)]
    in_specs += [
        pl.BlockSpec((1, page, FOX_W), bmap),
        pl.BlockSpec((1, page, FOX_W), bmap),
        pl.BlockSpec((1, H_FOX, page), bmap),
        pl.BlockSpec((page, page), c2),
        pl.BlockSpec((tnew, tnew * H_FOX), c2),
    ]
    grid_spec = pltpu.PrefetchScalarGridSpec(
        num_scalar_prefetch=1,
        grid=(nb, ngrp),
        in_specs=in_specs,
        out_specs=pl.BlockSpec((1, tnew, FOX_W), bmap),
        scratch_shapes=[
            pltpu.VMEM((tnew * H_FOX, 1), F32),
            pltpu.VMEM((tnew * H_FOX, 1), F32),
            pltpu.VMEM((tnew * H_FOX, FOX_W), F32),
            pltpu.VMEM((H_FOX, page), F32),
            pltpu.VMEM((FOX_W, npg * page), BF16),
            pltpu.VMEM((FOX_W, npg * page), BF16),
        ],
    )
    args = [page_table.reshape(-1), qm] + [cache_kt] * npg + [cache_vt] * npg + [cache_lft] * npg
    args += [knew, vnew, lfnew, u, sel]
    return pl.pallas_call(
        functools.partial(_fox_sample_kernel, npg, tnew),
        grid_spec=grid_spec,
        out_shape=jax.ShapeDtypeStruct((nb, tnew, FOX_W), F32),
        compiler_params=_params("arbitrary", "arbitrary"),
        name="fox_sample",
    )(*args)


_LANE_GROUP0 = N_EXPERTS


def _out_proj_kernel(n_prompt_tiles, xp_ref, xs_ref, retp_ref, rets_ref, s5p_ref, s5s_ref, foxp_ref, foxs_ref, wo_ref,
                     wof_ref, g_ref, wr_ref, br_ref, tri_ref, x1_ref, h2_ref, info_ref, cnt_ref, carry):
    i = pl.program_id(0)

    @pl.when(i == 0)
    def _():
        carry[...] = jnp.zeros_like(carry)

    def project(mm, x_ref, ret_ref, s5_ref, fox_ref, w_ref):
        x1 = x_ref[...] + mm.dot(ret_ref[...], w_ref[0:RET_W, :])
        x1 = x1 + mm.dot(s5_ref[...], w_ref[RET_W:RET_W + S5_W, :])
        x1_ref[...] = x1 + mm.dot(fox_ref[...], w_ref[RET_W + S5_W:, :])

    @pl.when(i < n_prompt_tiles)
    def _():
        project(_FAST, xp_ref, retp_ref, s5p_ref, foxp_ref, wo_ref)

    @pl.when(i >= n_prompt_tiles)
    def _():
        project(_EXACT, xs_ref, rets_ref, s5s_ref, foxs_ref, wof_ref)

    x1 = x1_ref[...]
    ms = jnp.mean(x1 * x1, axis=-1, keepdims=True)
    h2 = x1 * lax.rsqrt(ms + RMS_EPS) * g_ref[...]
    h2_ref[...] = h2
    logits = jnp.dot(h2, wr_ref[...], precision=HIGHEST, preferred_element_type=F32) + br_ref[...]
    tm = logits.shape[0]
    lane = lax.broadcasted_iota(jnp.int32, (tm, LANES), 1)
    big = jnp.int32(LANES)
    gmask = (lane >= _LANE_GROUP0) & (lane < _LANE_GROUP0 + N_EXPERT_GROUPS)
    gl = jnp.where(gmask, logits, NEG_BIG)
    gmax = jnp.max(gl, axis=1, keepdims=True)
    gsum = jnp.sum(jnp.where(gmask, jnp.exp(gl - gmax), 0.0), axis=1, keepdims=True)
    g_w = 1.0 / gsum
    g_idx = jnp.min(jnp.where(gl == gmax, lane, big), axis=1, keepdims=True) - _LANE_GROUP0
    lo = g_idx * EXPERTS_PER_GROUP
    emask = (lane >= lo) & (lane < lo + EXPERTS_PER_GROUP)
    el = jnp.where(emask, logits, NEG_BIG)
    e1 = jnp.max(el, axis=1, keepdims=True)
    idx1 = jnp.min(jnp.where(el == e1, lane, big), axis=1, keepdims=True)
    esum = jnp.sum(jnp.where(emask, jnp.exp(el - e1), 0.0), axis=1, keepdims=True)
    el2 = jnp.where(lane == idx1, NEG_BIG, el)
    e2 = jnp.max(el2, axis=1, keepdims=True)
    idx2 = jnp.min(jnp.where(el2 == e2, lane, big), axis=1, keepdims=True)
    p1 = 1.0 / esum
    p2 = jnp.exp(e2 - e1) / esum
    gate1 = g_w * p1 / (p1 + p2)
    gate2 = g_w * p2 / (p1 + p2)
    oh1 = lane == idx1
    oh2 = lane == idx2
    a = jnp.where(oh1 | oh2, 1.0, 0.0)
    cum = _dot(tri_ref[...], a.astype(BF16))
    before = cum - a + carry[...]
    rank1 = jnp.sum(jnp.where(oh1, before, 0.0), axis=1, keepdims=True)
    rank2 = jnp.sum(jnp.where(oh2, before, 0.0), axis=1, keepdims=True)
    carry[...] = carry[...] + cum[tm - 1:tm, :]
    cnt_ref[...] = carry[...]
    info = jnp.where(lane == 0, idx1.astype(F32), 0.0)
    info = jnp.where(lane == 1, idx2.astype(F32), info)
    info = jnp.where(lane == 2, rank1, info)
    info = jnp.where(lane == 3, rank2, info)
    info = jnp.where(lane == 4, gate1, info)
    info = jnp.where(lane == 5, gate2, info)
    info_ref[...] = info


def _out_proj(xp, xs, xs_tile0, ret_p, ret_s, s5_p, s5_s, fox_p, fox_s, w_out, g2, w_r, b_r, n_prompt_tiles):
    tm = TOKEN_TILE
    d = xp.shape[1]
    n = (n_prompt_tiles + ret_s.shape[0] // tm) * tm
    w_out_fast = w_out.astype(BF16)
    idx = jnp.arange(tm)
    tri = (idx[:, None] >= idx[None, :]).astype(BF16)

    def row(i):
        return (i, 0)

    def const(i):
        return (0, 0)

    def pmap(i):
        return (jnp.minimum(i, n_prompt_tiles - 1), 0)

    def smap(i):
        return (jnp.maximum(i - n_prompt_tiles, 0), 0)

    def xsmap(i):
        return (xs_tile0 + jnp.maximum(i - n_prompt_tiles, 0), 0)

    return pl.pallas_call(
        functools.partial(_out_proj_kernel, n_prompt_tiles),
        grid=(n // tm,),
        in_specs=[
            pl.BlockSpec((tm, d), pmap),
            pl.BlockSpec((tm, d), xsmap),
            pl.BlockSpec((tm, RET_W), pmap),
            pl.BlockSpec((tm, RET_W), smap),
            pl.BlockSpec((tm, S5_W), pmap),
            pl.BlockSpec((tm, S5_W), smap),
            pl.BlockSpec((tm, FOX_W), pmap),
            pl.BlockSpec((tm, FOX_W), smap),
            pl.BlockSpec((d, d), const),
            pl.BlockSpec((d, d), const),
            pl.BlockSpec((1, d), const),
            pl.BlockSpec((d, LANES), const),
            pl.BlockSpec((1, LANES), const),
            pl.BlockSpec((tm, tm), const),
        ],
        out_specs=(
            pl.BlockSpec((tm, d), row),
            pl.BlockSpec((tm, d), row),
            pl.BlockSpec((tm, LANES), row),
            pl.BlockSpec((1, LANES), const),
        ),
        out_shape=(
            jax.ShapeDtypeStruct((n, d), F32),
            jax.ShapeDtypeStruct((n, d), F32),
            jax.ShapeDtypeStruct((n, LANES), F32),
            jax.ShapeDtypeStruct((1, LANES), F32),
        ),
        scratch_shapes=[pltpu.VMEM((1, LANES), F32)],
        compiler_params=_params("arbitrary"),
        name="out_proj_router",
    )(xp, xs, ret_p, ret_s, s5_p, s5_s, fox_p, fox_s, w_out_fast, w_out, g2, w_r, b_r, tri)


_ROW_DMA_UNROLL = 8


def _dispatch_kernel(dest_ref, h_ref, rows_in_ref, rows_ref, sem):
    del rows_in_ref
    i = pl.program_id(0)
    tm = h_ref.shape[0]

    def issue(r, _):
        t = i * tm + r
        for k in range(2):
            pltpu.make_async_copy(h_ref.at[pl.ds(r, 1)], rows_ref.at[pl.ds(dest_ref[2 * t + k], 1)], sem).start()
        return 0

    lax.fori_loop(0, tm, issue, 0, unroll=_ROW_DMA_UNROLL)
    for _ in range(2):
        pltpu.make_async_copy(h_ref, rows_ref.at[pl.ds(0, tm)], sem).wait()


def _dispatch(dest, h2, n_rows):
    n, d = h2.shape
    tm = TOKEN_TILE
    grid_spec = pltpu.PrefetchScalarGridSpec(
        num_scalar_prefetch=1,
        grid=(n // tm,),
        in_specs=[
            pl.BlockSpec((tm, d), lambda i, dest: (i, 0)),
            pl.BlockSpec(memory_space=pl.ANY),
        ],
        out_specs=pl.BlockSpec(memory_space=pl.ANY),
        scratch_shapes=[pltpu.SemaphoreType.DMA(())],
    )
    return pl.pallas_call(
        _dispatch_kernel,
        grid_spec=grid_spec,
        out_shape=jax.ShapeDtypeStruct((n_rows, d), F32),
        input_output_aliases={2: 0},
        compiler_params=_params("arbitrary"),
        name="moe_dispatch",
    )(dest, h2, jnp.zeros((n_rows, d), F32))


def _experts_kernel(be_ref, nu_ref, x_ref, w1_ref, w3_ref, w2_ref, y_ref, w1b, w3b, w2b):
    i = pl.program_id(0)
    prev = be_ref[jnp.maximum(i - 1, 0)]
    fresh = (i == 0) | (be_ref[i] != prev)
    active = i < nu_ref[0]

    @pl.when(active & fresh)
    def _():
        w1b[...] = w1_ref[0, 0].astype(BF16)
        w3b[...] = w3_ref[0, 0].astype(BF16)
        w2b[...] = w2_ref[0, 0].astype(BF16)

    @pl.when(active)
    def _():
        xb = x_ref[...].astype(BF16)
        h1 = _dot(xb, w1b[...])
        h3 = _dot(xb, w3b[...])
        a = (h1 * _sigmoid(h1) * h3).astype(BF16)
        y_ref[...] = _dot(a, w2b[...])

    @pl.when(jnp.logical_not(active))
    def _():
        y_ref[...] = jnp.zeros_like(y_ref)


def _experts(layer, blk_e, n_used, x_rows, w1, w3, w2):
    n_rows, d = x_rows.shape
    de = w1.shape[3]
    nblk = n_rows // MOE_ROWS

    def last_used(i, nu):
        return jnp.minimum(i, jnp.maximum(nu[0] - 1, 0))

    def xmap(i, be, nu):
        return (last_used(i, nu), 0)

    def wmap(i, be, nu):
        return (layer, be[last_used(i, nu)], 0, 0)

    grid_spec = pltpu.PrefetchScalarGridSpec(
        num_scalar_prefetch=2,
        grid=(nblk,),
        in_specs=[
            pl.BlockSpec((MOE_ROWS, d), xmap),
            pl.BlockSpec((1, 1, d, de), wmap),
            pl.BlockSpec((1, 1, d, de), wmap),
            pl.BlockSpec((1, 1, de, d), wmap),
        ],
        out_specs=pl.BlockSpec((MOE_ROWS, d), lambda i, be, nu: (i, 0)),
        scratch_shapes=[
            pltpu.VMEM((d, de), BF16),
            pltpu.VMEM((d, de), BF16),
            pltpu.VMEM((de, d), BF16),
        ],
    )
    return pl.pallas_call(
        _experts_kernel,
        grid_spec=grid_spec,
        out_shape=jax.ShapeDtypeStruct((n_rows, d), F32),
        compiler_params=_params("arbitrary"),
        name="moe_experts",
    )(blk_e, n_used, x_rows, w1, w3, w2)


def _combine_kernel(n_prompt_tiles, dest_ref, x1_ref, info_ref, gf_ref, y_ref, o_ref, *rest):
    os_ref = rest[0] if n_prompt_tiles is not None else None
    buf, sem = rest[-2:]
    i = pl.program_id(0)
    tm = x1_ref.shape[0]

    def issue(r, _):
        t = i * tm + r
        for k in range(2):
            pltpu.make_async_copy(y_ref.at[pl.ds(dest_ref[2 * t + k], 1)], buf.at[k, pl.ds(r, 1)], sem).start()
        return 0

    lax.fori_loop(0, tm, issue, 0, unroll=_ROW_DMA_UNROLL)
    for k in range(2):
        pltpu.make_async_copy(y_ref.at[pl.ds(0, tm)], buf.at[k], sem).wait()
    info = info_ref[...]
    x2 = x1_ref[...] + info[:, 4:5] * buf[0] + info[:, 5:6] * buf[1]
    if n_prompt_tiles is None:
        o_ref[...] = x2
    else:
        ms = jnp.mean(x2 * x2, axis=-1, keepdims=True)
        y = x2 * lax.rsqrt(ms + RMS_EPS) * gf_ref[...]

        @pl.when(i < n_prompt_tiles)
        def _():
            o_ref[...] = y

        @pl.when(i >= n_prompt_tiles)
        def _():
            os_ref[...] = y


def _combine(dest, x1, info, gf, y_rows, n_prompt_tiles):
    n, d = x1.shape
    tm = TOKEN_TILE
    if n_prompt_tiles is None:
        out_specs = pl.BlockSpec((tm, d), lambda i, dest: (i, 0))
        out_shape = jax.ShapeDtypeStruct((n, d), F32)
    else:
        npt = n_prompt_tiles
        out_specs = (pl.BlockSpec((tm, d), lambda i, dest: (jnp.minimum(i, npt - 1), 0)),
                     pl.BlockSpec((tm, d), lambda i, dest: (jnp.maximum(i - npt, 0), 0)))
        out_shape = (jax.ShapeDtypeStruct((npt * tm, d), F32), jax.ShapeDtypeStruct((n - npt * tm, d), F32))
    grid_spec = pltpu.PrefetchScalarGridSpec(
        num_scalar_prefetch=1,
        grid=(n // tm,),
        in_specs=[
            pl.BlockSpec((tm, d), lambda i, dest: (i, 0)),
            pl.BlockSpec((tm, LANES), lambda i, dest: (i, 0)),
            pl.BlockSpec((1, d), lambda i, dest: (0, 0)),
            pl.BlockSpec(memory_space=pl.ANY),
        ],
        out_specs=out_specs,
        scratch_shapes=[
            pltpu.VMEM((2, tm, d), F32),
            pltpu.SemaphoreType.DMA(()),
        ],
    )
    return pl.pallas_call(
        functools.partial(_combine_kernel, n_prompt_tiles),
        grid_spec=grid_spec,
        out_shape=out_shape,
        compiler_params=_params("arbitrary"),
        name="moe_combine",
    )(dest, x1, info, gf, y_rows)


def _moe_plan(info, counts, n_tokens):
    eid = info[:, 0:2].astype(jnp.int32)
    rank = info[:, 2:4].astype(jnp.int32)
    cnt = counts[0, :N_EXPERTS].astype(jnp.int32)
    padded = (cnt + MOE_ROWS - 1) // MOE_ROWS * MOE_ROWS
    pend = jnp.cumsum(padded)
    pstart = pend - padded
    experts = jnp.arange(N_EXPERTS, dtype=jnp.int32)
    start_of = jnp.sum(jnp.where(eid[..., None] == experts, pstart, 0), axis=-1)
    dest = (start_of + rank).reshape(-1)
    nblk = -(-n_tokens * 2 // MOE_ROWS) + N_EXPERTS
    blk_start = jnp.arange(nblk, dtype=jnp.int32) * MOE_ROWS
    blk_e = jnp.minimum(jnp.sum((pend[None, :] <= blk_start[:, None]).astype(jnp.int32), axis=1), N_EXPERTS - 1)
    n_used = (pend[-1:] // MOE_ROWS).astype(jnp.int32)
    return dest.astype(jnp.int32), blk_e.astype(jnp.int32), n_used, nblk * MOE_ROWS


def _rope_tables(seq, past, tnew, tile):
    half = HEAD_DIM // 2
    inv = ROPE_THETA ** (-jnp.arange(half, dtype=F32) / half)
    pos = jnp.concatenate([jnp.arange(seq, dtype=F32), past + (jnp.arange(tile) % tnew).astype(F32)])
    ang = pos[:, None] * inv[None, :]
    cos = jnp.cos(ang)
    sin = jnp.sin(ang)
    cos_h = jnp.concatenate([cos, cos], axis=1)
    sin_h = jnp.concatenate([-sin, sin], axis=1)
    return jnp.tile(cos_h, (1, H_RET)), jnp.tile(sin_h, (1, H_RET))


def _block_diag_states(s):
    b = s.shape[0]
    eye = jnp.eye(H_RET, dtype=s.dtype)
    return jnp.einsum('bhde,hg->bhdge', s, eye).reshape(b, RET_W, RET_W)


def _diag_blocks(s_bd):
    b = s_bd.shape[0]
    s5 = s_bd.reshape(b, H_RET, HEAD_DIM, H_RET, HEAD_DIM)
    return jnp.stack([s5[:, h, :, h, :] for h in range(H_RET)], axis=1)


def kernel(x_prompt, x_sample, cache_k, cache_v, cache_logf, state_ret, state_s5, page_table, norm1, w_in, ret_gn,
           s5_lam_re, s5_lam_im, s5_b_re, s5_b_im, s5_c_re, s5_c_im, s5_d, s5_log_dt, s5_glu_w, s5_glu_b, fox_fb,
           w_out, norm2, w_rg, b_rg, w_re, b_re, w1, w3, w2, norm_f):
    bp, seq, d = x_prompt.shape
    bd_, tnew, _ = x_sample.shape
    depth = w_in.shape[0]
    n_phys, page = cache_k.shape[1], cache_k.shape[2]
    npages = page_table.shape[1]
    past = npages * page
    n_p = bp * seq
    n_s = bd_ * tnew
    n = n_p + n_s
    tm = TOKEN_TILE
    assert n_p % tm == 0 and n_s == tm and seq % tm == 0
    n_prompt_tiles = n_p // tm

    xp, xs, xs_tile0 = x_prompt.reshape(n_p, d), x_sample.reshape(n_s, d), 0
    cos_t, sin_t = _rope_tables(seq, past, tnew, tm)
    cache_kt = jnp.transpose(cache_k, (0, 1, 3, 4, 2)).reshape(depth, n_phys, FOX_W, page)
    cache_vt = jnp.transpose(cache_v, (0, 1, 3, 4, 2)).reshape(depth, n_phys, FOX_W, page)
    cache_lft = jnp.swapaxes(cache_logf, 2, 3)
    ff0 = w_in.shape[2] - H_FOX

    outs = {k: [] for k in ('kp', 'vp', 'lp', 'ks', 'vs', 'ls', 'rp', 'rs', 'sp', 'ss')}
    for l in range(depth):
        w_main = w_in[l, :, :ff0]
        w_ff = jnp.pad(w_in[l, :, ff0:], ((0, 0), (0, LANES - H_FOX)))
        fb = jnp.pad(fox_fb[l], (0, LANES - H_FOX))[None, :]
        g1 = norm1[l][None, :]
        ret_in, su, fq, fk, fv, fkb, fvb, logf = _in_proj(
            _FAST, xp, g1, w_main.astype(BF16), w_ff.astype(BF16), fb, cos_t, sin_t,
            0, n_prompt_tiles, 0, seq // tm)
        ret_in_s, su_s, fq_s, fk_s, fv_s, _, _, logf_s = _in_proj(
            _EXACT, xs, g1, w_main, w_ff, fb, cos_t, sin_t, xs_tile0, 1, seq // tm, 1)

        gw = ret_gn[l][None, :]
        zero_state = jnp.zeros((bp, RET_W, RET_W), F32)
        ret_p, rs_p = _retention(_FAST, ret_in, zero_state, gw, bp, 0, seq, 1, min(RET_CHUNK, seq),
                                 min(1024, seq))
        seq_per_chunk = 8
        chunk_rows = seq_per_chunk * tnew
        ret_s, rs_s = _retention(_EXACT, ret_in_s, _block_diag_states(state_ret[l]), gw, bd_ // seq_per_chunk, 0,
                                 chunk_rows, seq_per_chunk, tnew, chunk_rows)

        tables = _s5_tables(s5_lam_re[l], s5_lam_im[l], s5_b_re[l], s5_b_im[l], s5_c_re[l], s5_c_im[l],
                            s5_log_dt[l])
        dsk = s5_d[l][None, :]
        glub = s5_glu_b[l][None, :]
        zs = jnp.zeros((bp, 1, S5_LANES), F32)
        s5_p, sp_r, sp_i = _s5(_FAST, su, zs, zs, tables, dsk, s5_glu_w[l], glub, bp, 0, seq, min(512, seq),
                               False)
        x0 = state_s5[l].reshape(bd_, 1, S5_LANES, 2)
        s5_s, ss_r, ss_i = _s5(_EXACT, su_s, x0[..., 0], x0[..., 1], tables, dsk, s5_glu_w[l], glub, 1, 0, n_s,
                               n_s, True)

        qa, ka, va = _fox_prep(logf, fq, fkb, fvb, bp, seq, min(512, seq))
        fox_p = _fox_prompt(qa, ka, va, bp, seq)
        q_s = fq_s.reshape(bd_, tnew, 1, FOX_W)
        head_of_lane = jnp.arange(FOX_W) // HEAD_DIM
        hmask = (jnp.arange(H_FOX)[:, None] == head_of_lane[None, :])[None, None]
        qm = jnp.where(hmask, q_s, jnp.zeros((), BF16)).reshape(bd_, tnew * H_FOX, FOX_W)
        pad_rows = ((0, 0), (0, page - tnew), (0, 0))
        knew = jnp.pad(fk_s.reshape(bd_, tnew, FOX_W), pad_rows)
        vnew = jnp.pad(fv_s.reshape(bd_, tnew, FOX_W), pad_rows)
        lfnew = jnp.pad(jnp.swapaxes(logf_s[:, :H_FOX].reshape(bd_, tnew, H_FOX), 1, 2),
                        ((0, 0), (0, 0), (0, page - tnew)))
        fox_s = _fox_sample(l, page_table, qm, cache_kt, cache_vt, cache_lft, knew, vnew, lfnew)
        fox_s = fox_s.reshape(n_s, FOX_W)

        w_r = jnp.pad(jnp.concatenate([w_re[l], w_rg[l]], axis=1),
                      ((0, 0), (0, LANES - N_EXPERTS - N_EXPERT_GROUPS)))
        b_r = jnp.pad(jnp.concatenate([b_re[l], b_rg[l]]), (0, LANES - N_EXPERTS - N_EXPERT_GROUPS))[None, :]
        x1, h2, info, counts = _out_proj(xp, xs, xs_tile0, ret_p, ret_s, s5_p, s5_s, fox_p, fox_s, w_out[l],
                                         norm2[l][None, :], w_r, b_r, n_prompt_tiles)
        dest, blk_e, n_used, n_rows = _moe_plan(info, counts, n)
        x_rows = _dispatch(dest, h2, n_rows)
        y_rows = _experts(l, blk_e, n_used, x_rows, w1, w3, w2)
        if l < depth - 1:
            x_all = _combine(dest, x1, info, norm_f[None, :], y_rows, None)
            xp, xs, xs_tile0 = x_all, x_all, n_prompt_tiles
        else:
            y_p, y_s = _combine(dest, x1, info, norm_f[None, :], y_rows, n_prompt_tiles)

        outs['kp'].append(fk.reshape(bp, seq, H_FOX, HEAD_DIM))
        outs['vp'].append(fv.reshape(bp, seq, H_FOX, HEAD_DIM))
        outs['lp'].append(logf[:, :H_FOX].reshape(bp, seq, H_FOX))
        outs['ks'].append(fk_s.reshape(bd_, tnew, H_FOX, HEAD_DIM))
        outs['vs'].append(fv_s.reshape(bd_, tnew, H_FOX, HEAD_DIM))
        outs['ls'].append(logf_s[:, :H_FOX].reshape(bd_, tnew, H_FOX))
        outs['rp'].append(_diag_blocks(rs_p))
        outs['rs'].append(_diag_blocks(rs_s))
        outs['sp'].append(jnp.stack([sp_r.reshape(bp, S5_GROUPS, S5_STATE),
                                     sp_i.reshape(bp, S5_GROUPS, S5_STATE)], axis=-1))
        outs['ss'].append(jnp.stack([ss_r.reshape(bd_, S5_GROUPS, S5_STATE),
                                     ss_i.reshape(bd_, S5_GROUPS, S5_STATE)], axis=-1))

    y_prompt = y_p.reshape(bp, seq, d)
    y_sample = y_s.reshape(bd_, tnew, d)
    st = {k: jnp.stack(v) for k, v in outs.items()}
    return (y_prompt, y_sample, st['kp'], st['vp'], st['lp'], st['ks'], st['vs'], st['ls'],
            st['rp'], st['rs'], st['sp'], st['ss'])
```

```python
import functools
import math

import numpy as np

import jax
import jax.numpy as jnp
from jax import lax
from jax.experimental import pallas as pl
from jax.experimental.pallas import tpu as pltpu

F32 = jnp.float32
BF16 = jnp.bfloat16
HIGHEST = lax.Precision.HIGHEST

HEAD_DIM = 64
H_RET = 4
H_FOX = 8
RET_W = H_RET * HEAD_DIM
S5_W = 256
FOX_W = H_FOX * HEAD_DIM
S5_GROUPS = 16
S5_GROUP_CH = 16
S5_STATE = 64
S5_LANES = S5_GROUPS * S5_STATE
N_EXPERTS = 32
N_EXPERT_GROUPS = 4
EXPERTS_PER_GROUP = 8
RET_CHUNK = 128
ROPE_THETA = 10000.0
RMS_EPS = 1e-6
NEG_BIG = -1e30
LOG2E = math.log2(math.e)

LANES = 128
SUBLANES = 8
TOKEN_TILE = 256
MOE_ROWS = 256
FOX_TQ = 1024
PAGES_PER_STEP = 16
VMEM_LIMIT = 48 * 1024 * 1024


def _params(*sem):
    return pltpu.CompilerParams(dimension_semantics=sem, vmem_limit_bytes=VMEM_LIMIT)


def _sigmoid(x):
    return 1.0 / (1.0 + jnp.exp(-x))


def _dot(a, b):
    return jnp.dot(a, b, preferred_element_type=F32)


def _dot_nt(a, b):
    return lax.dot_general(a, b, (((1,), (1,)), ((), ())), preferred_element_type=F32)


def _dot_tn(a, b):
    return lax.dot_general(a, b, (((0,), (0,)), ((), ())), preferred_element_type=F32)


class _MatmulMode:
    def __init__(self, exact):
        self.dtype = F32 if exact else BF16
        self.precision = HIGHEST if exact else None

    def cast(self, x):
        return x.astype(self.dtype)

    def dot(self, a, b):
        return jnp.dot(a, b, precision=self.precision, preferred_element_type=F32)

    def dot_nt(self, a, b):
        return lax.dot_general(a, b, (((1,), (1,)), ((), ())), precision=self.precision,
                               preferred_element_type=F32)

    def dot_tn(self, a, b):
        return lax.dot_general(a, b, (((0,), (0,)), ((), ())), precision=self.precision,
                               preferred_element_type=F32)


_FAST = _MatmulMode(False)
_EXACT = _MatmulMode(True)


def _in_proj_kernel(mm, x_ref, g_ref, w_ref, wff_ref, fb_ref, cos_ref, sin_ref,
                    ret_ref, su_ref, fq_ref, fk_ref, fv_ref, fkb_ref, fvb_ref, logf_ref):
    _dot = mm.dot
    x = x_ref[...]
    ms = jnp.mean(x * x, axis=-1, keepdims=True)
    h = mm.cast(x * lax.rsqrt(ms + RMS_EPS) * g_ref[...])
    tm = x.shape[0]
    cos = cos_ref[...]
    sin = sin_ref[...]
    lane = lax.broadcasted_iota(jnp.int32, (tm, RET_W), 1)
    first_half = (lane % HEAD_DIM) < (HEAD_DIM // 2)

    def rope(t):
        swapped = jnp.where(first_half,
                            pltpu.roll(t, RET_W - HEAD_DIM // 2, axis=1),
                            pltpu.roll(t, HEAD_DIM // 2, axis=1))
        return t * cos + swapped * sin

    ret_ref[:, 0:RET_W] = rope(_dot(h, w_ref[:, 0:RET_W]))
    ret_ref[:, RET_W:2 * RET_W] = rope(_dot(h, w_ref[:, RET_W:2 * RET_W])) * (HEAD_DIM ** -0.5)
    ret_ref[:, 2 * RET_W:4 * RET_W] = _dot(h, w_ref[:, 2 * RET_W:4 * RET_W])
    c0 = 4 * RET_W
    su_ref[...] = _dot(h, w_ref[:, c0:c0 + S5_W])
    c0 += S5_W
    fq_ref[...] = (_dot(h, w_ref[:, c0:c0 + FOX_W]) * (HEAD_DIM ** -0.5 * LOG2E)).astype(BF16)
    c0 += FOX_W
    fk = _dot(h, w_ref[:, c0:c0 + FOX_W])
    fk_ref[...] = fk
    fkb_ref[...] = fk.astype(BF16)
    c0 += FOX_W
    fv = _dot(h, w_ref[:, c0:c0 + FOX_W])
    fv_ref[...] = fv
    fvb_ref[...] = fv.astype(BF16)
    z = _dot(h, wff_ref[...]) + fb_ref[...]
    logf = jnp.minimum(z, 0.0) - jnp.log1p(jnp.exp(-jnp.abs(z)))
    lane128 = lax.broadcasted_iota(jnp.int32, (tm, LANES), 1)
    logf_ref[...] = jnp.where(lane128 < H_FOX, logf, 0.0)


def _in_proj(mm, x, g, w_main, w_ff, fb, cos_t, sin_t, tile0, ntiles, pos_tile0, pos_tiles):
    tm = TOKEN_TILE
    n = ntiles * tm
    wcols = w_main.shape[1]

    def xrow(i):
        return (tile0 + i, 0)

    def row(i):
        return (i, 0)

    def const(i):
        return (0, 0)

    def pos_map(i):
        return (pos_tile0 + i % pos_tiles, 0)

    outs = (
        jax.ShapeDtypeStruct((n, 4 * RET_W), F32),
        jax.ShapeDtypeStruct((n, S5_W), F32),
        jax.ShapeDtypeStruct((n, FOX_W), BF16),
        jax.ShapeDtypeStruct((n, FOX_W), F32),
        jax.ShapeDtypeStruct((n, FOX_W), F32),
        jax.ShapeDtypeStruct((n, FOX_W), BF16),
        jax.ShapeDtypeStruct((n, FOX_W), BF16),
        jax.ShapeDtypeStruct((n, LANES), F32),
    )
    return pl.pallas_call(
        functools.partial(_in_proj_kernel, mm),
        grid=(ntiles,),
        in_specs=[
            pl.BlockSpec((tm, x.shape[1]), xrow),
            pl.BlockSpec((1, x.shape[1]), const),
            pl.BlockSpec((x.shape[1], wcols), const),
            pl.BlockSpec((x.shape[1], LANES), const),
            pl.BlockSpec((1, LANES), const),
            pl.BlockSpec((tm, RET_W), pos_map),
            pl.BlockSpec((tm, RET_W), pos_map),
        ],
        out_specs=(
            pl.BlockSpec((tm, 4 * RET_W), row),
            pl.BlockSpec((tm, S5_W), row),
            pl.BlockSpec((tm, FOX_W), row),
            pl.BlockSpec((tm, FOX_W), row),
            pl.BlockSpec((tm, FOX_W), row),
            pl.BlockSpec((tm, FOX_W), row),
            pl.BlockSpec((tm, FOX_W), row),
            pl.BlockSpec((tm, LANES), row),
        ),
        out_shape=outs,
        compiler_params=_params("arbitrary"),
        name="in_proj",
    )(x, g, w_main, w_ff, fb, cos_t, sin_t)


def _retention_kernel(mm, nseq, seq_len, nchunk, x_ref, s0_ref, dm_ref, qd_ref, kd_ref, cd_ref, bd_ref,
                      gn_ref, gw_ref, o_ref, sout_ref, s_scr):
    cast, _dot, _dot_nt, _dot_tn = mm.cast, mm.dot, mm.dot_nt, mm.dot_tn
    step = pl.program_id(1)
    rows = nseq * seq_len

    @pl.when(step == 0)
    def _():
        s_scr[...] = s0_ref[...]

    lane_head = lax.broadcasted_iota(jnp.int32, (rows, RET_W), 1) // HEAD_DIM
    row_seq = lax.broadcasted_iota(jnp.int32, (rows, 1), 0) // seq_len

    def chunk(c, _):
        r0 = pl.multiple_of(c * rows, rows)
        q = x_ref[pl.ds(r0, rows), 0:RET_W]
        k = x_ref[pl.ds(r0, rows), RET_W:2 * RET_W]
        v = x_ref[pl.ds(r0, rows), 2 * RET_W:3 * RET_W]
        g = x_ref[pl.ds(r0, rows), 3 * RET_W:4 * RET_W]
        qb = cast(q)
        kb = cast(k)
        vb = cast(v)
        ps = []
        vs = []
        for h in range(H_RET):
            hm = lane_head == h
            sc = _dot_nt(cast(jnp.where(hm, q, 0.0)), kb)
            ps.append(cast(sc * dm_ref[h]))
            vs.append(cast(jnp.where(hm, v, 0.0)))
        inner = _dot(jnp.concatenate(ps, axis=1), jnp.concatenate(vs, axis=0))
        kdec = k * kd_ref[...]
        cd = cd_ref[...]
        bd = bd_ref[...]
        if nseq == 1:
            s_old = s_scr[0]
            cross = _dot(qb, cast(s_old))
            s_scr[0] = s_old * cd + _dot_tn(cast(kdec), vb) * bd
        else:
            def seq_body(s, cross):
                rm = row_seq == s
                s_old = s_scr[s]
                cross = jnp.where(rm, _dot(qb, cast(s_old)), cross)
                kv = _dot_tn(cast(jnp.where(rm, kdec, 0.0)), vb)
                s_scr[s] = s_old * cd + kv * bd
                return cross

            cross = lax.fori_loop(0, nseq, seq_body, jnp.zeros((rows, RET_W), F32))
        o = inner + cross * qd_ref[...]
        o2 = o * o
        hi = o2.astype(BF16)
        lo = (o2 - hi.astype(F32)).astype(BF16)
        ms = jnp.dot(hi, gn_ref[...], preferred_element_type=F32) + jnp.dot(
            lo, gn_ref[...], preferred_element_type=F32)
        on = o * lax.rsqrt(ms + RMS_EPS)
        o_ref[pl.ds(r0, rows), :] = (g * _sigmoid(g) * on * gw_ref[...]).astype(o_ref.dtype)
        return 0

    lax.fori_loop(0, nchunk, chunk, 0)

    @pl.when(step == pl.num_programs(1) - 1)
    def _():
        sout_ref[...] = s_scr[...]


def _retention_tables(nseq, seq_len):
    f32 = np.float32
    rows = nseq * seq_len
    log_gamma = np.log1p(-np.power(f32(2.0), -5.0 - np.arange(H_RET, dtype=f32))).astype(f32)
    idx = np.arange(rows)
    pos = (idx % seq_len).astype(f32)
    seq = idx // seq_len
    diff = pos[:, None] - pos[None, :]
    ok = (seq[:, None] == seq[None, :]) & (diff >= 0)
    dm = np.where(ok[None], np.exp(log_gamma[:, None, None] * np.maximum(diff, f32(0.0))[None]), f32(0.0))
    lg_lane = np.repeat(log_gamma, HEAD_DIM)
    qd = np.exp((pos[:, None] + f32(1.0)) * lg_lane[None, :])
    kd = np.exp((f32(seq_len - 1.0) - pos[:, None]) * lg_lane[None, :])
    cd = np.exp(f32(seq_len) * lg_lane)[None, :]
    head = np.arange(RET_W) // HEAD_DIM
    bd = (head[:, None] == head[None, :]).astype(f32)
    to = lambda a: jnp.asarray(a, F32)
    return to(dm), to(qd), to(kd), to(cd), to(bd), jnp.asarray(bd / HEAD_DIM, BF16)


def _retention(mm, ret_in, s0_bd, gw, nbatch, row0, seq_total, nseq, seq_len, rows_per_step):
    rows = nseq * seq_len
    nchunk = rows_per_step // rows
    steps = seq_total // rows_per_step
    blk0 = row0 // rows_per_step
    dm, qd, kd, cd, bd, gn = _retention_tables(nseq, seq_len)

    def rmap(b, i):
        return (blk0 + b * steps + i, 0)

    def omap(b, i):
        return (b * steps + i, 0)

    def c2(b, i):
        return (0, 0)

    def c3(b, i):
        return (0, 0, 0)

    def smap(b, i):
        return (b, 0, 0)

    in_specs = [
        pl.BlockSpec((rows_per_step, 4 * RET_W), rmap),
        pl.BlockSpec((nseq, RET_W, RET_W), smap),
        pl.BlockSpec((H_RET, rows, rows), c3),
        pl.BlockSpec((rows, RET_W), c2),
        pl.BlockSpec((rows, RET_W), c2),
        pl.BlockSpec((1, RET_W), c2),
        pl.BlockSpec((RET_W, RET_W), c2),
        pl.BlockSpec((RET_W, RET_W), c2),
        pl.BlockSpec((1, RET_W), c2),
    ]
    return pl.pallas_call(
        functools.partial(_retention_kernel, mm, nseq, seq_len, nchunk),
        grid=(nbatch, steps),
        in_specs=in_specs,
        out_specs=(
            pl.BlockSpec((rows_per_step, RET_W), omap),
            pl.BlockSpec((nseq, RET_W, RET_W), smap),
        ),
        out_shape=(
            jax.ShapeDtypeStruct((nbatch * seq_total, RET_W), mm.dtype),
            jax.ShapeDtypeStruct((nbatch * nseq, RET_W, RET_W), F32),
        ),
        scratch_shapes=[pltpu.VMEM((nseq, RET_W, RET_W), F32)],
        compiler_params=_params("arbitrary", "arbitrary"),
        name="retention",
    )(ret_in, s0_bd, dm, qd, kd, cd, bd, gn, gw)


def _s5_kernel(mm, short, u_ref, x0r_ref, x0i_ref, bre_ref, bim_ref, pre_ref, pim_ref, cm_ref, d_ref,
               gw_ref, gb_ref, o_ref, str_ref, sti_ref, bur, bui, car, cai):
    cast, _dot = mm.cast, mm.dot
    step = pl.program_id(1)
    rows = u_ref.shape[0]
    u = u_ref[...]
    ub = cast(u)
    bur[...] = _dot(ub, bre_ref[...])
    bui[...] = _dot(ub, bim_ref[...])

    if not short:
        @pl.when(step == 0)
        def _():
            car[...] = x0r_ref[0]
            cai[...] = x0i_ref[0]

    pr = pre_ref[...]
    pi = pim_ref[...]
    rowi = lax.broadcasted_iota(jnp.int32, (SUBLANES, S5_LANES), 0)

    def group(j, carry):
        cr, ci = carry
        r0 = pl.multiple_of(j * SUBLANES, SUBLANES)
        xr = bur[pl.ds(r0, SUBLANES), :]
        xi = bui[pl.ds(r0, SUBLANES), :]
        for s in (1, 2, 4):
            ar = pr[s - 1:s]
            ai = pi[s - 1:s]
            sr = jnp.where(rowi >= s, pltpu.roll(xr, s, axis=0), 0.0)
            si = jnp.where(rowi >= s, pltpu.roll(xi, s, axis=0), 0.0)
            xr, xi = xr + ar * sr - ai * si, xi + ar * si + ai * sr
        if short:
            cr = x0r_ref[j]
            ci = x0i_ref[j]
        xr, xi = xr + pr * cr - pi * ci, xi + pr * ci + pi * cr
        bur[pl.ds(r0, SUBLANES), :] = xr
        bui[pl.ds(r0, SUBLANES), :] = xi
        cr = xr[SUBLANES - 1:SUBLANES]
        ci = xi[SUBLANES - 1:SUBLANES]
        if short:
            str_ref[j] = cr
            sti_ref[j] = ci
        return cr, ci

    if short:
        init = (jnp.zeros((1, S5_LANES), F32), jnp.zeros((1, S5_LANES), F32))
    else:
        init = (car[...], cai[...])
    cr, ci = lax.fori_loop(0, rows // SUBLANES, group, init)
    if not short:
        car[...] = cr
        cai[...] = ci

        @pl.when(step == pl.num_programs(1) - 1)
        def _():
            str_ref[0] = cr
            sti_ref[0] = ci

    xs = jnp.concatenate([cast(bur[...]), cast(bui[...])], axis=1)
    y = _dot(xs, cm_ref[...]) + d_ref[...] * u
    yg = 0.5 * y * (1.0 + jnp.tanh(math.sqrt(2.0 / math.pi) * (y + 0.044715 * (y * y * y))))
    z = _dot(cast(yg), gw_ref[...]) + gb_ref[...]
    o_ref[...] = (yg * _sigmoid(z)).astype(o_ref.dtype)


def _s5_tables(lam_re, lam_im, b_re, b_im, c_re, c_im, log_dt):
    lam = lax.complex(lam_re.astype(F32), lam_im.astype(F32))
    dt = jnp.exp(log_dt.astype(F32))[:, None]
    lam_bar = jnp.exp(lam * dt)
    b_bar = ((lam_bar - 1.0) / lam)[:, :, None] * lax.complex(b_re.astype(F32), b_im.astype(F32))
    k = jnp.arange(1, SUBLANES + 1, dtype=F32)[:, None, None]
    powers = jnp.exp((lam * dt)[None] * k)
    pre = jnp.real(powers).reshape(SUBLANES, S5_LANES)
    pim = jnp.imag(powers).reshape(SUBLANES, S5_LANES)
    eye = jnp.eye(S5_GROUPS, dtype=F32)
    bre = jnp.einsum('gpc,gh->gchp', jnp.real(b_bar), eye).reshape(S5_W, S5_LANES)
    bim = jnp.einsum('gpc,gh->gchp', jnp.imag(b_bar), eye).reshape(S5_W, S5_LANES)
    cre = jnp.einsum('gcp,gh->hpgc', c_re.astype(F32), eye).reshape(S5_LANES, S5_W)
    cim = jnp.einsum('gcp,gh->hpgc', c_im.astype(F32), eye).reshape(S5_LANES, S5_W)
    cm = jnp.concatenate([cre, -cim], axis=0)
    return bre, bim, pre, pim, cm


def _s5(mm, su, x0r, x0i, tables, d, glu_w, glu_b, nbatch, row0, seq_total, rows_per_step, short):
    bre, bim, pre, pim, cm = tables
    bre, bim, cm, glu_w = mm.cast(bre), mm.cast(bim), mm.cast(cm), mm.cast(glu_w)
    steps = seq_total // rows_per_step
    blk0 = row0 // rows_per_step
    nstate = x0r.shape[0]
    sblk = nstate if short else 1

    def rmap(b, i):
        return (blk0 + b * steps + i, 0)

    def omap(b, i):
        return (b * steps + i, 0)

    def c2(b, i):
        return (0, 0)

    def smap(b, i):
        return (0 if short else b, 0, 0)

    in_specs = [
        pl.BlockSpec((rows_per_step, S5_W), rmap),
        pl.BlockSpec((sblk, 1, S5_LANES), smap),
        pl.BlockSpec((sblk, 1, S5_LANES), smap),
        pl.BlockSpec((S5_W, S5_LANES), c2),
        pl.BlockSpec((S5_W, S5_LANES), c2),
        pl.BlockSpec((SUBLANES, S5_LANES), c2),
        pl.BlockSpec((SUBLANES, S5_LANES), c2),
        pl.BlockSpec((2 * S5_LANES, S5_W), c2),
        pl.BlockSpec((1, S5_W), c2),
        pl.BlockSpec((S5_W, S5_W), c2),
        pl.BlockSpec((1, S5_W), c2),
    ]
    return pl.pallas_call(
        functools.partial(_s5_kernel, mm, short),
        grid=(nbatch, steps),
        in_specs=in_specs,
        out_specs=(
            pl.BlockSpec((rows_per_step, S5_W), omap),
            pl.BlockSpec((sblk, 1, S5_LANES), smap),
            pl.BlockSpec((sblk, 1, S5_LANES), smap),
        ),
        out_shape=(
            jax.ShapeDtypeStruct((nbatch * seq_total, S5_W), mm.dtype),
            jax.ShapeDtypeStruct((nstate, 1, S5_LANES), F32),
            jax.ShapeDtypeStruct((nstate, 1, S5_LANES), F32),
        ),
        scratch_shapes=[
            pltpu.VMEM((rows_per_step, S5_LANES), F32),
            pltpu.VMEM((rows_per_step, S5_LANES), F32),
            pltpu.VMEM((1, S5_LANES), F32),
            pltpu.VMEM((1, S5_LANES), F32),
        ],
        compiler_params=_params("arbitrary", "arbitrary"),
        name="s5",
    )(su, x0r, x0i, bre, bim, pre, pim, cm, d, glu_w, glu_b)


_PAIR_W = 2 * LANES
_AUG_PER_HEAD = 6
_N_PAIRS = H_FOX // 2


def _aug_placement():
    pq = np.zeros((3 * LANES, _N_PAIRS * LANES), np.float32)
    pk = np.zeros((3 * LANES, _N_PAIRS * LANES), np.float32)
    oq = np.zeros((1, _N_PAIRS * LANES), np.float32)
    ok = np.zeros((1, _N_PAIRS * LANES), np.float32)
    for h in range(H_FOX):
        base = (h // 2) * LANES + (h % 2) * _AUG_PER_HEAD
        for part in range(3):
            pq[part * LANES + h, base + part] = 1.0
            oq[0, base + 3 + part] = 1.0
            ok[0, base + part] = 1.0
            pk[part * LANES + h, base + 3 + part] = -1.0
    return (jnp.asarray(pq, BF16), jnp.asarray(pk, BF16), jnp.asarray(oq), jnp.asarray(ok))


def _split3(v, axis=1):
    hi = v.astype(BF16)
    r1 = v - hi.astype(F32)
    mid = r1.astype(BF16)
    lo = (r1 - mid.astype(F32)).astype(BF16)
    return jnp.concatenate([hi, mid, lo], axis=axis)


def _cumsum_rows(tri_ref, x):
    c = _dot(tri_ref[...], _split3(x, axis=1))
    return c[:, 0:LANES] + c[:, LANES:2 * LANES] + c[:, 2 * LANES:3 * LANES]


def _cumsum_lanes(x, u_ref):
    r = x.shape[0]
    c = _dot(_split3(x, axis=0), u_ref[...])
    return c[0:r] + c[r:2 * r] + c[2 * r:3 * r]


def _fox_prep_kernel(lf_ref, q_ref, k_ref, v_ref, tri_ref, pq_ref, pk_ref, oq_ref, ok_ref,
                     qa_ref, ka_ref, va_ref, carry):
    @pl.when(pl.program_id(1) == 0)
    def _():
        carry[...] = jnp.zeros_like(carry)

    c = _cumsum_rows(tri_ref, lf_ref[...]) + carry[...]
    rows = c.shape[0]
    carry[...] = c[rows - 1:rows, :]
    parts = _split3(c * LOG2E)
    augq = (_dot(parts, pq_ref[...]) + oq_ref[...]).astype(BF16)
    augk = (_dot(parts, pk_ref[...]) + ok_ref[...]).astype(BF16)
    lane = lax.broadcasted_iota(jnp.int32, (rows, LANES), 1)
    one = jnp.ones((rows, LANES), BF16)
    for p in range(_N_PAIRS):
        src = slice(p * LANES, (p + 1) * LANES)
        qa_ref[:, p * _PAIR_W:p * _PAIR_W + LANES] = q_ref[:, src]
        qa_ref[:, p * _PAIR_W + LANES:(p + 1) * _PAIR_W] = augq[:, src]
        ka_ref[:, p * _PAIR_W:p * _PAIR_W + LANES] = k_ref[:, src]
        ka_ref[:, p * _PAIR_W + LANES:(p + 1) * _PAIR_W] = augk[:, src]
        vp = v_ref[:, src]
        va_ref[:, p * _PAIR_W:p * _PAIR_W + LANES] = jnp.where(lane < HEAD_DIM, vp, one)
        va_ref[:, p * _PAIR_W + LANES:(p + 1) * _PAIR_W] = jnp.where(lane < HEAD_DIM, one, vp)


def _fox_prep(logf, fq, fkb, fvb, nbatch, seq_total, rows_per_step):
    steps = seq_total // rows_per_step
    idx = np.arange(rows_per_step)
    tri = jnp.asarray(idx[:, None] >= idx[None, :], BF16)
    pq, pk, oq, ok = _aug_placement()
    wide = _N_PAIRS * _PAIR_W

    def rmap(b, i):
        return (b * steps + i, 0)

    def const(b, i):
        return (0, 0)

    out = jax.ShapeDtypeStruct((nbatch * seq_total, wide), BF16)
    return pl.pallas_call(
        _fox_prep_kernel,
        grid=(nbatch, steps),
        in_specs=[
            pl.BlockSpec((rows_per_step, LANES), rmap),
            pl.BlockSpec((rows_per_step, FOX_W), rmap),
            pl.BlockSpec((rows_per_step, FOX_W), rmap),
            pl.BlockSpec((rows_per_step, FOX_W), rmap),
            pl.BlockSpec((rows_per_step, rows_per_step), const),
            pl.BlockSpec(pq.shape, const),
            pl.BlockSpec(pk.shape, const),
            pl.BlockSpec(oq.shape, const),
            pl.BlockSpec(ok.shape, const),
        ],
        out_specs=(pl.BlockSpec((rows_per_step, wide), rmap),) * 3,
        out_shape=(out, out, out),
        scratch_shapes=[pltpu.VMEM((1, LANES), F32)],
        compiler_params=_params("arbitrary", "arbitrary"),
        name="fox_prep",
    )(logf, fq, fkb, fvb, tri, pq, pk, oq, ok)


_PAIRS_PER_STEP = 2


def _fox_prompt_kernel(tq, q_ref, k_ref, v_ref, o_ref, acc, m_scr):
    qi = pl.program_id(2)
    nhead = 2 * _PAIRS_PER_STEP
    lane = lax.broadcasted_iota(jnp.int32, (tq, _PAIR_W), 1)
    qs = []
    for h in range(nhead):
        par = h % 2
        q = q_ref[:, (h // 2) * _PAIR_W:(h // 2 + 1) * _PAIR_W]
        own = ((lane >= par * HEAD_DIM) & (lane < (par + 1) * HEAD_DIM)) | (
            (lane >= LANES + par * _AUG_PER_HEAD) & (lane < LANES + (par + 1) * _AUG_PER_HEAD))
        qs.append(jnp.where(own, q, jnp.zeros_like(q)))
        m_scr[h] = jnp.full((tq, LANES), NEG_BIG, F32)
        acc[h] = jnp.zeros((tq, LANES), F32)
    causal = (lax.broadcasted_iota(jnp.int32, (tq, tq), 0) >= lax.broadcasted_iota(jnp.int32, (tq, tq), 1))

    def kv_step(j, masked):
        ks = pl.multiple_of(j * tq, tq)
        for h in range(nhead):
            kb = k_ref[pl.ds(ks, tq), (h // 2) * _PAIR_W:(h // 2 + 1) * _PAIR_W]
            s = _dot_nt(qs[h], kb)
            if masked:
                s = jnp.where(causal, s, NEG_BIG)
            m_old = m_scr[h]
            m_new = jnp.maximum(m_old, jnp.max(s, axis=1, keepdims=True))
            p = jnp.exp2(s - jnp.concatenate([m_new] * (tq // LANES), axis=1))
            vb = v_ref[pl.ds(ks, tq), h * LANES:(h + 1) * LANES]
            acc[h] = jnp.exp2(m_old - m_new) * acc[h] + _dot(p.astype(BF16), vb)
            m_scr[h] = m_new

    def body(j, _):
        kv_step(j, False)
        return 0

    lax.fori_loop(0, qi, body, 0)
    kv_step(qi, True)
    lane128 = lax.broadcasted_iota(jnp.int32, (tq, LANES), 1)
    for pr in range(_PAIRS_PER_STEP):
        outs = []
        for par in range(2):
            a = acc[2 * pr + par]
            outs.append(a / pltpu.roll(a, HEAD_DIM, axis=1))
        o_ref[:, pr * LANES:(pr + 1) * LANES] = jnp.where(lane128 < HEAD_DIM, outs[0], outs[1]).astype(BF16)


def _fox_prompt(qa, ka, va, nbatch, seq_total):
    tq = min(FOX_TQ, seq_total)
    nq = seq_total // tq
    w = _PAIRS_PER_STEP * _PAIR_W
    return pl.pallas_call(
        functools.partial(_fox_prompt_kernel, tq),
        grid=(nbatch, _N_PAIRS // _PAIRS_PER_STEP, nq),
        in_specs=[
            pl.BlockSpec((tq, w), lambda b, g, i: (b * nq + i, g)),
            pl.BlockSpec((seq_total, w), lambda b, g, i: (b, g)),
            pl.BlockSpec((seq_total, w), lambda b, g, i: (b, g)),
        ],
        out_specs=pl.BlockSpec((tq, _PAIRS_PER_STEP * LANES), lambda b, g, i: (b * nq + i, g)),
        out_shape=jax.ShapeDtypeStruct((nbatch * seq_total, FOX_W), BF16),
        scratch_shapes=[
            pltpu.VMEM((2 * _PAIRS_PER_STEP, tq, LANES), F32),
            pltpu.VMEM((2 * _PAIRS_PER_STEP, tq, LANES), F32),
        ],
        compiler_params=_params("arbitrary", "arbitrary", "arbitrary"),
        name="fox_prompt",
    )(qa, ka, va)


def _fox_sample_kernel(npg, tnew, pt_ref, qm_ref, *refs):
    del pt_ref
    k_refs = refs[0:npg]
    v_refs = refs[npg:2 * npg]
    lf_refs = refs[2 * npg:3 * npg]
    knew_ref, vnew_ref, lfnew_ref, u_ref, sel_ref, o_ref, m_scr, l_scr, acc, run, kb_scr, vb_scr = refs[3 * npg:]
    g = pl.program_id(1)
    nrow = tnew * H_FOX
    npage = knew_ref.shape[1]

    @pl.when(g == 0)
    def _():
        m_scr[...] = jnp.full(m_scr.shape, NEG_BIG, F32)
        l_scr[...] = jnp.zeros(l_scr.shape, F32)
        acc[...] = jnp.zeros(acc.shape, F32)
        run[...] = jnp.zeros(run.shape, F32)

    qm = qm_ref[0]

    def cum_within(lft):
        return _cumsum_lanes(lft, u_ref)

    def attend(k, v, ckeys, mask, transposed):
        qk = _dot(qm, k) if transposed else _dot_nt(qm, k)
        s = qk - jnp.concatenate([ckeys * LOG2E] * tnew, axis=0)
        if mask is not None:
            s = jnp.where(mask, s, NEG_BIG)
        m_old = m_scr[...]
        m_new = jnp.maximum(m_old, jnp.max(s, axis=1, keepdims=True))
        p = jnp.exp2(s - m_new).astype(BF16)
        alpha = jnp.exp2(m_old - m_new)
        l_scr[...] = alpha * l_scr[...] + jnp.sum(p.astype(F32), axis=1, keepdims=True)
        acc[...] = alpha * acc[...] + (_dot_nt(p, v) if transposed else _dot(p, v))
        m_scr[...] = m_new

    base = run[...]
    cs = []
    for j in range(npg):
        kb_scr[:, j * npage:(j + 1) * npage] = k_refs[j][0, 0].astype(BF16)
        vb_scr[:, j * npage:(j + 1) * npage] = v_refs[j][0, 0].astype(BF16)
        cw = cum_within(lf_refs[j][0, 0])
        cs.append(base + cw)
        base = base + jnp.broadcast_to(cw[:, npage - 1:npage], base.shape)
    run[...] = base
    attend(kb_scr[...], vb_scr[...], jnp.concatenate(cs, axis=1), None, True)

    @pl.when(g == pl.num_programs(1) - 1)
    def _():
        rt = lax.broadcasted_iota(jnp.int32, (nrow, npage), 0) // H_FOX
        col = lax.broadcasted_iota(jnp.int32, (nrow, npage), 1)
        attend(knew_ref[0].astype(BF16), vnew_ref[0].astype(BF16), run[...] + cum_within(lfnew_ref[0]), col <= rt,
               False)
        o = acc[...] / l_scr[...]
        rh = lax.broadcasted_iota(jnp.int32, (nrow, FOX_W), 0) % H_FOX
        lh = lax.broadcasted_iota(jnp.int32, (nrow, FOX_W), 1) // HEAD_DIM
        om = jnp.where(rh == lh, o, 0.0)
        o_ref[0] = jnp.dot(sel_ref[...], om, precision=HIGHEST, preferred_element_type=F32)


def _fox_sample(layer, page_table, qm, cache_kt, cache_vt, cache_lft, knew, vnew, lfnew):
    nb, npages = page_table.shape
    page = cache_kt.shape[3]
    tnew = qm.shape[1] // H_FOX
    npg = PAGES_PER_STEP
    ngrp = npages // npg
    idx = np.arange(page)
    u = jnp.asarray(idx[:, None] <= idx[None, :], BF16)
    sel = jnp.asarray(np.arange(tnew)[:, None] == (np.arange(tnew * H_FOX)[None, :] // H_FOX), F32)

    def pmap(j):
        def f(b, g, pt):
            return (layer, pt[b * npages + g * npg + j], 0, 0)
        return f

    def bmap(b, g, pt):
        return (b, 0, 0)

    def c2(b, g, pt):
        return (0, 0)

    in_specs = [pl.BlockSpec((1, tnew * H_FOX, FOX_W), bmap)]
    in_specs += [pl.BlockSpec((1, 1, FOX_W, page), pmap(j)) for j in range(npg)]
    in_specs += [pl.BlockSpec((1, 1, FOX_W, page), pmap(j)) for j in range(npg)]
    in_specs += [pl.BlockSpec((1, 1, H_FOX, page), pmap(j)) for j in range(npg)]
    in_specs += [
        pl.BlockSpec((1, page, FOX_W), bmap),
        pl.BlockSpec((1, page, FOX_W), bmap),
        pl.BlockSpec((1, H_FOX, page), bmap),
        pl.BlockSpec((page, page), c2),
        pl.BlockSpec((tnew, tnew * H_FOX), c2),
    ]
    grid_spec = pltpu.PrefetchScalarGridSpec(
        num_scalar_prefetch=1,
        grid=(nb, ngrp),
        in_specs=in_specs,
        out_specs=pl.BlockSpec((1, tnew, FOX_W), bmap),
        scratch_shapes=[
            pltpu.VMEM((tnew * H_FOX, 1), F32),
            pltpu.VMEM((tnew * H_FOX, 1), F32),
            pltpu.VMEM((tnew * H_FOX, FOX_W), F32),
            pltpu.VMEM((H_FOX, page), F32),
            pltpu.VMEM((FOX_W, npg * page), BF16),
            pltpu.VMEM((FOX_W, npg * page), BF16),
        ],
    )
    args = [page_table.reshape(-1), qm] + [cache_kt] * npg + [cache_vt] * npg + [cache_lft] * npg
    args += [knew, vnew, lfnew, u, sel]
    return pl.pallas_call(
        functools.partial(_fox_sample_kernel, npg, tnew),
        grid_spec=grid_spec,
        out_shape=jax.ShapeDtypeStruct((nb, tnew, FOX_W), F32),
        compiler_params=_params("arbitrary", "arbitrary"),
        name="fox_sample",
    )(*args)


_LANE_GROUP0 = N_EXPERTS


def _out_proj_kernel(n_prompt_tiles, xp_ref, xs_ref, retp_ref, rets_ref, s5p_ref, s5s_ref, foxp_ref, foxs_ref, wo_ref,
                     wof_ref, g_ref, wr_ref, wrl_ref, br_ref, tri_ref, x1_ref, h2_ref, info_ref, cnt_ref, carry):
    i = pl.program_id(0)

    @pl.when(i == 0)
    def _():
        carry[...] = jnp.zeros_like(carry)

    def project(mm, x_ref, ret_ref, s5_ref, fox_ref, w_ref):
        x1 = x_ref[...] + mm.dot(ret_ref[...], w_ref[0:RET_W, :])
        x1 = x1 + mm.dot(s5_ref[...], w_ref[RET_W:RET_W + S5_W, :])
        x1_ref[...] = x1 + mm.dot(fox_ref[...], w_ref[RET_W + S5_W:, :])

    @pl.when(i < n_prompt_tiles)
    def _():
        project(_FAST, xp_ref, retp_ref, s5p_ref, foxp_ref, wo_ref)

    @pl.when(i >= n_prompt_tiles)
    def _():
        project(_EXACT, xs_ref, rets_ref, s5s_ref, foxs_ref, wof_ref)

    x1 = x1_ref[...]
    ms = jnp.mean(x1 * x1, axis=-1, keepdims=True)
    h2 = x1 * lax.rsqrt(ms + RMS_EPS) * g_ref[...]
    h2_ref[...] = h2
    h_hi = h2.astype(BF16)
    h_lo = (h2 - h_hi.astype(F32)).astype(BF16)
    logits = _dot(h_hi, wr_ref[...]) + _dot(h_hi, wrl_ref[...]) + _dot(h_lo, wr_ref[...]) + br_ref[...]
    tm = logits.shape[0]
    lane = lax.broadcasted_iota(jnp.int32, (tm, LANES), 1)
    big = jnp.int32(LANES)
    gmask = (lane >= _LANE_GROUP0) & (lane < _LANE_GROUP0 + N_EXPERT_GROUPS)
    gl = jnp.where(gmask, logits, NEG_BIG)
    gmax = jnp.max(gl, axis=1, keepdims=True)
    gsum = jnp.sum(jnp.where(gmask, jnp.exp(gl - gmax), 0.0), axis=1, keepdims=True)
    g_w = 1.0 / gsum
    g_idx = jnp.min(jnp.where(gl == gmax, lane, big), axis=1, keepdims=True) - _LANE_GROUP0
    lo = g_idx * EXPERTS_PER_GROUP
    emask = (lane >= lo) & (lane < lo + EXPERTS_PER_GROUP)
    el = jnp.where(emask, logits, NEG_BIG)
    e1 = jnp.max(el, axis=1, keepdims=True)
    idx1 = jnp.min(jnp.where(el == e1, lane, big), axis=1, keepdims=True)
    esum = jnp.sum(jnp.where(emask, jnp.exp(el - e1), 0.0), axis=1, keepdims=True)
    el2 = jnp.where(lane == idx1, NEG_BIG, el)
    e2 = jnp.max(el2, axis=1, keepdims=True)
    idx2 = jnp.min(jnp.where(el2 == e2, lane, big), axis=1, keepdims=True)
    p1 = 1.0 / esum
    p2 = jnp.exp(e2 - e1) / esum
    gate1 = g_w * p1 / (p1 + p2)
    gate2 = g_w * p2 / (p1 + p2)
    oh1 = lane == idx1
    oh2 = lane == idx2
    a = jnp.where(oh1 | oh2, 1.0, 0.0)
    cum = _dot(tri_ref[...], a.astype(BF16))
    before = cum - a + carry[...]
    rank1 = jnp.sum(jnp.where(oh1, before, 0.0), axis=1, keepdims=True)
    rank2 = jnp.sum(jnp.where(oh2, before, 0.0), axis=1, keepdims=True)
    carry[...] = carry[...] + cum[tm - 1:tm, :]
    cnt_ref[...] = carry[...]
    info = jnp.where(lane == 0, idx1.astype(F32), 0.0)
    info = jnp.where(lane == 1, idx2.astype(F32), info)
    info = jnp.where(lane == 2, rank1, info)
    info = jnp.where(lane == 3, rank2, info)
    info = jnp.where(lane == 4, gate1, info)
    info = jnp.where(lane == 5, gate2, info)
    info_ref[...] = info


def _out_proj(xp, xs, xs_tile0, ret_p, ret_s, s5_p, s5_s, fox_p, fox_s, w_out, g2, w_r, b_r, n_prompt_tiles):
    tm = TOKEN_TILE
    d = xp.shape[1]
    n = (n_prompt_tiles + ret_s.shape[0] // tm) * tm
    w_out_fast = w_out.astype(BF16)
    w_r_hi = w_r.astype(BF16)
    w_r_lo = (w_r - w_r_hi.astype(F32)).astype(BF16)
    idx = np.arange(tm)
    tri = jnp.asarray(idx[:, None] >= idx[None, :], BF16)

    def row(i):
        return (i, 0)

    def const(i):
        return (0, 0)

    def pmap(i):
        return (jnp.minimum(i, n_prompt_tiles - 1), 0)

    def smap(i):
        return (jnp.maximum(i - n_prompt_tiles, 0), 0)

    def xsmap(i):
        return (xs_tile0 + jnp.maximum(i - n_prompt_tiles, 0), 0)

    return pl.pallas_call(
        functools.partial(_out_proj_kernel, n_prompt_tiles),
        grid=(n // tm,),
        in_specs=[
            pl.BlockSpec((tm, d), pmap),
            pl.BlockSpec((tm, d), xsmap),
            pl.BlockSpec((tm, RET_W), pmap),
            pl.BlockSpec((tm, RET_W), smap),
            pl.BlockSpec((tm, S5_W), pmap),
            pl.BlockSpec((tm, S5_W), smap),
            pl.BlockSpec((tm, FOX_W), pmap),
            pl.BlockSpec((tm, FOX_W), smap),
            pl.BlockSpec((d, d), const),
            pl.BlockSpec((d, d), const),
            pl.BlockSpec((1, d), const),
            pl.BlockSpec((d, LANES), const),
            pl.BlockSpec((d, LANES), const),
            pl.BlockSpec((1, LANES), const),
            pl.BlockSpec((tm, tm), const),
        ],
        out_specs=(
            pl.BlockSpec((tm, d), row),
            pl.BlockSpec((tm, d), row),
            pl.BlockSpec((tm, LANES), row),
            pl.BlockSpec((1, LANES), const),
        ),
        out_shape=(
            jax.ShapeDtypeStruct((n, d), F32),
            jax.ShapeDtypeStruct((n, d), F32),
            jax.ShapeDtypeStruct((n, LANES), F32),
            jax.ShapeDtypeStruct((1, LANES), F32),
        ),
        scratch_shapes=[pltpu.VMEM((1, LANES), F32)],
        compiler_params=_params("arbitrary"),
        name="out_proj_router",
    )(xp, xs, ret_p, ret_s, s5_p, s5_s, fox_p, fox_s, w_out_fast, w_out, g2, w_r_hi, w_r_lo, b_r, tri)


_ROW_DMA_UNROLL = 8


def _dispatch_kernel(dest_ref, h_ref, rows_in_ref, rows_ref, sem):
    del rows_in_ref
    i = pl.program_id(0)
    tm = h_ref.shape[0]

    def issue(r, _):
        t = i * tm + r
        for k in range(2):
            pltpu.make_async_copy(h_ref.at[pl.ds(r, 1)], rows_ref.at[pl.ds(dest_ref[2 * t + k], 1)], sem).start()
        return 0

    lax.fori_loop(0, tm, issue, 0, unroll=_ROW_DMA_UNROLL)
    for _ in range(2):
        pltpu.make_async_copy(h_ref, rows_ref.at[pl.ds(0, tm)], sem).wait()


def _dispatch(dest, h2, n_rows):
    n, d = h2.shape
    tm = TOKEN_TILE
    grid_spec = pltpu.PrefetchScalarGridSpec(
        num_scalar_prefetch=1,
        grid=(n // tm,),
        in_specs=[
            pl.BlockSpec((tm, d), lambda i, dest: (i, 0)),
            pl.BlockSpec(memory_space=pl.ANY),
        ],
        out_specs=pl.BlockSpec(memory_space=pl.ANY),
        scratch_shapes=[pltpu.SemaphoreType.DMA(())],
    )
    return pl.pallas_call(
        _dispatch_kernel,
        grid_spec=grid_spec,
        out_shape=jax.ShapeDtypeStruct((n_rows, d), F32),
        input_output_aliases={2: 0},
        compiler_params=_params("arbitrary"),
        name="moe_dispatch",
    )(dest, h2, jnp.zeros((n_rows, d), F32))


def _experts_kernel(be_ref, nu_ref, x_ref, w1_ref, w3_ref, w2_ref, y_ref, w1b, w3b, w2b):
    i = pl.program_id(0)
    prev = be_ref[jnp.maximum(i - 1, 0)]
    fresh = (i == 0) | (be_ref[i] != prev)
    active = i < nu_ref[0]

    @pl.when(active & fresh)
    def _():
        w1b[...] = w1_ref[0, 0].astype(BF16)
        w3b[...] = w3_ref[0, 0].astype(BF16)
        w2b[...] = w2_ref[0, 0].astype(BF16)

    @pl.when(active)
    def _():
        xb = x_ref[...].astype(BF16)
        h1 = _dot(xb, w1b[...])
        h3 = _dot(xb, w3b[...])
        a = (h1 * _sigmoid(h1) * h3).astype(BF16)
        y_ref[...] = _dot(a, w2b[...])

    @pl.when(jnp.logical_not(active))
    def _():
        y_ref[...] = jnp.zeros_like(y_ref)


def _experts(layer, blk_e, n_used, x_rows, w1, w3, w2):
    n_rows, d = x_rows.shape
    de = w1.shape[3]
    nblk = n_rows // MOE_ROWS

    def last_used(i, nu):
        return jnp.minimum(i, jnp.maximum(nu[0] - 1, 0))

    def xmap(i, be, nu):
        return (last_used(i, nu), 0)

    def wmap(i, be, nu):
        return (layer, be[last_used(i, nu)], 0, 0)

    grid_spec = pltpu.PrefetchScalarGridSpec(
        num_scalar_prefetch=2,
        grid=(nblk,),
        in_specs=[
            pl.BlockSpec((MOE_ROWS, d), xmap),
            pl.BlockSpec((1, 1, d, de), wmap),
            pl.BlockSpec((1, 1, d, de), wmap),
            pl.BlockSpec((1, 1, de, d), wmap),
        ],
        out_specs=pl.BlockSpec((MOE_ROWS, d), lambda i, be, nu: (i, 0)),
        scratch_shapes=[
            pltpu.VMEM((d, de), BF16),
            pltpu.VMEM((d, de), BF16),
            pltpu.VMEM((de, d), BF16),
        ],
    )
    return pl.pallas_call(
        _experts_kernel,
        grid_spec=grid_spec,
        out_shape=jax.ShapeDtypeStruct((n_rows, d), F32),
        compiler_params=_params("arbitrary"),
        name="moe_experts",
    )(blk_e, n_used, x_rows, w1, w3, w2)


def _combine_kernel(n_prompt_tiles, dest_ref, x1_ref, info_ref, gf_ref, y_ref, o_ref, *rest):
    os_ref = rest[0] if n_prompt_tiles is not None else None
    buf, sem = rest[-2:]
    i = pl.program_id(0)
    tm = x1_ref.shape[0]

    def issue(r, _):
        t = i * tm + r
        for k in range(2):
            pltpu.make_async_copy(y_ref.at[pl.ds(dest_ref[2 * t + k], 1)], buf.at[k, pl.ds(r, 1)], sem).start()
        return 0

    lax.fori_loop(0, tm, issue, 0, unroll=_ROW_DMA_UNROLL)
    for k in range(2):
        pltpu.make_async_copy(y_ref.at[pl.ds(0, tm)], buf.at[k], sem).wait()
    info = info_ref[...]
    x2 = x1_ref[...] + info[:, 4:5] * buf[0] + info[:, 5:6] * buf[1]
    if n_prompt_tiles is None:
        o_ref[...] = x2
    else:
        ms = jnp.mean(x2 * x2, axis=-1, keepdims=True)
        y = x2 * lax.rsqrt(ms + RMS_EPS) * gf_ref[...]

        @pl.when(i < n_prompt_tiles)
        def _():
            o_ref[...] = y

        @pl.when(i >= n_prompt_tiles)
        def _():
            os_ref[...] = y


def _combine(dest, x1, info, gf, y_rows, n_prompt_tiles):
    n, d = x1.shape
    tm = TOKEN_TILE
    if n_prompt_tiles is None:
        out_specs = pl.BlockSpec((tm, d), lambda i, dest: (i, 0))
        out_shape = jax.ShapeDtypeStruct((n, d), F32)
    else:
        npt = n_prompt_tiles
        out_specs = (pl.BlockSpec((tm, d), lambda i, dest: (jnp.minimum(i, npt - 1), 0)),
                     pl.BlockSpec((tm, d), lambda i, dest: (jnp.maximum(i - npt, 0), 0)))
        out_shape = (jax.ShapeDtypeStruct((npt * tm, d), F32), jax.ShapeDtypeStruct((n - npt * tm, d), F32))
    grid_spec = pltpu.PrefetchScalarGridSpec(
        num_scalar_prefetch=1,
        grid=(n // tm,),
        in_specs=[
            pl.BlockSpec((tm, d), lambda i, dest: (i, 0)),
            pl.BlockSpec((tm, LANES), lambda i, dest: (i, 0)),
            pl.BlockSpec((1, d), lambda i, dest: (0, 0)),
            pl.BlockSpec(memory_space=pl.ANY),
        ],
        out_specs=out_specs,
        scratch_shapes=[
            pltpu.VMEM((2, tm, d), F32),
            pltpu.SemaphoreType.DMA(()),
        ],
    )
    return pl.pallas_call(
        functools.partial(_combine_kernel, n_prompt_tiles),
        grid_spec=grid_spec,
        out_shape=out_shape,
        compiler_params=_params("arbitrary"),
        name="moe_combine",
    )(dest, x1, info, gf, y_rows)


def _moe_plan(info, counts, n_tokens):
    eid = info[:, 0:2].astype(jnp.int32)
    rank = info[:, 2:4].astype(jnp.int32)
    cnt = counts[0, :N_EXPERTS].astype(jnp.int32)
    padded = (cnt + MOE_ROWS - 1) // MOE_ROWS * MOE_ROWS
    pend = jnp.cumsum(padded)
    pstart = pend - padded
    experts = jnp.arange(N_EXPERTS, dtype=jnp.int32)
    start_of = jnp.sum(jnp.where(eid[..., None] == experts, pstart, 0), axis=-1)
    dest = (start_of + rank).reshape(-1)
    nblk = -(-n_tokens * 2 // MOE_ROWS) + N_EXPERTS
    blk_start = jnp.arange(nblk, dtype=jnp.int32) * MOE_ROWS
    blk_e = jnp.minimum(jnp.sum((pend[None, :] <= blk_start[:, None]).astype(jnp.int32), axis=1), N_EXPERTS - 1)
    n_used = (pend[-1:] // MOE_ROWS).astype(jnp.int32)
    return dest.astype(jnp.int32), blk_e.astype(jnp.int32), n_used, nblk * MOE_ROWS


def _rope_tables(seq, past, tnew, tile):
    half = HEAD_DIM // 2
    inv = ROPE_THETA ** (-jnp.arange(half, dtype=F32) / half)
    pos = jnp.concatenate([jnp.arange(seq, dtype=F32), past + (jnp.arange(tile) % tnew).astype(F32)])
    ang = pos[:, None] * inv[None, :]
    cos = jnp.cos(ang)
    sin = jnp.sin(ang)
    cos_h = jnp.concatenate([cos, cos], axis=1)
    sin_h = jnp.concatenate([-sin, sin], axis=1)
    return jnp.tile(cos_h, (1, H_RET)), jnp.tile(sin_h, (1, H_RET))


def _block_diag_states(s):
    b = s.shape[0]
    eye = jnp.eye(H_RET, dtype=s.dtype)
    return jnp.einsum('bhde,hg->bhdge', s, eye).reshape(b, RET_W, RET_W)


def _diag_blocks(s_bd):
    b = s_bd.shape[0]
    s5 = s_bd.reshape(b, H_RET, HEAD_DIM, H_RET, HEAD_DIM)
    return jnp.stack([s5[:, h, :, h, :] for h in range(H_RET)], axis=1)


def kernel(x_prompt, x_sample, cache_k, cache_v, cache_logf, state_ret, state_s5, page_table, norm1, w_in, ret_gn,
           s5_lam_re, s5_lam_im, s5_b_re, s5_b_im, s5_c_re, s5_c_im, s5_d, s5_log_dt, s5_glu_w, s5_glu_b, fox_fb,
           w_out, norm2, w_rg, b_rg, w_re, b_re, w1, w3, w2, norm_f):
    bp, seq, d = x_prompt.shape
    bd_, tnew, _ = x_sample.shape
    depth = w_in.shape[0]
    n_phys, page = cache_k.shape[1], cache_k.shape[2]
    npages = page_table.shape[1]
    past = npages * page
    n_p = bp * seq
    n_s = bd_ * tnew
    n = n_p + n_s
    tm = TOKEN_TILE
    assert n_p % tm == 0 and n_s == tm and seq % tm == 0
    n_prompt_tiles = n_p // tm

    xp, xs, xs_tile0 = x_prompt.reshape(n_p, d), x_sample.reshape(n_s, d), 0
    cos_t, sin_t = _rope_tables(seq, past, tnew, tm)
    cache_kt = jnp.transpose(cache_k, (0, 1, 3, 4, 2)).reshape(depth, n_phys, FOX_W, page)
    cache_vt = jnp.transpose(cache_v, (0, 1, 3, 4, 2)).reshape(depth, n_phys, FOX_W, page)
    cache_lft = jnp.swapaxes(cache_logf, 2, 3)
    ff0 = w_in.shape[2] - H_FOX

    outs = {k: [] for k in ('kp', 'vp', 'lp', 'ks', 'vs', 'ls', 'rp', 'rs', 'sp', 'ss')}
    for l in range(depth):
        w_main = w_in[l, :, :ff0]
        w_ff = jnp.pad(w_in[l, :, ff0:], ((0, 0), (0, LANES - H_FOX)))
        fb = jnp.pad(fox_fb[l], (0, LANES - H_FOX))[None, :]
        g1 = norm1[l][None, :]
        ret_in, su, fq, fk, fv, fkb, fvb, logf = _in_proj(
            _FAST, xp, g1, w_main.astype(BF16), w_ff.astype(BF16), fb, cos_t, sin_t,
            0, n_prompt_tiles, 0, seq // tm)
        ret_in_s, su_s, fq_s, fk_s, fv_s, _, _, logf_s = _in_proj(
            _EXACT, xs, g1, w_main, w_ff, fb, cos_t, sin_t, xs_tile0, 1, seq // tm, 1)

        gw = ret_gn[l][None, :]
        zero_state = jnp.zeros((bp, RET_W, RET_W), F32)
        ret_p, rs_p = _retention(_FAST, ret_in, zero_state, gw, bp, 0, seq, 1, min(RET_CHUNK, seq),
                                 min(1024, seq))
        seq_per_chunk = 8
        chunk_rows = seq_per_chunk * tnew
        ret_s, rs_s = _retention(_EXACT, ret_in_s, _block_diag_states(state_ret[l]), gw, bd_ // seq_per_chunk, 0,
                                 chunk_rows, seq_per_chunk, tnew, chunk_rows)

        tables = _s5_tables(s5_lam_re[l], s5_lam_im[l], s5_b_re[l], s5_b_im[l], s5_c_re[l], s5_c_im[l],
                            s5_log_dt[l])
        dsk = s5_d[l][None, :]
        glub = s5_glu_b[l][None, :]
        zs = jnp.zeros((bp, 1, S5_LANES), F32)
        s5_p, sp_r, sp_i = _s5(_FAST, su, zs, zs, tables, dsk, s5_glu_w[l], glub, bp, 0, seq, min(512, seq),
                               False)
        x0 = state_s5[l].reshape(bd_, 1, S5_LANES, 2)
        s5_s, ss_r, ss_i = _s5(_EXACT, su_s, x0[..., 0], x0[..., 1], tables, dsk, s5_glu_w[l], glub, 1, 0, n_s,
                               n_s, True)

        qa, ka, va = _fox_prep(logf, fq, fkb, fvb, bp, seq, min(256, seq))
        fox_p = _fox_prompt(qa, ka, va, bp, seq)
        q_s = fq_s.reshape(bd_, tnew, 1, FOX_W)
        hmask = jnp.asarray(np.arange(H_FOX)[:, None] == (np.arange(FOX_W) // HEAD_DIM)[None, :])[None, None]
        qm = jnp.where(hmask, q_s, jnp.zeros((), BF16)).reshape(bd_, tnew * H_FOX, FOX_W)
        pad_rows = ((0, 0), (0, page - tnew), (0, 0))
        knew = jnp.pad(fk_s.reshape(bd_, tnew, FOX_W), pad_rows)
        vnew = jnp.pad(fv_s.reshape(bd_, tnew, FOX_W), pad_rows)
        lfnew = jnp.pad(jnp.swapaxes(logf_s[:, :H_FOX].reshape(bd_, tnew, H_FOX), 1, 2),
                        ((0, 0), (0, 0), (0, page - tnew)))
        fox_s = _fox_sample(l, page_table, qm, cache_kt, cache_vt, cache_lft, knew, vnew, lfnew)
        fox_s = fox_s.reshape(n_s, FOX_W)

        w_r = jnp.pad(jnp.concatenate([w_re[l], w_rg[l]], axis=1),
                      ((0, 0), (0, LANES - N_EXPERTS - N_EXPERT_GROUPS)))
        b_r = jnp.pad(jnp.concatenate([b_re[l], b_rg[l]]), (0, LANES - N_EXPERTS - N_EXPERT_GROUPS))[None, :]
        x1, h2, info, counts = _out_proj(xp, xs, xs_tile0, ret_p, ret_s, s5_p, s5_s, fox_p, fox_s, w_out[l],
                                         norm2[l][None, :], w_r, b_r, n_prompt_tiles)
        dest, blk_e, n_used, n_rows = _moe_plan(info, counts, n)
        x_rows = _dispatch(dest, h2, n_rows)
        y_rows = _experts(l, blk_e, n_used, x_rows, w1, w3, w2)
        if l < depth - 1:
            x_all = _combine(dest, x1, info, norm_f[None, :], y_rows, None)
            xp, xs, xs_tile0 = x_all, x_all, n_prompt_tiles
        else:
            y_p, y_s = _combine(dest, x1, info, norm_f[None, :], y_rows, n_prompt_tiles)

        outs['kp'].append(fk)
        outs['vp'].append(fv)
        outs['lp'].append(logf[:, :H_FOX].reshape(bp, seq, H_FOX))
        outs['ks'].append(fk_s.reshape(bd_, tnew, H_FOX, HEAD_DIM))
        outs['vs'].append(fv_s.reshape(bd_, tnew, H_FOX, HEAD_DIM))
        outs['ls'].append(logf_s[:, :H_FOX].reshape(bd_, tnew, H_FOX))
        outs['rp'].append(_diag_blocks(rs_p))
        outs['rs'].append(_diag_blocks(rs_s))
        outs['sp'].append(jnp.stack([sp_r.reshape(bp, S5_GROUPS, S5_STATE),
                                     sp_i.reshape(bp, S5_GROUPS, S5_STATE)], axis=-1))
        outs['ss'].append(jnp.stack([ss_r.reshape(bd_, S5_GROUPS, S5_STATE),
                                     ss_i.reshape(bd_, S5_GROUPS, S5_STATE)], axis=-1))

    y_prompt = y_p.reshape(bp, seq, d)
    y_sample = y_s.reshape(bd_, tnew, d)
    st = {k: jnp.stack(v) for k, v in outs.items()}
    for name in ('kp', 'vp'):
        st[name] = st[name].reshape(depth, bp, seq, H_FOX, HEAD_DIM)
    return (y_prompt, y_sample, st['kp'], st['vp'], st['lp'], st['ks'], st['vs'], st['ls'],
            st['rp'], st['rs'], st['sp'], st['ss'])
```

```python
import functools
import math

import numpy as np

import jax
import jax.numpy as jnp
from jax import lax
from jax.experimental import pallas as pl
from jax.experimental.pallas import tpu as pltpu

F32 = jnp.float32
BF16 = jnp.bfloat16
HIGHEST = lax.Precision.HIGHEST

HEAD_DIM = 64
H_RET = 4
H_FOX = 8
RET_W = H_RET * HEAD_DIM
S5_W = 256
FOX_W = H_FOX * HEAD_DIM
S5_GROUPS = 16
S5_GROUP_CH = 16
S5_STATE = 64
S5_LANES = S5_GROUPS * S5_STATE
N_EXPERTS = 32
N_EXPERT_GROUPS = 4
EXPERTS_PER_GROUP = 8
RET_CHUNK = 128
ROPE_THETA = 10000.0
RMS_EPS = 1e-6
NEG_BIG = -1e30
LOG2E = math.log2(math.e)

LANES = 128
SUBLANES = 8
TOKEN_TILE = 256
MOE_ROWS = 256
FOX_TQ = 1024
PAGES_PER_STEP = 32
VMEM_LIMIT = 48 * 1024 * 1024
VMEM_LIMIT_PAGED = 56 * 1024 * 1024


def _params(*sem, vmem=VMEM_LIMIT):
    return pltpu.CompilerParams(dimension_semantics=sem, vmem_limit_bytes=vmem)


def _sigmoid(x):
    return 1.0 / (1.0 + jnp.exp(-x))


def _dot(a, b):
    return jnp.dot(a, b, preferred_element_type=F32)


def _dot_nt(a, b):
    return lax.dot_general(a, b, (((1,), (1,)), ((), ())), preferred_element_type=F32)


def _dot_tn(a, b):
    return lax.dot_general(a, b, (((0,), (0,)), ((), ())), preferred_element_type=F32)


class _MatmulMode:
    def __init__(self, exact):
        self.dtype = F32 if exact else BF16
        self.precision = HIGHEST if exact else None

    def cast(self, x):
        return x.astype(self.dtype)

    def dot(self, a, b):
        return jnp.dot(a, b, precision=self.precision, preferred_element_type=F32)

    def dot_nt(self, a, b):
        return lax.dot_general(a, b, (((1,), (1,)), ((), ())), precision=self.precision,
                               preferred_element_type=F32)

    def dot_tn(self, a, b):
        return lax.dot_general(a, b, (((0,), (0,)), ((), ())), precision=self.precision,
                               preferred_element_type=F32)


_FAST = _MatmulMode(False)
_EXACT = _MatmulMode(True)


def _in_proj_kernel(mm, x_ref, g_ref, w_ref, wff_ref, fb_ref, cos_ref, sin_ref,
                    ret_ref, su_ref, fq_ref, fk_ref, fv_ref, fkb_ref, fvb_ref, logf_ref):
    _dot = mm.dot
    x = x_ref[...]
    ms = jnp.mean(x * x, axis=-1, keepdims=True)
    h = mm.cast(x * lax.rsqrt(ms + RMS_EPS) * g_ref[...])
    tm = x.shape[0]
    cos = cos_ref[...]
    sin = sin_ref[...]
    lane = lax.broadcasted_iota(jnp.int32, (tm, RET_W), 1)
    first_half = (lane % HEAD_DIM) < (HEAD_DIM // 2)

    def rope(t):
        swapped = jnp.where(first_half,
                            pltpu.roll(t, RET_W - HEAD_DIM // 2, axis=1),
                            pltpu.roll(t, HEAD_DIM // 2, axis=1))
        return t * cos + swapped * sin

    ret_ref[:, 0:RET_W] = rope(_dot(h, w_ref[:, 0:RET_W]))
    ret_ref[:, RET_W:2 * RET_W] = rope(_dot(h, w_ref[:, RET_W:2 * RET_W])) * (HEAD_DIM ** -0.5)
    ret_ref[:, 2 * RET_W:4 * RET_W] = _dot(h, w_ref[:, 2 * RET_W:4 * RET_W])
    c0 = 4 * RET_W
    su_ref[...] = _dot(h, w_ref[:, c0:c0 + S5_W])
    c0 += S5_W
    fq_ref[...] = (_dot(h, w_ref[:, c0:c0 + FOX_W]) * (HEAD_DIM ** -0.5 * LOG2E)).astype(BF16)
    c0 += FOX_W
    fk = _dot(h, w_ref[:, c0:c0 + FOX_W])
    fk_ref[...] = fk
    fkb_ref[...] = fk.astype(BF16)
    c0 += FOX_W
    fv = _dot(h, w_ref[:, c0:c0 + FOX_W])
    fv_ref[...] = fv
    fvb_ref[...] = fv.astype(BF16)
    z = _dot(h, wff_ref[...]) + fb_ref[...]
    logf = jnp.minimum(z, 0.0) - jnp.log1p(jnp.exp(-jnp.abs(z)))
    lane128 = lax.broadcasted_iota(jnp.int32, (tm, LANES), 1)
    logf_ref[...] = jnp.where(lane128 < H_FOX, logf, 0.0)


def _in_proj(mm, x, g, w_main, w_ff, fb, cos_t, sin_t, tile0, ntiles, pos_tile0, pos_tiles):
    tm = TOKEN_TILE
    n = ntiles * tm
    wcols = w_main.shape[1]

    def xrow(i):
        return (tile0 + i, 0)

    def row(i):
        return (i, 0)

    def const(i):
        return (0, 0)

    def pos_map(i):
        return (pos_tile0 + i % pos_tiles, 0)

    outs = (
        jax.ShapeDtypeStruct((n, 4 * RET_W), F32),
        jax.ShapeDtypeStruct((n, S5_W), F32),
        jax.ShapeDtypeStruct((n, FOX_W), BF16),
        jax.ShapeDtypeStruct((n, FOX_W), F32),
        jax.ShapeDtypeStruct((n, FOX_W), F32),
        jax.ShapeDtypeStruct((n, FOX_W), BF16),
        jax.ShapeDtypeStruct((n, FOX_W), BF16),
        jax.ShapeDtypeStruct((n, LANES), F32),
    )
    return pl.pallas_call(
        functools.partial(_in_proj_kernel, mm),
        grid=(ntiles,),
        in_specs=[
            pl.BlockSpec((tm, x.shape[1]), xrow),
            pl.BlockSpec((1, x.shape[1]), const),
            pl.BlockSpec((x.shape[1], wcols), const),
            pl.BlockSpec((x.shape[1], LANES), const),
            pl.BlockSpec((1, LANES), const),
            pl.BlockSpec((tm, RET_W), pos_map),
            pl.BlockSpec((tm, RET_W), pos_map),
        ],
        out_specs=(
            pl.BlockSpec((tm, 4 * RET_W), row),
            pl.BlockSpec((tm, S5_W), row),
            pl.BlockSpec((tm, FOX_W), row),
            pl.BlockSpec((tm, FOX_W), row),
            pl.BlockSpec((tm, FOX_W), row),
            pl.BlockSpec((tm, FOX_W), row),
            pl.BlockSpec((tm, FOX_W), row),
            pl.BlockSpec((tm, LANES), row),
        ),
        out_shape=outs,
        compiler_params=_params("arbitrary"),
        name="in_proj",
    )(x, g, w_main, w_ff, fb, cos_t, sin_t)


def _retention_kernel(mm, nseq, seq_len, nchunk, x_ref, s0_ref, dm_ref, qd_ref, kd_ref, cd_ref, bd_ref,
                      gn_ref, gw_ref, o_ref, sout_ref, s_scr):
    cast, _dot, _dot_nt, _dot_tn = mm.cast, mm.dot, mm.dot_nt, mm.dot_tn
    step = pl.program_id(1)
    rows = nseq * seq_len

    @pl.when(step == 0)
    def _():
        s_scr[...] = s0_ref[...]

    lane_head = lax.broadcasted_iota(jnp.int32, (rows, RET_W), 1) // HEAD_DIM
    row_seq = lax.broadcasted_iota(jnp.int32, (rows, 1), 0) // seq_len

    def chunk(c, _):
        r0 = pl.multiple_of(c * rows, rows)
        q = x_ref[pl.ds(r0, rows), 0:RET_W]
        k = x_ref[pl.ds(r0, rows), RET_W:2 * RET_W]
        v = x_ref[pl.ds(r0, rows), 2 * RET_W:3 * RET_W]
        g = x_ref[pl.ds(r0, rows), 3 * RET_W:4 * RET_W]
        qb = cast(q)
        kb = cast(k)
        vb = cast(v)
        ps = []
        vs = []
        for h in range(H_RET):
            hm = lane_head == h
            sc = _dot_nt(cast(jnp.where(hm, q, 0.0)), kb)
            ps.append(cast(sc * dm_ref[h]))
            vs.append(cast(jnp.where(hm, v, 0.0)))
        inner = _dot(jnp.concatenate(ps, axis=1), jnp.concatenate(vs, axis=0))
        kdec = k * kd_ref[...]
        cd = cd_ref[...]
        bd = bd_ref[...]
        if nseq == 1:
            s_old = s_scr[0]
            cross = _dot(qb, cast(s_old))
            s_scr[0] = s_old * cd + _dot_tn(cast(kdec), vb) * bd
        else:
            def seq_body(s, cross):
                rm = row_seq == s
                s_old = s_scr[s]
                cross = jnp.where(rm, _dot(qb, cast(s_old)), cross)
                kv = _dot_tn(cast(jnp.where(rm, kdec, 0.0)), vb)
                s_scr[s] = s_old * cd + kv * bd
                return cross

            cross = lax.fori_loop(0, nseq, seq_body, jnp.zeros((rows, RET_W), F32))
        o = inner + cross * qd_ref[...]
        o2 = o * o
        hi = o2.astype(BF16)
        lo = (o2 - hi.astype(F32)).astype(BF16)
        ms = jnp.dot(hi, gn_ref[...], preferred_element_type=F32) + jnp.dot(
            lo, gn_ref[...], preferred_element_type=F32)
        on = o * lax.rsqrt(ms + RMS_EPS)
        o_ref[pl.ds(r0, rows), :] = (g * _sigmoid(g) * on * gw_ref[...]).astype(o_ref.dtype)
        return 0

    lax.fori_loop(0, nchunk, chunk, 0)

    @pl.when(step == pl.num_programs(1) - 1)
    def _():
        sout_ref[...] = s_scr[...]


def _retention_tables(nseq, seq_len):
    f32 = np.float32
    rows = nseq * seq_len
    log_gamma = np.log1p(-np.power(f32(2.0), -5.0 - np.arange(H_RET, dtype=f32))).astype(f32)
    idx = np.arange(rows)
    pos = (idx % seq_len).astype(f32)
    seq = idx // seq_len
    diff = pos[:, None] - pos[None, :]
    ok = (seq[:, None] == seq[None, :]) & (diff >= 0)
    dm = np.where(ok[None], np.exp(log_gamma[:, None, None] * np.maximum(diff, f32(0.0))[None]), f32(0.0))
    lg_lane = np.repeat(log_gamma, HEAD_DIM)
    qd = np.exp((pos[:, None] + f32(1.0)) * lg_lane[None, :])
    kd = np.exp((f32(seq_len - 1.0) - pos[:, None]) * lg_lane[None, :])
    cd = np.exp(f32(seq_len) * lg_lane)[None, :]
    head = np.arange(RET_W) // HEAD_DIM
    bd = (head[:, None] == head[None, :]).astype(f32)
    to = lambda a: jnp.asarray(a, F32)
    return to(dm), to(qd), to(kd), to(cd), to(bd), jnp.asarray(bd / HEAD_DIM, BF16)


def _retention(mm, ret_in, s0_bd, gw, nbatch, row0, seq_total, nseq, seq_len, rows_per_step):
    rows = nseq * seq_len
    nchunk = rows_per_step // rows
    steps = seq_total // rows_per_step
    blk0 = row0 // rows_per_step
    dm, qd, kd, cd, bd, gn = _retention_tables(nseq, seq_len)

    def rmap(b, i):
        return (blk0 + b * steps + i, 0)

    def omap(b, i):
        return (b * steps + i, 0)

    def c2(b, i):
        return (0, 0)

    def c3(b, i):
        return (0, 0, 0)

    def smap(b, i):
        return (b, 0, 0)

    in_specs = [
        pl.BlockSpec((rows_per_step, 4 * RET_W), rmap),
        pl.BlockSpec((nseq, RET_W, RET_W), smap),
        pl.BlockSpec((H_RET, rows, rows), c3),
        pl.BlockSpec((rows, RET_W), c2),
        pl.BlockSpec((rows, RET_W), c2),
        pl.BlockSpec((1, RET_W), c2),
        pl.BlockSpec((RET_W, RET_W), c2),
        pl.BlockSpec((RET_W, RET_W), c2),
        pl.BlockSpec((1, RET_W), c2),
    ]
    return pl.pallas_call(
        functools.partial(_retention_kernel, mm, nseq, seq_len, nchunk),
        grid=(nbatch, steps),
        in_specs=in_specs,
        out_specs=(
            pl.BlockSpec((rows_per_step, RET_W), omap),
            pl.BlockSpec((nseq, RET_W, RET_W), smap),
        ),
        out_shape=(
            jax.ShapeDtypeStruct((nbatch * seq_total, RET_W), mm.dtype),
            jax.ShapeDtypeStruct((nbatch * nseq, RET_W, RET_W), F32),
        ),
        scratch_shapes=[pltpu.VMEM((nseq, RET_W, RET_W), F32)],
        compiler_params=_params("arbitrary", "arbitrary"),
        name="retention",
    )(ret_in, s0_bd, dm, qd, kd, cd, bd, gn, gw)


def _s5_kernel(mm, short, u_ref, x0r_ref, x0i_ref, bre_ref, bim_ref, pre_ref, pim_ref, cm_ref, d_ref,
               gw_ref, gb_ref, o_ref, str_ref, sti_ref, bur, bui, car, cai):
    cast, _dot = mm.cast, mm.dot
    step = pl.program_id(1)
    rows = u_ref.shape[0]
    u = u_ref[...]
    ub = cast(u)
    bur[...] = _dot(ub, bre_ref[...])
    bui[...] = _dot(ub, bim_ref[...])

    if not short:
        @pl.when(step == 0)
        def _():
            car[...] = x0r_ref[0]
            cai[...] = x0i_ref[0]

    pr = pre_ref[...]
    pi = pim_ref[...]
    rowi = lax.broadcasted_iota(jnp.int32, (SUBLANES, S5_LANES), 0)

    def group(j, carry):
        cr, ci = carry
        r0 = pl.multiple_of(j * SUBLANES, SUBLANES)
        xr = bur[pl.ds(r0, SUBLANES), :]
        xi = bui[pl.ds(r0, SUBLANES), :]
        for s in (1, 2, 4):
            ar = pr[s - 1:s]
            ai = pi[s - 1:s]
            sr = jnp.where(rowi >= s, pltpu.roll(xr, s, axis=0), 0.0)
            si = jnp.where(rowi >= s, pltpu.roll(xi, s, axis=0), 0.0)
            xr, xi = xr + ar * sr - ai * si, xi + ar * si + ai * sr
        if short:
            cr = x0r_ref[j]
            ci = x0i_ref[j]
        xr, xi = xr + pr * cr - pi * ci, xi + pr * ci + pi * cr
        bur[pl.ds(r0, SUBLANES), :] = xr
        bui[pl.ds(r0, SUBLANES), :] = xi
        cr = xr[SUBLANES - 1:SUBLANES]
        ci = xi[SUBLANES - 1:SUBLANES]
        if short:
            str_ref[j] = cr
            sti_ref[j] = ci
        return cr, ci

    if short:
        init = (jnp.zeros((1, S5_LANES), F32), jnp.zeros((1, S5_LANES), F32))
    else:
        init = (car[...], cai[...])
    cr, ci = lax.fori_loop(0, rows // SUBLANES, group, init)
    if not short:
        car[...] = cr
        cai[...] = ci

        @pl.when(step == pl.num_programs(1) - 1)
        def _():
            str_ref[0] = cr
            sti_ref[0] = ci

    xs = jnp.concatenate([cast(bur[...]), cast(bui[...])], axis=1)
    y = _dot(xs, cm_ref[...]) + d_ref[...] * u
    yg = 0.5 * y * (1.0 + jnp.tanh(math.sqrt(2.0 / math.pi) * (y + 0.044715 * (y * y * y))))
    z = _dot(cast(yg), gw_ref[...]) + gb_ref[...]
    o_ref[...] = (yg * _sigmoid(z)).astype(o_ref.dtype)


def _s5_tables(lam_re, lam_im, b_re, b_im, c_re, c_im, log_dt):
    lam = lax.complex(lam_re.astype(F32), lam_im.astype(F32))
    dt = jnp.exp(log_dt.astype(F32))[:, None]
    lam_bar = jnp.exp(lam * dt)
    b_bar = ((lam_bar - 1.0) / lam)[:, :, None] * lax.complex(b_re.astype(F32), b_im.astype(F32))
    k = jnp.arange(1, SUBLANES + 1, dtype=F32)[:, None, None]
    powers = jnp.exp((lam * dt)[None] * k)
    pre = jnp.real(powers).reshape(SUBLANES, S5_LANES)
    pim = jnp.imag(powers).reshape(SUBLANES, S5_LANES)
    eye = jnp.eye(S5_GROUPS, dtype=F32)
    bre = jnp.einsum('gpc,gh->gchp', jnp.real(b_bar), eye).reshape(S5_W, S5_LANES)
    bim = jnp.einsum('gpc,gh->gchp', jnp.imag(b_bar), eye).reshape(S5_W, S5_LANES)
    cre = jnp.einsum('gcp,gh->hpgc', c_re.astype(F32), eye).reshape(S5_LANES, S5_W)
    cim = jnp.einsum('gcp,gh->hpgc', c_im.astype(F32), eye).reshape(S5_LANES, S5_W)
    cm = jnp.concatenate([cre, -cim], axis=0)
    return bre, bim, pre, pim, cm


def _s5(mm, su, x0r, x0i, tables, d, glu_w, glu_b, nbatch, row0, seq_total, rows_per_step, short):
    bre, bim, pre, pim, cm = tables
    bre, bim, cm, glu_w = mm.cast(bre), mm.cast(bim), mm.cast(cm), mm.cast(glu_w)
    steps = seq_total // rows_per_step
    blk0 = row0 // rows_per_step
    nstate = x0r.shape[0]
    sblk = nstate if short else 1

    def rmap(b, i):
        return (blk0 + b * steps + i, 0)

    def omap(b, i):
        return (b * steps + i, 0)

    def c2(b, i):
        return (0, 0)

    def smap(b, i):
        return (0 if short else b, 0, 0)

    in_specs = [
        pl.BlockSpec((rows_per_step, S5_W), rmap),
        pl.BlockSpec((sblk, 1, S5_LANES), smap),
        pl.BlockSpec((sblk, 1, S5_LANES), smap),
        pl.BlockSpec((S5_W, S5_LANES), c2),
        pl.BlockSpec((S5_W, S5_LANES), c2),
        pl.BlockSpec((SUBLANES, S5_LANES), c2),
        pl.BlockSpec((SUBLANES, S5_LANES), c2),
        pl.BlockSpec((2 * S5_LANES, S5_W), c2),
        pl.BlockSpec((1, S5_W), c2),
        pl.BlockSpec((S5_W, S5_W), c2),
        pl.BlockSpec((1, S5_W), c2),
    ]
    return pl.pallas_call(
        functools.partial(_s5_kernel, mm, short),
        grid=(nbatch, steps),
        in_specs=in_specs,
        out_specs=(
            pl.BlockSpec((rows_per_step, S5_W), omap),
            pl.BlockSpec((sblk, 1, S5_LANES), smap),
            pl.BlockSpec((sblk, 1, S5_LANES), smap),
        ),
        out_shape=(
            jax.ShapeDtypeStruct((nbatch * seq_total, S5_W), mm.dtype),
            jax.ShapeDtypeStruct((nstate, 1, S5_LANES), F32),
            jax.ShapeDtypeStruct((nstate, 1, S5_LANES), F32),
        ),
        scratch_shapes=[
            pltpu.VMEM((rows_per_step, S5_LANES), F32),
            pltpu.VMEM((rows_per_step, S5_LANES), F32),
            pltpu.VMEM((1, S5_LANES), F32),
            pltpu.VMEM((1, S5_LANES), F32),
        ],
        compiler_params=_params("arbitrary", "arbitrary"),
        name="s5",
    )(su, x0r, x0i, bre, bim, pre, pim, cm, d, glu_w, glu_b)


_PAIR_W = 2 * LANES
_AUG_PER_HEAD = 6
_N_PAIRS = H_FOX // 2


def _aug_placement():
    pq = np.zeros((3 * LANES, _N_PAIRS * LANES), np.float32)
    pk = np.zeros((3 * LANES, _N_PAIRS * LANES), np.float32)
    oq = np.zeros((1, _N_PAIRS * LANES), np.float32)
    ok = np.zeros((1, _N_PAIRS * LANES), np.float32)
    for h in range(H_FOX):
        base = (h // 2) * LANES + (h % 2) * _AUG_PER_HEAD
        for part in range(3):
            pq[part * LANES + h, base + part] = 1.0
            oq[0, base + 3 + part] = 1.0
            ok[0, base + part] = 1.0
            pk[part * LANES + h, base + 3 + part] = -1.0
    return (jnp.asarray(pq, BF16), jnp.asarray(pk, BF16), jnp.asarray(oq), jnp.asarray(ok))


def _split3(v, axis=1):
    hi = v.astype(BF16)
    r1 = v - hi.astype(F32)
    mid = r1.astype(BF16)
    lo = (r1 - mid.astype(F32)).astype(BF16)
    return jnp.concatenate([hi, mid, lo], axis=axis)


def _cumsum_rows(tri_ref, x):
    c = _dot(tri_ref[...], _split3(x, axis=1))
    return c[:, 0:LANES] + c[:, LANES:2 * LANES] + c[:, 2 * LANES:3 * LANES]


def _cumsum_lanes(x, u_ref):
    r = x.shape[0]
    c = _dot(_split3(x, axis=0), u_ref[...])
    return c[0:r] + c[r:2 * r] + c[2 * r:3 * r]


def _fox_prep_kernel(lf_ref, q_ref, k_ref, v_ref, tri_ref, pq_ref, pk_ref, oq_ref, ok_ref,
                     qa_ref, ka_ref, va_ref, carry):
    @pl.when(pl.program_id(1) == 0)
    def _():
        carry[...] = jnp.zeros_like(carry)

    c = _cumsum_rows(tri_ref, lf_ref[...]) + carry[...]
    rows = c.shape[0]
    carry[...] = c[rows - 1:rows, :]
    parts = _split3(c * LOG2E)
    augq = (_dot(parts, pq_ref[...]) + oq_ref[...]).astype(BF16)
    augk = (_dot(parts, pk_ref[...]) + ok_ref[...]).astype(BF16)
    lane = lax.broadcasted_iota(jnp.int32, (rows, LANES), 1)
    one = jnp.ones((rows, LANES), BF16)
    for p in range(_N_PAIRS):
        src = slice(p * LANES, (p + 1) * LANES)
        qa_ref[:, p * _PAIR_W:p * _PAIR_W + LANES] = q_ref[:, src]
        qa_ref[:, p * _PAIR_W + LANES:(p + 1) * _PAIR_W] = augq[:, src]
        ka_ref[:, p * _PAIR_W:p * _PAIR_W + LANES] = k_ref[:, src]
        ka_ref[:, p * _PAIR_W + LANES:(p + 1) * _PAIR_W] = augk[:, src]
        vp = v_ref[:, src]
        va_ref[:, p * _PAIR_W:p * _PAIR_W + LANES] = jnp.where(lane < HEAD_DIM, vp, one)
        va_ref[:, p * _PAIR_W + LANES:(p + 1) * _PAIR_W] = jnp.where(lane < HEAD_DIM, one, vp)


def _fox_prep(logf, fq, fkb, fvb, nbatch, seq_total, rows_per_step):
    steps = seq_total // rows_per_step
    idx = np.arange(rows_per_step)
    tri = jnp.asarray(idx[:, None] >= idx[None, :], BF16)
    pq, pk, oq, ok = _aug_placement()
    wide = _N_PAIRS * _PAIR_W

    def rmap(b, i):
        return (b * steps + i, 0)

    def const(b, i):
        return (0, 0)

    out = jax.ShapeDtypeStruct((nbatch * seq_total, wide), BF16)
    return pl.pallas_call(
        _fox_prep_kernel,
        grid=(nbatch, steps),
        in_specs=[
            pl.BlockSpec((rows_per_step, LANES), rmap),
            pl.BlockSpec((rows_per_step, FOX_W), rmap),
            pl.BlockSpec((rows_per_step, FOX_W), rmap),
            pl.BlockSpec((rows_per_step, FOX_W), rmap),
            pl.BlockSpec((rows_per_step, rows_per_step), const),
            pl.BlockSpec(pq.shape, const),
            pl.BlockSpec(pk.shape, const),
            pl.BlockSpec(oq.shape, const),
            pl.BlockSpec(ok.shape, const),
        ],
        out_specs=(pl.BlockSpec((rows_per_step, wide), rmap),) * 3,
        out_shape=(out, out, out),
        scratch_shapes=[pltpu.VMEM((1, LANES), F32)],
        compiler_params=_params("arbitrary", "arbitrary"),
        name="fox_prep",
    )(logf, fq, fkb, fvb, tri, pq, pk, oq, ok)


_PAIRS_PER_STEP = 2


def _fox_prompt_kernel(tq, q_ref, k_ref, v_ref, o_ref, acc, m_scr):
    qi = pl.program_id(2)
    nhead = 2 * _PAIRS_PER_STEP
    lane = lax.broadcasted_iota(jnp.int32, (tq, _PAIR_W), 1)
    qs = []
    for h in range(nhead):
        par = h % 2
        q = q_ref[:, (h // 2) * _PAIR_W:(h // 2 + 1) * _PAIR_W]
        own = ((lane >= par * HEAD_DIM) & (lane < (par + 1) * HEAD_DIM)) | (
            (lane >= LANES + par * _AUG_PER_HEAD) & (lane < LANES + (par + 1) * _AUG_PER_HEAD))
        qs.append(jnp.where(own, q, jnp.zeros_like(q)))
        m_scr[h] = jnp.full((tq, LANES), NEG_BIG, F32)
        acc[h] = jnp.zeros((tq, LANES), F32)
    half = tq // 2

    def update(h, r0, nr, k0, nk, mask):
        kb = k_ref[pl.ds(k0, nk), (h // 2) * _PAIR_W:(h // 2 + 1) * _PAIR_W]
        s = _dot_nt(qs[h][r0:r0 + nr], kb)
        if mask is not None:
            s = jnp.where(mask, s, NEG_BIG)
        m_old = m_scr[h, r0:r0 + nr]
        m_new = jnp.maximum(m_old, jnp.max(s, axis=1, keepdims=True))
        p = jnp.exp2(s - jnp.concatenate([m_new] * (nk // LANES), axis=1))
        vb = v_ref[pl.ds(k0, nk), h * LANES:(h + 1) * LANES]
        acc[h, r0:r0 + nr] = jnp.exp2(m_old - m_new) * acc[h, r0:r0 + nr] + _dot(p.astype(BF16), vb)
        m_scr[h, r0:r0 + nr] = m_new

    def body(j, _):
        ks = pl.multiple_of(j * tq, tq)
        for h in range(nhead):
            update(h, 0, tq, ks, tq, None)
        return 0

    lax.fori_loop(0, qi, body, 0)
    ks = pl.multiple_of(qi * tq, tq)
    tri = lax.broadcasted_iota(jnp.int32, (half, half), 0) >= lax.broadcasted_iota(jnp.int32, (half, half), 1)
    low = lax.broadcasted_iota(jnp.int32, (half, tq), 0) + half >= lax.broadcasted_iota(jnp.int32, (half, tq), 1)
    for h in range(nhead):
        update(h, 0, half, ks, half, tri)
        update(h, half, half, ks, tq, low)
    lane128 = lax.broadcasted_iota(jnp.int32, (tq, LANES), 1)
    for pr in range(_PAIRS_PER_STEP):
        outs = []
        for par in range(2):
            a = acc[2 * pr + par]
            outs.append(a / pltpu.roll(a, HEAD_DIM, axis=1))
        o_ref[:, pr * LANES:(pr + 1) * LANES] = jnp.where(lane128 < HEAD_DIM, outs[0], outs[1]).astype(BF16)


def _fox_prompt(qa, ka, va, nbatch, seq_total):
    tq = min(FOX_TQ, seq_total)
    nq = seq_total // tq
    w = _PAIRS_PER_STEP * _PAIR_W
    return pl.pallas_call(
        functools.partial(_fox_prompt_kernel, tq),
        grid=(nbatch, _N_PAIRS // _PAIRS_PER_STEP, nq),
        in_specs=[
            pl.BlockSpec((tq, w), lambda b, g, i: (b * nq + i, g)),
            pl.BlockSpec((seq_total, w), lambda b, g, i: (b, g)),
            pl.BlockSpec((seq_total, w), lambda b, g, i: (b, g)),
        ],
        out_specs=pl.BlockSpec((tq, _PAIRS_PER_STEP * LANES), lambda b, g, i: (b * nq + i, g)),
        out_shape=jax.ShapeDtypeStruct((nbatch * seq_total, FOX_W), BF16),
        scratch_shapes=[
            pltpu.VMEM((2 * _PAIRS_PER_STEP, tq, LANES), F32),
            pltpu.VMEM((2 * _PAIRS_PER_STEP, tq, LANES), F32),
        ],
        compiler_params=_params("arbitrary", "arbitrary", "arbitrary"),
        name="fox_prompt",
    )(qa, ka, va)


def _fox_sample_kernel(npg, tnew, pt_ref, qm_ref, *refs):
    del pt_ref
    k_refs = refs[0:npg]
    v_refs = refs[npg:2 * npg]
    lf_refs = refs[2 * npg:3 * npg]
    knew_ref, vnew_ref, lfnew_ref, u_ref, sel_ref, o_ref, m_scr, l_scr, acc, run, kb_scr, vb_scr = refs[3 * npg:]
    g = pl.program_id(1)
    nrow = tnew * H_FOX
    npage = knew_ref.shape[1]

    @pl.when(g == 0)
    def _():
        m_scr[...] = jnp.full(m_scr.shape, NEG_BIG, F32)
        l_scr[...] = jnp.zeros(l_scr.shape, F32)
        acc[...] = jnp.zeros(acc.shape, F32)
        run[...] = jnp.zeros(run.shape, F32)

    qm = qm_ref[0]

    def cum_within(lft):
        return _cumsum_lanes(lft, u_ref)

    def attend(k, v, ckeys, mask, transposed):
        qk = _dot(qm, k) if transposed else _dot_nt(qm, k)
        s = qk - jnp.concatenate([ckeys * LOG2E] * tnew, axis=0)
        if mask is not None:
            s = jnp.where(mask, s, NEG_BIG)
        m_old = m_scr[...]
        m_new = jnp.maximum(m_old, jnp.max(s, axis=1, keepdims=True))
        p = jnp.exp2(s - m_new).astype(BF16)
        alpha = jnp.exp2(m_old - m_new)
        l_scr[...] = alpha * l_scr[...] + jnp.sum(p.astype(F32), axis=1, keepdims=True)
        acc[...] = alpha * acc[...] + (_dot_nt(p, v) if transposed else _dot(p, v))
        m_scr[...] = m_new

    base = run[...]
    cs = []
    for j in range(npg):
        kb_scr[:, j * npage:(j + 1) * npage] = k_refs[j][0, 0].astype(BF16)
        vb_scr[:, j * npage:(j + 1) * npage] = v_refs[j][0, 0].astype(BF16)
        cw = cum_within(lf_refs[j][0, 0])
        cs.append(base + cw)
        base = base + jnp.broadcast_to(cw[:, npage - 1:npage], base.shape)
    run[...] = base
    attend(kb_scr[...], vb_scr[...], jnp.concatenate(cs, axis=1), None, True)

    @pl.when(g == pl.num_programs(1) - 1)
    def _():
        rt = lax.broadcasted_iota(jnp.int32, (nrow, npage), 0) // H_FOX
        col = lax.broadcasted_iota(jnp.int32, (nrow, npage), 1)
        attend(knew_ref[0].astype(BF16), vnew_ref[0].astype(BF16), run[...] + cum_within(lfnew_ref[0]), col <= rt,
               False)
        o = acc[...] / l_scr[...]
        rh = lax.broadcasted_iota(jnp.int32, (nrow, FOX_W), 0) % H_FOX
        lh = lax.broadcasted_iota(jnp.int32, (nrow, FOX_W), 1) // HEAD_DIM
        om = jnp.where(rh == lh, o, 0.0)
        o_ref[0] = jnp.dot(sel_ref[...], om, precision=HIGHEST, preferred_element_type=F32)


def _fox_sample(layer, page_table, qm, cache_kt, cache_vt, cache_lft, knew, vnew, lfnew):
    nb, npages = page_table.shape
    page = cache_kt.shape[3]
    tnew = qm.shape[1] // H_FOX
    npg = PAGES_PER_STEP
    ngrp = npages // npg
    idx = np.arange(page)
    u = jnp.asarray(idx[:, None] <= idx[None, :], BF16)
    sel = jnp.asarray(np.arange(tnew)[:, None] == (np.arange(tnew * H_FOX)[None, :] // H_FOX), F32)

    def pmap(j):
        def f(b, g, pt):
            return (layer, pt[b * npages + g * npg + j], 0, 0)
        return f

    def bmap(b, g, pt):
        return (b, 0, 0)

    def c2(b, g, pt):
        return (0, 0)

    in_specs = [pl.BlockSpec((1, tnew * H_FOX, FOX_W), bmap)]
    in_specs += [pl.BlockSpec((1, 1, FOX_W, page), pmap(j)) for j in range(npg)]
    in_specs += [pl.BlockSpec((1, 1, FOX_W, page), pmap(j)) for j in range(npg)]
    in_specs += [pl.BlockSpec((1, 1, H_FOX, page), pmap(j)) for j in range(npg)]
    in_specs += [
        pl.BlockSpec((1, page, FOX_W), bmap),
        pl.BlockSpec((1, page, FOX_W), bmap),
        pl.BlockSpec((1, H_FOX, page), bmap),
        pl.BlockSpec((page, page), c2),
        pl.BlockSpec((tnew, tnew * H_FOX), c2),
    ]
    grid_spec = pltpu.PrefetchScalarGridSpec(
        num_scalar_prefetch=1,
        grid=(nb, ngrp),
        in_specs=in_specs,
        out_specs=pl.BlockSpec((1, tnew, FOX_W), bmap),
        scratch_shapes=[
            pltpu.VMEM((tnew * H_FOX, 1), F32),
            pltpu.VMEM((tnew * H_FOX, 1), F32),
            pltpu.VMEM((tnew * H_FOX, FOX_W), F32),
            pltpu.VMEM((H_FOX, page), F32),
            pltpu.VMEM((FOX_W, npg * page), BF16),
            pltpu.VMEM((FOX_W, npg * page), BF16),
        ],
    )
    args = [page_table.reshape(-1), qm] + [cache_kt] * npg + [cache_vt] * npg + [cache_lft] * npg
    args += [knew, vnew, lfnew, u, sel]
    return pl.pallas_call(
        functools.partial(_fox_sample_kernel, npg, tnew),
        grid_spec=grid_spec,
        out_shape=jax.ShapeDtypeStruct((nb, tnew, FOX_W), F32),
        compiler_params=_params("arbitrary", "arbitrary", vmem=VMEM_LIMIT_PAGED),
        name="fox_sample",
    )(*args)


_LANE_GROUP0 = N_EXPERTS


def _out_proj_kernel(n_prompt_tiles, xp_ref, xs_ref, retp_ref, rets_ref, s5p_ref, s5s_ref, foxp_ref, foxs_ref, wo_ref,
                     wof_ref, g_ref, wr_ref, wrl_ref, br_ref, tri_ref, x1_ref, h2_ref, info_ref, cnt_ref, carry):
    i = pl.program_id(0)

    @pl.when(i == 0)
    def _():
        carry[...] = jnp.zeros_like(carry)

    def project(mm, x_ref, ret_ref, s5_ref, fox_ref, w_ref):
        x1 = x_ref[...] + mm.dot(ret_ref[...], w_ref[0:RET_W, :])
        x1 = x1 + mm.dot(s5_ref[...], w_ref[RET_W:RET_W + S5_W, :])
        x1_ref[...] = x1 + mm.dot(fox_ref[...], w_ref[RET_W + S5_W:, :])

    @pl.when(i < n_prompt_tiles)
    def _():
        project(_FAST, xp_ref, retp_ref, s5p_ref, foxp_ref, wo_ref)

    @pl.when(i >= n_prompt_tiles)
    def _():
        project(_EXACT, xs_ref, rets_ref, s5s_ref, foxs_ref, wof_ref)

    x1 = x1_ref[...]
    ms = jnp.mean(x1 * x1, axis=-1, keepdims=True)
    h2 = x1 * lax.rsqrt(ms + RMS_EPS) * g_ref[...]
    h2_ref[...] = h2
    h_hi = h2.astype(BF16)
    h_lo = (h2 - h_hi.astype(F32)).astype(BF16)
    logits = _dot(h_hi, wr_ref[...]) + _dot(h_hi, wrl_ref[...]) + _dot(h_lo, wr_ref[...]) + br_ref[...]
    tm = logits.shape[0]
    lane = lax.broadcasted_iota(jnp.int32, (tm, LANES), 1)
    big = jnp.int32(LANES)
    gmask = (lane >= _LANE_GROUP0) & (lane < _LANE_GROUP0 + N_EXPERT_GROUPS)
    gl = jnp.where(gmask, logits, NEG_BIG)
    gmax = jnp.max(gl, axis=1, keepdims=True)
    gsum = jnp.sum(jnp.where(gmask, jnp.exp(gl - gmax), 0.0), axis=1, keepdims=True)
    g_w = 1.0 / gsum
    g_idx = jnp.min(jnp.where(gl == gmax, lane, big), axis=1, keepdims=True) - _LANE_GROUP0
    lo = g_idx * EXPERTS_PER_GROUP
    emask = (lane >= lo) & (lane < lo + EXPERTS_PER_GROUP)
    el = jnp.where(emask, logits, NEG_BIG)
    e1 = jnp.max(el, axis=1, keepdims=True)
    idx1 = jnp.min(jnp.where(el == e1, lane, big), axis=1, keepdims=True)
    esum = jnp.sum(jnp.where(emask, jnp.exp(el - e1), 0.0), axis=1, keepdims=True)
    el2 = jnp.where(lane == idx1, NEG_BIG, el)
    e2 = jnp.max(el2, axis=1, keepdims=True)
    idx2 = jnp.min(jnp.where(el2 == e2, lane, big), axis=1, keepdims=True)
    p1 = 1.0 / esum
    p2 = jnp.exp(e2 - e1) / esum
    gate1 = g_w * p1 / (p1 + p2)
    gate2 = g_w * p2 / (p1 + p2)
    oh1 = lane == idx1
    oh2 = lane == idx2
    a = jnp.where(oh1 | oh2, 1.0, 0.0)
    cum = _dot(tri_ref[...], a.astype(BF16))
    before = cum - a + carry[...]
    rank1 = jnp.sum(jnp.where(oh1, before, 0.0), axis=1, keepdims=True)
    rank2 = jnp.sum(jnp.where(oh2, before, 0.0), axis=1, keepdims=True)
    carry[...] = carry[...] + cum[tm - 1:tm, :]
    cnt_ref[...] = carry[...]
    info = jnp.where(lane == 0, idx1.astype(F32), 0.0)
    info = jnp.where(lane == 1, idx2.astype(F32), info)
    info = jnp.where(lane == 2, rank1, info)
    info = jnp.where(lane == 3, rank2, info)
    info = jnp.where(lane == 4, gate1, info)
    info = jnp.where(lane == 5, gate2, info)
    info_ref[...] = info


def _out_proj(xp, xs, xs_tile0, ret_p, ret_s, s5_p, s5_s, fox_p, fox_s, w_out, g2, w_r, b_r, n_prompt_tiles):
    tm = TOKEN_TILE
    d = xp.shape[1]
    n = (n_prompt_tiles + ret_s.shape[0] // tm) * tm
    w_out_fast = w_out.astype(BF16)
    w_r_hi = w_r.astype(BF16)
    w_r_lo = (w_r - w_r_hi.astype(F32)).astype(BF16)
    idx = np.arange(tm)
    tri = jnp.asarray(idx[:, None] >= idx[None, :], BF16)

    def row(i):
        return (i, 0)

    def const(i):
        return (0, 0)

    def pmap(i):
        return (jnp.minimum(i, n_prompt_tiles - 1), 0)

    def smap(i):
        return (jnp.maximum(i - n_prompt_tiles, 0), 0)

    def xsmap(i):
        return (xs_tile0 + jnp.maximum(i - n_prompt_tiles, 0), 0)

    return pl.pallas_call(
        functools.partial(_out_proj_kernel, n_prompt_tiles),
        grid=(n // tm,),
        in_specs=[
            pl.BlockSpec((tm, d), pmap),
            pl.BlockSpec((tm, d), xsmap),
            pl.BlockSpec((tm, RET_W), pmap),
            pl.BlockSpec((tm, RET_W), smap),
            pl.BlockSpec((tm, S5_W), pmap),
            pl.BlockSpec((tm, S5_W), smap),
            pl.BlockSpec((tm, FOX_W), pmap),
            pl.BlockSpec((tm, FOX_W), smap),
            pl.BlockSpec((d, d), const),
            pl.BlockSpec((d, d), const),
            pl.BlockSpec((1, d), const),
            pl.BlockSpec((d, LANES), const),
            pl.BlockSpec((d, LANES), const),
            pl.BlockSpec((1, LANES), const),
            pl.BlockSpec((tm, tm), const),
        ],
        out_specs=(
            pl.BlockSpec((tm, d), row),
            pl.BlockSpec((tm, d), row),
            pl.BlockSpec((tm, LANES), row),
            pl.BlockSpec((1, LANES), const),
        ),
        out_shape=(
            jax.ShapeDtypeStruct((n, d), F32),
            jax.ShapeDtypeStruct((n, d), F32),
            jax.ShapeDtypeStruct((n, LANES), F32),
            jax.ShapeDtypeStruct((1, LANES), F32),
        ),
        scratch_shapes=[pltpu.VMEM((1, LANES), F32)],
        compiler_params=_params("arbitrary"),
        name="out_proj_router",
    )(xp, xs, ret_p, ret_s, s5_p, s5_s, fox_p, fox_s, w_out_fast, w_out, g2, w_r_hi, w_r_lo, b_r, tri)


_ROW_DMA_UNROLL = 8


def _dispatch_kernel(dest_ref, h_ref, rows_in_ref, rows_ref, sem):
    del rows_in_ref
    i = pl.program_id(0)
    tm = h_ref.shape[0]

    def issue(r, _):
        t = i * tm + r
        for k in range(2):
            pltpu.make_async_copy(h_ref.at[pl.ds(r, 1)], rows_ref.at[pl.ds(dest_ref[2 * t + k], 1)], sem).start()
        return 0

    lax.fori_loop(0, tm, issue, 0, unroll=_ROW_DMA_UNROLL)
    for _ in range(2):
        pltpu.make_async_copy(h_ref, rows_ref.at[pl.ds(0, tm)], sem).wait()


def _dispatch(dest, h2, n_rows):
    n, d = h2.shape
    tm = TOKEN_TILE
    grid_spec = pltpu.PrefetchScalarGridSpec(
        num_scalar_prefetch=1,
        grid=(n // tm,),
        in_specs=[
            pl.BlockSpec((tm, d), lambda i, dest: (i, 0)),
            pl.BlockSpec(memory_space=pl.ANY),
        ],
        out_specs=pl.BlockSpec(memory_space=pl.ANY),
        scratch_shapes=[pltpu.SemaphoreType.DMA(())],
    )
    return pl.pallas_call(
        _dispatch_kernel,
        grid_spec=grid_spec,
        out_shape=jax.ShapeDtypeStruct((n_rows, d), F32),
        input_output_aliases={2: 0},
        compiler_params=_params("arbitrary"),
        name="moe_dispatch",
    )(dest, h2, jnp.zeros((n_rows, d), F32))


def _experts_kernel(be_ref, nu_ref, x_ref, w1_ref, w3_ref, w2_ref, y_ref, w1b, w3b, w2b):
    i = pl.program_id(0)
    prev = be_ref[jnp.maximum(i - 1, 0)]
    fresh = (i == 0) | (be_ref[i] != prev)
    active = i < nu_ref[0]

    @pl.when(active & fresh)
    def _():
        w1b[...] = w1_ref[0, 0].astype(BF16)
        w3b[...] = w3_ref[0, 0].astype(BF16)
        w2b[...] = w2_ref[0, 0].astype(BF16)

    @pl.when(active)
    def _():
        xb = x_ref[...].astype(BF16)
        h1 = _dot(xb, w1b[...])
        h3 = _dot(xb, w3b[...])
        a = (h1 * _sigmoid(h1) * h3).astype(BF16)
        y_ref[...] = _dot(a, w2b[...])

    @pl.when(jnp.logical_not(active))
    def _():
        y_ref[...] = jnp.zeros_like(y_ref)


def _experts(layer, blk_e, n_used, x_rows, w1, w3, w2):
    n_rows, d = x_rows.shape
    de = w1.shape[3]
    nblk = n_rows // MOE_ROWS

    def last_used(i, nu):
        return jnp.minimum(i, jnp.maximum(nu[0] - 1, 0))

    def xmap(i, be, nu):
        return (last_used(i, nu), 0)

    def wmap(i, be, nu):
        return (layer, be[last_used(i, nu)], 0, 0)

    grid_spec = pltpu.PrefetchScalarGridSpec(
        num_scalar_prefetch=2,
        grid=(nblk,),
        in_specs=[
            pl.BlockSpec((MOE_ROWS, d), xmap),
            pl.BlockSpec((1, 1, d, de), wmap),
            pl.BlockSpec((1, 1, d, de), wmap),
            pl.BlockSpec((1, 1, de, d), wmap),
        ],
        out_specs=pl.BlockSpec((MOE_ROWS, d), lambda i, be, nu: (i, 0)),
        scratch_shapes=[
            pltpu.VMEM((d, de), BF16),
            pltpu.VMEM((d, de), BF16),
            pltpu.VMEM((de, d), BF16),
        ],
    )
    return pl.pallas_call(
        _experts_kernel,
        grid_spec=grid_spec,
        out_shape=jax.ShapeDtypeStruct((n_rows, d), F32),
        compiler_params=_params("arbitrary"),
        name="moe_experts",
    )(blk_e, n_used, x_rows, w1, w3, w2)


def _combine_kernel(n_prompt_tiles, dest_ref, x1_ref, info_ref, gf_ref, y_ref, o_ref, *rest):
    os_ref = rest[0] if n_prompt_tiles is not None else None
    buf, sem = rest[-2:]
    i = pl.program_id(0)
    tm = x1_ref.shape[0]

    def issue(r, _):
        t = i * tm + r
        for k in range(2):
            pltpu.make_async_copy(y_ref.at[pl.ds(dest_ref[2 * t + k], 1)], buf.at[k, pl.ds(r, 1)], sem).start()
        return 0

    lax.fori_loop(0, tm, issue, 0, unroll=_ROW_DMA_UNROLL)
    for k in range(2):
        pltpu.make_async_copy(y_ref.at[pl.ds(0, tm)], buf.at[k], sem).wait()
    info = info_ref[...]
    x2 = x1_ref[...] + info[:, 4:5] * buf[0] + info[:, 5:6] * buf[1]
    if n_prompt_tiles is None:
        o_ref[...] = x2
    else:
        ms = jnp.mean(x2 * x2, axis=-1, keepdims=True)
        y = x2 * lax.rsqrt(ms + RMS_EPS) * gf_ref[...]

        @pl.when(i < n_prompt_tiles)
        def _():
            o_ref[...] = y

        @pl.when(i >= n_prompt_tiles)
        def _():
            os_ref[...] = y


def _combine(dest, x1, info, gf, y_rows, n_prompt_tiles):
    n, d = x1.shape
    tm = TOKEN_TILE
    if n_prompt_tiles is None:
        out_specs = pl.BlockSpec((tm, d), lambda i, dest: (i, 0))
        out_shape = jax.ShapeDtypeStruct((n, d), F32)
    else:
        npt = n_prompt_tiles
        out_specs = (pl.BlockSpec((tm, d), lambda i, dest: (jnp.minimum(i, npt - 1), 0)),
                     pl.BlockSpec((tm, d), lambda i, dest: (jnp.maximum(i - npt, 0), 0)))
        out_shape = (jax.ShapeDtypeStruct((npt * tm, d), F32), jax.ShapeDtypeStruct((n - npt * tm, d), F32))
    grid_spec = pltpu.PrefetchScalarGridSpec(
        num_scalar_prefetch=1,
        grid=(n // tm,),
        in_specs=[
            pl.BlockSpec((tm, d), lambda i, dest: (i, 0)),
            pl.BlockSpec((tm, LANES), lambda i, dest: (i, 0)),
            pl.BlockSpec((1, d), lambda i, dest: (0, 0)),
            pl.BlockSpec(memory_space=pl.ANY),
        ],
        out_specs=out_specs,
        scratch_shapes=[
            pltpu.VMEM((2, tm, d), F32),
            pltpu.SemaphoreType.DMA(()),
        ],
    )
    return pl.pallas_call(
        functools.partial(_combine_kernel, n_prompt_tiles),
        grid_spec=grid_spec,
        out_shape=out_shape,
        compiler_params=_params("arbitrary"),
        name="moe_combine",
    )(dest, x1, info, gf, y_rows)


def _moe_plan(info, counts, n_tokens):
    eid = info[:, 0:2].astype(jnp.int32)
    rank = info[:, 2:4].astype(jnp.int32)
    cnt = counts[0, :N_EXPERTS].astype(jnp.int32)
    padded = (cnt + MOE_ROWS - 1) // MOE_ROWS * MOE_ROWS
    pend = jnp.cumsum(padded)
    pstart = pend - padded
    experts = jnp.arange(N_EXPERTS, dtype=jnp.int32)
    start_of = jnp.sum(jnp.where(eid[..., None] == experts, pstart, 0), axis=-1)
    dest = (start_of + rank).reshape(-1)
    nblk = -(-n_tokens * 2 // MOE_ROWS) + N_EXPERTS
    blk_start = jnp.arange(nblk, dtype=jnp.int32) * MOE_ROWS
    blk_e = jnp.minimum(jnp.sum((pend[None, :] <= blk_start[:, None]).astype(jnp.int32), axis=1), N_EXPERTS - 1)
    n_used = (pend[-1:] // MOE_ROWS).astype(jnp.int32)
    return dest.astype(jnp.int32), blk_e.astype(jnp.int32), n_used, nblk * MOE_ROWS


def _rope_tables(seq, past, tnew, tile):
    half = HEAD_DIM // 2
    inv = ROPE_THETA ** (-jnp.arange(half, dtype=F32) / half)
    pos = jnp.concatenate([jnp.arange(seq, dtype=F32), past + (jnp.arange(tile) % tnew).astype(F32)])
    ang = pos[:, None] * inv[None, :]
    cos = jnp.cos(ang)
    sin = jnp.sin(ang)
    cos_h = jnp.concatenate([cos, cos], axis=1)
    sin_h = jnp.concatenate([-sin, sin], axis=1)
    return jnp.tile(cos_h, (1, H_RET)), jnp.tile(sin_h, (1, H_RET))


def _block_diag_states(s):
    b = s.shape[0]
    eye = jnp.eye(H_RET, dtype=s.dtype)
    return jnp.einsum('bhde,hg->bhdge', s, eye).reshape(b, RET_W, RET_W)


def _diag_blocks(s_bd):
    b = s_bd.shape[0]
    s5 = s_bd.reshape(b, H_RET, HEAD_DIM, H_RET, HEAD_DIM)
    return jnp.stack([s5[:, h, :, h, :] for h in range(H_RET)], axis=1)


def kernel(x_prompt, x_sample, cache_k, cache_v, cache_logf, state_ret, state_s5, page_table, norm1, w_in, ret_gn,
           s5_lam_re, s5_lam_im, s5_b_re, s5_b_im, s5_c_re, s5_c_im, s5_d, s5_log_dt, s5_glu_w, s5_glu_b, fox_fb,
           w_out, norm2, w_rg, b_rg, w_re, b_re, w1, w3, w2, norm_f):
    bp, seq, d = x_prompt.shape
    bd_, tnew, _ = x_sample.shape
    depth = w_in.shape[0]
    n_phys, page = cache_k.shape[1], cache_k.shape[2]
    npages = page_table.shape[1]
    past = npages * page
    n_p = bp * seq
    n_s = bd_ * tnew
    n = n_p + n_s
    tm = TOKEN_TILE
    assert n_p % tm == 0 and n_s == tm and seq % tm == 0
    n_prompt_tiles = n_p // tm

    xp, xs, xs_tile0 = x_prompt.reshape(n_p, d), x_sample.reshape(n_s, d), 0
    cos_t, sin_t = _rope_tables(seq, past, tnew, tm)
    cache_kt = jnp.transpose(cache_k, (0, 1, 3, 4, 2)).reshape(depth, n_phys, FOX_W, page)
    cache_vt = jnp.transpose(cache_v, (0, 1, 3, 4, 2)).reshape(depth, n_phys, FOX_W, page)
    cache_lft = jnp.swapaxes(cache_logf, 2, 3)
    ff0 = w_in.shape[2] - H_FOX

    outs = {k: [] for k in ('kp', 'vp', 'lp', 'ks', 'vs', 'ls', 'rp', 'rs', 'sp', 'ss')}
    for l in range(depth):
        w_main = w_in[l, :, :ff0]
        w_ff = jnp.pad(w_in[l, :, ff0:], ((0, 0), (0, LANES - H_FOX)))
        fb = jnp.pad(fox_fb[l], (0, LANES - H_FOX))[None, :]
        g1 = norm1[l][None, :]
        ret_in, su, fq, fk, fv, fkb, fvb, logf = _in_proj(
            _FAST, xp, g1, w_main.astype(BF16), w_ff.astype(BF16), fb, cos_t, sin_t,
            0, n_prompt_tiles, 0, seq // tm)
        ret_in_s, su_s, fq_s, fk_s, fv_s, _, _, logf_s = _in_proj(
            _EXACT, xs, g1, w_main, w_ff, fb, cos_t, sin_t, xs_tile0, 1, seq // tm, 1)

        gw = ret_gn[l][None, :]
        zero_state = jnp.zeros((bp, RET_W, RET_W), F32)
        ret_p, rs_p = _retention(_FAST, ret_in, zero_state, gw, bp, 0, seq, 1, min(RET_CHUNK, seq),
                                 min(1024, seq))
        seq_per_chunk = 8
        chunk_rows = seq_per_chunk * tnew
        ret_s, rs_s = _retention(_EXACT, ret_in_s, _block_diag_states(state_ret[l]), gw, bd_ // seq_per_chunk, 0,
                                 chunk_rows, seq_per_chunk, tnew, chunk_rows)

        tables = _s5_tables(s5_lam_re[l], s5_lam_im[l], s5_b_re[l], s5_b_im[l], s5_c_re[l], s5_c_im[l],
                            s5_log_dt[l])
        dsk = s5_d[l][None, :]
        glub = s5_glu_b[l][None, :]
        zs = jnp.zeros((bp, 1, S5_LANES), F32)
        s5_p, sp_r, sp_i = _s5(_FAST, su, zs, zs, tables, dsk, s5_glu_w[l], glub, bp, 0, seq, min(512, seq),
                               False)
        x0 = state_s5[l].reshape(bd_, 1, S5_LANES, 2)
        s5_s, ss_r, ss_i = _s5(_EXACT, su_s, x0[..., 0], x0[..., 1], tables, dsk, s5_glu_w[l], glub, 1, 0, n_s,
                               n_s, True)

        qa, ka, va = _fox_prep(logf, fq, fkb, fvb, bp, seq, min(256, seq))
        fox_p = _fox_prompt(qa, ka, va, bp, seq)
        q_s = fq_s.reshape(bd_, tnew, 1, FOX_W)
        hmask = jnp.asarray(np.arange(H_FOX)[:, None] == (np.arange(FOX_W) // HEAD_DIM)[None, :])[None, None]
        qm = jnp.where(hmask, q_s, jnp.zeros((), BF16)).reshape(bd_, tnew * H_FOX, FOX_W)
        pad_rows = ((0, 0), (0, page - tnew), (0, 0))
        knew = jnp.pad(fk_s.reshape(bd_, tnew, FOX_W), pad_rows)
        vnew = jnp.pad(fv_s.reshape(bd_, tnew, FOX_W), pad_rows)
        lfnew = jnp.pad(jnp.swapaxes(logf_s[:, :H_FOX].reshape(bd_, tnew, H_FOX), 1, 2),
                        ((0, 0), (0, 0), (0, page - tnew)))
        fox_s = _fox_sample(l, page_table, qm, cache_kt, cache_vt, cache_lft, knew, vnew, lfnew)
        fox_s = fox_s.reshape(n_s, FOX_W)

        w_r = jnp.pad(jnp.concatenate([w_re[l], w_rg[l]], axis=1),
                      ((0, 0), (0, LANES - N_EXPERTS - N_EXPERT_GROUPS)))
        b_r = jnp.pad(jnp.concatenate([b_re[l], b_rg[l]]), (0, LANES - N_EXPERTS - N_EXPERT_GROUPS))[None, :]
        x1, h2, info, counts = _out_proj(xp, xs, xs_tile0, ret_p, ret_s, s5_p, s5_s, fox_p, fox_s, w_out[l],
                                         norm2[l][None, :], w_r, b_r, n_prompt_tiles)
        dest, blk_e, n_used, n_rows = _moe_plan(info, counts, n)
        x_rows = _dispatch(dest, h2, n_rows)
        y_rows = _experts(l, blk_e, n_used, x_rows, w1, w3, w2)
        if l < depth - 1:
            x_all = _combine(dest, x1, info, norm_f[None, :], y_rows, None)
            xp, xs, xs_tile0 = x_all, x_all, n_prompt_tiles
        else:
            y_p, y_s = _combine(dest, x1, info, norm_f[None, :], y_rows, n_prompt_tiles)

        outs['kp'].append(fk)
        outs['vp'].append(fv)
        outs['lp'].append(logf[:, :H_FOX].reshape(bp, seq, H_FOX))
        outs['ks'].append(fk_s.reshape(bd_, tnew, H_FOX, HEAD_DIM))
        outs['vs'].append(fv_s.reshape(bd_, tnew, H_FOX, HEAD_DIM))
        outs['ls'].append(logf_s[:, :H_FOX].reshape(bd_, tnew, H_FOX))
        outs['rp'].append(_diag_blocks(rs_p))
        outs['rs'].append(_diag_blocks(rs_s))
        outs['sp'].append(jnp.stack([sp_r.reshape(bp, S5_GROUPS, S5_STATE),
                                     sp_i.reshape(bp, S5_GROUPS, S5_STATE)], axis=-1))
        outs['ss'].append(jnp.stack([ss_r.reshape(bd_, S5_GROUPS, S5_STATE),
                                     ss_i.reshape(bd_, S5_GROUPS, S5_STATE)], axis=-1))

    y_prompt = y_p.reshape(bp, seq, d)
    y_sample = y_s.reshape(bd_, tnew, d)
    st = {k: jnp.stack(v) for k, v in outs.items()}
    for name in ('kp', 'vp'):
        st[name] = st[name].reshape(depth, bp, seq, H_FOX, HEAD_DIM)
    return (y_prompt, y_sample, st['kp'], st['vp'], st['lp'], st['ks'], st['vs'], st['ls'],
            st['rp'], st['rs'], st['sp'], st['ss'])
```

```python
import functools
import math

import numpy as np

import jax
import jax.numpy as jnp
from jax import lax
from jax.experimental import pallas as pl
from jax.experimental.pallas import tpu as pltpu

F32 = jnp.float32
BF16 = jnp.bfloat16
HIGHEST = lax.Precision.HIGHEST

HEAD_DIM = 64
H_RET = 4
H_FOX = 8
RET_W = H_RET * HEAD_DIM
S5_W = 256
FOX_W = H_FOX * HEAD_DIM
S5_GROUPS = 16
S5_STATE = 64
S5_LANES = S5_GROUPS * S5_STATE
N_EXPERTS = 32
N_EXPERT_GROUPS = 4
EXPERTS_PER_GROUP = 8
RET_CHUNK = 128
ROPE_THETA = 10000.0
RMS_EPS = 1e-6
NEG_BIG = -1e30
LOG2E = math.log2(math.e)

LANES = 128
SUBLANES = 8
TOKEN_TILE = 256
MOE_ROWS = 256
FOX_TQ = 1024
PAGES_PER_STEP = 32
VMEM_LIMIT = 48 * 1024 * 1024
VMEM_LIMIT_PAGED = 56 * 1024 * 1024


def _params(*sem, vmem=VMEM_LIMIT):
    return pltpu.CompilerParams(dimension_semantics=sem, vmem_limit_bytes=vmem)


def _sigmoid(x):
    return 1.0 / (1.0 + jnp.exp(-x))


def _dot(a, b):
    return jnp.dot(a, b, preferred_element_type=F32)


def _dot_nt(a, b):
    return lax.dot_general(a, b, (((1,), (1,)), ((), ())), preferred_element_type=F32)


class _MatmulMode:
    def __init__(self, exact):
        self.dtype = F32 if exact else BF16
        self.precision = HIGHEST if exact else None

    def cast(self, x):
        return x.astype(self.dtype)

    def dot(self, a, b):
        return jnp.dot(a, b, precision=self.precision, preferred_element_type=F32)

    def dot_nt(self, a, b):
        return lax.dot_general(a, b, (((1,), (1,)), ((), ())), precision=self.precision,
                               preferred_element_type=F32)

    def dot_tn(self, a, b):
        return lax.dot_general(a, b, (((0,), (0,)), ((), ())), precision=self.precision,
                               preferred_element_type=F32)


_FAST = _MatmulMode(False)
_EXACT = _MatmulMode(True)


def _in_proj_kernel(mm, x_ref, g_ref, w_ref, wff_ref, fb_ref, cos_ref, sin_ref,
                    ret_ref, su_ref, fq_ref, fk_ref, fv_ref, fkb_ref, fvb_ref, logf_ref):
    _dot = mm.dot
    x = x_ref[...]
    ms = jnp.mean(x * x, axis=-1, keepdims=True)
    h = mm.cast(x * lax.rsqrt(ms + RMS_EPS) * g_ref[...])
    tm = x.shape[0]
    cos = cos_ref[...]
    sin = sin_ref[...]
    lane = lax.broadcasted_iota(jnp.int32, (tm, RET_W), 1)
    first_half = (lane % HEAD_DIM) < (HEAD_DIM // 2)

    def rope(t):
        swapped = jnp.where(first_half,
                            pltpu.roll(t, RET_W - HEAD_DIM // 2, axis=1),
                            pltpu.roll(t, HEAD_DIM // 2, axis=1))
        return t * cos + swapped * sin

    ret_ref[:, 0:RET_W] = rope(_dot(h, w_ref[:, 0:RET_W]))
    ret_ref[:, RET_W:2 * RET_W] = rope(_dot(h, w_ref[:, RET_W:2 * RET_W])) * (HEAD_DIM ** -0.5)
    ret_ref[:, 2 * RET_W:4 * RET_W] = _dot(h, w_ref[:, 2 * RET_W:4 * RET_W])
    c0 = 4 * RET_W
    su_ref[...] = _dot(h, w_ref[:, c0:c0 + S5_W])
    c0 += S5_W
    fq_ref[...] = (_dot(h, w_ref[:, c0:c0 + FOX_W]) * (HEAD_DIM ** -0.5 * LOG2E)).astype(BF16)
    c0 += FOX_W
    fk = _dot(h, w_ref[:, c0:c0 + FOX_W])
    fk_ref[...] = fk
    fkb_ref[...] = fk.astype(BF16)
    c0 += FOX_W
    fv = _dot(h, w_ref[:, c0:c0 + FOX_W])
    fv_ref[...] = fv
    fvb_ref[...] = fv.astype(BF16)
    z = _dot(h, wff_ref[...]) + fb_ref[...]
    logf = jnp.minimum(z, 0.0) - jnp.log1p(jnp.exp(-jnp.abs(z)))
    lane128 = lax.broadcasted_iota(jnp.int32, (tm, LANES), 1)
    logf_ref[...] = jnp.where(lane128 < H_FOX, logf, 0.0)


def _in_proj(mm, x, g, w_main, w_ff, fb, cos_t, sin_t, tile0, ntiles, pos_tile0, pos_tiles):
    tm = TOKEN_TILE
    n = ntiles * tm
    wcols = w_main.shape[1]

    def xrow(i):
        return (tile0 + i, 0)

    def row(i):
        return (i, 0)

    def const(i):
        return (0, 0)

    def pos_map(i):
        return (pos_tile0 + i % pos_tiles, 0)

    outs = (
        jax.ShapeDtypeStruct((n, 4 * RET_W), F32),
        jax.ShapeDtypeStruct((n, S5_W), F32),
        jax.ShapeDtypeStruct((n, FOX_W), BF16),
        jax.ShapeDtypeStruct((n, FOX_W), F32),
        jax.ShapeDtypeStruct((n, FOX_W), F32),
        jax.ShapeDtypeStruct((n, FOX_W), BF16),
        jax.ShapeDtypeStruct((n, FOX_W), BF16),
        jax.ShapeDtypeStruct((n, LANES), F32),
    )
    return pl.pallas_call(
        functools.partial(_in_proj_kernel, mm),
        grid=(ntiles,),
        in_specs=[
            pl.BlockSpec((tm, x.shape[1]), xrow),
            pl.BlockSpec((1, x.shape[1]), const),
            pl.BlockSpec((x.shape[1], wcols), const),
            pl.BlockSpec((x.shape[1], LANES), const),
            pl.BlockSpec((1, LANES), const),
            pl.BlockSpec((tm, RET_W), pos_map),
            pl.BlockSpec((tm, RET_W), pos_map),
        ],
        out_specs=(
            pl.BlockSpec((tm, 4 * RET_W), row),
            pl.BlockSpec((tm, S5_W), row),
            pl.BlockSpec((tm, FOX_W), row),
            pl.BlockSpec((tm, FOX_W), row),
            pl.BlockSpec((tm, FOX_W), row),
            pl.BlockSpec((tm, FOX_W), row),
            pl.BlockSpec((tm, FOX_W), row),
            pl.BlockSpec((tm, LANES), row),
        ),
        out_shape=outs,
        compiler_params=_params("arbitrary"),
        name="in_proj",
    )(x, g, w_main, w_ff, fb, cos_t, sin_t)


def _retention_kernel(mm, nseq, seq_len, nchunk, x_ref, s0_ref, dm_ref, qd_ref, kd_ref, cd_ref, bd_ref,
                      gn_ref, gw_ref, o_ref, sout_ref, s_scr):
    cast, _dot, _dot_nt, _dot_tn = mm.cast, mm.dot, mm.dot_nt, mm.dot_tn
    step = pl.program_id(1)
    rows = nseq * seq_len

    @pl.when(step == 0)
    def _():
        s_scr[...] = s0_ref[...]

    lane_head = lax.broadcasted_iota(jnp.int32, (rows, RET_W), 1) // HEAD_DIM
    row_seq = lax.broadcasted_iota(jnp.int32, (rows, 1), 0) // seq_len

    def chunk(c, _):
        r0 = pl.multiple_of(c * rows, rows)
        q = x_ref[pl.ds(r0, rows), 0:RET_W]
        k = x_ref[pl.ds(r0, rows), RET_W:2 * RET_W]
        v = x_ref[pl.ds(r0, rows), 2 * RET_W:3 * RET_W]
        g = x_ref[pl.ds(r0, rows), 3 * RET_W:4 * RET_W]
        qb = cast(q)
        kb = cast(k)
        vb = cast(v)
        ps = []
        vs = []
        for h in range(H_RET):
            hm = lane_head == h
            sc = _dot_nt(cast(jnp.where(hm, q, 0.0)), kb)
            ps.append(cast(sc * dm_ref[h]))
            vs.append(cast(jnp.where(hm, v, 0.0)))
        inner = _dot(jnp.concatenate(ps, axis=1), jnp.concatenate(vs, axis=0))
        kdec = k * kd_ref[...]
        cd = cd_ref[...]
        bd = bd_ref[...]
        if nseq == 1:
            s_old = s_scr[0]
            cross = _dot(qb, cast(s_old))
            s_scr[0] = s_old * cd + _dot_tn(cast(kdec), vb) * bd
        else:
            def seq_body(s, cross):
                rm = row_seq == s
                s_old = s_scr[s]
                cross = jnp.where(rm, _dot(qb, cast(s_old)), cross)
                kv = _dot_tn(cast(jnp.where(rm, kdec, 0.0)), vb)
                s_scr[s] = s_old * cd + kv * bd
                return cross

            cross = lax.fori_loop(0, nseq, seq_body, jnp.zeros((rows, RET_W), F32))
        o = inner + cross * qd_ref[...]
        o2 = o * o
        hi = o2.astype(BF16)
        lo = (o2 - hi.astype(F32)).astype(BF16)
        ms = jnp.dot(hi, gn_ref[...], preferred_element_type=F32) + jnp.dot(
            lo, gn_ref[...], preferred_element_type=F32)
        on = o * lax.rsqrt(ms + RMS_EPS)
        o_ref[pl.ds(r0, rows), :] = (g * _sigmoid(g) * on * gw_ref[...]).astype(o_ref.dtype)
        return 0

    lax.fori_loop(0, nchunk, chunk, 0)

    @pl.when(step == pl.num_programs(1) - 1)
    def _():
        sout_ref[...] = s_scr[...]


def _retention_tables(nseq, seq_len):
    f32 = np.float32
    rows = nseq * seq_len
    log_gamma = np.log1p(-np.power(f32(2.0), -5.0 - np.arange(H_RET, dtype=f32))).astype(f32)
    idx = np.arange(rows)
    pos = (idx % seq_len).astype(f32)
    seq = idx // seq_len
    diff = pos[:, None] - pos[None, :]
    ok = (seq[:, None] == seq[None, :]) & (diff >= 0)
    dm = np.where(ok[None], np.exp(log_gamma[:, None, None] * np.maximum(diff, f32(0.0))[None]), f32(0.0))
    lg_lane = np.repeat(log_gamma, HEAD_DIM)
    qd = np.exp((pos[:, None] + f32(1.0)) * lg_lane[None, :])
    kd = np.exp((f32(seq_len - 1.0) - pos[:, None]) * lg_lane[None, :])
    cd = np.exp(f32(seq_len) * lg_lane)[None, :]
    head = np.arange(RET_W) // HEAD_DIM
    bd = (head[:, None] == head[None, :]).astype(f32)
    to = lambda a: jnp.asarray(a, F32)
    return to(dm), to(qd), to(kd), to(cd), to(bd), jnp.asarray(bd / HEAD_DIM, BF16)


def _retention(mm, ret_in, s0_bd, gw, nbatch, row0, seq_total, nseq, seq_len, rows_per_step):
    rows = nseq * seq_len
    nchunk = rows_per_step // rows
    steps = seq_total // rows_per_step
    blk0 = row0 // rows_per_step
    dm, qd, kd, cd, bd, gn = _retention_tables(nseq, seq_len)

    def rmap(b, i):
        return (blk0 + b * steps + i, 0)

    def omap(b, i):
        return (b * steps + i, 0)

    def c2(b, i):
        return (0, 0)

    def c3(b, i):
        return (0, 0, 0)

    def smap(b, i):
        return (b, 0, 0)

    in_specs = [
        pl.BlockSpec((rows_per_step, 4 * RET_W), rmap),
        pl.BlockSpec((nseq, RET_W, RET_W), smap),
        pl.BlockSpec((H_RET, rows, rows), c3),
        pl.BlockSpec((rows, RET_W), c2),
        pl.BlockSpec((rows, RET_W), c2),
        pl.BlockSpec((1, RET_W), c2),
        pl.BlockSpec((RET_W, RET_W), c2),
        pl.BlockSpec((RET_W, RET_W), c2),
        pl.BlockSpec((1, RET_W), c2),
    ]
    return pl.pallas_call(
        functools.partial(_retention_kernel, mm, nseq, seq_len, nchunk),
        grid=(nbatch, steps),
        in_specs=in_specs,
        out_specs=(
            pl.BlockSpec((rows_per_step, RET_W), omap),
            pl.BlockSpec((nseq, RET_W, RET_W), smap),
        ),
        out_shape=(
            jax.ShapeDtypeStruct((nbatch * seq_total, RET_W), mm.dtype),
            jax.ShapeDtypeStruct((nbatch * nseq, RET_W, RET_W), F32),
        ),
        scratch_shapes=[pltpu.VMEM((nseq, RET_W, RET_W), F32)],
        compiler_params=_params("arbitrary", "arbitrary"),
        name="retention",
    )(ret_in, s0_bd, dm, qd, kd, cd, bd, gn, gw)


def _s5_kernel(mm, short, u_ref, x0r_ref, x0i_ref, bre_ref, bim_ref, pre_ref, pim_ref, cm_ref, d_ref,
               gw_ref, gb_ref, o_ref, str_ref, sti_ref, bur, bui, car, cai):
    cast, _dot = mm.cast, mm.dot
    step = pl.program_id(1)
    rows = u_ref.shape[0]
    u = u_ref[...]
    ub = cast(u)
    bur[...] = _dot(ub, bre_ref[...])
    bui[...] = _dot(ub, bim_ref[...])

    if not short:
        @pl.when(step == 0)
        def _():
            car[...] = x0r_ref[0]
            cai[...] = x0i_ref[0]

    pr = pre_ref[...]
    pi = pim_ref[...]
    rowi = lax.broadcasted_iota(jnp.int32, (SUBLANES, S5_LANES), 0)

    def group(j, carry):
        cr, ci = carry
        r0 = pl.multiple_of(j * SUBLANES, SUBLANES)
        xr = bur[pl.ds(r0, SUBLANES), :]
        xi = bui[pl.ds(r0, SUBLANES), :]
        for s in (1, 2, 4):
            ar = pr[s - 1:s]
            ai = pi[s - 1:s]
            sr = jnp.where(rowi >= s, pltpu.roll(xr, s, axis=0), 0.0)
            si = jnp.where(rowi >= s, pltpu.roll(xi, s, axis=0), 0.0)
            xr, xi = xr + ar * sr - ai * si, xi + ar * si + ai * sr
        if short:
            cr = x0r_ref[j]
            ci = x0i_ref[j]
        xr, xi = xr + pr * cr - pi * ci, xi + pr * ci + pi * cr
        bur[pl.ds(r0, SUBLANES), :] = xr
        bui[pl.ds(r0, SUBLANES), :] = xi
        cr = xr[SUBLANES - 1:SUBLANES]
        ci = xi[SUBLANES - 1:SUBLANES]
        if short:
            str_ref[j] = cr
            sti_ref[j] = ci
        return cr, ci

    if short:
        init = (jnp.zeros((1, S5_LANES), F32), jnp.zeros((1, S5_LANES), F32))
    else:
        init = (car[...], cai[...])
    cr, ci = lax.fori_loop(0, rows // SUBLANES, group, init)
    if not short:
        car[...] = cr
        cai[...] = ci

        @pl.when(step == pl.num_programs(1) - 1)
        def _():
            str_ref[0] = cr
            sti_ref[0] = ci

    xs = jnp.concatenate([cast(bur[...]), cast(bui[...])], axis=1)
    y = _dot(xs, cm_ref[...]) + d_ref[...] * u
    yg = 0.5 * y * (1.0 + jnp.tanh(math.sqrt(2.0 / math.pi) * (y + 0.044715 * (y * y * y))))
    z = _dot(cast(yg), gw_ref[...]) + gb_ref[...]
    o_ref[...] = (yg * _sigmoid(z)).astype(o_ref.dtype)


def _s5_tables(lam_re, lam_im, b_re, b_im, c_re, c_im, log_dt):
    lam = lax.complex(lam_re.astype(F32), lam_im.astype(F32))
    dt = jnp.exp(log_dt.astype(F32))[:, None]
    lam_bar = jnp.exp(lam * dt)
    b_bar = ((lam_bar - 1.0) / lam)[:, :, None] * lax.complex(b_re.astype(F32), b_im.astype(F32))
    k = jnp.arange(1, SUBLANES + 1, dtype=F32)[:, None, None]
    powers = jnp.exp((lam * dt)[None] * k)
    pre = jnp.real(powers).reshape(SUBLANES, S5_LANES)
    pim = jnp.imag(powers).reshape(SUBLANES, S5_LANES)
    eye = jnp.eye(S5_GROUPS, dtype=F32)
    bre = jnp.einsum('gpc,gh->gchp', jnp.real(b_bar), eye).reshape(S5_W, S5_LANES)
    bim = jnp.einsum('gpc,gh->gchp', jnp.imag(b_bar), eye).reshape(S5_W, S5_LANES)
    cre = jnp.einsum('gcp,gh->hpgc', c_re.astype(F32), eye).reshape(S5_LANES, S5_W)
    cim = jnp.einsum('gcp,gh->hpgc', c_im.astype(F32), eye).reshape(S5_LANES, S5_W)
    cm = jnp.concatenate([cre, -cim], axis=0)
    return bre, bim, pre, pim, cm


def _s5(mm, su, x0r, x0i, tables, d, glu_w, glu_b, nbatch, row0, seq_total, rows_per_step, short):
    bre, bim, pre, pim, cm = tables
    bre, bim, cm, glu_w = mm.cast(bre), mm.cast(bim), mm.cast(cm), mm.cast(glu_w)
    steps = seq_total // rows_per_step
    blk0 = row0 // rows_per_step
    nstate = x0r.shape[0]
    sblk = nstate if short else 1

    def rmap(b, i):
        return (blk0 + b * steps + i, 0)

    def omap(b, i):
        return (b * steps + i, 0)

    def c2(b, i):
        return (0, 0)

    def smap(b, i):
        return (0 if short else b, 0, 0)

    in_specs = [
        pl.BlockSpec((rows_per_step, S5_W), rmap),
        pl.BlockSpec((sblk, 1, S5_LANES), smap),
        pl.BlockSpec((sblk, 1, S5_LANES), smap),
        pl.BlockSpec((S5_W, S5_LANES), c2),
        pl.BlockSpec((S5_W, S5_LANES), c2),
        pl.BlockSpec((SUBLANES, S5_LANES), c2),
        pl.BlockSpec((SUBLANES, S5_LANES), c2),
        pl.BlockSpec((2 * S5_LANES, S5_W), c2),
        pl.BlockSpec((1, S5_W), c2),
        pl.BlockSpec((S5_W, S5_W), c2),
        pl.BlockSpec((1, S5_W), c2),
    ]
    return pl.pallas_call(
        functools.partial(_s5_kernel, mm, short),
        grid=(nbatch, steps),
        in_specs=in_specs,
        out_specs=(
            pl.BlockSpec((rows_per_step, S5_W), omap),
            pl.BlockSpec((sblk, 1, S5_LANES), smap),
            pl.BlockSpec((sblk, 1, S5_LANES), smap),
        ),
        out_shape=(
            jax.ShapeDtypeStruct((nbatch * seq_total, S5_W), mm.dtype),
            jax.ShapeDtypeStruct((nstate, 1, S5_LANES), F32),
            jax.ShapeDtypeStruct((nstate, 1, S5_LANES), F32),
        ),
        scratch_shapes=[
            pltpu.VMEM((rows_per_step, S5_LANES), F32),
            pltpu.VMEM((rows_per_step, S5_LANES), F32),
            pltpu.VMEM((1, S5_LANES), F32),
            pltpu.VMEM((1, S5_LANES), F32),
        ],
        compiler_params=_params("arbitrary", "arbitrary"),
        name="s5",
    )(su, x0r, x0i, bre, bim, pre, pim, cm, d, glu_w, glu_b)


_PAIR_W = 2 * LANES
_AUG_PER_HEAD = 6
_N_PAIRS = H_FOX // 2


def _aug_placement():
    pq = np.zeros((3 * LANES, _N_PAIRS * LANES), np.float32)
    pk = np.zeros((3 * LANES, _N_PAIRS * LANES), np.float32)
    oq = np.zeros((1, _N_PAIRS * LANES), np.float32)
    ok = np.zeros((1, _N_PAIRS * LANES), np.float32)
    for h in range(H_FOX):
        base = (h // 2) * LANES + (h % 2) * _AUG_PER_HEAD
        for part in range(3):
            pq[part * LANES + h, base + part] = 1.0
            oq[0, base + 3 + part] = 1.0
            ok[0, base + part] = 1.0
            pk[part * LANES + h, base + 3 + part] = -1.0
    return (jnp.asarray(pq, BF16), jnp.asarray(pk, BF16), jnp.asarray(oq), jnp.asarray(ok))


def _split3(v, axis=1):
    hi = v.astype(BF16)
    r1 = v - hi.astype(F32)
    mid = r1.astype(BF16)
    lo = (r1 - mid.astype(F32)).astype(BF16)
    return jnp.concatenate([hi, mid, lo], axis=axis)


def _cumsum_rows(tri_ref, x):
    c = _dot(tri_ref[...], _split3(x, axis=1))
    return c[:, 0:LANES] + c[:, LANES:2 * LANES] + c[:, 2 * LANES:3 * LANES]


def _cumsum_lanes(x, u_ref):
    r = x.shape[0]
    c = _dot(_split3(x, axis=0), u_ref[...])
    return c[0:r] + c[r:2 * r] + c[2 * r:3 * r]


def _fox_prep_kernel(lf_ref, q_ref, k_ref, v_ref, tri_ref, pq_ref, pk_ref, oq_ref, ok_ref,
                     qa_ref, ka_ref, va_ref, carry):
    @pl.when(pl.program_id(1) == 0)
    def _():
        carry[...] = jnp.zeros_like(carry)

    c = _cumsum_rows(tri_ref, lf_ref[...]) + carry[...]
    rows = c.shape[0]
    carry[...] = c[rows - 1:rows, :]
    parts = _split3(c * LOG2E)
    augq = (_dot(parts, pq_ref[...]) + oq_ref[...]).astype(BF16)
    augk = (_dot(parts, pk_ref[...]) + ok_ref[...]).astype(BF16)
    lane = lax.broadcasted_iota(jnp.int32, (rows, LANES), 1)
    one = jnp.ones((rows, LANES), BF16)
    for p in range(_N_PAIRS):
        src = slice(p * LANES, (p + 1) * LANES)
        qa_ref[:, p * _PAIR_W:p * _PAIR_W + LANES] = q_ref[:, src]
        qa_ref[:, p * _PAIR_W + LANES:(p + 1) * _PAIR_W] = augq[:, src]
        ka_ref[:, p * _PAIR_W:p * _PAIR_W + LANES] = k_ref[:, src]
        ka_ref[:, p * _PAIR_W + LANES:(p + 1) * _PAIR_W] = augk[:, src]
        vp = v_ref[:, src]
        va_ref[:, p * _PAIR_W:p * _PAIR_W + LANES] = jnp.where(lane < HEAD_DIM, vp, one)
        va_ref[:, p * _PAIR_W + LANES:(p + 1) * _PAIR_W] = jnp.where(lane < HEAD_DIM, one, vp)


def _fox_prep(logf, fq, fkb, fvb, nbatch, seq_total, rows_per_step):
    steps = seq_total // rows_per_step
    idx = np.arange(rows_per_step)
    tri = jnp.asarray(idx[:, None] >= idx[None, :], BF16)
    pq, pk, oq, ok = _aug_placement()
    wide = _N_PAIRS * _PAIR_W

    def rmap(b, i):
        return (b * steps + i, 0)

    def const(b, i):
        return (0, 0)

    out = jax.ShapeDtypeStruct((nbatch * seq_total, wide), BF16)
    return pl.pallas_call(
        _fox_prep_kernel,
        grid=(nbatch, steps),
        in_specs=[
            pl.BlockSpec((rows_per_step, LANES), rmap),
            pl.BlockSpec((rows_per_step, FOX_W), rmap),
            pl.BlockSpec((rows_per_step, FOX_W), rmap),
            pl.BlockSpec((rows_per_step, FOX_W), rmap),
            pl.BlockSpec((rows_per_step, rows_per_step), const),
            pl.BlockSpec(pq.shape, const),
            pl.BlockSpec(pk.shape, const),
            pl.BlockSpec(oq.shape, const),
            pl.BlockSpec(ok.shape, const),
        ],
        out_specs=(pl.BlockSpec((rows_per_step, wide), rmap),) * 3,
        out_shape=(out, out, out),
        scratch_shapes=[pltpu.VMEM((1, LANES), F32)],
        compiler_params=_params("arbitrary", "arbitrary"),
        name="fox_prep",
    )(logf, fq, fkb, fvb, tri, pq, pk, oq, ok)


_PAIRS_PER_STEP = 2


def _fox_prompt_kernel(tq, q_ref, k_ref, v_ref, o_ref, acc, m_scr):
    qi = pl.program_id(2)
    nhead = 2 * _PAIRS_PER_STEP
    lane = lax.broadcasted_iota(jnp.int32, (tq, _PAIR_W), 1)
    qs = []
    for h in range(nhead):
        par = h % 2
        q = q_ref[:, (h // 2) * _PAIR_W:(h // 2 + 1) * _PAIR_W]
        own = ((lane >= par * HEAD_DIM) & (lane < (par + 1) * HEAD_DIM)) | (
            (lane >= LANES + par * _AUG_PER_HEAD) & (lane < LANES + (par + 1) * _AUG_PER_HEAD))
        qs.append(jnp.where(own, q, jnp.zeros_like(q)))
        m_scr[h] = jnp.full((tq, LANES), NEG_BIG, F32)
        acc[h] = jnp.zeros((tq, LANES), F32)
    half = tq // 2

    def update(h, r0, nr, k0, nk, mask):
        kb = k_ref[pl.ds(k0, nk), (h // 2) * _PAIR_W:(h // 2 + 1) * _PAIR_W]
        s = _dot_nt(qs[h][r0:r0 + nr], kb)
        if mask is not None:
            s = jnp.where(mask, s, NEG_BIG)
        m_old = m_scr[h, r0:r0 + nr]
        m_new = jnp.maximum(m_old, jnp.max(s, axis=1, keepdims=True))
        p = jnp.exp2(s - jnp.concatenate([m_new] * (nk // LANES), axis=1))
        vb = v_ref[pl.ds(k0, nk), h * LANES:(h + 1) * LANES]
        acc[h, r0:r0 + nr] = jnp.exp2(m_old - m_new) * acc[h, r0:r0 + nr] + _dot(p.astype(BF16), vb)
        m_scr[h, r0:r0 + nr] = m_new

    def body(j, _):
        ks = pl.multiple_of(j * tq, tq)
        for h in range(nhead):
            update(h, 0, tq, ks, tq, None)
        return 0

    lax.fori_loop(0, qi, body, 0)
    ks = pl.multiple_of(qi * tq, tq)
    tri = lax.broadcasted_iota(jnp.int32, (half, half), 0) >= lax.broadcasted_iota(jnp.int32, (half, half), 1)
    low = lax.broadcasted_iota(jnp.int32, (half, tq), 0) + half >= lax.broadcasted_iota(jnp.int32, (half, tq), 1)
    for h in range(nhead):
        update(h, 0, half, ks, half, tri)
        update(h, half, half, ks, tq, low)
    lane128 = lax.broadcasted_iota(jnp.int32, (tq, LANES), 1)
    for pr in range(_PAIRS_PER_STEP):
        outs = []
        for par in range(2):
            a = acc[2 * pr + par]
            outs.append(a / pltpu.roll(a, HEAD_DIM, axis=1))
        o_ref[:, pr * LANES:(pr + 1) * LANES] = jnp.where(lane128 < HEAD_DIM, outs[0], outs[1]).astype(BF16)


def _fox_prompt(qa, ka, va, nbatch, seq_total):
    tq = min(FOX_TQ, seq_total)
    nq = seq_total // tq
    w = _PAIRS_PER_STEP * _PAIR_W
    return pl.pallas_call(
        functools.partial(_fox_prompt_kernel, tq),
        grid=(nbatch, _N_PAIRS // _PAIRS_PER_STEP, nq),
        in_specs=[
            pl.BlockSpec((tq, w), lambda b, g, i: (b * nq + i, g)),
            pl.BlockSpec((seq_total, w), lambda b, g, i: (b, g)),
            pl.BlockSpec((seq_total, w), lambda b, g, i: (b, g)),
        ],
        out_specs=pl.BlockSpec((tq, _PAIRS_PER_STEP * LANES), lambda b, g, i: (b * nq + i, g)),
        out_shape=jax.ShapeDtypeStruct((nbatch * seq_total, FOX_W), BF16),
        scratch_shapes=[
            pltpu.VMEM((2 * _PAIRS_PER_STEP, tq, LANES), F32),
            pltpu.VMEM((2 * _PAIRS_PER_STEP, tq, LANES), F32),
        ],
        compiler_params=_params("arbitrary", "arbitrary", "arbitrary"),
        name="fox_prompt",
    )(qa, ka, va)


def _fox_sample_kernel(npg, tnew, pt_ref, qm_ref, *refs):
    del pt_ref
    k_refs = refs[0:npg]
    v_refs = refs[npg:2 * npg]
    lf_refs = refs[2 * npg:3 * npg]
    knew_ref, vnew_ref, lfnew_ref, u_ref, sel_ref, o_ref, m_scr, l_scr, acc, run, kb_scr, vb_scr = refs[3 * npg:]
    g = pl.program_id(1)
    nrow = tnew * H_FOX
    npage = knew_ref.shape[1]

    @pl.when(g == 0)
    def _():
        m_scr[...] = jnp.full(m_scr.shape, NEG_BIG, F32)
        l_scr[...] = jnp.zeros(l_scr.shape, F32)
        acc[...] = jnp.zeros(acc.shape, F32)
        run[...] = jnp.zeros(run.shape, F32)

    qm = qm_ref[0]

    def cum_within(lft):
        return _cumsum_lanes(lft, u_ref)

    def attend(k, v, ckeys, mask, transposed):
        qk = _dot(qm, k) if transposed else _dot_nt(qm, k)
        s = qk - jnp.concatenate([ckeys * LOG2E] * tnew, axis=0)
        if mask is not None:
            s = jnp.where(mask, s, NEG_BIG)
        m_old = m_scr[...]
        m_new = jnp.maximum(m_old, jnp.max(s, axis=1, keepdims=True))
        p = jnp.exp2(s - m_new).astype(BF16)
        alpha = jnp.exp2(m_old - m_new)
        l_scr[...] = alpha * l_scr[...] + jnp.sum(p.astype(F32), axis=1, keepdims=True)
        acc[...] = alpha * acc[...] + (_dot_nt(p, v) if transposed else _dot(p, v))
        m_scr[...] = m_new

    base = run[...]
    cs = []
    for j in range(npg):
        kb_scr[:, j * npage:(j + 1) * npage] = k_refs[j][0, 0].astype(BF16)
        vb_scr[:, j * npage:(j + 1) * npage] = v_refs[j][0, 0].astype(BF16)
        cw = cum_within(lf_refs[j][0, 0])
        cs.append(base + cw)
        base = base + jnp.broadcast_to(cw[:, npage - 1:npage], base.shape)
    run[...] = base
    attend(kb_scr[...], vb_scr[...], jnp.concatenate(cs, axis=1), None, True)

    @pl.when(g == pl.num_programs(1) - 1)
    def _():
        rt = lax.broadcasted_iota(jnp.int32, (nrow, npage), 0) // H_FOX
        col = lax.broadcasted_iota(jnp.int32, (nrow, npage), 1)
        attend(knew_ref[0].astype(BF16), vnew_ref[0].astype(BF16), run[...] + cum_within(lfnew_ref[0]), col <= rt,
               False)
        o = acc[...] / l_scr[...]
        rh = lax.broadcasted_iota(jnp.int32, (nrow, FOX_W), 0) % H_FOX
        lh = lax.broadcasted_iota(jnp.int32, (nrow, FOX_W), 1) // HEAD_DIM
        om = jnp.where(rh == lh, o, 0.0)
        o_ref[0] = jnp.dot(sel_ref[...], om, precision=HIGHEST, preferred_element_type=F32)


def _fox_sample(layer, page_table, qm, cache_kt, cache_vt, cache_lft, knew, vnew, lfnew):
    nb, npages = page_table.shape
    page = cache_kt.shape[3]
    tnew = qm.shape[1] // H_FOX
    npg = PAGES_PER_STEP
    ngrp = npages // npg
    idx = np.arange(page)
    u = jnp.asarray(idx[:, None] <= idx[None, :], BF16)
    sel = jnp.asarray(np.arange(tnew)[:, None] == (np.arange(tnew * H_FOX)[None, :] // H_FOX), F32)

    def pmap(j):
        def f(b, g, pt):
            return (layer, pt[b * npages + g * npg + j], 0, 0)
        return f

    def bmap(b, g, pt):
        return (b, 0, 0)

    def c2(b, g, pt):
        return (0, 0)

    in_specs = [pl.BlockSpec((1, tnew * H_FOX, FOX_W), bmap)]
    in_specs += [pl.BlockSpec((1, 1, FOX_W, page), pmap(j)) for j in range(npg)]
    in_specs += [pl.BlockSpec((1, 1, FOX_W, page), pmap(j)) for j in range(npg)]
    in_specs += [pl.BlockSpec((1, 1, H_FOX, page), pmap(j)) for j in range(npg)]
    in_specs += [
        pl.BlockSpec((1, page, FOX_W), bmap),
        pl.BlockSpec((1, page, FOX_W), bmap),
        pl.BlockSpec((1, H_FOX, page), bmap),
        pl.BlockSpec((page, page), c2),
        pl.BlockSpec((tnew, tnew * H_FOX), c2),
    ]
    grid_spec = pltpu.PrefetchScalarGridSpec(
        num_scalar_prefetch=1,
        grid=(nb, ngrp),
        in_specs=in_specs,
        out_specs=pl.BlockSpec((1, tnew, FOX_W), bmap),
        scratch_shapes=[
            pltpu.VMEM((tnew * H_FOX, 1), F32),
            pltpu.VMEM((tnew * H_FOX, 1), F32),
            pltpu.VMEM((tnew * H_FOX, FOX_W), F32),
            pltpu.VMEM((H_FOX, page), F32),
            pltpu.VMEM((FOX_W, npg * page), BF16),
            pltpu.VMEM((FOX_W, npg * page), BF16),
        ],
    )
    args = [page_table.reshape(-1), qm] + [cache_kt] * npg + [cache_vt] * npg + [cache_lft] * npg
    args += [knew, vnew, lfnew, u, sel]
    return pl.pallas_call(
        functools.partial(_fox_sample_kernel, npg, tnew),
        grid_spec=grid_spec,
        out_shape=jax.ShapeDtypeStruct((nb, tnew, FOX_W), F32),
        compiler_params=_params("arbitrary", "arbitrary", vmem=VMEM_LIMIT_PAGED),
        name="fox_sample",
    )(*args)


_LANE_GROUP0 = N_EXPERTS


def _pack_bf16_pairs(x):
    n = x.shape[1] // 2
    hi = lax.bitcast_convert_type(x[:, :n].astype(BF16).astype(F32), jnp.uint32)
    lo = lax.bitcast_convert_type(x[:, n:].astype(BF16).astype(F32), jnp.uint32)
    return hi | (lo >> 16)


def _unpack_bf16_pairs(u):
    hi = lax.bitcast_convert_type(u & jnp.uint32(0xFFFF0000), F32)
    lo = lax.bitcast_convert_type(u << 16, F32)
    return jnp.concatenate([hi, lo], axis=1).astype(BF16)


def _out_proj_kernel(n_prompt_tiles, xp_ref, xs_ref, retp_ref, rets_ref, s5p_ref, s5s_ref, foxp_ref, foxs_ref, wo_ref,
                     wof_ref, g_ref, wr_ref, wrl_ref, br_ref, tri_ref, x1_ref, h2_ref, info_ref, cnt_ref, carry):
    i = pl.program_id(0)

    @pl.when(i == 0)
    def _():
        carry[...] = jnp.zeros_like(carry)

    def project(mm, x_ref, ret_ref, s5_ref, fox_ref, w_ref):
        x1 = x_ref[...] + mm.dot(ret_ref[...], w_ref[0:RET_W, :])
        x1 = x1 + mm.dot(s5_ref[...], w_ref[RET_W:RET_W + S5_W, :])
        x1_ref[...] = x1 + mm.dot(fox_ref[...], w_ref[RET_W + S5_W:, :])

    @pl.when(i < n_prompt_tiles)
    def _():
        project(_FAST, xp_ref, retp_ref, s5p_ref, foxp_ref, wo_ref)

    @pl.when(i >= n_prompt_tiles)
    def _():
        project(_EXACT, xs_ref, rets_ref, s5s_ref, foxs_ref, wof_ref)

    x1 = x1_ref[...]
    ms = jnp.mean(x1 * x1, axis=-1, keepdims=True)
    h2 = x1 * lax.rsqrt(ms + RMS_EPS) * g_ref[...]
    h2_ref[...] = _pack_bf16_pairs(h2)
    h_hi = h2.astype(BF16)
    h_lo = (h2 - h_hi.astype(F32)).astype(BF16)
    logits = _dot(h_hi, wr_ref[...]) + _dot(h_hi, wrl_ref[...]) + _dot(h_lo, wr_ref[...]) + br_ref[...]
    tm = logits.shape[0]
    lane = lax.broadcasted_iota(jnp.int32, (tm, LANES), 1)
    big = jnp.int32(LANES)
    gmask = (lane >= _LANE_GROUP0) & (lane < _LANE_GROUP0 + N_EXPERT_GROUPS)
    gl = jnp.where(gmask, logits, NEG_BIG)
    gmax = jnp.max(gl, axis=1, keepdims=True)
    gsum = jnp.sum(jnp.where(gmask, jnp.exp(gl - gmax), 0.0), axis=1, keepdims=True)
    g_w = 1.0 / gsum
    g_idx = jnp.min(jnp.where(gl == gmax, lane, big), axis=1, keepdims=True) - _LANE_GROUP0
    lo = g_idx * EXPERTS_PER_GROUP
    emask = (lane >= lo) & (lane < lo + EXPERTS_PER_GROUP)
    el = jnp.where(emask, logits, NEG_BIG)
    e1 = jnp.max(el, axis=1, keepdims=True)
    idx1 = jnp.min(jnp.where(el == e1, lane, big), axis=1, keepdims=True)
    esum = jnp.sum(jnp.where(emask, jnp.exp(el - e1), 0.0), axis=1, keepdims=True)
    el2 = jnp.where(lane == idx1, NEG_BIG, el)
    e2 = jnp.max(el2, axis=1, keepdims=True)
    idx2 = jnp.min(jnp.where(el2 == e2, lane, big), axis=1, keepdims=True)
    p1 = 1.0 / esum
    p2 = jnp.exp(e2 - e1) / esum
    gate1 = g_w * p1 / (p1 + p2)
    gate2 = g_w * p2 / (p1 + p2)
    oh1 = lane == idx1
    oh2 = lane == idx2
    a = jnp.where(oh1 | oh2, 1.0, 0.0)
    cum = _dot(tri_ref[...], a.astype(BF16))
    before = cum - a + carry[...]
    rank1 = jnp.sum(jnp.where(oh1, before, 0.0), axis=1, keepdims=True)
    rank2 = jnp.sum(jnp.where(oh2, before, 0.0), axis=1, keepdims=True)
    carry[...] = carry[...] + cum[tm - 1:tm, :]
    cnt_ref[...] = carry[...]
    info = jnp.where(lane == 0, idx1.astype(F32), 0.0)
    info = jnp.where(lane == 1, idx2.astype(F32), info)
    info = jnp.where(lane == 2, rank1, info)
    info = jnp.where(lane == 3, rank2, info)
    info = jnp.where(lane == 4, gate1, info)
    info = jnp.where(lane == 5, gate2, info)
    info_ref[...] = info


def _out_proj(xp, xs, xs_tile0, ret_p, ret_s, s5_p, s5_s, fox_p, fox_s, w_out, g2, w_r, b_r, n_prompt_tiles):
    tm = TOKEN_TILE
    d = xp.shape[1]
    n = (n_prompt_tiles + ret_s.shape[0] // tm) * tm
    w_out_fast = w_out.astype(BF16)
    w_r_hi = w_r.astype(BF16)
    w_r_lo = (w_r - w_r_hi.astype(F32)).astype(BF16)
    idx = np.arange(tm)
    tri = jnp.asarray(idx[:, None] >= idx[None, :], BF16)

    def row(i):
        return (i, 0)

    def const(i):
        return (0, 0)

    def pmap(i):
        return (jnp.minimum(i, n_prompt_tiles - 1), 0)

    def smap(i):
        return (jnp.maximum(i - n_prompt_tiles, 0), 0)

    def xsmap(i):
        return (xs_tile0 + jnp.maximum(i - n_prompt_tiles, 0), 0)

    return pl.pallas_call(
        functools.partial(_out_proj_kernel, n_prompt_tiles),
        grid=(n // tm,),
        in_specs=[
            pl.BlockSpec((tm, d), pmap),
            pl.BlockSpec((tm, d), xsmap),
            pl.BlockSpec((tm, RET_W), pmap),
            pl.BlockSpec((tm, RET_W), smap),
            pl.BlockSpec((tm, S5_W), pmap),
            pl.BlockSpec((tm, S5_W), smap),
            pl.BlockSpec((tm, FOX_W), pmap),
            pl.BlockSpec((tm, FOX_W), smap),
            pl.BlockSpec((d, d), const),
            pl.BlockSpec((d, d), const),
            pl.BlockSpec((1, d), const),
            pl.BlockSpec((d, LANES), const),
            pl.BlockSpec((d, LANES), const),
            pl.BlockSpec((1, LANES), const),
            pl.BlockSpec((tm, tm), const),
        ],
        out_specs=(
            pl.BlockSpec((tm, d), row),
            pl.BlockSpec((tm, d // 2), row),
            pl.BlockSpec((tm, LANES), row),
            pl.BlockSpec((1, LANES), const),
        ),
        out_shape=(
            jax.ShapeDtypeStruct((n, d), F32),
            jax.ShapeDtypeStruct((n, d // 2), jnp.uint32),
            jax.ShapeDtypeStruct((n, LANES), F32),
            jax.ShapeDtypeStruct((1, LANES), F32),
        ),
        scratch_shapes=[pltpu.VMEM((1, LANES), F32)],
        compiler_params=_params("arbitrary"),
        name="out_proj_router",
    )(xp, xs, ret_p, ret_s, s5_p, s5_s, fox_p, fox_s, w_out_fast, w_out, g2, w_r_hi, w_r_lo, b_r, tri)


_ROW_DMA_UNROLL = 8


def _dispatch_kernel(dest_ref, h_ref, rows_in_ref, rows_ref, sem):
    del rows_in_ref
    i = pl.program_id(0)
    tm = h_ref.shape[0]

    def issue(r, _):
        t = i * tm + r
        for k in range(2):
            pltpu.make_async_copy(h_ref.at[pl.ds(r, 1)], rows_ref.at[pl.ds(dest_ref[2 * t + k], 1)], sem).start()
        return 0

    lax.fori_loop(0, tm, issue, 0, unroll=_ROW_DMA_UNROLL)
    for _ in range(2):
        pltpu.make_async_copy(h_ref, rows_ref.at[pl.ds(0, tm)], sem).wait()


def _dispatch(dest, h2, n_rows):
    n, d = h2.shape
    tm = TOKEN_TILE
    grid_spec = pltpu.PrefetchScalarGridSpec(
        num_scalar_prefetch=1,
        grid=(n // tm,),
        in_specs=[
            pl.BlockSpec((tm, d), lambda i, dest: (i, 0)),
            pl.BlockSpec(memory_space=pl.ANY),
        ],
        out_specs=pl.BlockSpec(memory_space=pl.ANY),
        scratch_shapes=[pltpu.SemaphoreType.DMA(())],
    )
    return pl.pallas_call(
        _dispatch_kernel,
        grid_spec=grid_spec,
        out_shape=jax.ShapeDtypeStruct((n_rows, d), h2.dtype),
        input_output_aliases={2: 0},
        compiler_params=_params("arbitrary"),
        name="moe_dispatch",
    )(dest, h2, jnp.zeros((n_rows, d), h2.dtype))


def _experts_kernel(be_ref, nu_ref, x_ref, w1_ref, w3_ref, w2_ref, y_ref, w1b, w3b, w2b):
    i = pl.program_id(0)
    prev = be_ref[jnp.maximum(i - 1, 0)]
    fresh = (i == 0) | (be_ref[i] != prev)
    active = i < nu_ref[0]

    @pl.when(active & fresh)
    def _():
        w1b[...] = w1_ref[0, 0].astype(BF16)
        w3b[...] = w3_ref[0, 0].astype(BF16)
        w2b[...] = w2_ref[0, 0].astype(BF16)

    @pl.when(active)
    def _():
        xb = _unpack_bf16_pairs(x_ref[...])
        h1 = _dot(xb, w1b[...])
        h3 = _dot(xb, w3b[...])
        a = (h1 * _sigmoid(h1) * h3).astype(BF16)
        y_ref[...] = _dot(a, w2b[...])

    @pl.when(jnp.logical_not(active))
    def _():
        y_ref[...] = jnp.zeros_like(y_ref)


def _experts(layer, blk_e, n_used, x_rows, w1, w3, w2):
    n_rows, dpacked = x_rows.shape
    d, de = w1.shape[2], w1.shape[3]
    nblk = n_rows // MOE_ROWS

    def last_used(i, nu):
        return jnp.minimum(i, jnp.maximum(nu[0] - 1, 0))

    def xmap(i, be, nu):
        return (last_used(i, nu), 0)

    def wmap(i, be, nu):
        return (layer, be[last_used(i, nu)], 0, 0)

    grid_spec = pltpu.PrefetchScalarGridSpec(
        num_scalar_prefetch=2,
        grid=(nblk,),
        in_specs=[
            pl.BlockSpec((MOE_ROWS, dpacked), xmap),
            pl.BlockSpec((1, 1, d, de), wmap),
            pl.BlockSpec((1, 1, d, de), wmap),
            pl.BlockSpec((1, 1, de, d), wmap),
        ],
        out_specs=pl.BlockSpec((MOE_ROWS, d), lambda i, be, nu: (i, 0)),
        scratch_shapes=[
            pltpu.VMEM((d, de), BF16),
            pltpu.VMEM((d, de), BF16),
            pltpu.VMEM((de, d), BF16),
        ],
    )
    return pl.pallas_call(
        _experts_kernel,
        grid_spec=grid_spec,
        out_shape=jax.ShapeDtypeStruct((n_rows, d), F32),
        compiler_params=_params("arbitrary"),
        name="moe_experts",
    )(blk_e, n_used, x_rows, w1, w3, w2)


def _combine_kernel(n_prompt_tiles, dest_ref, x1_ref, info_ref, gf_ref, y_ref, o_ref, *rest):
    os_ref = rest[0] if n_prompt_tiles is not None else None
    buf, sem = rest[-2:]
    i = pl.program_id(0)
    tm = x1_ref.shape[0]

    def issue(r, _):
        t = i * tm + r
        for k in range(2):
            pltpu.make_async_copy(y_ref.at[pl.ds(dest_ref[2 * t + k], 1)], buf.at[k, pl.ds(r, 1)], sem).start()
        return 0

    lax.fori_loop(0, tm, issue, 0, unroll=_ROW_DMA_UNROLL)
    for k in range(2):
        pltpu.make_async_copy(y_ref.at[pl.ds(0, tm)], buf.at[k], sem).wait()
    info = info_ref[...]
    x2 = x1_ref[...] + info[:, 4:5] * buf[0] + info[:, 5:6] * buf[1]
    if n_prompt_tiles is None:
        o_ref[...] = x2
    else:
        ms = jnp.mean(x2 * x2, axis=-1, keepdims=True)
        y = x2 * lax.rsqrt(ms + RMS_EPS) * gf_ref[...]

        @pl.when(i < n_prompt_tiles)
        def _():
            o_ref[...] = y

        @pl.when(i >= n_prompt_tiles)
        def _():
            os_ref[...] = y


def _combine(dest, x1, info, gf, y_rows, n_prompt_tiles):
    n, d = x1.shape
    tm = TOKEN_TILE
    if n_prompt_tiles is None:
        out_specs = pl.BlockSpec((tm, d), lambda i, dest: (i, 0))
        out_shape = jax.ShapeDtypeStruct((n, d), F32)
    else:
        npt = n_prompt_tiles
        out_specs = (pl.BlockSpec((tm, d), lambda i, dest: (jnp.minimum(i, npt - 1), 0)),
                     pl.BlockSpec((tm, d), lambda i, dest: (jnp.maximum(i - npt, 0), 0)))
        out_shape = (jax.ShapeDtypeStruct((npt * tm, d), F32), jax.ShapeDtypeStruct((n - npt * tm, d), F32))
    grid_spec = pltpu.PrefetchScalarGridSpec(
        num_scalar_prefetch=1,
        grid=(n // tm,),
        in_specs=[
            pl.BlockSpec((tm, d), lambda i, dest: (i, 0)),
            pl.BlockSpec((tm, LANES), lambda i, dest: (i, 0)),
            pl.BlockSpec((1, d), lambda i, dest: (0, 0)),
            pl.BlockSpec(memory_space=pl.ANY),
        ],
        out_specs=out_specs,
        scratch_shapes=[
            pltpu.VMEM((2, tm, d), F32),
            pltpu.SemaphoreType.DMA(()),
        ],
    )
    return pl.pallas_call(
        functools.partial(_combine_kernel, n_prompt_tiles),
        grid_spec=grid_spec,
        out_shape=out_shape,
        compiler_params=_params("arbitrary"),
        name="moe_combine",
    )(dest, x1, info, gf, y_rows)


def _moe_plan(info, counts, n_tokens):
    eid = info[:, 0:2].astype(jnp.int32)
    rank = info[:, 2:4].astype(jnp.int32)
    cnt = counts[0, :N_EXPERTS].astype(jnp.int32)
    padded = (cnt + MOE_ROWS - 1) // MOE_ROWS * MOE_ROWS
    pend = jnp.cumsum(padded)
    pstart = pend - padded
    experts = jnp.arange(N_EXPERTS, dtype=jnp.int32)
    start_of = jnp.sum(jnp.where(eid[..., None] == experts, pstart, 0), axis=-1)
    dest = (start_of + rank).reshape(-1)
    nblk = -(-n_tokens * 2 // MOE_ROWS) + N_EXPERTS
    blk_start = jnp.arange(nblk, dtype=jnp.int32) * MOE_ROWS
    blk_e = jnp.minimum(jnp.sum((pend[None, :] <= blk_start[:, None]).astype(jnp.int32), axis=1), N_EXPERTS - 1)
    n_used = (pend[-1:] // MOE_ROWS).astype(jnp.int32)
    return dest.astype(jnp.int32), blk_e.astype(jnp.int32), n_used, nblk * MOE_ROWS


def _rope_tables(seq, past, tnew, tile):
    half = HEAD_DIM // 2
    inv = ROPE_THETA ** (-jnp.arange(half, dtype=F32) / half)
    pos = jnp.concatenate([jnp.arange(seq, dtype=F32), past + (jnp.arange(tile) % tnew).astype(F32)])
    ang = pos[:, None] * inv[None, :]
    cos = jnp.cos(ang)
    sin = jnp.sin(ang)
    cos_h = jnp.concatenate([cos, cos], axis=1)
    sin_h = jnp.concatenate([-sin, sin], axis=1)
    return jnp.tile(cos_h, (1, H_RET)), jnp.tile(sin_h, (1, H_RET))


def _block_diag_states(s):
    b = s.shape[0]
    eye = jnp.eye(H_RET, dtype=s.dtype)
    return jnp.einsum('bhde,hg->bhdge', s, eye).reshape(b, RET_W, RET_W)


def _diag_blocks(s_bd):
    b = s_bd.shape[0]
    s5 = s_bd.reshape(b, H_RET, HEAD_DIM, H_RET, HEAD_DIM)
    return jnp.stack([s5[:, h, :, h, :] for h in range(H_RET)], axis=1)


def kernel(x_prompt, x_sample, cache_k, cache_v, cache_logf, state_ret, state_s5, page_table, norm1, w_in, ret_gn,
           s5_lam_re, s5_lam_im, s5_b_re, s5_b_im, s5_c_re, s5_c_im, s5_d, s5_log_dt, s5_glu_w, s5_glu_b, fox_fb,
           w_out, norm2, w_rg, b_rg, w_re, b_re, w1, w3, w2, norm_f):
    bp, seq, d = x_prompt.shape
    bd_, tnew, _ = x_sample.shape
    depth = w_in.shape[0]
    n_phys, page = cache_k.shape[1], cache_k.shape[2]
    npages = page_table.shape[1]
    past = npages * page
    n_p = bp * seq
    n_s = bd_ * tnew
    n = n_p + n_s
    tm = TOKEN_TILE
    assert n_p % tm == 0 and n_s == tm and seq % tm == 0
    n_prompt_tiles = n_p // tm

    xp, xs, xs_tile0 = x_prompt.reshape(n_p, d), x_sample.reshape(n_s, d), 0
    cos_t, sin_t = _rope_tables(seq, past, tnew, tm)
    cache_kt = jnp.transpose(cache_k, (0, 1, 3, 4, 2)).reshape(depth, n_phys, FOX_W, page)
    cache_vt = jnp.transpose(cache_v, (0, 1, 3, 4, 2)).reshape(depth, n_phys, FOX_W, page)
    cache_lft = jnp.swapaxes(cache_logf, 2, 3)
    ff0 = w_in.shape[2] - H_FOX

    outs = {k: [] for k in ('kp', 'vp', 'lp', 'ks', 'vs', 'ls', 'rp', 'rs', 'sp', 'ss')}
    for l in range(depth):
        w_main = w_in[l, :, :ff0]
        w_ff = jnp.pad(w_in[l, :, ff0:], ((0, 0), (0, LANES - H_FOX)))
        fb = jnp.pad(fox_fb[l], (0, LANES - H_FOX))[None, :]
        g1 = norm1[l][None, :]
        ret_in, su, fq, fk, fv, fkb, fvb, logf = _in_proj(
            _FAST, xp, g1, w_main.astype(BF16), w_ff.astype(BF16), fb, cos_t, sin_t,
            0, n_prompt_tiles, 0, seq // tm)
        ret_in_s, su_s, fq_s, fk_s, fv_s, _, _, logf_s = _in_proj(
            _EXACT, xs, g1, w_main, w_ff, fb, cos_t, sin_t, xs_tile0, 1, seq // tm, 1)

        gw = ret_gn[l][None, :]
        zero_state = jnp.zeros((bp, RET_W, RET_W), F32)
        ret_p, rs_p = _retention(_FAST, ret_in, zero_state, gw, bp, 0, seq, 1, min(RET_CHUNK, seq),
                                 min(1024, seq))
        seq_per_chunk = 8
        chunk_rows = seq_per_chunk * tnew
        ret_s, rs_s = _retention(_EXACT, ret_in_s, _block_diag_states(state_ret[l]), gw, bd_ // seq_per_chunk, 0,
                                 chunk_rows, seq_per_chunk, tnew, chunk_rows)

        tables = _s5_tables(s5_lam_re[l], s5_lam_im[l], s5_b_re[l], s5_b_im[l], s5_c_re[l], s5_c_im[l],
                            s5_log_dt[l])
        dsk = s5_d[l][None, :]
        glub = s5_glu_b[l][None, :]
        zs = jnp.zeros((bp, 1, S5_LANES), F32)
        s5_p, sp_r, sp_i = _s5(_FAST, su, zs, zs, tables, dsk, s5_glu_w[l], glub, bp, 0, seq, min(512, seq),
                               False)
        x0 = state_s5[l].reshape(bd_, 1, S5_LANES, 2)
        s5_s, ss_r, ss_i = _s5(_EXACT, su_s, x0[..., 0], x0[..., 1], tables, dsk, s5_glu_w[l], glub, 1, 0, n_s,
                               n_s, True)

        qa, ka, va = _fox_prep(logf, fq, fkb, fvb, bp, seq, min(256, seq))
        fox_p = _fox_prompt(qa, ka, va, bp, seq)
        q_s = fq_s.reshape(bd_, tnew, 1, FOX_W)
        hmask = jnp.asarray(np.arange(H_FOX)[:, None] == (np.arange(FOX_W) // HEAD_DIM)[None, :])[None, None]
        qm = jnp.where(hmask, q_s, jnp.zeros((), BF16)).reshape(bd_, tnew * H_FOX, FOX_W)
        pad_rows = ((0, 0), (0, page - tnew), (0, 0))
        knew = jnp.pad(fk_s.reshape(bd_, tnew, FOX_W), pad_rows)
        vnew = jnp.pad(fv_s.reshape(bd_, tnew, FOX_W), pad_rows)
        lfnew = jnp.pad(jnp.swapaxes(logf_s[:, :H_FOX].reshape(bd_, tnew, H_FOX), 1, 2),
                        ((0, 0), (0, 0), (0, page - tnew)))
        fox_s = _fox_sample(l, page_table, qm, cache_kt, cache_vt, cache_lft, knew, vnew, lfnew)
        fox_s = fox_s.reshape(n_s, FOX_W)

        w_r = jnp.pad(jnp.concatenate([w_re[l], w_rg[l]], axis=1),
                      ((0, 0), (0, LANES - N_EXPERTS - N_EXPERT_GROUPS)))
        b_r = jnp.pad(jnp.concatenate([b_re[l], b_rg[l]]), (0, LANES - N_EXPERTS - N_EXPERT_GROUPS))[None, :]
        x1, h2, info, counts = _out_proj(xp, xs, xs_tile0, ret_p, ret_s, s5_p, s5_s, fox_p, fox_s, w_out[l],
                                         norm2[l][None, :], w_r, b_r, n_prompt_tiles)
        dest, blk_e, n_used, n_rows = _moe_plan(info, counts, n)
        x_rows = _dispatch(dest, h2, n_rows)
        y_rows = _experts(l, blk_e, n_used, x_rows, w1, w3, w2)
        if l < depth - 1:
            x_all = _combine(dest, x1, info, norm_f[None, :], y_rows, None)
            xp, xs, xs_tile0 = x_all, x_all, n_prompt_tiles
        else:
            y_p, y_s = _combine(dest, x1, info, norm_f[None, :], y_rows, n_prompt_tiles)

        outs['kp'].append(fk)
        outs['vp'].append(fv)
        outs['lp'].append(logf[:, :H_FOX].reshape(bp, seq, H_FOX))
        outs['ks'].append(fk_s.reshape(bd_, tnew, H_FOX, HEAD_DIM))
        outs['vs'].append(fv_s.reshape(bd_, tnew, H_FOX, HEAD_DIM))
        outs['ls'].append(logf_s[:, :H_FOX].reshape(bd_, tnew, H_FOX))
        outs['rp'].append(_diag_blocks(rs_p))
        outs['rs'].append(_diag_blocks(rs_s))
        outs['sp'].append(jnp.stack([sp_r.reshape(bp, S5_GROUPS, S5_STATE),
                                     sp_i.reshape(bp, S5_GROUPS, S5_STATE)], axis=-1))
        outs['ss'].append(jnp.stack([ss_r.reshape(bd_, S5_GROUPS, S5_STATE),
                                     ss_i.reshape(bd_, S5_GROUPS, S5_STATE)], axis=-1))

    y_prompt = y_p.reshape(bp, seq, d)
    y_sample = y_s.reshape(bd_, tnew, d)
    st = {k: jnp.stack(v) for k, v in outs.items()}
    for name in ('kp', 'vp'):
        st[name] = st[name].reshape(depth, bp, seq, H_FOX, HEAD_DIM)
    return (y_prompt, y_sample, st['kp'], st['vp'], st['lp'], st['ks'], st['vs'], st['ls'],
            st['rp'], st['rs'], st['sp'], st['ss'])
```

```python
import functools
import math

import numpy as np

import jax
import jax.numpy as jnp
from jax import lax
from jax.experimental import pallas as pl
from jax.experimental.pallas import tpu as pltpu

F32 = jnp.float32
BF16 = jnp.bfloat16
HIGHEST = lax.Precision.HIGHEST

HEAD_DIM = 64
H_RET = 4
H_FOX = 8
RET_W = H_RET * HEAD_DIM
S5_W = 256
FOX_W = H_FOX * HEAD_DIM
S5_GROUPS = 16
S5_STATE = 64
S5_LANES = S5_GROUPS * S5_STATE
N_EXPERTS = 32
N_EXPERT_GROUPS = 4
EXPERTS_PER_GROUP = 8
RET_CHUNK = 128
ROPE_THETA = 10000.0
RMS_EPS = 1e-6
NEG_BIG = -1e30
LOG2E = math.log2(math.e)

LANES = 128
SUBLANES = 8
TOKEN_TILE = 256
MOE_ROWS = 256
FOX_TQ = 1024
PAGES_PER_STEP = 32
VMEM_LIMIT = 48 * 1024 * 1024
VMEM_LIMIT_PAGED = 56 * 1024 * 1024


def _params(*sem, vmem=VMEM_LIMIT):
    return pltpu.CompilerParams(dimension_semantics=sem, vmem_limit_bytes=vmem)


def _sigmoid(x):
    return 1.0 / (1.0 + jnp.exp(-x))


def _dot(a, b):
    return jnp.dot(a, b, preferred_element_type=F32)


def _dot_nt(a, b):
    return lax.dot_general(a, b, (((1,), (1,)), ((), ())), preferred_element_type=F32)


class _MatmulMode:
    def __init__(self, exact):
        self.dtype = F32 if exact else BF16
        self.precision = HIGHEST if exact else None

    def cast(self, x):
        return x.astype(self.dtype)

    def dot(self, a, b):
        return jnp.dot(a, b, precision=self.precision, preferred_element_type=F32)

    def dot_nt(self, a, b):
        return lax.dot_general(a, b, (((1,), (1,)), ((), ())), precision=self.precision,
                               preferred_element_type=F32)

    def dot_tn(self, a, b):
        return lax.dot_general(a, b, (((0,), (0,)), ((), ())), precision=self.precision,
                               preferred_element_type=F32)


_FAST = _MatmulMode(False)
_EXACT = _MatmulMode(True)


def _in_proj_kernel(mm, x_ref, g_ref, w_ref, wff_ref, fb_ref, cos_ref, sin_ref,
                    ret_ref, su_ref, fq_ref, fk_ref, fv_ref, fkb_ref, fvb_ref, logf_ref):
    _dot = mm.dot
    x = x_ref[...]
    ms = jnp.mean(x * x, axis=-1, keepdims=True)
    h = mm.cast(x * lax.rsqrt(ms + RMS_EPS) * g_ref[...])
    tm = x.shape[0]
    cos = cos_ref[...]
    sin = sin_ref[...]
    lane = lax.broadcasted_iota(jnp.int32, (tm, RET_W), 1)
    first_half = (lane % HEAD_DIM) < (HEAD_DIM // 2)

    def rope(t):
        swapped = jnp.where(first_half,
                            pltpu.roll(t, RET_W - HEAD_DIM // 2, axis=1),
                            pltpu.roll(t, HEAD_DIM // 2, axis=1))
        return t * cos + swapped * sin

    ret_ref[:, 0:RET_W] = rope(_dot(h, w_ref[:, 0:RET_W]))
    ret_ref[:, RET_W:2 * RET_W] = rope(_dot(h, w_ref[:, RET_W:2 * RET_W])) * (HEAD_DIM ** -0.5)
    ret_ref[:, 2 * RET_W:4 * RET_W] = _dot(h, w_ref[:, 2 * RET_W:4 * RET_W])
    c0 = 4 * RET_W
    su_ref[...] = _dot(h, w_ref[:, c0:c0 + S5_W])
    c0 += S5_W
    fq_ref[...] = (_dot(h, w_ref[:, c0:c0 + FOX_W]) * (HEAD_DIM ** -0.5 * LOG2E)).astype(BF16)
    c0 += FOX_W
    fk = _dot(h, w_ref[:, c0:c0 + FOX_W])
    fk_ref[...] = fk
    fkb_ref[...] = fk.astype(BF16)
    c0 += FOX_W
    fv = _dot(h, w_ref[:, c0:c0 + FOX_W])
    fv_ref[...] = fv
    fvb_ref[...] = fv.astype(BF16)
    z = _dot(h, wff_ref[...]) + fb_ref[...]
    logf = jnp.minimum(z, 0.0) - jnp.log1p(jnp.exp(-jnp.abs(z)))
    lane128 = lax.broadcasted_iota(jnp.int32, (tm, LANES), 1)
    logf_ref[...] = jnp.where(lane128 < H_FOX, logf, 0.0)


def _in_proj(mm, x, g, w_main, w_ff, fb, cos_t, sin_t, tile0, ntiles, pos_tile0, pos_tiles):
    tm = TOKEN_TILE
    n = ntiles * tm
    wcols = w_main.shape[1]

    def xrow(i):
        return (tile0 + i, 0)

    def row(i):
        return (i, 0)

    def const(i):
        return (0, 0)

    def pos_map(i):
        return (pos_tile0 + i % pos_tiles, 0)

    outs = (
        jax.ShapeDtypeStruct((n, 4 * RET_W), F32),
        jax.ShapeDtypeStruct((n, S5_W), F32),
        jax.ShapeDtypeStruct((n, FOX_W), BF16),
        jax.ShapeDtypeStruct((n, FOX_W), F32),
        jax.ShapeDtypeStruct((n, FOX_W), F32),
        jax.ShapeDtypeStruct((n, FOX_W), BF16),
        jax.ShapeDtypeStruct((n, FOX_W), BF16),
        jax.ShapeDtypeStruct((n, LANES), F32),
    )
    return pl.pallas_call(
        functools.partial(_in_proj_kernel, mm),
        grid=(ntiles,),
        in_specs=[
            pl.BlockSpec((tm, x.shape[1]), xrow),
            pl.BlockSpec((1, x.shape[1]), const),
            pl.BlockSpec((x.shape[1], wcols), const),
            pl.BlockSpec((x.shape[1], LANES), const),
            pl.BlockSpec((1, LANES), const),
            pl.BlockSpec((tm, RET_W), pos_map),
            pl.BlockSpec((tm, RET_W), pos_map),
        ],
        out_specs=(
            pl.BlockSpec((tm, 4 * RET_W), row),
            pl.BlockSpec((tm, S5_W), row),
            pl.BlockSpec((tm, FOX_W), row),
            pl.BlockSpec((tm, FOX_W), row),
            pl.BlockSpec((tm, FOX_W), row),
            pl.BlockSpec((tm, FOX_W), row),
            pl.BlockSpec((tm, FOX_W), row),
            pl.BlockSpec((tm, LANES), row),
        ),
        out_shape=outs,
        compiler_params=_params("arbitrary"),
        name="in_proj",
    )(x, g, w_main, w_ff, fb, cos_t, sin_t)


def _retention_kernel(mm, nseq, seq_len, nchunk, x_ref, s0_ref, dm_ref, qd_ref, kd_ref, cd_ref, bd_ref,
                      gn_ref, gw_ref, o_ref, sout_ref, s_scr):
    cast, _dot, _dot_nt, _dot_tn = mm.cast, mm.dot, mm.dot_nt, mm.dot_tn
    step = pl.program_id(1)
    rows = nseq * seq_len

    @pl.when(step == 0)
    def _():
        s_scr[...] = s0_ref[...]

    lane_head = lax.broadcasted_iota(jnp.int32, (rows, RET_W), 1) // HEAD_DIM
    row_seq = lax.broadcasted_iota(jnp.int32, (rows, 1), 0) // seq_len

    def chunk(c, _):
        r0 = pl.multiple_of(c * rows, rows)
        q = x_ref[pl.ds(r0, rows), 0:RET_W]
        k = x_ref[pl.ds(r0, rows), RET_W:2 * RET_W]
        v = x_ref[pl.ds(r0, rows), 2 * RET_W:3 * RET_W]
        g = x_ref[pl.ds(r0, rows), 3 * RET_W:4 * RET_W]
        qb = cast(q)
        kb = cast(k)
        vb = cast(v)
        ps = []
        vs = []
        for h in range(H_RET):
            hm = lane_head == h
            sc = _dot_nt(cast(jnp.where(hm, q, 0.0)), kb)
            ps.append(cast(sc * dm_ref[h]))
            vs.append(cast(jnp.where(hm, v, 0.0)))
        inner = _dot(jnp.concatenate(ps, axis=1), jnp.concatenate(vs, axis=0))
        kdec = k * kd_ref[...]
        cd = cd_ref[...]
        bd = bd_ref[...]
        if nseq == 1:
            s_old = s_scr[0]
            cross = _dot(qb, cast(s_old))
            s_scr[0] = s_old * cd + _dot_tn(cast(kdec), vb) * bd
        else:
            def seq_body(s, cross):
                rm = row_seq == s
                s_old = s_scr[s]
                cross = jnp.where(rm, _dot(qb, cast(s_old)), cross)
                kv = _dot_tn(cast(jnp.where(rm, kdec, 0.0)), vb)
                s_scr[s] = s_old * cd + kv * bd
                return cross

            cross = lax.fori_loop(0, nseq, seq_body, jnp.zeros((rows, RET_W), F32))
        o = inner + cross * qd_ref[...]
        o2 = o * o
        hi = o2.astype(BF16)
        lo = (o2 - hi.astype(F32)).astype(BF16)
        ms = jnp.dot(hi, gn_ref[...], preferred_element_type=F32) + jnp.dot(
            lo, gn_ref[...], preferred_element_type=F32)
        on = o * lax.rsqrt(ms + RMS_EPS)
        o_ref[pl.ds(r0, rows), :] = (g * _sigmoid(g) * on * gw_ref[...]).astype(o_ref.dtype)
        return 0

    lax.fori_loop(0, nchunk, chunk, 0)

    @pl.when(step == pl.num_programs(1) - 1)
    def _():
        sout_ref[...] = s_scr[...]


def _retention_tables(nseq, seq_len):
    f32 = np.float32
    rows = nseq * seq_len
    log_gamma = np.log1p(-np.power(f32(2.0), -5.0 - np.arange(H_RET, dtype=f32))).astype(f32)
    idx = np.arange(rows)
    pos = (idx % seq_len).astype(f32)
    seq = idx // seq_len
    diff = pos[:, None] - pos[None, :]
    ok = (seq[:, None] == seq[None, :]) & (diff >= 0)
    dm = np.where(ok[None], np.exp(log_gamma[:, None, None] * np.maximum(diff, f32(0.0))[None]), f32(0.0))
    lg_lane = np.repeat(log_gamma, HEAD_DIM)
    qd = np.exp((pos[:, None] + f32(1.0)) * lg_lane[None, :])
    kd = np.exp((f32(seq_len - 1.0) - pos[:, None]) * lg_lane[None, :])
    cd = np.exp(f32(seq_len) * lg_lane)[None, :]
    head = np.arange(RET_W) // HEAD_DIM
    bd = (head[:, None] == head[None, :]).astype(f32)
    to = lambda a: jnp.asarray(a, F32)
    return to(dm), to(qd), to(kd), to(cd), to(bd), jnp.asarray(bd / HEAD_DIM, BF16)


def _retention(mm, ret_in, s0_bd, gw, nbatch, row0, seq_total, nseq, seq_len, rows_per_step):
    rows = nseq * seq_len
    nchunk = rows_per_step // rows
    steps = seq_total // rows_per_step
    blk0 = row0 // rows_per_step
    dm, qd, kd, cd, bd, gn = _retention_tables(nseq, seq_len)

    def rmap(b, i):
        return (blk0 + b * steps + i, 0)

    def omap(b, i):
        return (b * steps + i, 0)

    def c2(b, i):
        return (0, 0)

    def c3(b, i):
        return (0, 0, 0)

    def smap(b, i):
        return (b, 0, 0)

    in_specs = [
        pl.BlockSpec((rows_per_step, 4 * RET_W), rmap),
        pl.BlockSpec((nseq, RET_W, RET_W), smap),
        pl.BlockSpec((H_RET, rows, rows), c3),
        pl.BlockSpec((rows, RET_W), c2),
        pl.BlockSpec((rows, RET_W), c2),
        pl.BlockSpec((1, RET_W), c2),
        pl.BlockSpec((RET_W, RET_W), c2),
        pl.BlockSpec((RET_W, RET_W), c2),
        pl.BlockSpec((1, RET_W), c2),
    ]
    return pl.pallas_call(
        functools.partial(_retention_kernel, mm, nseq, seq_len, nchunk),
        grid=(nbatch, steps),
        in_specs=in_specs,
        out_specs=(
            pl.BlockSpec((rows_per_step, RET_W), omap),
            pl.BlockSpec((nseq, RET_W, RET_W), smap),
        ),
        out_shape=(
            jax.ShapeDtypeStruct((nbatch * seq_total, RET_W), mm.dtype),
            jax.ShapeDtypeStruct((nbatch * nseq, RET_W, RET_W), F32),
        ),
        scratch_shapes=[pltpu.VMEM((nseq, RET_W, RET_W), F32)],
        compiler_params=_params("arbitrary", "arbitrary"),
        name="retention",
    )(ret_in, s0_bd, dm, qd, kd, cd, bd, gn, gw)


def _s5_kernel(mm, short, u_ref, x0r_ref, x0i_ref, bre_ref, bim_ref, pre_ref, pim_ref, cm_ref, d_ref,
               gw_ref, gb_ref, o_ref, str_ref, sti_ref, bur, bui, car, cai):
    cast, _dot = mm.cast, mm.dot
    step = pl.program_id(1)
    rows = u_ref.shape[0]
    u = u_ref[...]
    ub = cast(u)
    bur[...] = _dot(ub, bre_ref[...])
    bui[...] = _dot(ub, bim_ref[...])

    if not short:
        @pl.when(step == 0)
        def _():
            car[...] = x0r_ref[0]
            cai[...] = x0i_ref[0]

    pr = pre_ref[...]
    pi = pim_ref[...]
    rowi = lax.broadcasted_iota(jnp.int32, (SUBLANES, S5_LANES), 0)

    def group(j, carry):
        cr, ci = carry
        r0 = pl.multiple_of(j * SUBLANES, SUBLANES)
        xr = bur[pl.ds(r0, SUBLANES), :]
        xi = bui[pl.ds(r0, SUBLANES), :]
        for s in (1, 2, 4):
            ar = pr[s - 1:s]
            ai = pi[s - 1:s]
            sr = jnp.where(rowi >= s, pltpu.roll(xr, s, axis=0), 0.0)
            si = jnp.where(rowi >= s, pltpu.roll(xi, s, axis=0), 0.0)
            xr, xi = xr + ar * sr - ai * si, xi + ar * si + ai * sr
        if short:
            cr = x0r_ref[j]
            ci = x0i_ref[j]
        xr, xi = xr + pr * cr - pi * ci, xi + pr * ci + pi * cr
        bur[pl.ds(r0, SUBLANES), :] = xr
        bui[pl.ds(r0, SUBLANES), :] = xi
        cr = xr[SUBLANES - 1:SUBLANES]
        ci = xi[SUBLANES - 1:SUBLANES]
        if short:
            str_ref[j] = cr
            sti_ref[j] = ci
        return cr, ci

    if short:
        init = (jnp.zeros((1, S5_LANES), F32), jnp.zeros((1, S5_LANES), F32))
    else:
        init = (car[...], cai[...])
    cr, ci = lax.fori_loop(0, rows // SUBLANES, group, init)
    if not short:
        car[...] = cr
        cai[...] = ci

        @pl.when(step == pl.num_programs(1) - 1)
        def _():
            str_ref[0] = cr
            sti_ref[0] = ci

    xs = jnp.concatenate([cast(bur[...]), cast(bui[...])], axis=1)
    y = _dot(xs, cm_ref[...]) + d_ref[...] * u
    yg = 0.5 * y * (1.0 + jnp.tanh(math.sqrt(2.0 / math.pi) * (y + 0.044715 * (y * y * y))))
    z = _dot(cast(yg), gw_ref[...]) + gb_ref[...]
    o_ref[...] = (yg * _sigmoid(z)).astype(o_ref.dtype)


def _s5_tables(lam_re, lam_im, b_re, b_im, c_re, c_im, log_dt):
    lam = lax.complex(lam_re.astype(F32), lam_im.astype(F32))
    dt = jnp.exp(log_dt.astype(F32))[:, None]
    lam_bar = jnp.exp(lam * dt)
    b_bar = ((lam_bar - 1.0) / lam)[:, :, None] * lax.complex(b_re.astype(F32), b_im.astype(F32))
    k = jnp.arange(1, SUBLANES + 1, dtype=F32)[:, None, None]
    powers = jnp.exp((lam * dt)[None] * k)
    pre = jnp.real(powers).reshape(SUBLANES, S5_LANES)
    pim = jnp.imag(powers).reshape(SUBLANES, S5_LANES)
    eye = jnp.eye(S5_GROUPS, dtype=F32)
    bre = jnp.einsum('gpc,gh->gchp', jnp.real(b_bar), eye).reshape(S5_W, S5_LANES)
    bim = jnp.einsum('gpc,gh->gchp', jnp.imag(b_bar), eye).reshape(S5_W, S5_LANES)
    cre = jnp.einsum('gcp,gh->hpgc', c_re.astype(F32), eye).reshape(S5_LANES, S5_W)
    cim = jnp.einsum('gcp,gh->hpgc', c_im.astype(F32), eye).reshape(S5_LANES, S5_W)
    cm = jnp.concatenate([cre, -cim], axis=0)
    return bre, bim, pre, pim, cm


def _s5(mm, su, x0r, x0i, tables, d, glu_w, glu_b, nbatch, row0, seq_total, rows_per_step, short):
    bre, bim, pre, pim, cm = tables
    bre, bim, cm, glu_w = mm.cast(bre), mm.cast(bim), mm.cast(cm), mm.cast(glu_w)
    steps = seq_total // rows_per_step
    blk0 = row0 // rows_per_step
    nstate = x0r.shape[0]
    sblk = nstate if short else 1

    def rmap(b, i):
        return (blk0 + b * steps + i, 0)

    def omap(b, i):
        return (b * steps + i, 0)

    def c2(b, i):
        return (0, 0)

    def smap(b, i):
        return (0 if short else b, 0, 0)

    in_specs = [
        pl.BlockSpec((rows_per_step, S5_W), rmap),
        pl.BlockSpec((sblk, 1, S5_LANES), smap),
        pl.BlockSpec((sblk, 1, S5_LANES), smap),
        pl.BlockSpec((S5_W, S5_LANES), c2),
        pl.BlockSpec((S5_W, S5_LANES), c2),
        pl.BlockSpec((SUBLANES, S5_LANES), c2),
        pl.BlockSpec((SUBLANES, S5_LANES), c2),
        pl.BlockSpec((2 * S5_LANES, S5_W), c2),
        pl.BlockSpec((1, S5_W), c2),
        pl.BlockSpec((S5_W, S5_W), c2),
        pl.BlockSpec((1, S5_W), c2),
    ]
    return pl.pallas_call(
        functools.partial(_s5_kernel, mm, short),
        grid=(nbatch, steps),
        in_specs=in_specs,
        out_specs=(
            pl.BlockSpec((rows_per_step, S5_W), omap),
            pl.BlockSpec((sblk, 1, S5_LANES), smap),
            pl.BlockSpec((sblk, 1, S5_LANES), smap),
        ),
        out_shape=(
            jax.ShapeDtypeStruct((nbatch * seq_total, S5_W), mm.dtype),
            jax.ShapeDtypeStruct((nstate, 1, S5_LANES), F32),
            jax.ShapeDtypeStruct((nstate, 1, S5_LANES), F32),
        ),
        scratch_shapes=[
            pltpu.VMEM((rows_per_step, S5_LANES), F32),
            pltpu.VMEM((rows_per_step, S5_LANES), F32),
            pltpu.VMEM((1, S5_LANES), F32),
            pltpu.VMEM((1, S5_LANES), F32),
        ],
        compiler_params=_params("arbitrary", "arbitrary"),
        name="s5",
    )(su, x0r, x0i, bre, bim, pre, pim, cm, d, glu_w, glu_b)


_PAIR_W = 2 * LANES
_AUG_PER_HEAD = 6
_N_PAIRS = H_FOX // 2


def _aug_placement():
    pq = np.zeros((3 * LANES, _N_PAIRS * LANES), np.float32)
    pk = np.zeros((3 * LANES, _N_PAIRS * LANES), np.float32)
    oq = np.zeros((1, _N_PAIRS * LANES), np.float32)
    ok = np.zeros((1, _N_PAIRS * LANES), np.float32)
    for h in range(H_FOX):
        base = (h // 2) * LANES + (h % 2) * _AUG_PER_HEAD
        for part in range(3):
            pq[part * LANES + h, base + part] = 1.0
            oq[0, base + 3 + part] = 1.0
            ok[0, base + part] = 1.0
            pk[part * LANES + h, base + 3 + part] = -1.0
    return (jnp.asarray(pq, BF16), jnp.asarray(pk, BF16), jnp.asarray(oq), jnp.asarray(ok))


def _split3(v, axis=1):
    hi = v.astype(BF16)
    r1 = v - hi.astype(F32)
    mid = r1.astype(BF16)
    lo = (r1 - mid.astype(F32)).astype(BF16)
    return jnp.concatenate([hi, mid, lo], axis=axis)


def _cumsum_rows(tri_ref, x):
    c = _dot(tri_ref[...], _split3(x, axis=1))
    return c[:, 0:LANES] + c[:, LANES:2 * LANES] + c[:, 2 * LANES:3 * LANES]


def _cumsum_lanes(x, u_ref):
    r = x.shape[0]
    c = _dot(_split3(x, axis=0), u_ref[...])
    return c[0:r] + c[r:2 * r] + c[2 * r:3 * r]


def _fox_prep_kernel(lf_ref, q_ref, k_ref, v_ref, tri_ref, pq_ref, pk_ref, oq_ref, ok_ref,
                     qa_ref, ka_ref, va_ref, carry):
    @pl.when(pl.program_id(1) == 0)
    def _():
        carry[...] = jnp.zeros_like(carry)

    c = _cumsum_rows(tri_ref, lf_ref[...]) + carry[...]
    rows = c.shape[0]
    carry[...] = c[rows - 1:rows, :]
    parts = _split3(c * LOG2E)
    augq = (_dot(parts, pq_ref[...]) + oq_ref[...]).astype(BF16)
    augk = (_dot(parts, pk_ref[...]) + ok_ref[...]).astype(BF16)
    lane = lax.broadcasted_iota(jnp.int32, (rows, LANES), 1)
    one = jnp.ones((rows, LANES), BF16)
    for p in range(_N_PAIRS):
        src = slice(p * LANES, (p + 1) * LANES)
        qa_ref[:, p * _PAIR_W:p * _PAIR_W + LANES] = q_ref[:, src]
        qa_ref[:, p * _PAIR_W + LANES:(p + 1) * _PAIR_W] = augq[:, src]
        ka_ref[:, p * _PAIR_W:p * _PAIR_W + LANES] = k_ref[:, src]
        ka_ref[:, p * _PAIR_W + LANES:(p + 1) * _PAIR_W] = augk[:, src]
        vp = v_ref[:, src]
        va_ref[:, p * _PAIR_W:p * _PAIR_W + LANES] = jnp.where(lane < HEAD_DIM, vp, one)
        va_ref[:, p * _PAIR_W + LANES:(p + 1) * _PAIR_W] = jnp.where(lane < HEAD_DIM, one, vp)


def _fox_prep(logf, fq, fkb, fvb, nbatch, seq_total, rows_per_step):
    steps = seq_total // rows_per_step
    idx = np.arange(rows_per_step)
    tri = jnp.asarray(idx[:, None] >= idx[None, :], BF16)
    pq, pk, oq, ok = _aug_placement()
    wide = _N_PAIRS * _PAIR_W

    def rmap(b, i):
        return (b * steps + i, 0)

    def const(b, i):
        return (0, 0)

    out = jax.ShapeDtypeStruct((nbatch * seq_total, wide), BF16)
    return pl.pallas_call(
        _fox_prep_kernel,
        grid=(nbatch, steps),
        in_specs=[
            pl.BlockSpec((rows_per_step, LANES), rmap),
            pl.BlockSpec((rows_per_step, FOX_W), rmap),
            pl.BlockSpec((rows_per_step, FOX_W), rmap),
            pl.BlockSpec((rows_per_step, FOX_W), rmap),
            pl.BlockSpec((rows_per_step, rows_per_step), const),
            pl.BlockSpec(pq.shape, const),
            pl.BlockSpec(pk.shape, const),
            pl.BlockSpec(oq.shape, const),
            pl.BlockSpec(ok.shape, const),
        ],
        out_specs=(pl.BlockSpec((rows_per_step, wide), rmap),) * 3,
        out_shape=(out, out, out),
        scratch_shapes=[pltpu.VMEM((1, LANES), F32)],
        compiler_params=_params("arbitrary", "arbitrary"),
        name="fox_prep",
    )(logf, fq, fkb, fvb, tri, pq, pk, oq, ok)


_PAIRS_PER_STEP = 2


def _fox_prompt_kernel(tq, q_ref, k_ref, v_ref, o_ref, acc, m_scr):
    qi = pl.program_id(2)
    nhead = 2 * _PAIRS_PER_STEP
    lane = lax.broadcasted_iota(jnp.int32, (tq, _PAIR_W), 1)
    qs = []
    for h in range(nhead):
        par = h % 2
        q = q_ref[:, (h // 2) * _PAIR_W:(h // 2 + 1) * _PAIR_W]
        own = ((lane >= par * HEAD_DIM) & (lane < (par + 1) * HEAD_DIM)) | (
            (lane >= LANES + par * _AUG_PER_HEAD) & (lane < LANES + (par + 1) * _AUG_PER_HEAD))
        qs.append(jnp.where(own, q, jnp.zeros_like(q)))
        m_scr[h] = jnp.full((tq, LANES), NEG_BIG, F32)
        acc[h] = jnp.zeros((tq, LANES), F32)
    half = tq // 2

    def update(h, r0, nr, k0, nk, mask):
        kb = k_ref[pl.ds(k0, nk), (h // 2) * _PAIR_W:(h // 2 + 1) * _PAIR_W]
        s = _dot_nt(qs[h][r0:r0 + nr], kb)
        if mask is not None:
            s = jnp.where(mask, s, NEG_BIG)
        m_old = m_scr[h, r0:r0 + nr]
        m_new = jnp.maximum(m_old, jnp.max(s, axis=1, keepdims=True))
        p = jnp.exp2(s - jnp.concatenate([m_new] * (nk // LANES), axis=1))
        vb = v_ref[pl.ds(k0, nk), h * LANES:(h + 1) * LANES]
        acc[h, r0:r0 + nr] = jnp.exp2(m_old - m_new) * acc[h, r0:r0 + nr] + _dot(p.astype(BF16), vb)
        m_scr[h, r0:r0 + nr] = m_new

    def body(j, _):
        ks = pl.multiple_of(j * tq, tq)
        for h in range(nhead):
            update(h, 0, tq, ks, tq, None)
        return 0

    lax.fori_loop(0, qi, body, 0)
    ks = pl.multiple_of(qi * tq, tq)
    tri = lax.broadcasted_iota(jnp.int32, (half, half), 0) >= lax.broadcasted_iota(jnp.int32, (half, half), 1)
    low = lax.broadcasted_iota(jnp.int32, (half, tq), 0) + half >= lax.broadcasted_iota(jnp.int32, (half, tq), 1)
    for h in range(nhead):
        update(h, 0, half, ks, half, tri)
        update(h, half, half, ks, tq, low)
    lane128 = lax.broadcasted_iota(jnp.int32, (tq, LANES), 1)
    for pr in range(_PAIRS_PER_STEP):
        outs = []
        for par in range(2):
            a = acc[2 * pr + par]
            outs.append(a / pltpu.roll(a, HEAD_DIM, axis=1))
        o_ref[:, pr * LANES:(pr + 1) * LANES] = jnp.where(lane128 < HEAD_DIM, outs[0], outs[1]).astype(BF16)


def _fox_prompt(qa, ka, va, nbatch, seq_total):
    tq = min(FOX_TQ, seq_total)
    nq = seq_total // tq
    w = _PAIRS_PER_STEP * _PAIR_W
    return pl.pallas_call(
        functools.partial(_fox_prompt_kernel, tq),
        grid=(nbatch, _N_PAIRS // _PAIRS_PER_STEP, nq),
        in_specs=[
            pl.BlockSpec((tq, w), lambda b, g, i: (b * nq + i, g)),
            pl.BlockSpec((seq_total, w), lambda b, g, i: (b, g)),
            pl.BlockSpec((seq_total, w), lambda b, g, i: (b, g)),
        ],
        out_specs=pl.BlockSpec((tq, _PAIRS_PER_STEP * LANES), lambda b, g, i: (b * nq + i, g)),
        out_shape=jax.ShapeDtypeStruct((nbatch * seq_total, FOX_W), BF16),
        scratch_shapes=[
            pltpu.VMEM((2 * _PAIRS_PER_STEP, tq, LANES), F32),
            pltpu.VMEM((2 * _PAIRS_PER_STEP, tq, LANES), F32),
        ],
        compiler_params=_params("arbitrary", "arbitrary", "arbitrary"),
        name="fox_prompt",
    )(qa, ka, va)


def _fox_sample_kernel(npg, tnew, pt_ref, qm_ref, *refs):
    del pt_ref
    k_refs = refs[0:npg]
    v_refs = refs[npg:2 * npg]
    lf_refs = refs[2 * npg:3 * npg]
    knew_ref, vnew_ref, lfnew_ref, u_ref, sel_ref, o_ref, m_scr, l_scr, acc, run, kb_scr, vb_scr = refs[3 * npg:]
    g = pl.program_id(1)
    nrow = tnew * H_FOX
    npage = knew_ref.shape[1]

    @pl.when(g == 0)
    def _():
        m_scr[...] = jnp.full(m_scr.shape, NEG_BIG, F32)
        l_scr[...] = jnp.zeros(l_scr.shape, F32)
        acc[...] = jnp.zeros(acc.shape, F32)
        run[...] = jnp.zeros(run.shape, F32)

    qm = qm_ref[0]

    def cum_within(lft):
        return _cumsum_lanes(lft, u_ref)

    def attend(k, v, ckeys, mask, transposed):
        qk = _dot(qm, k) if transposed else _dot_nt(qm, k)
        s = qk - jnp.concatenate([ckeys * LOG2E] * tnew, axis=0)
        if mask is not None:
            s = jnp.where(mask, s, NEG_BIG)
        m_old = m_scr[...]
        m_new = jnp.maximum(m_old, jnp.max(s, axis=1, keepdims=True))
        p = jnp.exp2(s - m_new)
        alpha = jnp.exp2(m_old - m_new)
        l_scr[...] = alpha * l_scr[...] + jnp.sum(p, axis=1, keepdims=True)
        pb = p.astype(BF16)
        acc[...] = alpha * acc[...] + (_dot_nt(pb, v) if transposed else _dot(pb, v))
        m_scr[...] = m_new

    base = run[...]
    cs = []
    for j in range(npg):
        kb_scr[:, j * npage:(j + 1) * npage] = k_refs[j][0, 0].astype(BF16)
        vb_scr[:, j * npage:(j + 1) * npage] = v_refs[j][0, 0].astype(BF16)
        cw = cum_within(lf_refs[j][0, 0])
        cs.append(base + cw)
        base = base + jnp.broadcast_to(cw[:, npage - 1:npage], base.shape)
    run[...] = base
    attend(kb_scr[...], vb_scr[...], jnp.concatenate(cs, axis=1), None, True)

    @pl.when(g == pl.num_programs(1) - 1)
    def _():
        rt = lax.broadcasted_iota(jnp.int32, (nrow, npage), 0) // H_FOX
        col = lax.broadcasted_iota(jnp.int32, (nrow, npage), 1)
        attend(knew_ref[0].astype(BF16), vnew_ref[0].astype(BF16), run[...] + cum_within(lfnew_ref[0]), col <= rt,
               False)
        o = acc[...] / l_scr[...]
        rh = lax.broadcasted_iota(jnp.int32, (nrow, FOX_W), 0) % H_FOX
        lh = lax.broadcasted_iota(jnp.int32, (nrow, FOX_W), 1) // HEAD_DIM
        om = jnp.where(rh == lh, o, 0.0)
        o_ref[0] = jnp.dot(sel_ref[...], om, precision=HIGHEST, preferred_element_type=F32)


def _fox_sample(layer, page_table, qm, cache_kt, cache_vt, cache_lft, knew, vnew, lfnew):
    nb, npages = page_table.shape
    page = cache_kt.shape[3]
    tnew = qm.shape[1] // H_FOX
    npg = PAGES_PER_STEP
    ngrp = npages // npg
    idx = np.arange(page)
    u = jnp.asarray(idx[:, None] <= idx[None, :], BF16)
    sel = jnp.asarray(np.arange(tnew)[:, None] == (np.arange(tnew * H_FOX)[None, :] // H_FOX), F32)

    def pmap(j):
        def f(b, g, pt):
            return (layer, pt[b * npages + g * npg + j], 0, 0)
        return f

    def bmap(b, g, pt):
        return (b, 0, 0)

    def c2(b, g, pt):
        return (0, 0)

    in_specs = [pl.BlockSpec((1, tnew * H_FOX, FOX_W), bmap)]
    in_specs += [pl.BlockSpec((1, 1, FOX_W, page), pmap(j)) for j in range(npg)]
    in_specs += [pl.BlockSpec((1, 1, FOX_W, page), pmap(j)) for j in range(npg)]
    in_specs += [pl.BlockSpec((1, 1, H_FOX, page), pmap(j)) for j in range(npg)]
    in_specs += [
        pl.BlockSpec((1, page, FOX_W), bmap),
        pl.BlockSpec((1, page, FOX_W), bmap),
        pl.BlockSpec((1, H_FOX, page), bmap),
        pl.BlockSpec((page, page), c2),
        pl.BlockSpec((tnew, tnew * H_FOX), c2),
    ]
    grid_spec = pltpu.PrefetchScalarGridSpec(
        num_scalar_prefetch=1,
        grid=(nb, ngrp),
        in_specs=in_specs,
        out_specs=pl.BlockSpec((1, tnew, FOX_W), bmap),
        scratch_shapes=[
            pltpu.VMEM((tnew * H_FOX, 1), F32),
            pltpu.VMEM((tnew * H_FOX, 1), F32),
            pltpu.VMEM((tnew * H_FOX, FOX_W), F32),
            pltpu.VMEM((H_FOX, page), F32),
            pltpu.VMEM((FOX_W, npg * page), BF16),
            pltpu.VMEM((FOX_W, npg * page), BF16),
        ],
    )
    args = [page_table.reshape(-1), qm] + [cache_kt] * npg + [cache_vt] * npg + [cache_lft] * npg
    args += [knew, vnew, lfnew, u, sel]
    return pl.pallas_call(
        functools.partial(_fox_sample_kernel, npg, tnew),
        grid_spec=grid_spec,
        out_shape=jax.ShapeDtypeStruct((nb, tnew, FOX_W), F32),
        compiler_params=_params("arbitrary", "arbitrary", vmem=VMEM_LIMIT_PAGED),
        name="fox_sample",
    )(*args)


_LANE_GROUP0 = N_EXPERTS


def _pack_bf16_pairs(x):
    n = x.shape[1] // 2
    hi = lax.bitcast_convert_type(x[:, :n].astype(BF16).astype(F32), jnp.uint32)
    lo = lax.bitcast_convert_type(x[:, n:].astype(BF16).astype(F32), jnp.uint32)
    return hi | (lo >> 16)


def _unpack_bf16_pairs(u):
    hi = lax.bitcast_convert_type(u & jnp.uint32(0xFFFF0000), F32)
    lo = lax.bitcast_convert_type(u << 16, F32)
    return jnp.concatenate([hi, lo], axis=1).astype(BF16)


def _out_proj_kernel(n_prompt_tiles, xp_ref, xs_ref, retp_ref, rets_ref, s5p_ref, s5s_ref, foxp_ref, foxs_ref, wo_ref,
                     wof_ref, g_ref, wr_ref, wrl_ref, wrf_ref, br_ref, tri_ref, x1_ref, h2_ref, info_ref, cnt_ref,
                     carry, logit_scr):
    i = pl.program_id(0)

    @pl.when(i == 0)
    def _():
        carry[...] = jnp.zeros_like(carry)

    def project(mm, x_ref, ret_ref, s5_ref, fox_ref, w_ref):
        x1 = x_ref[...] + mm.dot(ret_ref[...], w_ref[0:RET_W, :])
        x1 = x1 + mm.dot(s5_ref[...], w_ref[RET_W:RET_W + S5_W, :])
        x1_ref[...] = x1 + mm.dot(fox_ref[...], w_ref[RET_W + S5_W:, :])

    @pl.when(i < n_prompt_tiles)
    def _():
        project(_FAST, xp_ref, retp_ref, s5p_ref, foxp_ref, wo_ref)

    @pl.when(i >= n_prompt_tiles)
    def _():
        project(_EXACT, xs_ref, rets_ref, s5s_ref, foxs_ref, wof_ref)

    x1 = x1_ref[...]
    ms = jnp.mean(x1 * x1, axis=-1, keepdims=True)
    h2 = x1 * lax.rsqrt(ms + RMS_EPS) * g_ref[...]
    h2_ref[...] = _pack_bf16_pairs(h2)
    @pl.when(i < n_prompt_tiles)
    def _():
        h_hi = h2.astype(BF16)
        h_lo = (h2 - h_hi.astype(F32)).astype(BF16)
        logit_scr[...] = _dot(h_hi, wr_ref[...]) + _dot(h_hi, wrl_ref[...]) + _dot(h_lo, wr_ref[...])

    @pl.when(i >= n_prompt_tiles)
    def _():
        logit_scr[...] = _EXACT.dot(h2, wrf_ref[...])

    logits = logit_scr[...] + br_ref[...]
    tm = logits.shape[0]
    lane = lax.broadcasted_iota(jnp.int32, (tm, LANES), 1)
    big = jnp.int32(LANES)
    gmask = (lane >= _LANE_GROUP0) & (lane < _LANE_GROUP0 + N_EXPERT_GROUPS)
    gl = jnp.where(gmask, logits, NEG_BIG)
    gmax = jnp.max(gl, axis=1, keepdims=True)
    gsum = jnp.sum(jnp.where(gmask, jnp.exp(gl - gmax), 0.0), axis=1, keepdims=True)
    g_w = 1.0 / gsum
    g_idx = jnp.min(jnp.where(gl == gmax, lane, big), axis=1, keepdims=True) - _LANE_GROUP0
    lo = g_idx * EXPERTS_PER_GROUP
    emask = (lane >= lo) & (lane < lo + EXPERTS_PER_GROUP)
    el = jnp.where(emask, logits, NEG_BIG)
    e1 = jnp.max(el, axis=1, keepdims=True)
    idx1 = jnp.min(jnp.where(el == e1, lane, big), axis=1, keepdims=True)
    esum = jnp.sum(jnp.where(emask, jnp.exp(el - e1), 0.0), axis=1, keepdims=True)
    el2 = jnp.where(lane == idx1, NEG_BIG, el)
    e2 = jnp.max(el2, axis=1, keepdims=True)
    idx2 = jnp.min(jnp.where(el2 == e2, lane, big), axis=1, keepdims=True)
    p1 = 1.0 / esum
    p2 = jnp.exp(e2 - e1) / esum
    gate1 = g_w * p1 / (p1 + p2)
    gate2 = g_w * p2 / (p1 + p2)
    oh1 = lane == idx1
    oh2 = lane == idx2
    a = jnp.where(oh1 | oh2, 1.0, 0.0)
    cum = _dot(tri_ref[...], a.astype(BF16))
    before = cum - a + carry[...]
    rank1 = jnp.sum(jnp.where(oh1, before, 0.0), axis=1, keepdims=True)
    rank2 = jnp.sum(jnp.where(oh2, before, 0.0), axis=1, keepdims=True)
    carry[...] = carry[...] + cum[tm - 1:tm, :]
    cnt_ref[...] = carry[...]
    info = jnp.where(lane == 0, idx1.astype(F32), 0.0)
    info = jnp.where(lane == 1, idx2.astype(F32), info)
    info = jnp.where(lane == 2, rank1, info)
    info = jnp.where(lane == 3, rank2, info)
    info = jnp.where(lane == 4, gate1, info)
    info = jnp.where(lane == 5, gate2, info)
    info_ref[...] = info


def _out_proj(xp, xs, xs_tile0, ret_p, ret_s, s5_p, s5_s, fox_p, fox_s, w_out, g2, w_r, b_r, n_prompt_tiles):
    tm = TOKEN_TILE
    d = xp.shape[1]
    n = (n_prompt_tiles + ret_s.shape[0] // tm) * tm
    w_out_fast = w_out.astype(BF16)
    w_r_hi = w_r.astype(BF16)
    w_r_lo = (w_r - w_r_hi.astype(F32)).astype(BF16)
    idx = np.arange(tm)
    tri = jnp.asarray(idx[:, None] >= idx[None, :], BF16)

    def row(i):
        return (i, 0)

    def const(i):
        return (0, 0)

    def pmap(i):
        return (jnp.minimum(i, n_prompt_tiles - 1), 0)

    def smap(i):
        return (jnp.maximum(i - n_prompt_tiles, 0), 0)

    def xsmap(i):
        return (xs_tile0 + jnp.maximum(i - n_prompt_tiles, 0), 0)

    return pl.pallas_call(
        functools.partial(_out_proj_kernel, n_prompt_tiles),
        grid=(n // tm,),
        in_specs=[
            pl.BlockSpec((tm, d), pmap),
            pl.BlockSpec((tm, d), xsmap),
            pl.BlockSpec((tm, RET_W), pmap),
            pl.BlockSpec((tm, RET_W), smap),
            pl.BlockSpec((tm, S5_W), pmap),
            pl.BlockSpec((tm, S5_W), smap),
            pl.BlockSpec((tm, FOX_W), pmap),
            pl.BlockSpec((tm, FOX_W), smap),
            pl.BlockSpec((d, d), const),
            pl.BlockSpec((d, d), const),
            pl.BlockSpec((1, d), const),
            pl.BlockSpec((d, LANES), const),
            pl.BlockSpec((d, LANES), const),
            pl.BlockSpec((d, LANES), const),
            pl.BlockSpec((1, LANES), const),
            pl.BlockSpec((tm, tm), const),
        ],
        out_specs=(
            pl.BlockSpec((tm, d), row),
            pl.BlockSpec((tm, d // 2), row),
            pl.BlockSpec((tm, LANES), row),
            pl.BlockSpec((1, LANES), const),
        ),
        out_shape=(
            jax.ShapeDtypeStruct((n, d), F32),
            jax.ShapeDtypeStruct((n, d // 2), jnp.uint32),
            jax.ShapeDtypeStruct((n, LANES), F32),
            jax.ShapeDtypeStruct((1, LANES), F32),
        ),
        scratch_shapes=[pltpu.VMEM((1, LANES), F32), pltpu.VMEM((tm, LANES), F32)],
        compiler_params=_params("arbitrary"),
        name="out_proj_router",
    )(xp, xs, ret_p, ret_s, s5_p, s5_s, fox_p, fox_s, w_out_fast, w_out, g2, w_r_hi, w_r_lo, w_r, b_r, tri)


_ROW_DMA_UNROLL = 8


def _dispatch_kernel(dest_ref, h_ref, rows_in_ref, rows_ref, sem):
    del rows_in_ref
    i = pl.program_id(0)
    tm = h_ref.shape[0]

    def issue(r, _):
        t = i * tm + r
        for k in range(2):
            pltpu.make_async_copy(h_ref.at[pl.ds(r, 1)], rows_ref.at[pl.ds(dest_ref[2 * t + k], 1)], sem).start()
        return 0

    lax.fori_loop(0, tm, issue, 0, unroll=_ROW_DMA_UNROLL)
    for _ in range(2):
        pltpu.make_async_copy(h_ref, rows_ref.at[pl.ds(0, tm)], sem).wait()


def _dispatch(dest, h2, n_rows):
    n, d = h2.shape
    tm = TOKEN_TILE
    grid_spec = pltpu.PrefetchScalarGridSpec(
        num_scalar_prefetch=1,
        grid=(n // tm,),
        in_specs=[
            pl.BlockSpec((tm, d), lambda i, dest: (i, 0)),
            pl.BlockSpec(memory_space=pl.ANY),
        ],
        out_specs=pl.BlockSpec(memory_space=pl.ANY),
        scratch_shapes=[pltpu.SemaphoreType.DMA(())],
    )
    return pl.pallas_call(
        _dispatch_kernel,
        grid_spec=grid_spec,
        out_shape=jax.ShapeDtypeStruct((n_rows, d), h2.dtype),
        input_output_aliases={2: 0},
        compiler_params=_params("arbitrary"),
        name="moe_dispatch",
    )(dest, h2, jnp.zeros((n_rows, d), h2.dtype))


def _experts_kernel(be_ref, nu_ref, x_ref, w1_ref, w3_ref, w2_ref, y_ref, w1b, w3b, w2b):
    i = pl.program_id(0)
    prev = be_ref[jnp.maximum(i - 1, 0)]
    fresh = (i == 0) | (be_ref[i] != prev)
    active = i < nu_ref[0]

    @pl.when(active & fresh)
    def _():
        w1b[...] = w1_ref[0, 0].astype(BF16)
        w3b[...] = w3_ref[0, 0].astype(BF16)
        w2b[...] = w2_ref[0, 0].astype(BF16)

    @pl.when(active)
    def _():
        xb = _unpack_bf16_pairs(x_ref[...])
        h1 = _dot(xb, w1b[...])
        h3 = _dot(xb, w3b[...])
        a = (h1 * _sigmoid(h1) * h3).astype(BF16)
        y_ref[...] = _dot(a, w2b[...])

    @pl.when(jnp.logical_not(active))
    def _():
        y_ref[...] = jnp.zeros_like(y_ref)


def _experts(layer, blk_e, n_used, x_rows, w1, w3, w2):
    n_rows, dpacked = x_rows.shape
    d, de = w1.shape[2], w1.shape[3]
    nblk = n_rows // MOE_ROWS

    def last_used(i, nu):
        return jnp.minimum(i, jnp.maximum(nu[0] - 1, 0))

    def xmap(i, be, nu):
        return (last_used(i, nu), 0)

    def wmap(i, be, nu):
        return (layer, be[last_used(i, nu)], 0, 0)

    grid_spec = pltpu.PrefetchScalarGridSpec(
        num_scalar_prefetch=2,
        grid=(nblk,),
        in_specs=[
            pl.BlockSpec((MOE_ROWS, dpacked), xmap),
            pl.BlockSpec((1, 1, d, de), wmap),
            pl.BlockSpec((1, 1, d, de), wmap),
            pl.BlockSpec((1, 1, de, d), wmap),
        ],
        out_specs=pl.BlockSpec((MOE_ROWS, d), lambda i, be, nu: (i, 0)),
        scratch_shapes=[
            pltpu.VMEM((d, de), BF16),
            pltpu.VMEM((d, de), BF16),
            pltpu.VMEM((de, d), BF16),
        ],
    )
    return pl.pallas_call(
        _experts_kernel,
        grid_spec=grid_spec,
        out_shape=jax.ShapeDtypeStruct((n_rows, d), F32),
        compiler_params=_params("arbitrary"),
        name="moe_experts",
    )(blk_e, n_used, x_rows, w1, w3, w2)


def _combine_kernel(n_prompt_tiles, dest_ref, x1_ref, info_ref, gf_ref, y_ref, o_ref, *rest):
    os_ref = rest[0] if n_prompt_tiles is not None else None
    buf, sem = rest[-2:]
    i = pl.program_id(0)
    tm = x1_ref.shape[0]

    def issue(r, _):
        t = i * tm + r
        for k in range(2):
            pltpu.make_async_copy(y_ref.at[pl.ds(dest_ref[2 * t + k], 1)], buf.at[k, pl.ds(r, 1)], sem).start()
        return 0

    lax.fori_loop(0, tm, issue, 0, unroll=_ROW_DMA_UNROLL)
    for k in range(2):
        pltpu.make_async_copy(y_ref.at[pl.ds(0, tm)], buf.at[k], sem).wait()
    info = info_ref[...]
    x2 = x1_ref[...] + info[:, 4:5] * buf[0] + info[:, 5:6] * buf[1]
    if n_prompt_tiles is None:
        o_ref[...] = x2
    else:
        ms = jnp.mean(x2 * x2, axis=-1, keepdims=True)
        y = x2 * lax.rsqrt(ms + RMS_EPS) * gf_ref[...]

        @pl.when(i < n_prompt_tiles)
        def _():
            o_ref[...] = y

        @pl.when(i >= n_prompt_tiles)
        def _():
            os_ref[...] = y


def _combine(dest, x1, info, gf, y_rows, n_prompt_tiles):
    n, d = x1.shape
    tm = TOKEN_TILE
    if n_prompt_tiles is None:
        out_specs = pl.BlockSpec((tm, d), lambda i, dest: (i, 0))
        out_shape = jax.ShapeDtypeStruct((n, d), F32)
    else:
        npt = n_prompt_tiles
        out_specs = (pl.BlockSpec((tm, d), lambda i, dest: (jnp.minimum(i, npt - 1), 0)),
                     pl.BlockSpec((tm, d), lambda i, dest: (jnp.maximum(i - npt, 0), 0)))
        out_shape = (jax.ShapeDtypeStruct((npt * tm, d), F32), jax.ShapeDtypeStruct((n - npt * tm, d), F32))
    grid_spec = pltpu.PrefetchScalarGridSpec(
        num_scalar_prefetch=1,
        grid=(n // tm,),
        in_specs=[
            pl.BlockSpec((tm, d), lambda i, dest: (i, 0)),
            pl.BlockSpec((tm, LANES), lambda i, dest: (i, 0)),
            pl.BlockSpec((1, d), lambda i, dest: (0, 0)),
            pl.BlockSpec(memory_space=pl.ANY),
        ],
        out_specs=out_specs,
        scratch_shapes=[
            pltpu.VMEM((2, tm, d), F32),
            pltpu.SemaphoreType.DMA(()),
        ],
    )
    return pl.pallas_call(
        functools.partial(_combine_kernel, n_prompt_tiles),
        grid_spec=grid_spec,
        out_shape=out_shape,
        compiler_params=_params("arbitrary"),
        name="moe_combine",
    )(dest, x1, info, gf, y_rows)


def _moe_plan(info, counts, n_tokens):
    eid = info[:, 0:2].astype(jnp.int32)
    rank = info[:, 2:4].astype(jnp.int32)
    cnt = counts[0, :N_EXPERTS].astype(jnp.int32)
    padded = (cnt + MOE_ROWS - 1) // MOE_ROWS * MOE_ROWS
    pend = jnp.cumsum(padded)
    pstart = pend - padded
    experts = jnp.arange(N_EXPERTS, dtype=jnp.int32)
    start_of = jnp.sum(jnp.where(eid[..., None] == experts, pstart, 0), axis=-1)
    dest = (start_of + rank).reshape(-1)
    nblk = -(-n_tokens * 2 // MOE_ROWS) + N_EXPERTS
    blk_start = jnp.arange(nblk, dtype=jnp.int32) * MOE_ROWS
    blk_e = jnp.minimum(jnp.sum((pend[None, :] <= blk_start[:, None]).astype(jnp.int32), axis=1), N_EXPERTS - 1)
    n_used = (pend[-1:] // MOE_ROWS).astype(jnp.int32)
    return dest.astype(jnp.int32), blk_e.astype(jnp.int32), n_used, nblk * MOE_ROWS


def _rope_tables(seq, past, tnew, tile):
    half = HEAD_DIM // 2
    inv = ROPE_THETA ** (-jnp.arange(half, dtype=F32) / half)
    pos = jnp.concatenate([jnp.arange(seq, dtype=F32), past + (jnp.arange(tile) % tnew).astype(F32)])
    ang = pos[:, None] * inv[None, :]
    cos = jnp.cos(ang)
    sin = jnp.sin(ang)
    cos_h = jnp.concatenate([cos, cos], axis=1)
    sin_h = jnp.concatenate([-sin, sin], axis=1)
    return jnp.tile(cos_h, (1, H_RET)), jnp.tile(sin_h, (1, H_RET))


def _block_diag_states(s):
    b = s.shape[0]
    eye = jnp.eye(H_RET, dtype=s.dtype)
    return jnp.einsum('bhde,hg->bhdge', s, eye).reshape(b, RET_W, RET_W)


def _diag_blocks(s_bd):
    b = s_bd.shape[0]
    s5 = s_bd.reshape(b, H_RET, HEAD_DIM, H_RET, HEAD_DIM)
    return jnp.stack([s5[:, h, :, h, :] for h in range(H_RET)], axis=1)


def kernel(x_prompt, x_sample, cache_k, cache_v, cache_logf, state_ret, state_s5, page_table, norm1, w_in, ret_gn,
           s5_lam_re, s5_lam_im, s5_b_re, s5_b_im, s5_c_re, s5_c_im, s5_d, s5_log_dt, s5_glu_w, s5_glu_b, fox_fb,
           w_out, norm2, w_rg, b_rg, w_re, b_re, w1, w3, w2, norm_f):
    bp, seq, d = x_prompt.shape
    bd_, tnew, _ = x_sample.shape
    depth = w_in.shape[0]
    n_phys, page = cache_k.shape[1], cache_k.shape[2]
    npages = page_table.shape[1]
    past = npages * page
    n_p = bp * seq
    n_s = bd_ * tnew
    n = n_p + n_s
    tm = TOKEN_TILE
    assert n_p % tm == 0 and n_s == tm and seq % tm == 0
    n_prompt_tiles = n_p // tm

    xp, xs, xs_tile0 = x_prompt.reshape(n_p, d), x_sample.reshape(n_s, d), 0
    cos_t, sin_t = _rope_tables(seq, past, tnew, tm)
    cache_kt = jnp.transpose(cache_k, (0, 1, 3, 4, 2)).reshape(depth, n_phys, FOX_W, page)
    cache_vt = jnp.transpose(cache_v, (0, 1, 3, 4, 2)).reshape(depth, n_phys, FOX_W, page)
    cache_lft = jnp.swapaxes(cache_logf, 2, 3)
    ff0 = w_in.shape[2] - H_FOX

    outs = {k: [] for k in ('kp', 'vp', 'lp', 'ks', 'vs', 'ls', 'rp', 'rs', 'sp', 'ss')}
    for l in range(depth):
        w_main = w_in[l, :, :ff0]
        w_ff = jnp.pad(w_in[l, :, ff0:], ((0, 0), (0, LANES - H_FOX)))
        fb = jnp.pad(fox_fb[l], (0, LANES - H_FOX))[None, :]
        g1 = norm1[l][None, :]
        ret_in, su, fq, fk, fv, fkb, fvb, logf = _in_proj(
            _FAST, xp, g1, w_main.astype(BF16), w_ff.astype(BF16), fb, cos_t, sin_t,
            0, n_prompt_tiles, 0, seq // tm)
        ret_in_s, su_s, fq_s, fk_s, fv_s, _, _, logf_s = _in_proj(
            _EXACT, xs, g1, w_main, w_ff, fb, cos_t, sin_t, xs_tile0, 1, seq // tm, 1)

        gw = ret_gn[l][None, :]
        zero_state = jnp.zeros((bp, RET_W, RET_W), F32)
        ret_p, rs_p = _retention(_FAST, ret_in, zero_state, gw, bp, 0, seq, 1, min(RET_CHUNK, seq),
                                 min(1024, seq))
        seq_per_chunk = 8
        chunk_rows = seq_per_chunk * tnew
        ret_s, rs_s = _retention(_EXACT, ret_in_s, _block_diag_states(state_ret[l]), gw, bd_ // seq_per_chunk, 0,
                                 chunk_rows, seq_per_chunk, tnew, chunk_rows)

        tables = _s5_tables(s5_lam_re[l], s5_lam_im[l], s5_b_re[l], s5_b_im[l], s5_c_re[l], s5_c_im[l],
                            s5_log_dt[l])
        dsk = s5_d[l][None, :]
        glub = s5_glu_b[l][None, :]
        zs = jnp.zeros((bp, 1, S5_LANES), F32)
        s5_p, sp_r, sp_i = _s5(_FAST, su, zs, zs, tables, dsk, s5_glu_w[l], glub, bp, 0, seq, min(512, seq),
                               False)
        x0 = state_s5[l].reshape(bd_, 1, S5_LANES, 2)
        s5_s, ss_r, ss_i = _s5(_EXACT, su_s, x0[..., 0], x0[..., 1], tables, dsk, s5_glu_w[l], glub, 1, 0, n_s,
                               n_s, True)

        qa, ka, va = _fox_prep(logf, fq, fkb, fvb, bp, seq, min(256, seq))
        fox_p = _fox_prompt(qa, ka, va, bp, seq)
        q_s = fq_s.reshape(bd_, tnew, 1, FOX_W)
        hmask = jnp.asarray(np.arange(H_FOX)[:, None] == (np.arange(FOX_W) // HEAD_DIM)[None, :])[None, None]
        qm = jnp.where(hmask, q_s, jnp.zeros((), BF16)).reshape(bd_, tnew * H_FOX, FOX_W)
        pad_rows = ((0, 0), (0, page - tnew), (0, 0))
        knew = jnp.pad(fk_s.reshape(bd_, tnew, FOX_W), pad_rows)
        vnew = jnp.pad(fv_s.reshape(bd_, tnew, FOX_W), pad_rows)
        lfnew = jnp.pad(jnp.swapaxes(logf_s[:, :H_FOX].reshape(bd_, tnew, H_FOX), 1, 2),
                        ((0, 0), (0, 0), (0, page - tnew)))
        fox_s = _fox_sample(l, page_table, qm, cache_kt, cache_vt, cache_lft, knew, vnew, lfnew)
        fox_s = fox_s.reshape(n_s, FOX_W)

        w_r = jnp.pad(jnp.concatenate([w_re[l], w_rg[l]], axis=1),
                      ((0, 0), (0, LANES - N_EXPERTS - N_EXPERT_GROUPS)))
        b_r = jnp.pad(jnp.concatenate([b_re[l], b_rg[l]]), (0, LANES - N_EXPERTS - N_EXPERT_GROUPS))[None, :]
        x1, h2, info, counts = _out_proj(xp, xs, xs_tile0, ret_p, ret_s, s5_p, s5_s, fox_p, fox_s, w_out[l],
                                         norm2[l][None, :], w_r, b_r, n_prompt_tiles)
        dest, blk_e, n_used, n_rows = _moe_plan(info, counts, n)
        x_rows = _dispatch(dest, h2, n_rows)
        y_rows = _experts(l, blk_e, n_used, x_rows, w1, w3, w2)
        if l < depth - 1:
            x_all = _combine(dest, x1, info, norm_f[None, :], y_rows, None)
            xp, xs, xs_tile0 = x_all, x_all, n_prompt_tiles
        else:
            y_p, y_s = _combine(dest, x1, info, norm_f[None, :], y_rows, n_prompt_tiles)

        outs['kp'].append(fk)
        outs['vp'].append(fv)
        outs['lp'].append(logf[:, :H_FOX].reshape(bp, seq, H_FOX))
        outs['ks'].append(fk_s.reshape(bd_, tnew, H_FOX, HEAD_DIM))
        outs['vs'].append(fv_s.reshape(bd_, tnew, H_FOX, HEAD_DIM))
        outs['ls'].append(logf_s[:, :H_FOX].reshape(bd_, tnew, H_FOX))
        outs['rp'].append(_diag_blocks(rs_p))
        outs['rs'].append(_diag_blocks(rs_s))
        outs['sp'].append(jnp.stack([sp_r.reshape(bp, S5_GROUPS, S5_STATE),
                                     sp_i.reshape(bp, S5_GROUPS, S5_STATE)], axis=-1))
        outs['ss'].append(jnp.stack([ss_r.reshape(bd_, S5_GROUPS, S5_STATE),
                                     ss_i.reshape(bd_, S5_GROUPS, S5_STATE)], axis=-1))

    y_prompt = y_p.reshape(bp, seq, d)
    y_sample = y_s.reshape(bd_, tnew, d)
    st = {k: jnp.stack(v) for k, v in outs.items()}
    for name in ('kp', 'vp'):
        st[name] = st[name].reshape(depth, bp, seq, H_FOX, HEAD_DIM)
    return (y_prompt, y_sample, st['kp'], st['vp'], st['lp'], st['ks'], st['vs'], st['ls'],
            st['rp'], st['rs'], st['sp'], st['ss'])
```

```python
import functools
import math

import numpy as np

import jax
import jax.numpy as jnp
from jax import lax
from jax.experimental import pallas as pl
from jax.experimental.pallas import tpu as pltpu

F32 = jnp.float32
BF16 = jnp.bfloat16
HIGHEST = lax.Precision.HIGHEST

HEAD_DIM = 64
H_RET = 4
H_FOX = 8
RET_W = H_RET * HEAD_DIM
S5_W = 256
FOX_W = H_FOX * HEAD_DIM
S5_GROUPS = 16
S5_STATE = 64
S5_LANES = S5_GROUPS * S5_STATE
N_EXPERTS = 32
N_EXPERT_GROUPS = 4
EXPERTS_PER_GROUP = 8
RET_CHUNK = 128
ROPE_THETA = 10000.0
RMS_EPS = 1e-6
NEG_BIG = -1e30
LOG2E = math.log2(math.e)

LANES = 128
SUBLANES = 8
TOKEN_TILE = 256
MOE_ROWS = 256
FOX_TQ = 1024
PAGES_PER_STEP = 32
VMEM_LIMIT = 48 * 1024 * 1024
VMEM_LIMIT_PAGED = 56 * 1024 * 1024


def _params(*sem, vmem=VMEM_LIMIT):
    return pltpu.CompilerParams(dimension_semantics=sem, vmem_limit_bytes=vmem)


def _sigmoid(x):
    return 1.0 / (1.0 + jnp.exp(-x))


def _dot(a, b):
    return jnp.dot(a, b, preferred_element_type=F32)


def _dot_nt(a, b):
    return lax.dot_general(a, b, (((1,), (1,)), ((), ())), preferred_element_type=F32)


class _MatmulMode:
    def __init__(self, exact):
        self.dtype = F32 if exact else BF16
        self.precision = HIGHEST if exact else None

    def cast(self, x):
        return x.astype(self.dtype)

    def dot(self, a, b):
        return jnp.dot(a, b, precision=self.precision, preferred_element_type=F32)

    def dot_nt(self, a, b):
        return lax.dot_general(a, b, (((1,), (1,)), ((), ())), precision=self.precision,
                               preferred_element_type=F32)

    def dot_tn(self, a, b):
        return lax.dot_general(a, b, (((0,), (0,)), ((), ())), precision=self.precision,
                               preferred_element_type=F32)


_FAST = _MatmulMode(False)
_EXACT = _MatmulMode(True)


def _in_proj_kernel(mm, x_ref, g_ref, w_ref, wff_ref, fb_ref, cos_ref, sin_ref,
                    ret_ref, su_ref, fq_ref, fk_ref, fv_ref, fkb_ref, fvb_ref, logf_ref):
    _dot = mm.dot
    x = x_ref[...]
    ms = jnp.mean(x * x, axis=-1, keepdims=True)
    h = mm.cast(x * lax.rsqrt(ms + RMS_EPS) * g_ref[...])
    tm = x.shape[0]
    cos = cos_ref[...]
    sin = sin_ref[...]
    lane = lax.broadcasted_iota(jnp.int32, (tm, RET_W), 1)
    first_half = (lane % HEAD_DIM) < (HEAD_DIM // 2)

    def rope(t):
        swapped = jnp.where(first_half,
                            pltpu.roll(t, RET_W - HEAD_DIM // 2, axis=1),
                            pltpu.roll(t, HEAD_DIM // 2, axis=1))
        return t * cos + swapped * sin

    ret_ref[:, 0:RET_W] = rope(_dot(h, w_ref[:, 0:RET_W]))
    ret_ref[:, RET_W:2 * RET_W] = rope(_dot(h, w_ref[:, RET_W:2 * RET_W])) * (HEAD_DIM ** -0.5)
    ret_ref[:, 2 * RET_W:4 * RET_W] = _dot(h, w_ref[:, 2 * RET_W:4 * RET_W])
    c0 = 4 * RET_W
    su_ref[...] = _dot(h, w_ref[:, c0:c0 + S5_W])
    c0 += S5_W
    fq_ref[...] = (_dot(h, w_ref[:, c0:c0 + FOX_W]) * (HEAD_DIM ** -0.5 * LOG2E)).astype(BF16)
    c0 += FOX_W
    fk = _dot(h, w_ref[:, c0:c0 + FOX_W])
    fk_ref[...] = fk
    fkb_ref[...] = fk.astype(BF16)
    c0 += FOX_W
    fv = _dot(h, w_ref[:, c0:c0 + FOX_W])
    fv_ref[...] = fv
    fvb_ref[...] = fv.astype(BF16)
    z = _dot(h, wff_ref[...]) + fb_ref[...]
    logf = jnp.minimum(z, 0.0) - jnp.log1p(jnp.exp(-jnp.abs(z)))
    lane128 = lax.broadcasted_iota(jnp.int32, (tm, LANES), 1)
    logf_ref[...] = jnp.where(lane128 < H_FOX, logf, 0.0)


def _in_proj(mm, x, g, w_main, w_ff, fb, cos_t, sin_t, tile0, ntiles, pos_tile0, pos_tiles):
    tm = TOKEN_TILE
    n = ntiles * tm
    wcols = w_main.shape[1]

    def xrow(i):
        return (tile0 + i, 0)

    def row(i):
        return (i, 0)

    def const(i):
        return (0, 0)

    def pos_map(i):
        return (pos_tile0 + i % pos_tiles, 0)

    outs = (
        jax.ShapeDtypeStruct((n, 4 * RET_W), F32),
        jax.ShapeDtypeStruct((n, S5_W), F32),
        jax.ShapeDtypeStruct((n, FOX_W), BF16),
        jax.ShapeDtypeStruct((n, FOX_W), F32),
        jax.ShapeDtypeStruct((n, FOX_W), F32),
        jax.ShapeDtypeStruct((n, FOX_W), BF16),
        jax.ShapeDtypeStruct((n, FOX_W), BF16),
        jax.ShapeDtypeStruct((n, LANES), F32),
    )
    return pl.pallas_call(
        functools.partial(_in_proj_kernel, mm),
        grid=(ntiles,),
        in_specs=[
            pl.BlockSpec((tm, x.shape[1]), xrow),
            pl.BlockSpec((1, x.shape[1]), const),
            pl.BlockSpec((x.shape[1], wcols), const),
            pl.BlockSpec((x.shape[1], LANES), const),
            pl.BlockSpec((1, LANES), const),
            pl.BlockSpec((tm, RET_W), pos_map),
            pl.BlockSpec((tm, RET_W), pos_map),
        ],
        out_specs=(
            pl.BlockSpec((tm, 4 * RET_W), row),
            pl.BlockSpec((tm, S5_W), row),
            pl.BlockSpec((tm, FOX_W), row),
            pl.BlockSpec((tm, FOX_W), row),
            pl.BlockSpec((tm, FOX_W), row),
            pl.BlockSpec((tm, FOX_W), row),
            pl.BlockSpec((tm, FOX_W), row),
            pl.BlockSpec((tm, LANES), row),
        ),
        out_shape=outs,
        compiler_params=_params("arbitrary"),
        name="in_proj",
    )(x, g, w_main, w_ff, fb, cos_t, sin_t)


def _retention_kernel(mm, nseq, seq_len, nchunk, x_ref, s0_ref, dm_ref, qd_ref, kd_ref, cd_ref, bd_ref,
                      gn_ref, gw_ref, o_ref, sout_ref, s_scr):
    cast, _dot, _dot_nt, _dot_tn = mm.cast, mm.dot, mm.dot_nt, mm.dot_tn
    step = pl.program_id(1)
    rows = nseq * seq_len

    @pl.when(step == 0)
    def _():
        s_scr[...] = s0_ref[...]

    lane_head = lax.broadcasted_iota(jnp.int32, (rows, RET_W), 1) // HEAD_DIM
    row_seq = lax.broadcasted_iota(jnp.int32, (rows, 1), 0) // seq_len

    def chunk(c, _):
        r0 = pl.multiple_of(c * rows, rows)
        q = x_ref[pl.ds(r0, rows), 0:RET_W]
        k = x_ref[pl.ds(r0, rows), RET_W:2 * RET_W]
        v = x_ref[pl.ds(r0, rows), 2 * RET_W:3 * RET_W]
        g = x_ref[pl.ds(r0, rows), 3 * RET_W:4 * RET_W]
        qb = cast(q)
        kb = cast(k)
        vb = cast(v)
        ps = []
        vs = []
        for h in range(H_RET):
            hm = lane_head == h
            sc = _dot_nt(cast(jnp.where(hm, q, 0.0)), kb)
            ps.append(cast(sc * dm_ref[h]))
            vs.append(cast(jnp.where(hm, v, 0.0)))
        inner = _dot(jnp.concatenate(ps, axis=1), jnp.concatenate(vs, axis=0))
        kdec = k * kd_ref[...]
        cd = cd_ref[...]
        bd = bd_ref[...]
        if nseq == 1:
            s_old = s_scr[0]
            cross = _dot(qb, cast(s_old))
            s_scr[0] = s_old * cd + _dot_tn(cast(kdec), vb) * bd
        else:
            def seq_body(s, cross):
                rm = row_seq == s
                s_old = s_scr[s]
                cross = jnp.where(rm, _dot(qb, cast(s_old)), cross)
                kv = _dot_tn(cast(jnp.where(rm, kdec, 0.0)), vb)
                s_scr[s] = s_old * cd + kv * bd
                return cross

            cross = lax.fori_loop(0, nseq, seq_body, jnp.zeros((rows, RET_W), F32))
        o = inner + cross * qd_ref[...]
        o2 = o * o
        hi = o2.astype(BF16)
        lo = (o2 - hi.astype(F32)).astype(BF16)
        ms = jnp.dot(hi, gn_ref[...], preferred_element_type=F32) + jnp.dot(
            lo, gn_ref[...], preferred_element_type=F32)
        on = o * lax.rsqrt(ms + RMS_EPS)
        o_ref[pl.ds(r0, rows), :] = (g * _sigmoid(g) * on * gw_ref[...]).astype(o_ref.dtype)
        return 0

    lax.fori_loop(0, nchunk, chunk, 0)

    @pl.when(step == pl.num_programs(1) - 1)
    def _():
        sout_ref[...] = s_scr[...]


def _retention_tables(nseq, seq_len):
    f32 = np.float32
    rows = nseq * seq_len
    log_gamma = np.log1p(-np.power(f32(2.0), -5.0 - np.arange(H_RET, dtype=f32))).astype(f32)
    idx = np.arange(rows)
    pos = (idx % seq_len).astype(f32)
    seq = idx // seq_len
    diff = pos[:, None] - pos[None, :]
    ok = (seq[:, None] == seq[None, :]) & (diff >= 0)
    dm = np.where(ok[None], np.exp(log_gamma[:, None, None] * np.maximum(diff, f32(0.0))[None]), f32(0.0))
    lg_lane = np.repeat(log_gamma, HEAD_DIM)
    qd = np.exp((pos[:, None] + f32(1.0)) * lg_lane[None, :])
    kd = np.exp((f32(seq_len - 1.0) - pos[:, None]) * lg_lane[None, :])
    cd = np.exp(f32(seq_len) * lg_lane)[None, :]
    head = np.arange(RET_W) // HEAD_DIM
    bd = (head[:, None] == head[None, :]).astype(f32)
    to = lambda a: jnp.asarray(a, F32)
    return to(dm), to(qd), to(kd), to(cd), to(bd), jnp.asarray(bd / HEAD_DIM, BF16)


def _retention(mm, ret_in, s0_bd, gw, nbatch, row0, seq_total, nseq, seq_len, rows_per_step):
    rows = nseq * seq_len
    nchunk = rows_per_step // rows
    steps = seq_total // rows_per_step
    blk0 = row0 // rows_per_step
    dm, qd, kd, cd, bd, gn = _retention_tables(nseq, seq_len)

    def rmap(b, i):
        return (blk0 + b * steps + i, 0)

    def omap(b, i):
        return (b * steps + i, 0)

    def c2(b, i):
        return (0, 0)

    def c3(b, i):
        return (0, 0, 0)

    def smap(b, i):
        return (b, 0, 0)

    in_specs = [
        pl.BlockSpec((rows_per_step, 4 * RET_W), rmap),
        pl.BlockSpec((nseq, RET_W, RET_W), smap),
        pl.BlockSpec((H_RET, rows, rows), c3),
        pl.BlockSpec((rows, RET_W), c2),
        pl.BlockSpec((rows, RET_W), c2),
        pl.BlockSpec((1, RET_W), c2),
        pl.BlockSpec((RET_W, RET_W), c2),
        pl.BlockSpec((RET_W, RET_W), c2),
        pl.BlockSpec((1, RET_W), c2),
    ]
    return pl.pallas_call(
        functools.partial(_retention_kernel, mm, nseq, seq_len, nchunk),
        grid=(nbatch, steps),
        in_specs=in_specs,
        out_specs=(
            pl.BlockSpec((rows_per_step, RET_W), omap),
            pl.BlockSpec((nseq, RET_W, RET_W), smap),
        ),
        out_shape=(
            jax.ShapeDtypeStruct((nbatch * seq_total, RET_W), mm.dtype),
            jax.ShapeDtypeStruct((nbatch * nseq, RET_W, RET_W), F32),
        ),
        scratch_shapes=[pltpu.VMEM((nseq, RET_W, RET_W), F32)],
        compiler_params=_params("arbitrary", "arbitrary"),
        name="retention",
    )(ret_in, s0_bd, dm, qd, kd, cd, bd, gn, gw)


def _s5_kernel(mm, short, u_ref, x0r_ref, x0i_ref, bre_ref, bim_ref, pre_ref, pim_ref, cm_ref, d_ref,
               gw_ref, gb_ref, o_ref, str_ref, sti_ref, bur, bui, car, cai):
    cast, _dot = mm.cast, mm.dot
    step = pl.program_id(1)
    rows = u_ref.shape[0]
    u = u_ref[...]
    ub = cast(u)
    bur[...] = _dot(ub, bre_ref[...])
    bui[...] = _dot(ub, bim_ref[...])

    if not short:
        @pl.when(step == 0)
        def _():
            car[...] = x0r_ref[0]
            cai[...] = x0i_ref[0]

    pr = pre_ref[...]
    pi = pim_ref[...]
    rowi = lax.broadcasted_iota(jnp.int32, (SUBLANES, S5_LANES), 0)

    def group(j, carry):
        cr, ci = carry
        r0 = pl.multiple_of(j * SUBLANES, SUBLANES)
        xr = bur[pl.ds(r0, SUBLANES), :]
        xi = bui[pl.ds(r0, SUBLANES), :]
        for s in (1, 2, 4):
            ar = pr[s - 1:s]
            ai = pi[s - 1:s]
            sr = jnp.where(rowi >= s, pltpu.roll(xr, s, axis=0), 0.0)
            si = jnp.where(rowi >= s, pltpu.roll(xi, s, axis=0), 0.0)
            xr, xi = xr + ar * sr - ai * si, xi + ar * si + ai * sr
        if short:
            cr = x0r_ref[j]
            ci = x0i_ref[j]
        xr, xi = xr + pr * cr - pi * ci, xi + pr * ci + pi * cr
        bur[pl.ds(r0, SUBLANES), :] = xr
        bui[pl.ds(r0, SUBLANES), :] = xi
        cr = xr[SUBLANES - 1:SUBLANES]
        ci = xi[SUBLANES - 1:SUBLANES]
        if short:
            str_ref[j] = cr
            sti_ref[j] = ci
        return cr, ci

    if short:
        init = (jnp.zeros((1, S5_LANES), F32), jnp.zeros((1, S5_LANES), F32))
    else:
        init = (car[...], cai[...])
    cr, ci = lax.fori_loop(0, rows // SUBLANES, group, init)
    if not short:
        car[...] = cr
        cai[...] = ci

        @pl.when(step == pl.num_programs(1) - 1)
        def _():
            str_ref[0] = cr
            sti_ref[0] = ci

    xs = jnp.concatenate([cast(bur[...]), cast(bui[...])], axis=1)
    y = _dot(xs, cm_ref[...]) + d_ref[...] * u
    yg = 0.5 * y * (1.0 + jnp.tanh(math.sqrt(2.0 / math.pi) * (y + 0.044715 * (y * y * y))))
    z = _dot(cast(yg), gw_ref[...]) + gb_ref[...]
    o_ref[...] = (yg * _sigmoid(z)).astype(o_ref.dtype)


def _s5_tables(lam_re, lam_im, b_re, b_im, c_re, c_im, log_dt):
    lam = lax.complex(lam_re.astype(F32), lam_im.astype(F32))
    dt = jnp.exp(log_dt.astype(F32))[:, None]
    lam_bar = jnp.exp(lam * dt)
    b_bar = ((lam_bar - 1.0) / lam)[:, :, None] * lax.complex(b_re.astype(F32), b_im.astype(F32))
    k = jnp.arange(1, SUBLANES + 1, dtype=F32)[:, None, None]
    powers = jnp.exp((lam * dt)[None] * k)
    pre = jnp.real(powers).reshape(SUBLANES, S5_LANES)
    pim = jnp.imag(powers).reshape(SUBLANES, S5_LANES)
    eye = jnp.eye(S5_GROUPS, dtype=F32)
    bre = jnp.einsum('gpc,gh->gchp', jnp.real(b_bar), eye).reshape(S5_W, S5_LANES)
    bim = jnp.einsum('gpc,gh->gchp', jnp.imag(b_bar), eye).reshape(S5_W, S5_LANES)
    cre = jnp.einsum('gcp,gh->hpgc', c_re.astype(F32), eye).reshape(S5_LANES, S5_W)
    cim = jnp.einsum('gcp,gh->hpgc', c_im.astype(F32), eye).reshape(S5_LANES, S5_W)
    cm = jnp.concatenate([cre, -cim], axis=0)
    return bre, bim, pre, pim, cm


def _s5(mm, su, x0r, x0i, tables, d, glu_w, glu_b, nbatch, row0, seq_total, rows_per_step, short):
    bre, bim, pre, pim, cm = tables
    bre, bim, cm, glu_w = mm.cast(bre), mm.cast(bim), mm.cast(cm), mm.cast(glu_w)
    steps = seq_total // rows_per_step
    blk0 = row0 // rows_per_step
    nstate = x0r.shape[0]
    sblk = nstate if short else 1

    def rmap(b, i):
        return (blk0 + b * steps + i, 0)

    def omap(b, i):
        return (b * steps + i, 0)

    def c2(b, i):
        return (0, 0)

    def smap(b, i):
        return (0 if short else b, 0, 0)

    in_specs = [
        pl.BlockSpec((rows_per_step, S5_W), rmap),
        pl.BlockSpec((sblk, 1, S5_LANES), smap),
        pl.BlockSpec((sblk, 1, S5_LANES), smap),
        pl.BlockSpec((S5_W, S5_LANES), c2),
        pl.BlockSpec((S5_W, S5_LANES), c2),
        pl.BlockSpec((SUBLANES, S5_LANES), c2),
        pl.BlockSpec((SUBLANES, S5_LANES), c2),
        pl.BlockSpec((2 * S5_LANES, S5_W), c2),
        pl.BlockSpec((1, S5_W), c2),
        pl.BlockSpec((S5_W, S5_W), c2),
        pl.BlockSpec((1, S5_W), c2),
    ]
    return pl.pallas_call(
        functools.partial(_s5_kernel, mm, short),
        grid=(nbatch, steps),
        in_specs=in_specs,
        out_specs=(
            pl.BlockSpec((rows_per_step, S5_W), omap),
            pl.BlockSpec((sblk, 1, S5_LANES), smap),
            pl.BlockSpec((sblk, 1, S5_LANES), smap),
        ),
        out_shape=(
            jax.ShapeDtypeStruct((nbatch * seq_total, S5_W), mm.dtype),
            jax.ShapeDtypeStruct((nstate, 1, S5_LANES), F32),
            jax.ShapeDtypeStruct((nstate, 1, S5_LANES), F32),
        ),
        scratch_shapes=[
            pltpu.VMEM((rows_per_step, S5_LANES), F32),
            pltpu.VMEM((rows_per_step, S5_LANES), F32),
            pltpu.VMEM((1, S5_LANES), F32),
            pltpu.VMEM((1, S5_LANES), F32),
        ],
        compiler_params=_params("arbitrary", "arbitrary"),
        name="s5",
    )(su, x0r, x0i, bre, bim, pre, pim, cm, d, glu_w, glu_b)


_PAIR_W = 2 * LANES
_AUG_PER_HEAD = 6
_N_PAIRS = H_FOX // 2


def _aug_placement():
    pq = np.zeros((3 * LANES, _N_PAIRS * LANES), np.float32)
    pk = np.zeros((3 * LANES, _N_PAIRS * LANES), np.float32)
    oq = np.zeros((1, _N_PAIRS * LANES), np.float32)
    ok = np.zeros((1, _N_PAIRS * LANES), np.float32)
    for h in range(H_FOX):
        base = (h // 2) * LANES + (h % 2) * _AUG_PER_HEAD
        for part in range(3):
            pq[part * LANES + h, base + part] = 1.0
            oq[0, base + 3 + part] = 1.0
            ok[0, base + part] = 1.0
            pk[part * LANES + h, base + 3 + part] = -1.0
    return (jnp.asarray(pq, BF16), jnp.asarray(pk, BF16), jnp.asarray(oq), jnp.asarray(ok))


def _split3(v, axis=1):
    hi = v.astype(BF16)
    r1 = v - hi.astype(F32)
    mid = r1.astype(BF16)
    lo = (r1 - mid.astype(F32)).astype(BF16)
    return jnp.concatenate([hi, mid, lo], axis=axis)


def _cumsum_rows(tri_ref, x):
    c = _dot(tri_ref[...], _split3(x, axis=1))
    return c[:, 0:LANES] + c[:, LANES:2 * LANES] + c[:, 2 * LANES:3 * LANES]


def _cumsum_lanes(x, u_ref):
    r = x.shape[0]
    c = _dot(_split3(x, axis=0), u_ref[...])
    return c[0:r] + c[r:2 * r] + c[2 * r:3 * r]


def _fox_prep_kernel(lf_ref, q_ref, k_ref, v_ref, tri_ref, pq_ref, pk_ref, oq_ref, ok_ref,
                     qa_ref, ka_ref, va_ref, carry):
    @pl.when(pl.program_id(1) == 0)
    def _():
        carry[...] = jnp.zeros_like(carry)

    c = _cumsum_rows(tri_ref, lf_ref[...]) + carry[...]
    rows = c.shape[0]
    carry[...] = c[rows - 1:rows, :]
    parts = _split3(c * LOG2E)
    augq = (_dot(parts, pq_ref[...]) + oq_ref[...]).astype(BF16)
    augk = (_dot(parts, pk_ref[...]) + ok_ref[...]).astype(BF16)
    lane = lax.broadcasted_iota(jnp.int32, (rows, LANES), 1)
    one = jnp.ones((rows, LANES), BF16)
    for p in range(_N_PAIRS):
        src = slice(p * LANES, (p + 1) * LANES)
        qa_ref[:, p * _PAIR_W:p * _PAIR_W + LANES] = q_ref[:, src]
        qa_ref[:, p * _PAIR_W + LANES:(p + 1) * _PAIR_W] = augq[:, src]
        ka_ref[:, p * _PAIR_W:p * _PAIR_W + LANES] = k_ref[:, src]
        ka_ref[:, p * _PAIR_W + LANES:(p + 1) * _PAIR_W] = augk[:, src]
        vp = v_ref[:, src]
        va_ref[:, p * _PAIR_W:p * _PAIR_W + LANES] = jnp.where(lane < HEAD_DIM, vp, one)
        va_ref[:, p * _PAIR_W + LANES:(p + 1) * _PAIR_W] = jnp.where(lane < HEAD_DIM, one, vp)


def _fox_prep(logf, fq, fkb, fvb, nbatch, seq_total, rows_per_step):
    steps = seq_total // rows_per_step
    idx = np.arange(rows_per_step)
    tri = jnp.asarray(idx[:, None] >= idx[None, :], BF16)
    pq, pk, oq, ok = _aug_placement()
    wide = _N_PAIRS * _PAIR_W

    def rmap(b, i):
        return (b * steps + i, 0)

    def const(b, i):
        return (0, 0)

    out = jax.ShapeDtypeStruct((nbatch * seq_total, wide), BF16)
    return pl.pallas_call(
        _fox_prep_kernel,
        grid=(nbatch, steps),
        in_specs=[
            pl.BlockSpec((rows_per_step, LANES), rmap),
            pl.BlockSpec((rows_per_step, FOX_W), rmap),
            pl.BlockSpec((rows_per_step, FOX_W), rmap),
            pl.BlockSpec((rows_per_step, FOX_W), rmap),
            pl.BlockSpec((rows_per_step, rows_per_step), const),
            pl.BlockSpec(pq.shape, const),
            pl.BlockSpec(pk.shape, const),
            pl.BlockSpec(oq.shape, const),
            pl.BlockSpec(ok.shape, const),
        ],
        out_specs=(pl.BlockSpec((rows_per_step, wide), rmap),) * 3,
        out_shape=(out, out, out),
        scratch_shapes=[pltpu.VMEM((1, LANES), F32)],
        compiler_params=_params("arbitrary", "arbitrary"),
        name="fox_prep",
    )(logf, fq, fkb, fvb, tri, pq, pk, oq, ok)


_PAIRS_PER_STEP = 2


def _fox_prompt_kernel(tq, q_ref, k_ref, v_ref, o_ref, acc, m_scr):
    qi = pl.program_id(2)
    nhead = 2 * _PAIRS_PER_STEP
    lane = lax.broadcasted_iota(jnp.int32, (tq, _PAIR_W), 1)
    qs = []
    for h in range(nhead):
        par = h % 2
        q = q_ref[:, (h // 2) * _PAIR_W:(h // 2 + 1) * _PAIR_W]
        own = ((lane >= par * HEAD_DIM) & (lane < (par + 1) * HEAD_DIM)) | (
            (lane >= LANES + par * _AUG_PER_HEAD) & (lane < LANES + (par + 1) * _AUG_PER_HEAD))
        qs.append(jnp.where(own, q, jnp.zeros_like(q)))
        m_scr[h] = jnp.full((tq, LANES), NEG_BIG, F32)
        acc[h] = jnp.zeros((tq, LANES), F32)
    half = tq // 2

    def update(h, r0, nr, k0, nk, mask):
        kb = k_ref[pl.ds(k0, nk), (h // 2) * _PAIR_W:(h // 2 + 1) * _PAIR_W]
        s = _dot_nt(qs[h][r0:r0 + nr], kb)
        if mask is not None:
            s = jnp.where(mask, s, NEG_BIG)
        m_old = m_scr[h, r0:r0 + nr]
        m_new = jnp.maximum(m_old, jnp.max(s, axis=1, keepdims=True))
        p = jnp.exp2(s - jnp.concatenate([m_new] * (nk // LANES), axis=1))
        vb = v_ref[pl.ds(k0, nk), h * LANES:(h + 1) * LANES]
        acc[h, r0:r0 + nr] = jnp.exp2(m_old - m_new) * acc[h, r0:r0 + nr] + _dot(p.astype(BF16), vb)
        m_scr[h, r0:r0 + nr] = m_new

    def body(j, _):
        ks = pl.multiple_of(j * tq, tq)
        for h in range(nhead):
            update(h, 0, tq, ks, tq, None)
        return 0

    lax.fori_loop(0, qi, body, 0)
    ks = pl.multiple_of(qi * tq, tq)
    tri = lax.broadcasted_iota(jnp.int32, (half, half), 0) >= lax.broadcasted_iota(jnp.int32, (half, half), 1)
    low = lax.broadcasted_iota(jnp.int32, (half, tq), 0) + half >= lax.broadcasted_iota(jnp.int32, (half, tq), 1)
    for h in range(nhead):
        update(h, 0, half, ks, half, tri)
        update(h, half, half, ks, tq, low)
    lane128 = lax.broadcasted_iota(jnp.int32, (tq, LANES), 1)
    for pr in range(_PAIRS_PER_STEP):
        outs = []
        for par in range(2):
            a = acc[2 * pr + par]
            outs.append(a / pltpu.roll(a, HEAD_DIM, axis=1))
        o_ref[:, pr * LANES:(pr + 1) * LANES] = jnp.where(lane128 < HEAD_DIM, outs[0], outs[1]).astype(BF16)


def _fox_prompt(qa, ka, va, nbatch, seq_total):
    tq = min(FOX_TQ, seq_total)
    nq = seq_total // tq
    w = _PAIRS_PER_STEP * _PAIR_W
    return pl.pallas_call(
        functools.partial(_fox_prompt_kernel, tq),
        grid=(nbatch, _N_PAIRS // _PAIRS_PER_STEP, nq),
        in_specs=[
            pl.BlockSpec((tq, w), lambda b, g, i: (b * nq + i, g)),
            pl.BlockSpec((seq_total, w), lambda b, g, i: (b, g)),
            pl.BlockSpec((seq_total, w), lambda b, g, i: (b, g)),
        ],
        out_specs=pl.BlockSpec((tq, _PAIRS_PER_STEP * LANES), lambda b, g, i: (b * nq + i, g)),
        out_shape=jax.ShapeDtypeStruct((nbatch * seq_total, FOX_W), BF16),
        scratch_shapes=[
            pltpu.VMEM((2 * _PAIRS_PER_STEP, tq, LANES), F32),
            pltpu.VMEM((2 * _PAIRS_PER_STEP, tq, LANES), F32),
        ],
        compiler_params=_params("arbitrary", "arbitrary", "arbitrary"),
        name="fox_prompt",
    )(qa, ka, va)


def _fox_sample_kernel(npg, tnew, pt_ref, qm_ref, *refs):
    del pt_ref
    k_refs = refs[0:npg]
    v_refs = refs[npg:2 * npg]
    lf_refs = refs[2 * npg:3 * npg]
    knew_ref, vnew_ref, lfnew_ref, u_ref, sel_ref, o_ref, m_scr, l_scr, acc, run, kb_scr, vb_scr = refs[3 * npg:]
    g = pl.program_id(1)
    nrow = tnew * H_FOX
    npage = knew_ref.shape[1]

    @pl.when(g == 0)
    def _():
        m_scr[...] = jnp.full(m_scr.shape, NEG_BIG, F32)
        l_scr[...] = jnp.zeros(l_scr.shape, F32)
        acc[...] = jnp.zeros(acc.shape, F32)
        run[...] = jnp.zeros(run.shape, F32)

    qm = qm_ref[0]

    def cum_within(lft):
        return _cumsum_lanes(lft, u_ref)

    def attend(k, v, ckeys, mask, transposed):
        qk = _dot(qm, k) if transposed else _dot_nt(qm, k)
        s = qk - jnp.concatenate([ckeys * LOG2E] * tnew, axis=0)
        if mask is not None:
            s = jnp.where(mask, s, NEG_BIG)
        m_old = m_scr[...]
        m_new = jnp.maximum(m_old, jnp.max(s, axis=1, keepdims=True))
        p = jnp.exp2(s - m_new)
        alpha = jnp.exp2(m_old - m_new)
        l_scr[...] = alpha * l_scr[...] + jnp.sum(p, axis=1, keepdims=True)
        pb = p.astype(BF16)
        acc[...] = alpha * acc[...] + (_dot_nt(pb, v) if transposed else _dot(pb, v))
        m_scr[...] = m_new

    base = run[...]
    cs = []
    for j in range(npg):
        kb_scr[:, j * npage:(j + 1) * npage] = k_refs[j][0, 0].astype(BF16)
        vb_scr[:, j * npage:(j + 1) * npage] = v_refs[j][0, 0].astype(BF16)
        cw = cum_within(lf_refs[j][0, 0])
        cs.append(base + cw)
        base = base + jnp.broadcast_to(cw[:, npage - 1:npage], base.shape)
    run[...] = base
    attend(kb_scr[...], vb_scr[...], jnp.concatenate(cs, axis=1), None, True)

    @pl.when(g == pl.num_programs(1) - 1)
    def _():
        rt = lax.broadcasted_iota(jnp.int32, (nrow, npage), 0) // H_FOX
        col = lax.broadcasted_iota(jnp.int32, (nrow, npage), 1)
        attend(knew_ref[0].astype(BF16), vnew_ref[0].astype(BF16), run[...] + cum_within(lfnew_ref[0]), col <= rt,
               False)
        o = acc[...] / l_scr[...]
        rh = lax.broadcasted_iota(jnp.int32, (nrow, FOX_W), 0) % H_FOX
        lh = lax.broadcasted_iota(jnp.int32, (nrow, FOX_W), 1) // HEAD_DIM
        om = jnp.where(rh == lh, o, 0.0)
        o_ref[0] = jnp.dot(sel_ref[...], om, precision=HIGHEST, preferred_element_type=F32)


def _fox_sample(layer, page_table, qm, cache_kt, cache_vt, cache_lft, knew, vnew, lfnew):
    nb, npages = page_table.shape
    page = cache_kt.shape[3]
    tnew = qm.shape[1] // H_FOX
    npg = PAGES_PER_STEP
    ngrp = npages // npg
    idx = np.arange(page)
    u = jnp.asarray(idx[:, None] <= idx[None, :], BF16)
    sel = jnp.asarray(np.arange(tnew)[:, None] == (np.arange(tnew * H_FOX)[None, :] // H_FOX), F32)

    def pmap(j):
        def f(b, g, pt):
            return (layer, pt[b * npages + g * npg + j], 0, 0)
        return f

    def bmap(b, g, pt):
        return (b, 0, 0)

    def c2(b, g, pt):
        return (0, 0)

    in_specs = [pl.BlockSpec((1, tnew * H_FOX, FOX_W), bmap)]
    in_specs += [pl.BlockSpec((1, 1, FOX_W, page), pmap(j)) for j in range(npg)]
    in_specs += [pl.BlockSpec((1, 1, FOX_W, page), pmap(j)) for j in range(npg)]
    in_specs += [pl.BlockSpec((1, 1, H_FOX, page), pmap(j)) for j in range(npg)]
    in_specs += [
        pl.BlockSpec((1, page, FOX_W), bmap),
        pl.BlockSpec((1, page, FOX_W), bmap),
        pl.BlockSpec((1, H_FOX, page), bmap),
        pl.BlockSpec((page, page), c2),
        pl.BlockSpec((tnew, tnew * H_FOX), c2),
    ]
    grid_spec = pltpu.PrefetchScalarGridSpec(
        num_scalar_prefetch=1,
        grid=(nb, ngrp),
        in_specs=in_specs,
        out_specs=pl.BlockSpec((1, tnew, FOX_W), bmap),
        scratch_shapes=[
            pltpu.VMEM((tnew * H_FOX, 1), F32),
            pltpu.VMEM((tnew * H_FOX, 1), F32),
            pltpu.VMEM((tnew * H_FOX, FOX_W), F32),
            pltpu.VMEM((H_FOX, page), F32),
            pltpu.VMEM((FOX_W, npg * page), BF16),
            pltpu.VMEM((FOX_W, npg * page), BF16),
        ],
    )
    args = [page_table.reshape(-1), qm] + [cache_kt] * npg + [cache_vt] * npg + [cache_lft] * npg
    args += [knew, vnew, lfnew, u, sel]
    return pl.pallas_call(
        functools.partial(_fox_sample_kernel, npg, tnew),
        grid_spec=grid_spec,
        out_shape=jax.ShapeDtypeStruct((nb, tnew, FOX_W), F32),
        compiler_params=_params("arbitrary", "arbitrary", vmem=VMEM_LIMIT_PAGED),
        name="fox_sample",
    )(*args)


_LANE_GROUP0 = N_EXPERTS


def _pack_bf16_pairs(x):
    n = x.shape[1] // 2
    hi = lax.bitcast_convert_type(x[:, :n].astype(BF16).astype(F32), jnp.uint32)
    lo = lax.bitcast_convert_type(x[:, n:].astype(BF16).astype(F32), jnp.uint32)
    return hi | (lo >> 16)


def _unpack_bf16_pairs(u):
    hi = lax.bitcast_convert_type(u & jnp.uint32(0xFFFF0000), F32)
    lo = lax.bitcast_convert_type(u << 16, F32)
    return jnp.concatenate([hi, lo], axis=1).astype(BF16)


def _out_proj_kernel(n_prompt_tiles, xp_ref, xs_ref, retp_ref, rets_ref, s5p_ref, s5s_ref, foxp_ref, foxs_ref, wo_ref,
                     wof_ref, g_ref, wr_ref, wrl_ref, wrf_ref, br_ref, tri_ref, x1_ref, h2_ref, info_ref, cnt_ref,
                     carry, logit_scr):
    i = pl.program_id(0)

    @pl.when(i == 0)
    def _():
        carry[...] = jnp.zeros_like(carry)

    def project(mm, x_ref, ret_ref, s5_ref, fox_ref, w_ref):
        x1 = x_ref[...] + mm.dot(ret_ref[...], w_ref[0:RET_W, :])
        x1 = x1 + mm.dot(s5_ref[...], w_ref[RET_W:RET_W + S5_W, :])
        x1_ref[...] = x1 + mm.dot(fox_ref[...], w_ref[RET_W + S5_W:, :])

    @pl.when(i < n_prompt_tiles)
    def _():
        project(_FAST, xp_ref, retp_ref, s5p_ref, foxp_ref, wo_ref)

    @pl.when(i >= n_prompt_tiles)
    def _():
        project(_EXACT, xs_ref, rets_ref, s5s_ref, foxs_ref, wof_ref)

    x1 = x1_ref[...]
    ms = jnp.mean(x1 * x1, axis=-1, keepdims=True)
    h2 = x1 * lax.rsqrt(ms + RMS_EPS) * g_ref[...]
    h2_ref[...] = _pack_bf16_pairs(h2)
    @pl.when(i < n_prompt_tiles)
    def _():
        h_hi = h2.astype(BF16)
        h_lo = (h2 - h_hi.astype(F32)).astype(BF16)
        logit_scr[...] = _dot(h_hi, wr_ref[...]) + _dot(h_hi, wrl_ref[...]) + _dot(h_lo, wr_ref[...])

    @pl.when(i >= n_prompt_tiles)
    def _():
        logit_scr[...] = _EXACT.dot(h2, wrf_ref[...])

    logits = logit_scr[...] + br_ref[...]
    tm = logits.shape[0]
    lane = lax.broadcasted_iota(jnp.int32, (tm, LANES), 1)
    big = jnp.int32(LANES)
    gmask = (lane >= _LANE_GROUP0) & (lane < _LANE_GROUP0 + N_EXPERT_GROUPS)
    gl = jnp.where(gmask, logits, NEG_BIG)
    gmax = jnp.max(gl, axis=1, keepdims=True)
    gsum = jnp.sum(jnp.where(gmask, jnp.exp(gl - gmax), 0.0), axis=1, keepdims=True)
    g_w = 1.0 / gsum
    g_idx = jnp.min(jnp.where(gl == gmax, lane, big), axis=1, keepdims=True) - _LANE_GROUP0
    lo = g_idx * EXPERTS_PER_GROUP
    emask = (lane >= lo) & (lane < lo + EXPERTS_PER_GROUP)
    el = jnp.where(emask, logits, NEG_BIG)
    e1 = jnp.max(el, axis=1, keepdims=True)
    idx1 = jnp.min(jnp.where(el == e1, lane, big), axis=1, keepdims=True)
    esum = jnp.sum(jnp.where(emask, jnp.exp(el - e1), 0.0), axis=1, keepdims=True)
    el2 = jnp.where(lane == idx1, NEG_BIG, el)
    e2 = jnp.max(el2, axis=1, keepdims=True)
    idx2 = jnp.min(jnp.where(el2 == e2, lane, big), axis=1, keepdims=True)
    p1 = 1.0 / esum
    p2 = jnp.exp(e2 - e1) / esum
    gate1 = g_w * p1 / (p1 + p2)
    gate2 = g_w * p2 / (p1 + p2)
    oh1 = lane == idx1
    oh2 = lane == idx2
    a = jnp.where(oh1 | oh2, 1.0, 0.0)
    cum = _dot(tri_ref[...], a.astype(BF16))
    before = cum - a + carry[...]
    rank1 = jnp.sum(jnp.where(oh1, before, 0.0), axis=1, keepdims=True)
    rank2 = jnp.sum(jnp.where(oh2, before, 0.0), axis=1, keepdims=True)
    carry[...] = carry[...] + cum[tm - 1:tm, :]
    cnt_ref[...] = carry[...]
    info = jnp.where(lane == 0, idx1.astype(F32), 0.0)
    info = jnp.where(lane == 1, idx2.astype(F32), info)
    info = jnp.where(lane == 2, rank1, info)
    info = jnp.where(lane == 3, rank2, info)
    info = jnp.where(lane == 4, gate1, info)
    info = jnp.where(lane == 5, gate2, info)
    info_ref[...] = info


def _out_proj(xp, xs, xs_tile0, ret_p, ret_s, s5_p, s5_s, fox_p, fox_s, w_out, g2, w_r, b_r, n_prompt_tiles):
    tm = TOKEN_TILE
    d = xp.shape[1]
    n = (n_prompt_tiles + ret_s.shape[0] // tm) * tm
    w_out_fast = w_out.astype(BF16)
    w_r_hi = w_r.astype(BF16)
    w_r_lo = (w_r - w_r_hi.astype(F32)).astype(BF16)
    idx = np.arange(tm)
    tri = jnp.asarray(idx[:, None] >= idx[None, :], BF16)

    def row(i):
        return (i, 0)

    def const(i):
        return (0, 0)

    def pmap(i):
        return (jnp.minimum(i, n_prompt_tiles - 1), 0)

    def smap(i):
        return (jnp.maximum(i - n_prompt_tiles, 0), 0)

    def xsmap(i):
        return (xs_tile0 + jnp.maximum(i - n_prompt_tiles, 0), 0)

    return pl.pallas_call(
        functools.partial(_out_proj_kernel, n_prompt_tiles),
        grid=(n // tm,),
        in_specs=[
            pl.BlockSpec((tm, d), pmap),
            pl.BlockSpec((tm, d), xsmap),
            pl.BlockSpec((tm, RET_W), pmap),
            pl.BlockSpec((tm, RET_W), smap),
            pl.BlockSpec((tm, S5_W), pmap),
            pl.BlockSpec((tm, S5_W), smap),
            pl.BlockSpec((tm, FOX_W), pmap),
            pl.BlockSpec((tm, FOX_W), smap),
            pl.BlockSpec((d, d), const),
            pl.BlockSpec((d, d), const),
            pl.BlockSpec((1, d), const),
            pl.BlockSpec((d, LANES), const),
            pl.BlockSpec((d, LANES), const),
            pl.BlockSpec((d, LANES), const),
            pl.BlockSpec((1, LANES), const),
            pl.BlockSpec((tm, tm), const),
        ],
        out_specs=(
            pl.BlockSpec((tm, d), row),
            pl.BlockSpec((tm, d // 2), row),
            pl.BlockSpec((tm, LANES), row),
            pl.BlockSpec((1, LANES), const),
        ),
        out_shape=(
            jax.ShapeDtypeStruct((n, d), F32),
            jax.ShapeDtypeStruct((n, d // 2), jnp.uint32),
            jax.ShapeDtypeStruct((n, LANES), F32),
            jax.ShapeDtypeStruct((1, LANES), F32),
        ),
        scratch_shapes=[pltpu.VMEM((1, LANES), F32), pltpu.VMEM((tm, LANES), F32)],
        compiler_params=_params("arbitrary"),
        name="out_proj_router",
    )(xp, xs, ret_p, ret_s, s5_p, s5_s, fox_p, fox_s, w_out_fast, w_out, g2, w_r_hi, w_r_lo, w_r, b_r, tri)


_ROW_DMA_UNROLL = 8


def _dispatch_kernel(dest_ref, h_ref, rows_in_ref, rows_ref, sem):
    del rows_in_ref
    i = pl.program_id(0)
    tm = h_ref.shape[0]

    def issue(r, _):
        t = i * tm + r
        for k in range(2):
            pltpu.make_async_copy(h_ref.at[pl.ds(r, 1)], rows_ref.at[pl.ds(dest_ref[2 * t + k], 1)], sem).start(
                priority=k)
        return 0

    lax.fori_loop(0, tm, issue, 0, unroll=_ROW_DMA_UNROLL)
    for _ in range(2):
        pltpu.make_async_copy(h_ref, rows_ref.at[pl.ds(0, tm)], sem).wait()


def _dispatch(dest, h2, n_rows):
    n, d = h2.shape
    tm = TOKEN_TILE
    grid_spec = pltpu.PrefetchScalarGridSpec(
        num_scalar_prefetch=1,
        grid=(n // tm,),
        in_specs=[
            pl.BlockSpec((tm, d), lambda i, dest: (i, 0)),
            pl.BlockSpec(memory_space=pl.ANY),
        ],
        out_specs=pl.BlockSpec(memory_space=pl.ANY),
        scratch_shapes=[pltpu.SemaphoreType.DMA(())],
    )
    return pl.pallas_call(
        _dispatch_kernel,
        grid_spec=grid_spec,
        out_shape=jax.ShapeDtypeStruct((n_rows, d), h2.dtype),
        input_output_aliases={2: 0},
        compiler_params=_params("arbitrary"),
        name="moe_dispatch",
    )(dest, h2, jnp.zeros((n_rows, d), h2.dtype))


def _experts_kernel(be_ref, nu_ref, x_ref, w1_ref, w3_ref, w2_ref, y_ref, w1b, w3b, w2b):
    i = pl.program_id(0)
    prev = be_ref[jnp.maximum(i - 1, 0)]
    fresh = (i == 0) | (be_ref[i] != prev)
    active = i < nu_ref[0]

    @pl.when(active & fresh)
    def _():
        w1b[...] = w1_ref[0, 0].astype(BF16)
        w3b[...] = w3_ref[0, 0].astype(BF16)
        w2b[...] = w2_ref[0, 0].astype(BF16)

    @pl.when(active)
    def _():
        xb = _unpack_bf16_pairs(x_ref[...])
        h1 = _dot(xb, w1b[...])
        h3 = _dot(xb, w3b[...])
        a = (h1 * _sigmoid(h1) * h3).astype(BF16)
        y_ref[...] = _dot(a, w2b[...])

    @pl.when(jnp.logical_not(active))
    def _():
        y_ref[...] = jnp.zeros_like(y_ref)


def _experts(layer, blk_e, n_used, x_rows, w1, w3, w2):
    n_rows, dpacked = x_rows.shape
    d, de = w1.shape[2], w1.shape[3]
    nblk = n_rows // MOE_ROWS

    def last_used(i, nu):
        return jnp.minimum(i, jnp.maximum(nu[0] - 1, 0))

    def xmap(i, be, nu):
        return (last_used(i, nu), 0)

    def wmap(i, be, nu):
        return (layer, be[last_used(i, nu)], 0, 0)

    grid_spec = pltpu.PrefetchScalarGridSpec(
        num_scalar_prefetch=2,
        grid=(nblk,),
        in_specs=[
            pl.BlockSpec((MOE_ROWS, dpacked), xmap),
            pl.BlockSpec((1, 1, d, de), wmap),
            pl.BlockSpec((1, 1, d, de), wmap),
            pl.BlockSpec((1, 1, de, d), wmap),
        ],
        out_specs=pl.BlockSpec((MOE_ROWS, d), lambda i, be, nu: (i, 0)),
        scratch_shapes=[
            pltpu.VMEM((d, de), BF16),
            pltpu.VMEM((d, de), BF16),
            pltpu.VMEM((de, d), BF16),
        ],
    )
    return pl.pallas_call(
        _experts_kernel,
        grid_spec=grid_spec,
        out_shape=jax.ShapeDtypeStruct((n_rows, d), F32),
        compiler_params=_params("arbitrary"),
        name="moe_experts",
    )(blk_e, n_used, x_rows, w1, w3, w2)


def _combine_kernel(n_prompt_tiles, dest_ref, x1_ref, info_ref, gf_ref, y_ref, o_ref, *rest):
    os_ref = rest[0] if n_prompt_tiles is not None else None
    buf, sem = rest[-2:]
    i = pl.program_id(0)
    tm = x1_ref.shape[0]

    def issue(r, _):
        t = i * tm + r
        for k in range(2):
            pltpu.make_async_copy(y_ref.at[pl.ds(dest_ref[2 * t + k], 1)], buf.at[k, pl.ds(r, 1)], sem).start(
                priority=k)
        return 0

    lax.fori_loop(0, tm, issue, 0, unroll=_ROW_DMA_UNROLL)
    for k in range(2):
        pltpu.make_async_copy(y_ref.at[pl.ds(0, tm)], buf.at[k], sem).wait()
    info = info_ref[...]
    x2 = x1_ref[...] + info[:, 4:5] * buf[0] + info[:, 5:6] * buf[1]
    if n_prompt_tiles is None:
        o_ref[...] = x2
    else:
        ms = jnp.mean(x2 * x2, axis=-1, keepdims=True)
        y = x2 * lax.rsqrt(ms + RMS_EPS) * gf_ref[...]

        @pl.when(i < n_prompt_tiles)
        def _():
            o_ref[...] = y

        @pl.when(i >= n_prompt_tiles)
        def _():
            os_ref[...] = y


def _combine(dest, x1, info, gf, y_rows, n_prompt_tiles):
    n, d = x1.shape
    tm = TOKEN_TILE
    if n_prompt_tiles is None:
        out_specs = pl.BlockSpec((tm, d), lambda i, dest: (i, 0))
        out_shape = jax.ShapeDtypeStruct((n, d), F32)
    else:
        npt = n_prompt_tiles
        out_specs = (pl.BlockSpec((tm, d), lambda i, dest: (jnp.minimum(i, npt - 1), 0)),
                     pl.BlockSpec((tm, d), lambda i, dest: (jnp.maximum(i - npt, 0), 0)))
        out_shape = (jax.ShapeDtypeStruct((npt * tm, d), F32), jax.ShapeDtypeStruct((n - npt * tm, d), F32))
    grid_spec = pltpu.PrefetchScalarGridSpec(
        num_scalar_prefetch=1,
        grid=(n // tm,),
        in_specs=[
            pl.BlockSpec((tm, d), lambda i, dest: (i, 0)),
            pl.BlockSpec((tm, LANES), lambda i, dest: (i, 0)),
            pl.BlockSpec((1, d), lambda i, dest: (0, 0)),
            pl.BlockSpec(memory_space=pl.ANY),
        ],
        out_specs=out_specs,
        scratch_shapes=[
            pltpu.VMEM((2, tm, d), F32),
            pltpu.SemaphoreType.DMA(()),
        ],
    )
    return pl.pallas_call(
        functools.partial(_combine_kernel, n_prompt_tiles),
        grid_spec=grid_spec,
        out_shape=out_shape,
        compiler_params=_params("arbitrary"),
        name="moe_combine",
    )(dest, x1, info, gf, y_rows)


def _moe_plan(info, counts, n_tokens):
    eid = info[:, 0:2].astype(jnp.int32)
    rank = info[:, 2:4].astype(jnp.int32)
    cnt = counts[0, :N_EXPERTS].astype(jnp.int32)
    padded = (cnt + MOE_ROWS - 1) // MOE_ROWS * MOE_ROWS
    pend = jnp.cumsum(padded)
    pstart = pend - padded
    experts = jnp.arange(N_EXPERTS, dtype=jnp.int32)
    start_of = jnp.sum(jnp.where(eid[..., None] == experts, pstart, 0), axis=-1)
    dest = (start_of + rank).reshape(-1)
    nblk = -(-n_tokens * 2 // MOE_ROWS) + N_EXPERTS
    blk_start = jnp.arange(nblk, dtype=jnp.int32) * MOE_ROWS
    blk_e = jnp.minimum(jnp.sum((pend[None, :] <= blk_start[:, None]).astype(jnp.int32), axis=1), N_EXPERTS - 1)
    n_used = (pend[-1:] // MOE_ROWS).astype(jnp.int32)
    return dest.astype(jnp.int32), blk_e.astype(jnp.int32), n_used, nblk * MOE_ROWS


def _rope_tables(seq, past, tnew, tile):
    half = HEAD_DIM // 2
    inv = ROPE_THETA ** (-jnp.arange(half, dtype=F32) / half)
    pos = jnp.concatenate([jnp.arange(seq, dtype=F32), past + (jnp.arange(tile) % tnew).astype(F32)])
    ang = pos[:, None] * inv[None, :]
    cos = jnp.cos(ang)
    sin = jnp.sin(ang)
    cos_h = jnp.concatenate([cos, cos], axis=1)
    sin_h = jnp.concatenate([-sin, sin], axis=1)
    return jnp.tile(cos_h, (1, H_RET)), jnp.tile(sin_h, (1, H_RET))


def _block_diag_states(s):
    b = s.shape[0]
    eye = jnp.eye(H_RET, dtype=s.dtype)
    return jnp.einsum('bhde,hg->bhdge', s, eye).reshape(b, RET_W, RET_W)


def _diag_blocks(s_bd):
    b = s_bd.shape[0]
    s5 = s_bd.reshape(b, H_RET, HEAD_DIM, H_RET, HEAD_DIM)
    return jnp.stack([s5[:, h, :, h, :] for h in range(H_RET)], axis=1)


def kernel(x_prompt, x_sample, cache_k, cache_v, cache_logf, state_ret, state_s5, page_table, norm1, w_in, ret_gn,
           s5_lam_re, s5_lam_im, s5_b_re, s5_b_im, s5_c_re, s5_c_im, s5_d, s5_log_dt, s5_glu_w, s5_glu_b, fox_fb,
           w_out, norm2, w_rg, b_rg, w_re, b_re, w1, w3, w2, norm_f):
    bp, seq, d = x_prompt.shape
    bd_, tnew, _ = x_sample.shape
    depth = w_in.shape[0]
    n_phys, page = cache_k.shape[1], cache_k.shape[2]
    npages = page_table.shape[1]
    past = npages * page
    n_p = bp * seq
    n_s = bd_ * tnew
    n = n_p + n_s
    tm = TOKEN_TILE
    assert n_p % tm == 0 and n_s == tm and seq % tm == 0
    n_prompt_tiles = n_p // tm

    xp, xs, xs_tile0 = x_prompt.reshape(n_p, d), x_sample.reshape(n_s, d), 0
    cos_t, sin_t = _rope_tables(seq, past, tnew, tm)
    cache_kt = jnp.transpose(cache_k, (0, 1, 3, 4, 2)).reshape(depth, n_phys, FOX_W, page)
    cache_vt = jnp.transpose(cache_v, (0, 1, 3, 4, 2)).reshape(depth, n_phys, FOX_W, page)
    cache_lft = jnp.swapaxes(cache_logf, 2, 3)
    ff0 = w_in.shape[2] - H_FOX

    outs = {k: [] for k in ('kp', 'vp', 'lp', 'ks', 'vs', 'ls', 'rp', 'rs', 'sp', 'ss')}
    for l in range(depth):
        w_main = w_in[l, :, :ff0]
        w_ff = jnp.pad(w_in[l, :, ff0:], ((0, 0), (0, LANES - H_FOX)))
        fb = jnp.pad(fox_fb[l], (0, LANES - H_FOX))[None, :]
        g1 = norm1[l][None, :]
        ret_in, su, fq, fk, fv, fkb, fvb, logf = _in_proj(
            _FAST, xp, g1, w_main.astype(BF16), w_ff.astype(BF16), fb, cos_t, sin_t,
            0, n_prompt_tiles, 0, seq // tm)
        ret_in_s, su_s, fq_s, fk_s, fv_s, _, _, logf_s = _in_proj(
            _EXACT, xs, g1, w_main, w_ff, fb, cos_t, sin_t, xs_tile0, 1, seq // tm, 1)

        gw = ret_gn[l][None, :]
        zero_state = jnp.zeros((bp, RET_W, RET_W), F32)
        ret_p, rs_p = _retention(_FAST, ret_in, zero_state, gw, bp, 0, seq, 1, min(RET_CHUNK, seq),
                                 min(1024, seq))
        seq_per_chunk = 8
        chunk_rows = seq_per_chunk * tnew
        ret_s, rs_s = _retention(_EXACT, ret_in_s, _block_diag_states(state_ret[l]), gw, bd_ // seq_per_chunk, 0,
                                 chunk_rows, seq_per_chunk, tnew, chunk_rows)

        tables = _s5_tables(s5_lam_re[l], s5_lam_im[l], s5_b_re[l], s5_b_im[l], s5_c_re[l], s5_c_im[l],
                            s5_log_dt[l])
        dsk = s5_d[l][None, :]
        glub = s5_glu_b[l][None, :]
        zs = jnp.zeros((bp, 1, S5_LANES), F32)
        s5_p, sp_r, sp_i = _s5(_FAST, su, zs, zs, tables, dsk, s5_glu_w[l], glub, bp, 0, seq, min(512, seq),
                               False)
        x0 = state_s5[l].reshape(bd_, 1, S5_LANES, 2)
        s5_s, ss_r, ss_i = _s5(_EXACT, su_s, x0[..., 0], x0[..., 1], tables, dsk, s5_glu_w[l], glub, 1, 0, n_s,
                               n_s, True)

        qa, ka, va = _fox_prep(logf, fq, fkb, fvb, bp, seq, min(256, seq))
        fox_p = _fox_prompt(qa, ka, va, bp, seq)
        q_s = fq_s.reshape(bd_, tnew, 1, FOX_W)
        hmask = jnp.asarray(np.arange(H_FOX)[:, None] == (np.arange(FOX_W) // HEAD_DIM)[None, :])[None, None]
        qm = jnp.where(hmask, q_s, jnp.zeros((), BF16)).reshape(bd_, tnew * H_FOX, FOX_W)
        pad_rows = ((0, 0), (0, page - tnew), (0, 0))
        knew = jnp.pad(fk_s.reshape(bd_, tnew, FOX_W), pad_rows)
        vnew = jnp.pad(fv_s.reshape(bd_, tnew, FOX_W), pad_rows)
        lfnew = jnp.pad(jnp.swapaxes(logf_s[:, :H_FOX].reshape(bd_, tnew, H_FOX), 1, 2),
                        ((0, 0), (0, 0), (0, page - tnew)))
        fox_s = _fox_sample(l, page_table, qm, cache_kt, cache_vt, cache_lft, knew, vnew, lfnew)
        fox_s = fox_s.reshape(n_s, FOX_W)

        w_r = jnp.pad(jnp.concatenate([w_re[l], w_rg[l]], axis=1),
                      ((0, 0), (0, LANES - N_EXPERTS - N_EXPERT_GROUPS)))
        b_r = jnp.pad(jnp.concatenate([b_re[l], b_rg[l]]), (0, LANES - N_EXPERTS - N_EXPERT_GROUPS))[None, :]
        x1, h2, info, counts = _out_proj(xp, xs, xs_tile0, ret_p, ret_s, s5_p, s5_s, fox_p, fox_s, w_out[l],
                                         norm2[l][None, :], w_r, b_r, n_prompt_tiles)
        dest, blk_e, n_used, n_rows = _moe_plan(info, counts, n)
        x_rows = _dispatch(dest, h2, n_rows)
        y_rows = _experts(l, blk_e, n_used, x_rows, w1, w3, w2)
        if l < depth - 1:
            x_all = _combine(dest, x1, info, norm_f[None, :], y_rows, None)
            xp, xs, xs_tile0 = x_all, x_all, n_prompt_tiles
        else:
            y_p, y_s = _combine(dest, x1, info, norm_f[None, :], y_rows, n_prompt_tiles)

        outs['kp'].append(fk)
        outs['vp'].append(fv)
        outs['lp'].append(logf[:, :H_FOX].reshape(bp, seq, H_FOX))
        outs['ks'].append(fk_s.reshape(bd_, tnew, H_FOX, HEAD_DIM))
        outs['vs'].append(fv_s.reshape(bd_, tnew, H_FOX, HEAD_DIM))
        outs['ls'].append(logf_s[:, :H_FOX].reshape(bd_, tnew, H_FOX))
        outs['rp'].append(_diag_blocks(rs_p))
        outs['rs'].append(_diag_blocks(rs_s))
        outs['sp'].append(jnp.stack([sp_r.reshape(bp, S5_GROUPS, S5_STATE),
                                     sp_i.reshape(bp, S5_GROUPS, S5_STATE)], axis=-1))
        outs['ss'].append(jnp.stack([ss_r.reshape(bd_, S5_GROUPS, S5_STATE),
                                     ss_i.reshape(bd_, S5_GROUPS, S5_STATE)], axis=-1))

    y_prompt = y_p.reshape(bp, seq, d)
    y_sample = y_s.reshape(bd_, tnew, d)
    st = {k: jnp.stack(v) for k, v in outs.items()}
    for name in ('kp', 'vp'):
        st[name] = st[name].reshape(depth, bp, seq, H_FOX, HEAD_DIM)
    return (y_prompt, y_sample, st['kp'], st['vp'], st['lp'], st['ks'], st['vs'], st['ls'],
            st['rp'], st['rs'], st['sp'], st['ss'])
```
